```python
import jax
import jax.numpy as jnp
from jax import lax
import numpy as np

D_MODEL = 2048
BATCH = 8
SEQ = 2048
DEPTH = 4

PLE_DIM = 256
D_FF = 5632
RMS_EPS = 1e-6
N_EVEN = (DEPTH + 1) // 2
N_ODD = DEPTH // 2
N_VRES = max(N_ODD - 1, 0)

A_WIDTH = D_MODEL // 2
A_GROUP = 128
A_GROUPS = A_WIDTH // A_GROUP
A_CHUNK = 128
B_WIDTH = D_MODEL // 2
B_HEAD_DIM = 128
B_HEADS = B_WIDTH // B_HEAD_DIM
B_CHUNK = 64
B_MIN_F = 1e-30
EVEN_IN = 2 * A_WIDTH + 4 * B_WIDTH
C_HEAD = 64
C_HEADS = D_MODEL // C_HEAD
C_DECAY_LORA = 96
C_AAA_LORA = 96
C_MV_LORA = 64
C_GATE_LORA = 256
C_GN_EPS = 64e-5

kernel_name = 'hybrid_gmlp_hgrn2_rwkv7_macaron'


def rmsnorm(x, g, eps=RMS_EPS):
    xf = x.astype(jnp.float32)
    y = xf * lax.rsqrt(jnp.mean(xf * xf, axis=-1, keepdims=True) + eps)
    return (y * g.astype(jnp.float32)).astype(x.dtype)


def swiglu(x, w_gate, w_up, w_down):
    return (jax.nn.silu(x @ w_gate) * (x @ w_up)) @ w_down


def chunked_gmlp(u, v, v_gain, w_s, b_s):
    bsz, seq, _ = u.shape
    n_chunks = seq // A_CHUNK
    vg = rmsnorm(v.reshape(bsz, seq, A_GROUPS, A_GROUP), v_gain.reshape(A_GROUPS, A_GROUP))
    vg = vg.reshape(bsz, n_chunks, A_CHUNK, A_GROUPS, A_GROUP)
    causal = jnp.tril(jnp.ones((A_CHUNK, A_CHUNK), dtype=bool))
    w = jnp.where(causal[None], w_s, jnp.zeros_like(w_s))
    s = jnp.einsum('gts,bnsgc->bntgc', w, vg) + b_s.T[None, None, :, :, None]
    return u * s.reshape(bsz, seq, A_WIDTH)


def hgrn2(q, f_logit, i_in, lb):
    bsz, seq, _ = q.shape
    n_chunks = seq // B_CHUNK
    z = f_logit.astype(jnp.float32)
    lb = lb.astype(jnp.float32)
    sig = jax.nn.sigmoid(z)
    f = lb + (1.0 - lb) * sig
    log_f = jnp.log(jnp.maximum(f, B_MIN_F))
    k = 1.0 - f
    qf = jax.nn.silu(q.astype(jnp.float32))

    def to_chunks(t):
        return t.reshape(bsz, n_chunks, B_CHUNK, B_HEADS, B_HEAD_DIM).transpose(1, 0, 3, 2, 4)

    xs = (to_chunks(qf), to_chunks(k), to_chunks(i_in.astype(jnp.float32)), to_chunks(log_f))
    causal = jnp.tril(jnp.ones((B_CHUNK, B_CHUNK), dtype=bool))[:, :, None]

    def step(state, chunk):
        qb, kb, vb, gb = chunk
        cum = jnp.cumsum(gb, axis=2)
        last = cum[:, :, -1:, :]
        o_inter = jnp.einsum('bhtk,bhkv->bhtv', qb * jnp.exp(cum), state)
        rel = cum[:, :, :, None, :] - cum[:, :, None, :, :]
        dec = jnp.where(causal, jnp.exp(jnp.minimum(rel, 0.0)), 0.0)
        att = jnp.einsum('bhtk,bhsk,bhtsk->bhts', qb, kb, dec)
        o = o_inter + jnp.einsum('bhts,bhsv->bhtv', att, vb)
        new_state = jnp.exp(last[:, :, 0, :])[..., None] * state + jnp.einsum('bhsk,bhsv->bhkv', kb * jnp.exp(last - cum), vb)
        return new_state, o

    s0 = jnp.zeros((bsz, B_HEADS, B_HEAD_DIM, B_HEAD_DIM), jnp.float32)
    _, o = lax.scan(step, s0, xs)
    return o.transpose(1, 0, 3, 2, 4).reshape(bsz, seq, B_WIDTH)


def rwkv7(x, mix, w_r, w_k, w_v, w_o, w0, w1, w2, a0, a1, a2, g1, g2, k_k, k_a, r_k, gn_g, gn_b, v_first, v_res):
    bsz, seq, d = x.shape
    xx = jnp.pad(x, ((0, 0), (1, 0), (0, 0)))[:, :-1] - x
    xr, xw, xk, xv, xa, xg = [x + xx * mix[j] for j in range(6)]
    r = xr @ w_r
    k = xk @ w_k
    v = xv @ w_v
    w = -jax.nn.softplus(-(w0 + jnp.tanh(xw @ w1) @ w2)) - 0.5
    if v_res is None:
        v_first = v
    else:
        v0, v1, v2 = v_res
        v = v + (v_first - v) * jax.nn.sigmoid(v0 + (xv @ v1) @ v2)
    a = jax.nn.sigmoid(a0 + (xa @ a1) @ a2)
    g = jax.nn.sigmoid(xg @ g1) @ g2

    def heads(t):
        return t.reshape(bsz, seq, C_HEADS, C_HEAD).astype(jnp.float32)

    kk = heads(k * k_k)
    kk = kk / jnp.maximum(jnp.sqrt(jnp.sum(kk * kk, axis=-1, keepdims=True)), 1e-12)
    k = k * (1 + (a - 1) * k_a)
    decay = jnp.exp(-jnp.exp(w.astype(jnp.float32)))
    rh, kh, vh, ah = heads(r), heads(k), heads(v), heads(a)

    def step(state, inp):
        r_t, w_t, k_t, v_t, a_t, b_t = inp
        sa = jnp.einsum('bhvk,bhk->bhv', state, a_t)
        state = state * w_t[:, :, None, :] + v_t[..., None] * k_t[:, :, None, :] + sa[..., None] * b_t[:, :, None, :]
        return state, jnp.einsum('bhvk,bhk->bhv', state, r_t)

    def time_major(t):
        return jnp.moveaxis(t, 1, 0)

    xs = tuple(time_major(t) for t in (rh, heads(decay), kh, vh, -kk, kk * ah))
    s0 = jnp.zeros((bsz, C_HEADS, C_HEAD, C_HEAD), jnp.float32)
    _, y = lax.scan(step, s0, xs)
    y = jnp.moveaxis(y, 0, 1)
    mu = jnp.mean(y, axis=-1, keepdims=True)
    var = jnp.mean(jnp.square(y - mu), axis=-1, keepdims=True)
    y = ((y - mu) * lax.rsqrt(var + C_GN_EPS)).reshape(bsz, seq, d) * gn_g + gn_b
    bonus = jnp.sum(rh * kh * r_k, axis=-1, keepdims=True) * vh
    y = (y + bonus.reshape(bsz, seq, d)).astype(x.dtype)
    return ((y * g) @ w_o).astype(x.dtype), v_first


def _fwd_setup_inputs(seed: int = 0) -> dict:
    key = jax.random.key(seed)
    ks = iter(jax.random.split(key, 64))
    f32 = jnp.float32

    def nrm(shape, scale):
        return jax.random.normal(next(ks), shape, f32) * scale

    def gain(shape):
        return 1.0 + nrm(shape, 0.05)

    D = D_MODEL
    return {
        'x': nrm((BATCH, SEQ, D), 1.0),
        'p': nrm((DEPTH, BATCH, SEQ, PLE_DIM), 1.0),
        'norms': gain((DEPTH, 4, D)),
        'final_norm': gain((D,)),
        'ffn_wg': nrm((DEPTH, 2, D, D_FF), D ** -0.5),
        'ffn_wu': nrm((DEPTH, 2, D, D_FF), D ** -0.5),
        'ffn_wd': nrm((DEPTH, 2, D_FF, D), 0.5 * D_FF ** -0.5),
        'ple_wp': nrm((DEPTH, PLE_DIM, D), 0.5 * PLE_DIM ** -0.5),
        'ple_wg': nrm((DEPTH, D, D), D ** -0.5),
        'e_w_in': nrm((N_EVEN, D, EVEN_IN), D ** -0.5),
        'e_w_out': nrm((N_EVEN, D, D), 0.5 * D ** -0.5),
        'a_vnorm': gain((N_EVEN, A_WIDTH)),
        'a_ws': nrm((N_EVEN, A_GROUPS, A_CHUNK, A_CHUNK), 0.5 * A_CHUNK ** -0.5),
        'a_bs': 1.0 + nrm((N_EVEN, A_GROUPS, A_CHUNK), 0.1),
        'b_onorm': gain((N_EVEN, B_WIDTH)),
        'b_lb_logits': nrm((DEPTH, B_WIDTH), 0.5),
        'c_mix': jax.random.uniform(next(ks), (N_ODD, 6, D), f32),
        'c_wr': nrm((N_ODD, D, D), D ** -0.5),
        'c_wk': nrm((N_ODD, D, D), D ** -0.5),
        'c_wv': nrm((N_ODD, D, D), D ** -0.5),
        'c_wo': nrm((N_ODD, D, D), 0.5 * D ** -0.5),
        'c_w0': jax.random.uniform(next(ks), (N_ODD, D), f32, -6.0, -1.0),
        'c_w1': nrm((N_ODD, D, C_DECAY_LORA), D ** -0.5),
        'c_w2': nrm((N_ODD, C_DECAY_LORA, D), 0.5 * C_DECAY_LORA ** -0.5),
        'c_a0': nrm((N_ODD, D), 0.5),
        'c_a1': nrm((N_ODD, D, C_AAA_LORA), D ** -0.5),
        'c_a2': nrm((N_ODD, C_AAA_LORA, D), 0.5 * C_AAA_LORA ** -0.5),
        'c_g1': nrm((N_ODD, D, C_GATE_LORA), D ** -0.5),
        'c_g2': nrm((N_ODD, C_GATE_LORA, D), C_GATE_LORA ** -0.5),
        'c_kk': 0.85 + nrm((N_ODD, D), 0.05),
        'c_ka': 1.0 + nrm((N_ODD, D), 0.05),
        'c_rk': nrm((N_ODD, C_HEADS, C_HEAD), 0.1),
        'c_gn_g': gain((N_ODD, D)),
        'c_gn_b': nrm((N_ODD, D), 0.01),
        'c_v0': 1.0 + nrm((N_VRES, D), 0.1),
        'c_v1': nrm((N_VRES, D, C_MV_LORA), 0.5 * D ** -0.5),
        'c_v2': nrm((N_VRES, C_MV_LORA, D), 0.5 * C_MV_LORA ** -0.5),
    }


def _fwd_reference(x, p, norms, final_norm, ffn_wg, ffn_wu, ffn_wd, ple_wp, ple_wg, e_w_in, e_w_out,
              a_vnorm, a_ws, a_bs, b_onorm, b_lb_logits, c_mix, c_wr, c_wk, c_wv, c_wo, c_w0, c_w1, c_w2,
              c_a0, c_a1, c_a2, c_g1, c_g2, c_kk, c_ka, c_rk, c_gn_g, c_gn_b, c_v0, c_v1, c_v2):
    probs = jax.nn.softmax(b_lb_logits.astype(jnp.float32), axis=0)
    lower_bounds = jnp.cumsum(probs, axis=0) - probs[0]
    split_at = [A_WIDTH, 2 * A_WIDTH, 2 * A_WIDTH + B_WIDTH, 2 * A_WIDTH + 2 * B_WIDTH, 2 * A_WIDTH + 3 * B_WIDTH]
    h = x
    v_first = None
    for i in range(DEPTH):
        j = i // 2
        h = h + 0.5 * swiglu(rmsnorm(h, norms[i, 0]), ffn_wg[i, 0], ffn_wu[i, 0], ffn_wd[i, 0])
        hn = rmsnorm(h, norms[i, 1])
        if i % 2 == 0:
            proj = hn @ e_w_in[j]
            au, av, bq, bf, bi, bg = jnp.split(proj, split_at, axis=-1)
            a_out = chunked_gmlp(jax.nn.gelu(au), jax.nn.gelu(av), a_vnorm[j], a_ws[j], a_bs[j])
            b_o = hgrn2(bq, bf, bi, lower_bounds[i]).astype(hn.dtype)
            bsz, seq, _ = b_o.shape
            b_o = rmsnorm(b_o.reshape(bsz, seq, B_HEADS, B_HEAD_DIM), b_onorm[j].reshape(B_HEADS, B_HEAD_DIM))
            b_out = b_o.reshape(bsz, seq, B_WIDTH) * jax.nn.silu(bg)
            mixed = jnp.concatenate([a_out.astype(hn.dtype), b_out.astype(hn.dtype)], axis=-1) @ e_w_out[j]
        else:
            v_res = None if j == 0 else (c_v0[j - 1], c_v1[j - 1], c_v2[j - 1])
            mixed, v_first = rwkv7(hn, c_mix[j], c_wr[j], c_wk[j], c_wv[j], c_wo[j], c_w0[j], c_w1[j], c_w2[j],
                                   c_a0[j], c_a1[j], c_a2[j], c_g1[j], c_g2[j], c_kk[j], c_ka[j], c_rk[j],
                                   c_gn_g[j], c_gn_b[j], v_first, v_res)
        h = h + mixed.astype(h.dtype)
        h = h + 0.5 * swiglu(rmsnorm(h, norms[i, 2]), ffn_wg[i, 1], ffn_wu[i, 1], ffn_wd[i, 1])
        gate = jax.nn.sigmoid(rmsnorm(h, norms[i, 3]) @ ple_wg[i])
        h = h + gate * (p[i] @ ple_wp[i])
    return rmsnorm(h, final_norm)


import jax as _jax
import jax.numpy as _jnp

TWIN_FORMAT = 'train_step'
FWD_PARAMS = ['x', 'p', 'norms', 'final_norm', 'ffn_wg', 'ffn_wu', 'ffn_wd', 'ple_wp', 'ple_wg', 'e_w_in', 'e_w_out', 'a_vnorm', 'a_ws', 'a_bs', 'b_onorm', 'b_lb_logits', 'c_mix', 'c_wr', 'c_wk', 'c_wv', 'c_wo', 'c_w0', 'c_w1', 'c_w2', 'c_a0', 'c_a1', 'c_a2', 'c_g1', 'c_g2', 'c_kk', 'c_ka', 'c_rk', 'c_gn_g', 'c_gn_b', 'c_v0', 'c_v1', 'c_v2']
TWIN_WEIGHTS = ['norms', 'final_norm', 'ffn_wg', 'ffn_wu', 'ffn_wd', 'ple_wp', 'ple_wg', 'e_w_in', 'e_w_out', 'a_vnorm', 'a_ws', 'a_bs', 'b_onorm', 'b_lb_logits', 'c_mix', 'c_wr', 'c_wk', 'c_wv', 'c_wo', 'c_w0', 'c_w1', 'c_w2', 'c_a0', 'c_a1', 'c_a2', 'c_g1', 'c_g2', 'c_kk', 'c_ka', 'c_rk', 'c_gn_g', 'c_gn_b', 'c_v0', 'c_v1', 'c_v2']
TWIN_DIFF_INPUT = 'x'
TWIN_INPUTS = ['x', 'p', 'norms', 'final_norm', 'ffn_wg', 'ffn_wu', 'ffn_wd', 'ple_wp', 'ple_wg', 'e_w_in', 'e_w_out', 'a_vnorm', 'a_ws', 'a_bs', 'b_onorm', 'b_lb_logits', 'c_mix', 'c_wr', 'c_wk', 'c_wv', 'c_wo', 'c_w0', 'c_w1', 'c_w2', 'c_a0', 'c_a1', 'c_a2', 'c_g1', 'c_g2', 'c_kk', 'c_ka', 'c_rk', 'c_gn_g', 'c_gn_b', 'c_v0', 'c_v1', 'c_v2', 'loss_target', 'm_norms', 'm_final_norm', 'm_ffn_wg', 'm_ffn_wu', 'm_ffn_wd', 'm_ple_wp', 'm_ple_wg', 'm_e_w_in', 'm_e_w_out', 'm_a_vnorm', 'm_a_ws', 'm_a_bs', 'm_b_onorm', 'm_b_lb_logits', 'm_c_mix', 'm_c_wr', 'm_c_wk', 'm_c_wv', 'm_c_wo', 'm_c_w0', 'm_c_w1', 'm_c_w2', 'm_c_a0', 'm_c_a1', 'm_c_a2', 'm_c_g1', 'm_c_g2', 'm_c_kk', 'm_c_ka', 'm_c_rk', 'm_c_gn_g', 'm_c_gn_b', 'm_c_v0', 'm_c_v1', 'm_c_v2', 'v_norms', 'v_final_norm', 'v_ffn_wg', 'v_ffn_wu', 'v_ffn_wd', 'v_ple_wp', 'v_ple_wg', 'v_e_w_in', 'v_e_w_out', 'v_a_vnorm', 'v_a_ws', 'v_a_bs', 'v_b_onorm', 'v_b_lb_logits', 'v_c_mix', 'v_c_wr', 'v_c_wk', 'v_c_wv', 'v_c_wo', 'v_c_w0', 'v_c_w1', 'v_c_w2', 'v_c_a0', 'v_c_a1', 'v_c_a2', 'v_c_g1', 'v_c_g2', 'v_c_kk', 'v_c_ka', 'v_c_rk', 'v_c_gn_g', 'v_c_gn_b', 'v_c_v0', 'v_c_v1', 'v_c_v2']
TWIN_OUTPUTS = ['loss', 'grad_x', 'grad_norms', 'grad_final_norm', 'grad_ffn_wg', 'grad_ffn_wu', 'grad_ffn_wd', 'grad_ple_wp', 'grad_ple_wg', 'grad_e_w_in', 'grad_e_w_out', 'grad_a_vnorm', 'grad_a_ws', 'grad_a_bs', 'grad_b_onorm', 'grad_b_lb_logits', 'grad_c_mix', 'grad_c_wr', 'grad_c_wk', 'grad_c_wv', 'grad_c_wo', 'grad_c_w0', 'grad_c_w1', 'grad_c_w2', 'grad_c_a0', 'grad_c_a1', 'grad_c_a2', 'grad_c_g1', 'grad_c_g2', 'grad_c_kk', 'grad_c_ka', 'grad_c_rk', 'grad_c_gn_g', 'grad_c_gn_b', 'grad_c_v0', 'grad_c_v1', 'grad_c_v2', 'delta_norms', 'delta_final_norm', 'delta_ffn_wg', 'delta_ffn_wu', 'delta_ffn_wd', 'delta_ple_wp', 'delta_ple_wg', 'delta_e_w_in', 'delta_e_w_out', 'delta_a_vnorm', 'delta_a_ws', 'delta_a_bs', 'delta_b_onorm', 'delta_b_lb_logits', 'delta_c_mix', 'delta_c_wr', 'delta_c_wk', 'delta_c_wv', 'delta_c_wo', 'delta_c_w0', 'delta_c_w1', 'delta_c_w2', 'delta_c_a0', 'delta_c_a1', 'delta_c_a2', 'delta_c_g1', 'delta_c_g2', 'delta_c_kk', 'delta_c_ka', 'delta_c_rk', 'delta_c_gn_g', 'delta_c_gn_b', 'delta_c_v0', 'delta_c_v1', 'delta_c_v2', 'new_m_norms', 'new_m_final_norm', 'new_m_ffn_wg', 'new_m_ffn_wu', 'new_m_ffn_wd', 'new_m_ple_wp', 'new_m_ple_wg', 'new_m_e_w_in', 'new_m_e_w_out', 'new_m_a_vnorm', 'new_m_a_ws', 'new_m_a_bs', 'new_m_b_onorm', 'new_m_b_lb_logits', 'new_m_c_mix', 'new_m_c_wr', 'new_m_c_wk', 'new_m_c_wv', 'new_m_c_wo', 'new_m_c_w0', 'new_m_c_w1', 'new_m_c_w2', 'new_m_c_a0', 'new_m_c_a1', 'new_m_c_a2', 'new_m_c_g1', 'new_m_c_g2', 'new_m_c_kk', 'new_m_c_ka', 'new_m_c_rk', 'new_m_c_gn_g', 'new_m_c_gn_b', 'new_m_c_v0', 'new_m_c_v1', 'new_m_c_v2', 'new_v_norms', 'new_v_final_norm', 'new_v_ffn_wg', 'new_v_ffn_wu', 'new_v_ffn_wd', 'new_v_ple_wp', 'new_v_ple_wg', 'new_v_e_w_in', 'new_v_e_w_out', 'new_v_a_vnorm', 'new_v_a_ws', 'new_v_a_bs', 'new_v_b_onorm', 'new_v_b_lb_logits', 'new_v_c_mix', 'new_v_c_wr', 'new_v_c_wk', 'new_v_c_wv', 'new_v_c_wo', 'new_v_c_w0', 'new_v_c_w1', 'new_v_c_w2', 'new_v_c_a0', 'new_v_c_a1', 'new_v_c_a2', 'new_v_c_g1', 'new_v_c_g2', 'new_v_c_kk', 'new_v_c_ka', 'new_v_c_rk', 'new_v_c_gn_g', 'new_v_c_gn_b', 'new_v_c_v0', 'new_v_c_v1', 'new_v_c_v2']
TWIN_LEAF_KINDS = {'loss': 'loss', 'grad_x': 'grad_x', 'grad_norms': 'grad_w', 'grad_final_norm': 'grad_w', 'grad_ffn_wg': 'grad_w', 'grad_ffn_wu': 'grad_w', 'grad_ffn_wd': 'grad_w', 'grad_ple_wp': 'grad_w', 'grad_ple_wg': 'grad_w', 'grad_e_w_in': 'grad_w', 'grad_e_w_out': 'grad_w', 'grad_a_vnorm': 'grad_w', 'grad_a_ws': 'grad_w', 'grad_a_bs': 'grad_w', 'grad_b_onorm': 'grad_w', 'grad_b_lb_logits': 'grad_w', 'grad_c_mix': 'grad_w', 'grad_c_wr': 'grad_w', 'grad_c_wk': 'grad_w', 'grad_c_wv': 'grad_w', 'grad_c_wo': 'grad_w', 'grad_c_w0': 'grad_w', 'grad_c_w1': 'grad_w', 'grad_c_w2': 'grad_w', 'grad_c_a0': 'grad_w', 'grad_c_a1': 'grad_w', 'grad_c_a2': 'grad_w', 'grad_c_g1': 'grad_w', 'grad_c_g2': 'grad_w', 'grad_c_kk': 'grad_w', 'grad_c_ka': 'grad_w', 'grad_c_rk': 'grad_w', 'grad_c_gn_g': 'grad_w', 'grad_c_gn_b': 'grad_w', 'grad_c_v0': 'grad_w', 'grad_c_v1': 'grad_w', 'grad_c_v2': 'grad_w', 'delta_norms': 'delta_w', 'delta_final_norm': 'delta_w', 'delta_ffn_wg': 'delta_w', 'delta_ffn_wu': 'delta_w', 'delta_ffn_wd': 'delta_w', 'delta_ple_wp': 'delta_w', 'delta_ple_wg': 'delta_w', 'delta_e_w_in': 'delta_w', 'delta_e_w_out': 'delta_w', 'delta_a_vnorm': 'delta_w', 'delta_a_ws': 'delta_w', 'delta_a_bs': 'delta_w', 'delta_b_onorm': 'delta_w', 'delta_b_lb_logits': 'delta_w', 'delta_c_mix': 'delta_w', 'delta_c_wr': 'delta_w', 'delta_c_wk': 'delta_w', 'delta_c_wv': 'delta_w', 'delta_c_wo': 'delta_w', 'delta_c_w0': 'delta_w', 'delta_c_w1': 'delta_w', 'delta_c_w2': 'delta_w', 'delta_c_a0': 'delta_w', 'delta_c_a1': 'delta_w', 'delta_c_a2': 'delta_w', 'delta_c_g1': 'delta_w', 'delta_c_g2': 'delta_w', 'delta_c_kk': 'delta_w', 'delta_c_ka': 'delta_w', 'delta_c_rk': 'delta_w', 'delta_c_gn_g': 'delta_w', 'delta_c_gn_b': 'delta_w', 'delta_c_v0': 'delta_w', 'delta_c_v1': 'delta_w', 'delta_c_v2': 'delta_w', 'new_m_norms': 'new_m', 'new_m_final_norm': 'new_m', 'new_m_ffn_wg': 'new_m', 'new_m_ffn_wu': 'new_m', 'new_m_ffn_wd': 'new_m', 'new_m_ple_wp': 'new_m', 'new_m_ple_wg': 'new_m', 'new_m_e_w_in': 'new_m', 'new_m_e_w_out': 'new_m', 'new_m_a_vnorm': 'new_m', 'new_m_a_ws': 'new_m', 'new_m_a_bs': 'new_m', 'new_m_b_onorm': 'new_m', 'new_m_b_lb_logits': 'new_m', 'new_m_c_mix': 'new_m', 'new_m_c_wr': 'new_m', 'new_m_c_wk': 'new_m', 'new_m_c_wv': 'new_m', 'new_m_c_wo': 'new_m', 'new_m_c_w0': 'new_m', 'new_m_c_w1': 'new_m', 'new_m_c_w2': 'new_m', 'new_m_c_a0': 'new_m', 'new_m_c_a1': 'new_m', 'new_m_c_a2': 'new_m', 'new_m_c_g1': 'new_m', 'new_m_c_g2': 'new_m', 'new_m_c_kk': 'new_m', 'new_m_c_ka': 'new_m', 'new_m_c_rk': 'new_m', 'new_m_c_gn_g': 'new_m', 'new_m_c_gn_b': 'new_m', 'new_m_c_v0': 'new_m', 'new_m_c_v1': 'new_m', 'new_m_c_v2': 'new_m', 'new_v_norms': 'new_v', 'new_v_final_norm': 'new_v', 'new_v_ffn_wg': 'new_v', 'new_v_ffn_wu': 'new_v', 'new_v_ffn_wd': 'new_v', 'new_v_ple_wp': 'new_v', 'new_v_ple_wg': 'new_v', 'new_v_e_w_in': 'new_v', 'new_v_e_w_out': 'new_v', 'new_v_a_vnorm': 'new_v', 'new_v_a_ws': 'new_v', 'new_v_a_bs': 'new_v', 'new_v_b_onorm': 'new_v', 'new_v_b_lb_logits': 'new_v', 'new_v_c_mix': 'new_v', 'new_v_c_wr': 'new_v', 'new_v_c_wk': 'new_v', 'new_v_c_wv': 'new_v', 'new_v_c_wo': 'new_v', 'new_v_c_w0': 'new_v', 'new_v_c_w1': 'new_v', 'new_v_c_w2': 'new_v', 'new_v_c_a0': 'new_v', 'new_v_c_a1': 'new_v', 'new_v_c_a2': 'new_v', 'new_v_c_g1': 'new_v', 'new_v_c_g2': 'new_v', 'new_v_c_kk': 'new_v', 'new_v_c_ka': 'new_v', 'new_v_c_rk': 'new_v', 'new_v_c_gn_g': 'new_v', 'new_v_c_gn_b': 'new_v', 'new_v_c_v0': 'new_v', 'new_v_c_v1': 'new_v', 'new_v_c_v2': 'new_v'}


def _forward(args):
    return _fwd_reference(*[args[k] for k in FWD_PARAMS])


def _output_shape():
    out = _jax.eval_shape(lambda: _forward(_fwd_setup_inputs(0)))
    return out.shape, out.dtype

N_MICROBATCH = 1
ADAM_LR = 0.001
ADAM_B1 = 0.9
ADAM_B2 = 0.999
ADAM_EPS = 1e-08
ADAM_WD = 0.01
ADAM_STEP = 10
PER_EXAMPLE_BATCH_AXIS = {'x': 0, 'p': 1, 'loss_target': 0}
SHARED_INPUTS = []
_WEIGHT_DTYPES = {'norms': _jnp.float32, 'final_norm': _jnp.float32, 'ffn_wg': _jnp.float32, 'ffn_wu': _jnp.float32, 'ffn_wd': _jnp.float32, 'ple_wp': _jnp.float32, 'ple_wg': _jnp.float32, 'e_w_in': _jnp.float32, 'e_w_out': _jnp.float32, 'a_vnorm': _jnp.float32, 'a_ws': _jnp.float32, 'a_bs': _jnp.float32, 'b_onorm': _jnp.float32, 'b_lb_logits': _jnp.float32, 'c_mix': _jnp.float32, 'c_wr': _jnp.float32, 'c_wk': _jnp.float32, 'c_wv': _jnp.float32, 'c_wo': _jnp.float32, 'c_w0': _jnp.float32, 'c_w1': _jnp.float32, 'c_w2': _jnp.float32, 'c_a0': _jnp.float32, 'c_a1': _jnp.float32, 'c_a2': _jnp.float32, 'c_g1': _jnp.float32, 'c_g2': _jnp.float32, 'c_kk': _jnp.float32, 'c_ka': _jnp.float32, 'c_rk': _jnp.float32, 'c_gn_g': _jnp.float32, 'c_gn_b': _jnp.float32, 'c_v0': _jnp.float32, 'c_v1': _jnp.float32, 'c_v2': _jnp.float32}
MOMENT_SCALE = {'norms': 1.575908e-02, 'final_norm': 8.001273e+00, 'ffn_wg': 5.360120e-03, 'ffn_wu': 5.190091e-03, 'ffn_wd': 1.722304e-02, 'ple_wp': 2.958738e-02, 'ple_wg': 5.812968e-03, 'e_w_in': 1.397723e-02, 'e_w_out': 4.000099e-02, 'a_vnorm': 6.786709e-03, 'a_ws': 1.361278e-02, 'a_bs': 1.962522e-02, 'b_onorm': 1.769033e-02, 'b_lb_logits': 7.562825e-04, 'c_mix': 1.801323e-02, 'c_wr': 1.547179e-02, 'c_wk': 1.635462e-02, 'c_wv': 1.419284e-02, 'c_wo': 2.831366e-02, 'c_w0': 5.223673e-03, 'c_w1': 1.777786e-03, 'c_w2': 7.407161e-04, 'c_a0': 5.923006e-03, 'c_a1': 1.268524e-02, 'c_a2': 5.476248e-03, 'c_g1': 1.367165e-02, 'c_g2': 1.462668e-02, 'c_kk': 1.594955e-02, 'c_ka': 1.623330e-02, 'c_rk': 2.958039e-02, 'c_gn_g': 1.394390e-02, 'c_gn_b': 1.890294e-02, 'c_v0': 4.412790e-03, 'c_v1': 1.044317e-02, 'c_v2': 1.835581e-03}


def _to_microbatches(a, axis):
    t = _jnp.moveaxis(a, axis, 0)
    t = t.reshape((N_MICROBATCH, t.shape[0] // N_MICROBATCH) + t.shape[1:])
    return _jnp.moveaxis(t, 1, axis + 1)


def setup_inputs(seed: int = 0) -> dict:
    inp = _fwd_setup_inputs(seed)
    key = _jax.random.fold_in(_jax.random.key(seed), 7919)
    shape, _ = _output_shape()
    out = dict(inp)
    out["loss_target"] = _jax.random.normal(_jax.random.fold_in(key, 0), shape, _jnp.float32)
    for i, name in enumerate(TWIN_WEIGHTS):
        w = inp[name].astype(_jnp.float32)
        if MOMENT_SCALE is None:
            s = _jnp.sqrt(_jnp.mean(_jnp.square(w)) + 1e-30)
        else:
            s = MOMENT_SCALE[name]
        km, kv = _jax.random.split(_jax.random.fold_in(key, i + 1))
        out[name] = w
        out["m_" + name] = s * _jax.random.normal(km, w.shape, _jnp.float32)
        out["v_" + name] = (s * s) * _jax.random.uniform(kv, w.shape, _jnp.float32, 0.5, 1.5)
    if N_MICROBATCH > 1:
        for name, axis in PER_EXAMPLE_BATCH_AXIS.items():
            out[name] = _to_microbatches(out[name], axis)
    return {'x': out['x'], 'p': out['p'], 'norms': out['norms'], 'final_norm': out['final_norm'], 'ffn_wg': out['ffn_wg'], 'ffn_wu': out['ffn_wu'], 'ffn_wd': out['ffn_wd'], 'ple_wp': out['ple_wp'], 'ple_wg': out['ple_wg'], 'e_w_in': out['e_w_in'], 'e_w_out': out['e_w_out'], 'a_vnorm': out['a_vnorm'], 'a_ws': out['a_ws'], 'a_bs': out['a_bs'], 'b_onorm': out['b_onorm'], 'b_lb_logits': out['b_lb_logits'], 'c_mix': out['c_mix'], 'c_wr': out['c_wr'], 'c_wk': out['c_wk'], 'c_wv': out['c_wv'], 'c_wo': out['c_wo'], 'c_w0': out['c_w0'], 'c_w1': out['c_w1'], 'c_w2': out['c_w2'], 'c_a0': out['c_a0'], 'c_a1': out['c_a1'], 'c_a2': out['c_a2'], 'c_g1': out['c_g1'], 'c_g2': out['c_g2'], 'c_kk': out['c_kk'], 'c_ka': out['c_ka'], 'c_rk': out['c_rk'], 'c_gn_g': out['c_gn_g'], 'c_gn_b': out['c_gn_b'], 'c_v0': out['c_v0'], 'c_v1': out['c_v1'], 'c_v2': out['c_v2'], 'loss_target': out['loss_target'], 'm_norms': out['m_norms'], 'm_final_norm': out['m_final_norm'], 'm_ffn_wg': out['m_ffn_wg'], 'm_ffn_wu': out['m_ffn_wu'], 'm_ffn_wd': out['m_ffn_wd'], 'm_ple_wp': out['m_ple_wp'], 'm_ple_wg': out['m_ple_wg'], 'm_e_w_in': out['m_e_w_in'], 'm_e_w_out': out['m_e_w_out'], 'm_a_vnorm': out['m_a_vnorm'], 'm_a_ws': out['m_a_ws'], 'm_a_bs': out['m_a_bs'], 'm_b_onorm': out['m_b_onorm'], 'm_b_lb_logits': out['m_b_lb_logits'], 'm_c_mix': out['m_c_mix'], 'm_c_wr': out['m_c_wr'], 'm_c_wk': out['m_c_wk'], 'm_c_wv': out['m_c_wv'], 'm_c_wo': out['m_c_wo'], 'm_c_w0': out['m_c_w0'], 'm_c_w1': out['m_c_w1'], 'm_c_w2': out['m_c_w2'], 'm_c_a0': out['m_c_a0'], 'm_c_a1': out['m_c_a1'], 'm_c_a2': out['m_c_a2'], 'm_c_g1': out['m_c_g1'], 'm_c_g2': out['m_c_g2'], 'm_c_kk': out['m_c_kk'], 'm_c_ka': out['m_c_ka'], 'm_c_rk': out['m_c_rk'], 'm_c_gn_g': out['m_c_gn_g'], 'm_c_gn_b': out['m_c_gn_b'], 'm_c_v0': out['m_c_v0'], 'm_c_v1': out['m_c_v1'], 'm_c_v2': out['m_c_v2'], 'v_norms': out['v_norms'], 'v_final_norm': out['v_final_norm'], 'v_ffn_wg': out['v_ffn_wg'], 'v_ffn_wu': out['v_ffn_wu'], 'v_ffn_wd': out['v_ffn_wd'], 'v_ple_wp': out['v_ple_wp'], 'v_ple_wg': out['v_ple_wg'], 'v_e_w_in': out['v_e_w_in'], 'v_e_w_out': out['v_e_w_out'], 'v_a_vnorm': out['v_a_vnorm'], 'v_a_ws': out['v_a_ws'], 'v_a_bs': out['v_a_bs'], 'v_b_onorm': out['v_b_onorm'], 'v_b_lb_logits': out['v_b_lb_logits'], 'v_c_mix': out['v_c_mix'], 'v_c_wr': out['v_c_wr'], 'v_c_wk': out['v_c_wk'], 'v_c_wv': out['v_c_wv'], 'v_c_wo': out['v_c_wo'], 'v_c_w0': out['v_c_w0'], 'v_c_w1': out['v_c_w1'], 'v_c_w2': out['v_c_w2'], 'v_c_a0': out['v_c_a0'], 'v_c_a1': out['v_c_a1'], 'v_c_a2': out['v_c_a2'], 'v_c_g1': out['v_c_g1'], 'v_c_g2': out['v_c_g2'], 'v_c_kk': out['v_c_kk'], 'v_c_ka': out['v_c_ka'], 'v_c_rk': out['v_c_rk'], 'v_c_gn_g': out['v_c_gn_g'], 'v_c_gn_b': out['v_c_gn_b'], 'v_c_v0': out['v_c_v0'], 'v_c_v1': out['v_c_v1'], 'v_c_v2': out['v_c_v2']}


def _loss(weights, diff, rest, loss_target):
    with _jax.named_scope("forward"):
        args = {**rest, TWIN_DIFF_INPUT: diff, **{k: w.astype(_WEIGHT_DTYPES[k]) for k, w in weights.items()}}
        y = _forward(args)
    with _jax.named_scope("loss_head"):
        err = _jnp.square(y.astype(_jnp.float32) - loss_target)
        return 0.5 * _jnp.sum(_jnp.mean(err, axis=-1)) if err.ndim else 0.5 * err


def _adamw(w, g, m, v):
    m = ADAM_B1 * m + (1.0 - ADAM_B1) * g
    v = ADAM_B2 * v + (1.0 - ADAM_B2) * _jnp.square(g)
    m_hat = m / (1.0 - ADAM_B1 ** ADAM_STEP)
    v_hat = v / (1.0 - ADAM_B2 ** ADAM_STEP)
    delta = -ADAM_LR * (m_hat / (_jnp.sqrt(v_hat) + ADAM_EPS) + ADAM_WD * w)
    return delta, m, v


def reference(x, p, norms, final_norm, ffn_wg, ffn_wu, ffn_wd, ple_wp, ple_wg, e_w_in, e_w_out, a_vnorm, a_ws, a_bs, b_onorm, b_lb_logits, c_mix, c_wr, c_wk, c_wv, c_wo, c_w0, c_w1, c_w2, c_a0, c_a1, c_a2, c_g1, c_g2, c_kk, c_ka, c_rk, c_gn_g, c_gn_b, c_v0, c_v1, c_v2, loss_target, m_norms, m_final_norm, m_ffn_wg, m_ffn_wu, m_ffn_wd, m_ple_wp, m_ple_wg, m_e_w_in, m_e_w_out, m_a_vnorm, m_a_ws, m_a_bs, m_b_onorm, m_b_lb_logits, m_c_mix, m_c_wr, m_c_wk, m_c_wv, m_c_wo, m_c_w0, m_c_w1, m_c_w2, m_c_a0, m_c_a1, m_c_a2, m_c_g1, m_c_g2, m_c_kk, m_c_ka, m_c_rk, m_c_gn_g, m_c_gn_b, m_c_v0, m_c_v1, m_c_v2, v_norms, v_final_norm, v_ffn_wg, v_ffn_wu, v_ffn_wd, v_ple_wp, v_ple_wg, v_e_w_in, v_e_w_out, v_a_vnorm, v_a_ws, v_a_bs, v_b_onorm, v_b_lb_logits, v_c_mix, v_c_wr, v_c_wk, v_c_wv, v_c_wo, v_c_w0, v_c_w1, v_c_w2, v_c_a0, v_c_a1, v_c_a2, v_c_g1, v_c_g2, v_c_kk, v_c_ka, v_c_rk, v_c_gn_g, v_c_gn_b, v_c_v0, v_c_v1, v_c_v2):
    given = dict(x=x, p=p, norms=norms, final_norm=final_norm, ffn_wg=ffn_wg, ffn_wu=ffn_wu, ffn_wd=ffn_wd, ple_wp=ple_wp, ple_wg=ple_wg, e_w_in=e_w_in, e_w_out=e_w_out, a_vnorm=a_vnorm, a_ws=a_ws, a_bs=a_bs, b_onorm=b_onorm, b_lb_logits=b_lb_logits, c_mix=c_mix, c_wr=c_wr, c_wk=c_wk, c_wv=c_wv, c_wo=c_wo, c_w0=c_w0, c_w1=c_w1, c_w2=c_w2, c_a0=c_a0, c_a1=c_a1, c_a2=c_a2, c_g1=c_g1, c_g2=c_g2, c_kk=c_kk, c_ka=c_ka, c_rk=c_rk, c_gn_g=c_gn_g, c_gn_b=c_gn_b, c_v0=c_v0, c_v1=c_v1, c_v2=c_v2, loss_target=loss_target, m_norms=m_norms, m_final_norm=m_final_norm, m_ffn_wg=m_ffn_wg, m_ffn_wu=m_ffn_wu, m_ffn_wd=m_ffn_wd, m_ple_wp=m_ple_wp, m_ple_wg=m_ple_wg, m_e_w_in=m_e_w_in, m_e_w_out=m_e_w_out, m_a_vnorm=m_a_vnorm, m_a_ws=m_a_ws, m_a_bs=m_a_bs, m_b_onorm=m_b_onorm, m_b_lb_logits=m_b_lb_logits, m_c_mix=m_c_mix, m_c_wr=m_c_wr, m_c_wk=m_c_wk, m_c_wv=m_c_wv, m_c_wo=m_c_wo, m_c_w0=m_c_w0, m_c_w1=m_c_w1, m_c_w2=m_c_w2, m_c_a0=m_c_a0, m_c_a1=m_c_a1, m_c_a2=m_c_a2, m_c_g1=m_c_g1, m_c_g2=m_c_g2, m_c_kk=m_c_kk, m_c_ka=m_c_ka, m_c_rk=m_c_rk, m_c_gn_g=m_c_gn_g, m_c_gn_b=m_c_gn_b, m_c_v0=m_c_v0, m_c_v1=m_c_v1, m_c_v2=m_c_v2, v_norms=v_norms, v_final_norm=v_final_norm, v_ffn_wg=v_ffn_wg, v_ffn_wu=v_ffn_wu, v_ffn_wd=v_ffn_wd, v_ple_wp=v_ple_wp, v_ple_wg=v_ple_wg, v_e_w_in=v_e_w_in, v_e_w_out=v_e_w_out, v_a_vnorm=v_a_vnorm, v_a_ws=v_a_ws, v_a_bs=v_a_bs, v_b_onorm=v_b_onorm, v_b_lb_logits=v_b_lb_logits, v_c_mix=v_c_mix, v_c_wr=v_c_wr, v_c_wk=v_c_wk, v_c_wv=v_c_wv, v_c_wo=v_c_wo, v_c_w0=v_c_w0, v_c_w1=v_c_w1, v_c_w2=v_c_w2, v_c_a0=v_c_a0, v_c_a1=v_c_a1, v_c_a2=v_c_a2, v_c_g1=v_c_g1, v_c_g2=v_c_g2, v_c_kk=v_c_kk, v_c_ka=v_c_ka, v_c_rk=v_c_rk, v_c_gn_g=v_c_gn_g, v_c_gn_b=v_c_gn_b, v_c_v0=v_c_v0, v_c_v1=v_c_v1, v_c_v2=v_c_v2)
    weights = {n: given[n] for n in TWIN_WEIGHTS}
    shared = {n: given[n] for n in SHARED_INPUTS}
    per_example = {n: given[n] for n in ['x', 'p']}
    grad_fn = _jax.value_and_grad(_loss, argnums=(0, 1))

    def one_microbatch(ex, loss_target):
        ex = dict(ex)
        diff = ex.pop(TWIN_DIFF_INPUT)
        return grad_fn(weights, diff, {**shared, **ex}, loss_target)

    if N_MICROBATCH == 1:
        loss, (grad_w, grad_x) = one_microbatch(per_example, given["loss_target"])
    else:
        def body(carry, xs):
            loss_sum, grad_sum = carry
            l_k, (gw_k, gx_k) = one_microbatch(xs[0], xs[1])
            with _jax.named_scope("update"):
                return (loss_sum + l_k, _jax.tree.map(_jnp.add, grad_sum, gw_k)), gx_k

        init = (_jnp.zeros((), _jnp.float32), _jax.tree.map(_jnp.zeros_like, weights))
        (loss, grad_w), grad_x = _jax.lax.scan(body, init, (per_example, given["loss_target"]))
    with _jax.named_scope("update"):
        delta_w, new_m, new_v = {}, {}, {}
        for n in TWIN_WEIGHTS:
            delta_w[n], new_m[n], new_v[n] = _adamw(weights[n], grad_w[n], given["m_" + n], given["v_" + n])
    return (loss, grad_x, *[grad_w[n] for n in TWIN_WEIGHTS], *[delta_w[n] for n in TWIN_WEIGHTS],
            *[new_m[n] for n in TWIN_WEIGHTS], *[new_v[n] for n in TWIN_WEIGHTS])
```

```python
import functools

import jax
import jax.numpy as jnp
from jax import lax
from jax.experimental import pallas as pl
from jax.experimental.pallas import tpu as pltpu

F32 = jnp.float32
BF16 = jnp.bfloat16
MESH = pl.DeviceIdType.MESH

LANES = 128
VMEM_LIMIT = 56 * 1024 * 1024
MM_VMEM_BUDGET = 36 * 1024 * 1024
N_CHIPS = 4

RMS_EPS = 1e-6
A_GROUP = 128
A_CHUNK = 128
B_HEAD = 128
B_MIN_F = 1e-30
C_HEAD = 64
C_GN_EPS = 64e-5
ADAM_LR = 0.001
ADAM_B1 = 0.9
ADAM_B2 = 0.999
ADAM_EPS = 1e-08
ADAM_WD = 0.01
ADAM_STEP = 10


def _sigmoid(x):
    return 1.0 / (1.0 + jnp.exp(-x))


def _silu(x):
    return x * _sigmoid(x)


def _dsilu(x):
    s = _sigmoid(x)
    return s * (1.0 + x * (1.0 - s))


_GELU_C = 0.7978845608028654


def _gelu(x):
    return 0.5 * x * (1.0 + jnp.tanh(_GELU_C * (x + 0.044715 * x * x * x)))


def _dgelu(x):
    th = jnp.tanh(_GELU_C * (x + 0.044715 * x * x * x))
    return 0.5 * (1.0 + th) + 0.5 * x * (1.0 - th * th) * _GELU_C * (1.0 + 3.0 * 0.044715 * x * x)


def _softplus(x):
    return jnp.maximum(x, 0.0) + jnp.log(1.0 + jnp.exp(-jnp.abs(x)))


def _seg_ones(seg):
    i = lax.broadcasted_iota(jnp.int32, (LANES, LANES), 0) // seg
    j = lax.broadcasted_iota(jnp.int32, (LANES, LANES), 1) // seg
    return jnp.where(i == j, 1.0, 0.0).astype(BF16)


def _segsum(x, seg):
    ones = _seg_ones(seg)
    outs = []
    for j in range(x.shape[1] // LANES):
        xb = x[:, j * LANES:(j + 1) * LANES]
        hi = xb.astype(BF16)
        r1 = xb - hi.astype(F32)
        mid = r1.astype(BF16)
        lo = (r1 - mid.astype(F32)).astype(BF16)
        acc = jnp.dot(hi, ones, preferred_element_type=F32)
        acc = acc + jnp.dot(mid, ones, preferred_element_type=F32)
        acc = acc + jnp.dot(lo, ones, preferred_element_type=F32)
        outs.append(acc)
    return outs[0] if len(outs) == 1 else jnp.concatenate(outs, axis=1)


def _rowwise(fn, name, rows, tm, ins, outs, accs=()):
    n_in, n_out, n_acc = len(ins), len(outs), len(accs)
    arrays, in_specs = [], []
    for spec in ins:
        kind, arr = spec[0], spec[1]
        arrays.append(arr)
        if kind == 'row':
            in_specs.append(pl.BlockSpec((tm, arr.shape[1]), lambda i: (i, 0)))
        elif kind == 'col':
            in_specs.append(pl.BlockSpec((tm, spec[2]), functools.partial(lambda i, cb: (i, cb), cb=spec[3])))
        else:
            in_specs.append(pl.BlockSpec(arr.shape, functools.partial(lambda i, nd: (0,) * nd, nd=arr.ndim)))
    out_shape = [jax.ShapeDtypeStruct((rows, w), dt) for (w, dt) in outs]
    out_specs = [pl.BlockSpec((tm, w), lambda i: (i, 0)) for (w, _) in outs]
    out_shape += [jax.ShapeDtypeStruct(s, F32) for s in accs]
    out_specs += [pl.BlockSpec(s, functools.partial(lambda i, nd: (0,) * nd, nd=len(s))) for s in accs]

    def body(*refs):
        vals = fn(*[r[...] for r in refs[:n_in]])
        if not isinstance(vals, (tuple, list)):
            vals = (vals,)
        for r, v in zip(refs[n_in:n_in + n_out], vals[:n_out]):
            r[...] = v.astype(r.dtype)
        if n_acc:
            acc_refs = refs[n_in + n_out:]

            @pl.when(pl.program_id(0) == 0)
            def _():
                for r in acc_refs:
                    r[...] = jnp.zeros(r.shape, F32)

            for r, v in zip(acc_refs, vals[n_out:]):
                r[...] += v

    res = pl.pallas_call(
        body, name=name, grid=(rows // tm,), in_specs=in_specs, out_specs=out_specs, out_shape=out_shape,
        compiler_params=pltpu.CompilerParams(
            dimension_semantics=("arbitrary",) if n_acc else ("parallel",), vmem_limit_bytes=VMEM_LIMIT),
    )(*arrays)
    return res


class WV:
    def __init__(self, arr, idx, kind):
        self.arr, self.idx, self.kind = arr, tuple(idx), kind
        self.ns = arr.shape[0]
        self.R, self.C = arr.shape[-2:]
        self.K = self.R * (self.ns if kind == 'row' else 1)
        self.N = self.C * (self.ns if kind == 'col' else 1)

    def spec(self, br, bc, rmap, cmap):
        lead = (None,) * (1 + len(self.idx))
        nrb, ncb = self.R // br, self.C // bc
        idx, kind = self.idx, self.kind

        def index_map(*g):
            ri, ci = rmap(*g), cmap(*g)
            if kind == 'row':
                return (ri // nrb, *idx, ri % nrb, ci)
            return (ci // ncb, *idx, ri, ci % ncb)

        return pl.BlockSpec(lead + (br, bc), index_map)


def _tile_options(n):
    return [n] + [d for d in range(n - LANES, LANES - 1, -LANES) if n % d == 0]


def _pick_tiles(opt_m, opt_n, opt_k, out_bytes, has_res, full_k):
    best, best_key = None, None
    for tm in opt_m:
        for tn in opt_n:
            for tk in opt_k:
                multi = tk != full_k
                est = 2 * (tm * tk * 2 + tk * tn * 2 + tm * tn * out_bytes) + tm * tn * 4 * (2 if multi else 1)
                if has_res:
                    est += 2 * tm * tn * 4
                if est > MM_VMEM_BUDGET:
                    continue
                key = (tm * tn * tk, tk)
                if best is None or key > best_key:
                    best, best_key = (tm, tn, tk), key
    return best


def _mm_call(name, dims, grid, in_specs, out_spec, out_shape, operands, nk, acc_shape, scale, has_res):
    def body(*refs):
        a_ref, b_ref = refs[0], refs[1]
        res_ref = refs[2] if has_res else None
        o_ref = refs[3] if has_res else refs[2]

        def finalize(acc):
            acc = acc * scale if scale != 1.0 else acc
            if has_res:
                acc = acc + res_ref[...]
            o_ref[...] = acc.astype(o_ref.dtype)

        part = lax.dot_general(a_ref[...], b_ref[...], dims, preferred_element_type=F32)
        if nk == 1:
            finalize(part)
        else:
            acc_ref = refs[-1]
            k = pl.program_id(2)

            @pl.when(k == 0)
            def _():
                acc_ref[...] = part

            @pl.when(k > 0)
            def _():
                acc_ref[...] += part

            @pl.when(k == nk - 1)
            def _():
                finalize(acc_ref[...])

    return pl.pallas_call(
        body, name=name, grid=grid, in_specs=in_specs, out_specs=out_spec, out_shape=out_shape,
        scratch_shapes=[pltpu.VMEM(acc_shape, F32)] if nk > 1 else [],
        compiler_params=pltpu.CompilerParams(
            dimension_semantics=("parallel", "parallel", "arbitrary"), vmem_limit_bytes=VMEM_LIMIT),
    )(*operands)


def _mm_nn(a, w, name, out_dtype=F32, res=None, scale=1.0):
    M, K = a.shape
    N = w.N
    opt_n = _tile_options(w.C)
    opt_k = _tile_options(w.R)
    tm, tn, tk = _pick_tiles(_tile_options(M), opt_n, opt_k, jnp.dtype(out_dtype).itemsize, res is not None, K)
    nk = K // tk
    in_specs = [pl.BlockSpec((tm, tk), lambda n, m, k: (m, k)),
                w.spec(tk, tn, lambda n, m, k: k, lambda n, m, k: n)]
    operands = [a, w.arr]
    if res is not None:
        in_specs.append(pl.BlockSpec((tm, tn), lambda n, m, k: (m, n)))
        operands.append(res)
    return _mm_call(name, (((1,), (0,)), ((), ())), (N // tn, M // tm, nk), in_specs,
                    pl.BlockSpec((tm, tn), lambda n, m, k: (m, n)), jax.ShapeDtypeStruct((M, N), out_dtype),
                    operands, nk, (tm, tn), scale, res is not None)


def _mm_nt(a, w, name, out_dtype=F32, res=None, scale=1.0):
    M, C = a.shape
    Ko = w.K
    opt_n = _tile_options(w.R)
    opt_k = _tile_options(w.C)
    tm, tn, tk = _pick_tiles(_tile_options(M), opt_n, opt_k, jnp.dtype(out_dtype).itemsize, res is not None, C)
    nk = C // tk
    in_specs = [pl.BlockSpec((tm, tk), lambda n, m, k: (m, k)),
                w.spec(tn, tk, lambda n, m, k: n, lambda n, m, k: k)]
    operands = [a, w.arr]
    if res is not None:
        in_specs.append(pl.BlockSpec((tm, tn), lambda n, m, k: (m, n)))
        operands.append(res)
    return _mm_call(name, (((1,), (1,)), ((), ())), (Ko // tn, M // tm, nk), in_specs,
                    pl.BlockSpec((tm, tn), lambda n, m, k: (m, n)), jax.ShapeDtypeStruct((M, Ko), out_dtype),
                    operands, nk, (tm, tn), scale, res is not None)


def _mm_tn(a, dy, like, name, out_dtype=BF16, scale=1.0):
    M, K = a.shape
    N = dy.shape[1]
    out = WV(jax.ShapeDtypeStruct((like.ns, like.R, like.C), out_dtype), (), like.kind)
    opt_m = _tile_options(like.R)
    opt_n = _tile_options(like.C)
    tko, tno, tc = _pick_tiles(opt_m, opt_n, _tile_options(M), jnp.dtype(out_dtype).itemsize, False, M)
    nk = M // tc
    in_specs = [pl.BlockSpec((tc, tko), lambda i, j, c: (c, i)),
                pl.BlockSpec((tc, tno), lambda i, j, c: (c, j))]
    return _mm_call(name, (((0,), (0,)), ((), ())), (K // tko, N // tno, nk), in_specs,
                    out.spec(tko, tno, lambda i, j, c: i, lambda i, j, c: j), out.arr,
                    [a, dy], nk, (tko, tno), scale, False)


SCAN_TB = 16


def _allsum(x, n_heads):
    r = x.shape[0]
    while r > 1:
        r //= 2
        x = x + pltpu.roll(x, r, 0)
    sh = LANES // 2
    while sh >= n_heads:
        x = x + pltpu.roll(x, sh, 1)
        sh //= 2
    return x


def _rows_of(vals):
    n = len(vals)
    if vals[0].shape[0] >= n:
        idx = lax.broadcasted_iota(jnp.int32, (n, LANES), 0)
        out = vals[0][:n]
        for i in range(1, n):
            out = jnp.where(idx == i, vals[i][:n], out)
        return out
    return jnp.concatenate([v[0:1] for v in vals], axis=0)


def _scan_fwd(name, n_heads, wk, kk, rk, vrow, ak=None, bk=None):
    S, R, _ = wk.shape
    dv = vrow.shape[1]
    ab = ak is not None
    tb = min(SCAN_TB, S)
    grp = min(8, dv)

    def body(*refs):
        if ab:
            w_ref, k_ref, r_ref, v_ref, a_ref, b_ref, y_ref, hist_ref, sa_ref, s_ref = refs
        else:
            w_ref, k_ref, r_ref, v_ref, y_ref, hist_ref, s_ref = refs

        @pl.when(pl.program_id(0) == 0)
        def _():
            s_ref[...] = jnp.zeros(s_ref.shape, F32)

        def step(t, carry):
            w, k, r = w_ref[t], k_ref[t], r_ref[t]
            if ab:
                a, b = a_ref[t], b_ref[t]
            for g0 in range(0, dv, grp):
                ys, sas = [], []
                for i in range(grp):
                    v = g0 + i
                    st = s_ref[v]
                    hist_ref[t, v] = st
                    vr = v_ref[t, pl.ds(v, 1), :]
                    if ab:
                        sa = _allsum(st * a, n_heads)
                        sas.append(sa)
                        st = st * w + vr * k + sa * b
                    else:
                        st = st * w + vr * k
                    s_ref[v] = st
                    ys.append(_allsum(st * r, n_heads))
                y_ref[t, pl.ds(g0, grp), :] = _rows_of(ys)
                if ab:
                    sa_ref[t, pl.ds(g0, grp), :] = _rows_of(sas)
            return carry

        lax.fori_loop(0, tb, step, 0)

    kspec = pl.BlockSpec((tb, R, LANES), lambda i: (i, 0, 0))
    rspec = pl.BlockSpec((tb, dv, LANES), lambda i: (i, 0, 0))
    operands = [wk, kk, rk, vrow] + ([ak, bk] if ab else [])
    in_specs = [kspec, kspec, kspec, rspec] + ([kspec, kspec] if ab else [])
    out_shape = [jax.ShapeDtypeStruct((S, dv, LANES), F32), jax.ShapeDtypeStruct((S, dv, R, LANES), F32)]
    out_specs = [rspec, pl.BlockSpec((tb, dv, R, LANES), lambda i: (i, 0, 0, 0))]
    if ab:
        out_shape.append(jax.ShapeDtypeStruct((S, dv, LANES), F32))
        out_specs.append(rspec)
    res = pl.pallas_call(
        body, name=name, grid=(S // tb,), in_specs=in_specs, out_specs=out_specs, out_shape=out_shape,
        scratch_shapes=[pltpu.VMEM((dv, R, LANES), F32)],
        compiler_params=pltpu.CompilerParams(dimension_semantics=("arbitrary",), vmem_limit_bytes=VMEM_LIMIT),
    )(*operands)
    return (res[0], res[1], res[2]) if ab else (res[0], res[1], None)


def _scan_bwd(name, n_heads, wk, kk, rk, vrow, hist, dyrow, ak=None, bk=None, sarow=None):
    S, R, _ = wk.shape
    dv = vrow.shape[1]
    ab = ak is not None
    tb = min(SCAN_TB, S)
    nb = S // tb
    grp = min(8, dv)

    def body(*refs):
        if ab:
            (w_ref, k_ref, r_ref, v_ref, hist_ref, dy_ref, a_ref, b_ref, sa_ref,
             dw_ref, dk_ref, dr_ref, dv_ref, da_ref, db_ref, ds_ref) = refs
        else:
            (w_ref, k_ref, r_ref, v_ref, hist_ref, dy_ref,
             dw_ref, dk_ref, dr_ref, dv_ref, ds_ref) = refs

        @pl.when(pl.program_id(0) == 0)
        def _():
            ds_ref[...] = jnp.zeros(ds_ref.shape, F32)

        def step(j, carry):
            t = tb - 1 - j
            w, k, r = w_ref[t], k_ref[t], r_ref[t]
            if ab:
                a, b = a_ref[t], b_ref[t]
            zero = jnp.zeros((R, LANES), F32)
            u, dw, dk, da, db = zero, zero, zero, zero, zero
            vd = jnp.sum(v_ref[t] * dy_ref[t], axis=0, keepdims=True)
            sd = jnp.sum(sa_ref[t] * dy_ref[t], axis=0, keepdims=True) if ab else None
            for g0 in range(0, dv, grp):
                dvs = []
                for i in range(grp):
                    v = g0 + i
                    sp = hist_ref[t, v]
                    dyr = dy_ref[t, pl.ds(v, 1), :]
                    vr = v_ref[t, pl.ds(v, 1), :]
                    dst = ds_ref[v] + dyr * r
                    u = u + sp * dyr
                    dw = dw + dst * sp
                    dk = dk + dst * vr
                    dvs.append(_allsum(dst * k, n_heads))
                    if ab:
                        db = db + dst * sa_ref[t, pl.ds(v, 1), :]
                        dsa = _allsum(dst * b, n_heads)
                        da = da + sp * dsa
                        dst = dst * w + dsa * a
                    else:
                        dst = dst * w
                    ds_ref[v] = dst
                dv_ref[t, pl.ds(g0, grp), :] = _rows_of(dvs)
            dr = w * u + k * vd
            if ab:
                dr = dr + b * sd
                da_ref[t] = da
                db_ref[t] = db
            dw_ref[t] = dw
            dk_ref[t] = dk
            dr_ref[t] = dr
            return carry

        lax.fori_loop(0, tb, step, 0)

    kspec = pl.BlockSpec((tb, R, LANES), lambda i: (nb - 1 - i, 0, 0))
    rspec = pl.BlockSpec((tb, dv, LANES), lambda i: (nb - 1 - i, 0, 0))
    hspec = pl.BlockSpec((tb, dv, R, LANES), lambda i: (nb - 1 - i, 0, 0, 0))
    operands = [wk, kk, rk, vrow, hist, dyrow] + ([ak, bk, sarow] if ab else [])
    in_specs = [kspec, kspec, kspec, rspec, hspec, rspec] + ([kspec, kspec, rspec] if ab else [])
    kshape = jax.ShapeDtypeStruct((S, R, LANES), F32)
    out_shape = [kshape, kshape, kshape, jax.ShapeDtypeStruct((S, dv, LANES), F32)] + ([kshape, kshape] if ab else [])
    out_specs = [kspec, kspec, kspec, rspec] + ([kspec, kspec] if ab else [])
    return pl.pallas_call(
        body, name=name, grid=(nb,), in_specs=in_specs, out_specs=out_specs, out_shape=out_shape,
        scratch_shapes=[pltpu.VMEM((dv, R, LANES), F32)],
        compiler_params=pltpu.CompilerParams(dimension_semantics=("arbitrary",), vmem_limit_bytes=VMEM_LIMIT),
    )(*operands)


def _to_k(x, n_heads, dk):
    S = x.shape[0]
    kl = LANES // n_heads
    return x.reshape(S, n_heads, dk // kl, kl).transpose(0, 2, 3, 1).reshape(S, dk // kl, LANES)


def _from_k(x, n_heads, dk):
    S = x.shape[0]
    kl = LANES // n_heads
    return x.reshape(S, dk // kl, kl, n_heads).transpose(0, 3, 1, 2).reshape(S, n_heads * dk)


def _to_row(x, n_heads, dv):
    S = x.shape[0]
    return jnp.tile(x.reshape(S, n_heads, dv).transpose(0, 2, 1), (1, 1, LANES // n_heads))


def _from_row(x, n_heads, dv):
    S = x.shape[0]
    return x[:, :, :n_heads].transpose(0, 2, 1).reshape(S, n_heads * dv)


def _tril_mask():
    t = lax.broadcasted_iota(jnp.int32, (A_CHUNK, A_CHUNK), 0)
    s = lax.broadcasted_iota(jnp.int32, (A_CHUNK, A_CHUNK), 1)
    return s <= t


def _gmlp_fwd(name, proj, aw, vgain, ws, bs_t):
    S = proj.shape[0]
    G = aw // A_GROUP

    def body(u_ref, v_ref, gain_ref, ws_ref, bs_ref, o_ref):
        mask = _tril_mask()
        lane = lax.broadcasted_iota(jnp.int32, (A_CHUNK, G), 1)
        bs = bs_ref[...]
        for g in range(G):
            seg = slice(g * A_GROUP, (g + 1) * A_GROUP)
            ua = _gelu(u_ref[:, seg])
            va = _gelu(v_ref[:, seg])
            rs = lax.rsqrt(jnp.mean(va * va, axis=-1, keepdims=True) + RMS_EPS)
            vg = (va * rs) * gain_ref[:, seg]
            wm = jnp.where(mask, ws_ref[g], 0.0).astype(BF16)
            bcol = jnp.sum(jnp.where(lane == g, bs, 0.0), axis=1, keepdims=True)
            s = jnp.dot(wm, vg.astype(BF16), preferred_element_type=F32) + bcol
            o_ref[:, seg] = (ua * s).astype(o_ref.dtype)

    return pl.pallas_call(
        body, name=name, grid=(S // A_CHUNK,),
        in_specs=[pl.BlockSpec((A_CHUNK, aw), lambda i: (i, 0)), pl.BlockSpec((A_CHUNK, aw), lambda i: (i, 1)),
                  pl.BlockSpec((1, aw), lambda i: (0, 0)), pl.BlockSpec((G, A_CHUNK, A_CHUNK), lambda i: (0, 0, 0)),
                  pl.BlockSpec((A_CHUNK, G), lambda i: (0, 0))],
        out_specs=pl.BlockSpec((A_CHUNK, aw), lambda i: (i, 0)),
        out_shape=jax.ShapeDtypeStruct((S, aw), BF16),
        compiler_params=pltpu.CompilerParams(dimension_semantics=("parallel",), vmem_limit_bytes=VMEM_LIMIT),
    )(proj, proj, vgain, ws, bs_t)


def _gmlp_bwd(name, proj, dout, dout_cb, aw, vgain, ws, bs_t):
    S = proj.shape[0]
    G = aw // A_GROUP

    def body(u_ref, v_ref, do_ref, gain_ref, ws_ref, bs_ref, du_ref, dv_ref, dws_ref, dbs_ref, dgain_ref):
        @pl.when(pl.program_id(0) == 0)
        def _():
            dws_ref[...] = jnp.zeros(dws_ref.shape, F32)
            dbs_ref[...] = jnp.zeros(dbs_ref.shape, F32)
            dgain_ref[...] = jnp.zeros(dgain_ref.shape, F32)

        mask = _tril_mask()
        lane = lax.broadcasted_iota(jnp.int32, (A_CHUNK, G), 1)
        bs = bs_ref[...]
        dbs = jnp.zeros((A_CHUNK, G), F32)
        for g in range(G):
            seg = slice(g * A_GROUP, (g + 1) * A_GROUP)
            u, v = u_ref[:, seg], v_ref[:, seg]
            do = do_ref[:, seg].astype(F32)
            ua, va = _gelu(u), _gelu(v)
            rs = lax.rsqrt(jnp.mean(va * va, axis=-1, keepdims=True) + RMS_EPS)
            xh = va * rs
            gain = gain_ref[:, seg]
            vg = (xh * gain).astype(BF16)
            wm = jnp.where(mask, ws_ref[g], 0.0).astype(BF16)
            bcol = jnp.sum(jnp.where(lane == g, bs, 0.0), axis=1, keepdims=True)
            s = jnp.dot(wm, vg, preferred_element_type=F32) + bcol
            du_ref[:, seg] = (do * s * _dgelu(u)).astype(du_ref.dtype)
            ds = do * ua
            dsb = ds.astype(BF16)
            dw = lax.dot_general(dsb, vg, (((1,), (1,)), ((), ())), preferred_element_type=F32)
            dws_ref[g] += jnp.where(mask, dw, 0.0)
            dbs = dbs + jnp.where(lane == g, jnp.sum(ds, axis=1, keepdims=True), 0.0)
            dvg = lax.dot_general(wm, dsb, (((0,), (0,)), ((), ())), preferred_element_type=F32)
            dgain_ref[:, seg] += jnp.sum(dvg * xh, axis=0, keepdims=True)
            dxh = dvg * gain
            dva = rs * (dxh - xh * jnp.mean(dxh * xh, axis=-1, keepdims=True))
            dv_ref[:, seg] = (dva * _dgelu(v)).astype(dv_ref.dtype)
        dbs_ref[...] += dbs

    return pl.pallas_call(
        body, name=name, grid=(S // A_CHUNK,),
        in_specs=[pl.BlockSpec((A_CHUNK, aw), lambda i: (i, 0)), pl.BlockSpec((A_CHUNK, aw), lambda i: (i, 1)),
                  pl.BlockSpec((A_CHUNK, aw), functools.partial(lambda i, cb: (i, cb), cb=dout_cb)),
                  pl.BlockSpec((1, aw), lambda i: (0, 0)), pl.BlockSpec((G, A_CHUNK, A_CHUNK), lambda i: (0, 0, 0)),
                  pl.BlockSpec((A_CHUNK, G), lambda i: (0, 0))],
        out_specs=[pl.BlockSpec((A_CHUNK, aw), lambda i: (i, 0)), pl.BlockSpec((A_CHUNK, aw), lambda i: (i, 0)),
                   pl.BlockSpec((G, A_CHUNK, A_CHUNK), lambda i: (0, 0, 0)), pl.BlockSpec((A_CHUNK, G), lambda i: (0, 0)),
                   pl.BlockSpec((1, aw), lambda i: (0, 0))],
        out_shape=[jax.ShapeDtypeStruct((S, aw), BF16), jax.ShapeDtypeStruct((S, aw), BF16),
                   jax.ShapeDtypeStruct((G, A_CHUNK, A_CHUNK), F32), jax.ShapeDtypeStruct((A_CHUNK, G), F32),
                   jax.ShapeDtypeStruct((1, aw), F32)],
        compiler_params=pltpu.CompilerParams(dimension_semantics=("arbitrary",), vmem_limit_bytes=VMEM_LIMIT),
    )(proj, proj, dout, vgain, ws, bs_t)


ANY = pl.BlockSpec(memory_space=pl.ANY)


def _place():
    return lax.axis_index("x"), lax.axis_index("y"), lax.axis_index("c")


def _other_chips(x, y):
    return [(1 - x, y), (x, 1 - y), (1 - x, 1 - y)]


def _remote(src, dst, send_sem, recv_sem, device):
    return pltpu.make_async_remote_copy(src_ref=src, dst_ref=dst, send_sem=send_sem, recv_sem=recv_sem,
                                        device_id=device, device_id_type=MESH)


def _comm_call(body, name, operands, out_shape, n_dma, n_local):
    return pl.pallas_call(
        body, name=name, in_specs=[ANY] * len(operands), out_specs=[ANY] * len(out_shape), out_shape=out_shape,
        scratch_shapes=[pltpu.SemaphoreType.DMA((n_dma,)), pltpu.SemaphoreType.DMA((n_dma,)),
                        pltpu.SemaphoreType.DMA((max(n_local, 1),))],
        compiler_params=pltpu.CompilerParams(has_side_effects=True),
    )(*operands)


def _all_gather(name, bigs, smalls):
    nb, n = len(bigs), len(bigs) + len(smalls)
    arrays = list(bigs) + list(smalls)

    def body(*refs):
        ins, outs = refs[:n], refs[n:2 * n]
        send_sems, recv_sems, local_sems = refs[2 * n:]
        x, y, c = _place()
        me = 2 * x + y
        chips = _other_chips(x, y)
        sibling = (x, y, 1 - c)
        started, local = [], []
        for e in range(n):
            cp = pltpu.make_async_copy(ins[e], outs[e].at[me], local_sems.at[e])
            cp.start()
            local.append(cp)
            if e < nb:
                half = ins[e].shape[0] // 2
                src = ins[e].at[pl.ds(c * half, half)]
                dst = outs[e].at[me, pl.ds(c * half, half)]
            else:
                src, dst = ins[e], outs[e].at[me]
            for j, chip in enumerate(chips):
                cp = _remote(src, dst, send_sems.at[3 * e + j], recv_sems.at[3 * e + j], (*chip, c))
                cp.start()
                started.append(cp)
        fwd = 3 * n
        for e in range(nb):
            half = ins[e].shape[0] // 2
            for j, chip in enumerate(chips):
                landed = outs[e].at[2 * chip[0] + chip[1], pl.ds(c * half, half)]
                _remote(landed, landed, send_sems.at[3 * e + j], recv_sems.at[3 * e + j], sibling).wait_recv()
                cp = _remote(landed, landed, send_sems.at[fwd + 3 * e + j], recv_sems.at[fwd + 3 * e + j], sibling)
                cp.start()
                started.append(cp)
        for e in range(nb, n):
            for j, chip in enumerate(chips):
                landed = outs[e].at[2 * chip[0] + chip[1]]
                _remote(landed, landed, send_sems.at[3 * e + j], recv_sems.at[3 * e + j], sibling).wait_recv()
        for e in range(nb):
            half = ins[e].shape[0] // 2
            for j, chip in enumerate(chips):
                passed = outs[e].at[2 * chip[0] + chip[1], pl.ds((1 - c) * half, half)]
                _remote(passed, passed, send_sems.at[fwd + 3 * e + j], recv_sems.at[fwd + 3 * e + j],
                        sibling).wait_recv()
        for cp in started:
            cp.wait_send()
        for cp in local:
            cp.wait()

    out_shape = [jax.ShapeDtypeStruct((N_CHIPS,) + a.shape, a.dtype) for a in arrays]
    return _comm_call(body, name, arrays, out_shape, 3 * n + 3 * nb, n)


def _swap_halves(name, grads):
    n = len(grads)

    def body(*refs):
        ins, outs = refs[:n], refs[n:2 * n]
        send_sems, recv_sems, _ = refs[2 * n:]
        x, y, c = _place()
        cps = []
        for e in range(n):
            half = ins[e].shape[1] // 2
            src = ins[e].at[pl.ds(0, N_CHIPS), pl.ds((1 - c) * half, half)]
            cp = _remote(src, outs[e], send_sems.at[e], recv_sems.at[e], (x, y, 1 - c))
            cp.start()
            cps.append(cp)
        for cp in cps:
            cp.wait()

    out_shape = [jax.ShapeDtypeStruct((N_CHIPS, g.shape[1] // 2, g.shape[2]), g.dtype) for g in grads]
    return _comm_call(body, name, list(grads), out_shape, n, 0)


def _scatter_chips(name, parts):
    n = len(parts)

    def body(*refs):
        ins, outs = refs[:n], refs[n:2 * n]
        send_sems, recv_sems, local_sems = refs[2 * n:]
        x, y, c = _place()
        me = 2 * x + y
        cps = []
        for e in range(n):
            cp = pltpu.make_async_copy(ins[e].at[me], outs[e].at[3], local_sems.at[e])
            cp.start()
            cps.append(cp)
            for j, chip in enumerate(_other_chips(x, y)):
                cp = _remote(ins[e].at[2 * chip[0] + chip[1]], outs[e].at[j], send_sems.at[3 * e + j],
                             recv_sems.at[3 * e + j], (*chip, c))
                cp.start()
                cps.append(cp)
        for cp in cps:
            cp.wait()

    out_shape = [jax.ShapeDtypeStruct(p.shape, p.dtype) for p in parts]
    return _comm_call(body, name, list(parts), out_shape, 3 * n, n)


def _join_halves(name, halves, places, out_shapes):
    n, n_out = len(halves), len(out_shapes)

    def body(*refs):
        ins, outs = refs[:n], refs[n:n + n_out]
        send_sems, recv_sems, local_sems = refs[n + n_out:]
        x, y, c = _place()
        cps = []
        for e in range(n):
            o, idx = places[e]
            half = ins[e].shape[0]
            dst = outs[o].at[(*idx, pl.ds(c * half, half))]
            cp = pltpu.make_async_copy(ins[e], dst, local_sems.at[e])
            cp.start()
            cps.append(cp)
            cp = _remote(ins[e], dst, send_sems.at[e], recv_sems.at[e], (x, y, 1 - c))
            cp.start()
            cps.append(cp)
        for cp in cps:
            cp.wait()

    return _comm_call(body, name, list(halves), list(out_shapes), n, n)


def _spread_all(name, per_chip, everywhere):
    def body(pc_ref, ev_ref, pc_out, ev_out, send_sems, recv_sems, local_sems):
        x, y, c = _place()
        me = 4 * x + 2 * y + c
        cps = [pltpu.make_async_copy(pc_ref.at[2 * x + y], pc_out.at[me], local_sems.at[0]),
               pltpu.make_async_copy(ev_ref, ev_out.at[me], local_sems.at[1])]
        for f in range(1, 8):
            fx, fy, fc = f // 4, (f // 2) % 2, f % 2
            tx = 1 - x if fx else x
            ty = 1 - y if fy else y
            tc = 1 - c if fc else c
            cps.append(_remote(pc_ref.at[2 * tx + ty], pc_out.at[me], send_sems.at[2 * f], recv_sems.at[2 * f],
                               (tx, ty, tc)))
            cps.append(_remote(ev_ref, ev_out.at[me], send_sems.at[2 * f + 1], recv_sems.at[2 * f + 1],
                               (tx, ty, tc)))
        for cp in cps:
            cp.start()
        for cp in cps:
            cp.wait()

    out_shape = [jax.ShapeDtypeStruct((8,) + per_chip.shape[1:], F32), jax.ShapeDtypeStruct((8,) + everywhere.shape, F32)]
    return _comm_call(body, name, [per_chip, everywhere], out_shape, 16, 2)


def _row_tile(rows, width, itemsize, n_arrays):
    tm = 1
    while rows % (tm * 2) == 0 and (tm * 2) * width * itemsize * n_arrays * 2 <= 24 * 1024 * 1024 and tm * 2 <= 1024:
        tm *= 2
    return tm


def _add_own_half(name, grad, swapped, c_arr):
    ns, R, C = grad.shape
    half = R // 2
    th = _row_tile(half, C, 2, 3)
    g4 = grad.reshape(ns, 2, half, C)

    def body(c_ref, g_ref, s_ref, o_ref):
        o_ref[...] = (g_ref[...].astype(F32) + s_ref[...].astype(F32)).astype(o_ref.dtype)

    return pl.pallas_call(
        body, name=name,
        grid_spec=pltpu.PrefetchScalarGridSpec(
            num_scalar_prefetch=1, grid=(ns, half // th),
            in_specs=[pl.BlockSpec((None, None, th, C), lambda k, i, c_ref: (k, c_ref[0], i, 0)),
                      pl.BlockSpec((None, th, C), lambda k, i, c_ref: (k, i, 0))],
            out_specs=pl.BlockSpec((None, th, C), lambda k, i, c_ref: (k, i, 0))),
        out_shape=jax.ShapeDtypeStruct((ns, half, C), grad.dtype),
        compiler_params=pltpu.CompilerParams(dimension_semantics=("parallel", "parallel"), vmem_limit_bytes=VMEM_LIMIT),
    )(c_arr, g4, swapped)


def _sum_slots(name, slots, order, out_dtype=F32):
    n, rows, C = slots.shape
    th = _row_tile(rows, C, 4, n + 1)

    def body(s_ref, o_ref):
        acc = s_ref[order[0]].astype(F32)
        for k in order[1:]:
            acc = acc + s_ref[k].astype(F32)
        o_ref[...] = acc.astype(o_ref.dtype)

    return pl.pallas_call(
        body, name=name, grid=(rows // th,),
        in_specs=[pl.BlockSpec((n, th, C), lambda i: (0, i, 0))],
        out_specs=pl.BlockSpec((th, C), lambda i: (i, 0)),
        out_shape=jax.ShapeDtypeStruct((rows, C), out_dtype),
        compiler_params=pltpu.CompilerParams(dimension_semantics=("parallel",), vmem_limit_bytes=VMEM_LIMIT),
    )(slots)


def _adamw(name, g, w, m, v):
    rows, C = g.shape
    tm = _row_tile(rows, C, 4, 7)
    c1 = 1.0 - ADAM_B1 ** ADAM_STEP
    c2 = 1.0 - ADAM_B2 ** ADAM_STEP

    def fn(g, w, m, v):
        m = ADAM_B1 * m + (1.0 - ADAM_B1) * g
        v = ADAM_B2 * v + (1.0 - ADAM_B2) * (g * g)
        delta = -ADAM_LR * ((m / c1) / (jnp.sqrt(v / c2) + ADAM_EPS) + ADAM_WD * w)
        return delta, m, v

    return _rowwise(fn, name, rows, tm, [('row', g), ('row', w), ('row', m), ('row', v)],
                    [(C, F32), (C, F32), (C, F32)])


def _tm(S, width, n_arrays):
    return _row_tile(S, width, 4, n_arrays)


def _rms_fwd(name, h, gain, out_dtype):
    S, D = h.shape

    def fn(hb, g):
        r = lax.rsqrt(jnp.mean(hb * hb, axis=-1, keepdims=True) + RMS_EPS)
        return ((hb * r) * g,)

    return _rowwise(fn, name, S, _tm(S, D, 4), [('row', h), ('full', gain)], [(D, out_dtype)])[0]


def _rms_bwd(name, h_in, gain, dn, dh):
    S, D = h_in.shape

    def fn(hb, g, dnb, dhb):
        r = lax.rsqrt(jnp.mean(hb * hb, axis=-1, keepdims=True) + RMS_EPS)
        xh = hb * r
        dnb = dnb.astype(F32)
        dxh = dnb * g
        dx = r * (dxh - xh * jnp.mean(dxh * xh, axis=-1, keepdims=True))
        return dhb + dx, jnp.sum(dnb * xh, axis=0, keepdims=True)

    return _rowwise(fn, name, S, _tm(S, D, 8), [('row', h_in), ('full', gain), ('row', dn), ('row', dh)],
                    [(D, F32)], [(1, D)])


def _ffn_fwd(h, gain, wg, wu, wd):
    S, D = h.shape
    n = _rms_fwd("rms_fwd_bf16", h, gain, BF16)
    g = _mm_nn(n, wg, "ffn_up", BF16)
    u = _mm_nn(n, wu, "ffn_up", BF16)
    FF = g.shape[1]

    def fn(gb, ub):
        return (_silu(gb.astype(F32)) * ub.astype(F32),)

    act = _rowwise(fn, "ffn_act", S, _tm(S, FF, 6), [('row', g), ('row', u)], [(FF, BF16)])[0]
    return _mm_nn(act, wd, "ffn_down", F32, res=h, scale=0.5), (h, n, g, u)


def _ffn_bwd(dh, cache, gain, wg, wu, wd):
    h, n, g, u = cache
    S, FF = g.shape
    dhb = dh.astype(BF16)
    dact = _mm_nt(dhb, wd, "ffn_dact", BF16, scale=0.5)

    def fn(gb, ub, db):
        gb, ub, db = gb.astype(F32), ub.astype(F32), db.astype(F32)
        sg = _sigmoid(gb)
        sl = gb * sg
        return db * ub * (sg * (1.0 + gb * (1.0 - sg))), db * sl, sl * ub

    dg, du, act = _rowwise(fn, "ffn_act_bwd", S, _tm(S, FF, 10), [('row', g), ('row', u), ('row', dact)],
                           [(FF, BF16), (FF, BF16), (FF, BF16)])
    dwd = _mm_tn(act, dhb, wd, "ffn_dwd", BF16, scale=0.5)
    dwg = _mm_tn(n, dg, wg, "ffn_dwup", BF16)
    dwu = _mm_tn(n, du, wu, "ffn_dwup", BF16)
    dn = _mm_nt(dg, wg, "ffn_dn", F32)
    dn = _mm_nt(du, wu, "ffn_dn_acc", F32, res=dn)
    dh2, dgain = _rms_bwd("rms_bwd", h, gain, dn, dh)
    return dh2, dgain, dwg, dwu, dwd


def _ple_fwd(h, gain, pb, wgate, wproj):
    S, D = h.shape
    n = _rms_fwd("rms_fwd_bf16", h, gain, BF16)
    pre = _mm_nn(n, wgate, "ple_gate", F32)
    e = _mm_nn(pb, wproj, "ple_proj", F32)

    def fn(hb, pr, eb):
        return (hb + _sigmoid(pr) * eb,)

    h2 = _rowwise(fn, "ple_add", S, _tm(S, D, 6), [('row', h), ('row', pre), ('row', e)], [(D, F32)])[0]
    return h2, (h, n, pre, e)


def _ple_bwd(dh, cache, gain, pb, wgate, wproj):
    h, n, pre, e = cache
    S, D = h.shape

    def fn(db, pr, eb):
        gt = _sigmoid(pr)
        return db * eb * gt * (1.0 - gt), db * gt

    dpre, de = _rowwise(fn, "ple_bwd", S, _tm(S, D, 6), [('row', dh), ('row', pre), ('row', e)],
                        [(D, BF16), (D, BF16)])
    dwproj = _mm_tn(pb, de, wproj, "ple_dwproj", F32)
    dwgate = _mm_tn(n, dpre, wgate, "ple_dwgate", BF16)
    dn = _mm_nt(dpre, wgate, "ple_dn", F32)
    dh2, dgain = _rms_bwd("rms_bwd", h, gain, dn, dh)
    return dh2, dgain, dwgate, dwproj


def _even_fwd(h, gain, w_in, w_out, vgain, ws, bs_t, onorm, lb):
    S, D = h.shape
    aw = D // 2
    nh = aw // B_HEAD
    hn = _rms_fwd("rms_fwd_bf16", h, gain, BF16)
    proj = _mm_nn(hn, w_in, "even_in", F32)
    a_out = _gmlp_fwd("gmlp_fwd", proj, aw, vgain, ws, bs_t)

    def pre(bq, bf, lbv):
        f = lbv + (1.0 - lbv) * _sigmoid(bf)
        return _silu(bq), jnp.maximum(f, B_MIN_F), 1.0 - f

    q, w, k = _rowwise(pre, "hgrn_pre", S, _tm(S, aw, 8), [('col', proj, aw, 2), ('col', proj, aw, 3), ('full', lb)],
                       [(aw, F32), (aw, F32), (aw, F32)])
    wk, kk, qk = _to_k(w, nh, B_HEAD), _to_k(k, nh, B_HEAD), _to_k(q, nh, B_HEAD)
    vrow = _to_row(proj[:, 4 * aw:5 * aw], nh, B_HEAD)
    yrow, hist, _ = _scan_fwd("hgrn_scan_fwd", nh, wk, kk, qk, vrow)
    o = _from_row(yrow, nh, B_HEAD)

    def post(ob, bg, on):
        rs = lax.rsqrt(_segsum(ob * ob, B_HEAD) * (1.0 / B_HEAD) + RMS_EPS)
        return ((ob * rs * on) * _silu(bg),)

    b_out = _rowwise(post, "hgrn_post", S, _tm(S, aw, 8), [('row', o), ('col', proj, aw, 5), ('full', onorm)],
                     [(aw, BF16)])[0]
    cat = jnp.concatenate([a_out, b_out], axis=1)
    h2 = _mm_nn(cat, w_out, "even_out", F32, res=h)
    return h2, (h, hn, proj, wk, kk, qk, vrow, hist, o, cat)


def _even_bwd(dh, cache, gain, w_in, w_out, vgain, ws, bs_t, onorm, lb):
    h, hn, proj, wk, kk, qk, vrow, hist, o, cat = cache
    S, D = h.shape
    aw = D // 2
    nh = aw // B_HEAD
    dhb = dh.astype(BF16)
    dw_out = _mm_tn(cat, dhb, w_out, "even_dwout", BF16)
    dcat = _mm_nt(dhb, w_out, "even_dcat", F32)

    def post_bwd(ob, bg, on, db):
        rs = lax.rsqrt(_segsum(ob * ob, B_HEAD) * (1.0 / B_HEAD) + RMS_EPS)
        xh = ob * rs
        dy = db * _silu(bg)
        dbg = db * (xh * on) * _dsilu(bg)
        dxh = dy * on
        do = rs * (dxh - xh * (_segsum(dxh * xh, B_HEAD) * (1.0 / B_HEAD)))
        return do, dbg, jnp.sum(dy * xh, axis=0, keepdims=True)

    do, dbg, donorm = _rowwise(post_bwd, "hgrn_post_bwd", S, _tm(S, aw, 10),
                               [('row', o), ('col', proj, aw, 5), ('full', onorm), ('col', dcat, aw, 1)],
                               [(aw, F32), (aw, BF16)], [(1, aw)])
    dwk, dkk, dqk, dvrow = _scan_bwd("hgrn_scan_bwd", nh, wk, kk, qk, vrow, hist, _to_row(do, nh, B_HEAD))
    dq, dw, dk = _from_k(dqk, nh, B_HEAD), _from_k(dwk, nh, B_HEAD), _from_k(dkk, nh, B_HEAD)
    dbi = _from_row(dvrow, nh, B_HEAD).astype(BF16)

    def pre_bwd(bq, bf, lbv, dqb, dwb, dkb):
        sig = _sigmoid(bf)
        f = lbv + (1.0 - lbv) * sig
        df = jnp.where(f > B_MIN_F, dwb, 0.0) - dkb
        return dqb * _dsilu(bq), df * (1.0 - lbv) * sig * (1.0 - sig), jnp.sum(df * (1.0 - sig), axis=0, keepdims=True)

    dbq, dbf, dlb = _rowwise(pre_bwd, "hgrn_pre_bwd", S, _tm(S, aw, 12),
                             [('col', proj, aw, 2), ('col', proj, aw, 3), ('full', lb), ('row', dq), ('row', dw),
                              ('row', dk)], [(aw, BF16), (aw, BF16)], [(1, aw)])
    dau, dav, dws, dbs_t, dvgain = _gmlp_bwd("gmlp_bwd", proj, dcat, 0, aw, vgain, ws, bs_t)
    dproj = jnp.concatenate([dau, dav, dbq, dbf, dbi, dbg], axis=1)
    dw_in = _mm_tn(hn, dproj, w_in, "even_dwin", BF16)
    dn = _mm_nt(dproj, w_in, "even_dn", F32)
    dh2, dgain = _rms_bwd("rms_bwd", h, gain, dn, dh)
    return dh2, dict(gain=dgain, w_in=dw_in, w_out=dw_out, vgain=dvgain, ws=dws, bs_t=dbs_t, onorm=donorm, lb=dlb)


def _rwkv_prep(r, k, v0, wpl, apl, w0, a0, kkg, kag, svl=None, vf=None, v0p=None):
    wp = w0 + wpl
    w = -_softplus(-wp) - 0.5
    ew = jnp.exp(w)
    decay = jnp.exp(-ew)
    a = _sigmoid(a0 + apl)
    if svl is not None:
        sv = _sigmoid(v0p + svl)
        v = v0 + (vf - v0) * sv
    else:
        sv, v = None, v0
    kkp = k * kkg
    nrm = jnp.sqrt(_segsum(kkp * kkp, C_HEAD))
    inv = 1.0 / jnp.maximum(nrm, 1e-12)
    kk = kkp * inv
    k2 = k * (1.0 + (a - 1.0) * kag)
    return dict(wp=wp, ew=ew, decay=decay, a=a, sv=sv, v=v, kkp=kkp, nrm=nrm, inv=inv, kk=kk, k2=k2)


def _rwkv_post(y, r, k2, v, gn_g, gn_b, rk):
    mu = _segsum(y, C_HEAD) * (1.0 / C_HEAD)
    yc = y - mu
    rstd = lax.rsqrt(_segsum(yc * yc, C_HEAD) * (1.0 / C_HEAD) + C_GN_EPS)
    yh = yc * rstd
    s = _segsum(r * k2 * rk, C_HEAD)
    return yh, rstd, s, yh * gn_g + gn_b + s * v


def _rwkv_fwd(h, gain, P, vfirst):
    S, D = h.shape
    nh = D // C_HEAD
    vres = vfirst is not None
    hn = _rms_fwd("rms_fwd_f32", h, gain, F32)
    hs = jnp.concatenate([jnp.zeros((1, D), F32), hn[:-1]], axis=0)

    def mixf(x, xs, m0, m1, m2, m3, m4, m5):
        xx = xs - x
        return tuple(x + xx * m for m in (m0, m1, m2, m3, m4, m5))

    xr, xw, xk, xv, xa, xg = _rowwise(mixf, "rwkv_mix", S, _tm(S, D, 8),
                                      [('row', hn), ('row', hs)] + [('full', m) for m in P['mix']], [(D, BF16)] * 6)
    r = _mm_nn(xr, P['wr'], "rwkv_proj", F32)
    k = _mm_nn(xk, P['wk'], "rwkv_proj", F32)
    v0 = _mm_nn(xv, P['wv'], "rwkv_proj", F32)
    lw1 = _mm_nn(xw, P['w1'], "rwkv_lora_in", F32)
    la1 = _mm_nn(xa, P['a1'], "rwkv_lora_in", F32)
    lg1 = _mm_nn(xg, P['g1'], "rwkv_lora_in_g", F32)
    ins = [('row', lw1), ('row', la1), ('row', lg1)]
    outs = [(lw1.shape[1], BF16), (la1.shape[1], BF16), (lg1.shape[1], BF16)]
    if vres:
        lv1 = _mm_nn(xv, P['v1'], "rwkv_lora_in_v", F32)
        ins.append(('row', lv1))
        outs.append((lv1.shape[1], BF16))

    def lora_act(*xs):
        res = [jnp.tanh(xs[0]), xs[1], _sigmoid(xs[2])]
        return tuple(res + list(xs[3:]))

    acts = _rowwise(lora_act, "rwkv_lora_act", S, _tm(S, 1024, 4), ins, outs)
    tw, la1b, sg = acts[0], acts[1], acts[2]
    wpl = _mm_nn(tw, P['w2'], "rwkv_lora_out", F32)
    apl = _mm_nn(la1b, P['a2'], "rwkv_lora_out", F32)
    g = _mm_nn(sg, P['g2'], "rwkv_lora_out_g", F32)
    prep_ins = [('row', r), ('row', k), ('row', v0), ('row', wpl), ('row', apl),
                ('full', P['w0']), ('full', P['a0']), ('full', P['kk']), ('full', P['ka'])]
    svl = lv1b = None
    if vres:
        lv1b = acts[3]
        svl = _mm_nn(lv1b, P['v2'], "rwkv_lora_out_v", F32)
        prep_ins += [('row', svl), ('row', vfirst), ('full', P['v0'])]

    def prep(*xs):
        q = _rwkv_prep(*xs)
        return q['decay'], q['k2'], q['v'], -q['kk'], q['kk'] * q['a']

    decay, k2, v, av, bv = _rowwise(prep, "rwkv_prep", S, _tm(S, D, 24), prep_ins, [(D, F32)] * 5)
    tk = functools.partial(_to_k, n_heads=nh, dk=C_HEAD)
    wk_, kk_, rk_, ak_, bk_ = tk(decay), tk(k2), tk(r), tk(av), tk(bv)
    vrow = _to_row(v, nh, C_HEAD)
    yrow, hist, sarow = _scan_fwd("rwkv_scan_fwd", nh, wk_, kk_, rk_, vrow, ak_, bk_)
    y = _from_row(yrow, nh, C_HEAD)

    def post(yb, rb, k2b, vb, gb, gn_g, gn_b, rkf):
        return (_rwkv_post(yb, rb, k2b, vb, gn_g, gn_b, rkf)[3] * gb,)

    zg = _rowwise(post, "rwkv_post", S, _tm(S, D, 16),
                  [('row', y), ('row', r), ('row', k2), ('row', v), ('row', g),
                   ('full', P['gn_g']), ('full', P['gn_b']), ('full', P['rk'])], [(D, BF16)])[0]
    h2 = _mm_nn(zg, P['wo'], "rwkv_out", F32, res=h)
    cache = dict(h=h, hn=hn, hs=hs, x=(xr, xw, xk, xv, xa, xg), r=r, k=k, v0=v0, tw=tw, la1b=la1b, sg=sg, lv1b=lv1b,
                 wpl=wpl, apl=apl, svl=svl, g=g, k2=k2, v=v, scan=(wk_, kk_, rk_, vrow, ak_, bk_, hist, sarow), y=y,
                 zg=zg, vfirst=vfirst)
    return h2, cache, (v if not vres else vfirst)


def _rwkv_bwd(dh, cache, gain, P, dvfirst_in):
    c = cache
    h = c['h']
    S, D = h.shape
    nh = D // C_HEAD
    vres = c['vfirst'] is not None
    xr, xw, xk, xv, xa, xg = c['x']
    dhb = dh.astype(BF16)
    dwo = _mm_tn(c['zg'], dhb, P['wo'], "rwkv_dwo", BF16)
    dzg = _mm_nt(dhb, P['wo'], "rwkv_dzg", F32)

    def post_bwd(dzgb, yb, rb, k2b, vb, gb, gn_g, gn_b, rkf):
        yh, rstd, s, z = _rwkv_post(yb, rb, k2b, vb, gn_g, gn_b, rkf)
        dz = dzgb * gb
        dyh = dz * gn_g
        m1 = _segsum(dyh, C_HEAD) * (1.0 / C_HEAD)
        m2 = _segsum(dyh * yh, C_HEAD) * (1.0 / C_HEAD)
        dy = rstd * (dyh - m1 - yh * m2)
        ds = _segsum(dz * vb, C_HEAD)
        return (dy, dzgb * z, ds * k2b * rkf, ds * rb * rkf, dz * s,
                jnp.sum(dz * yh, axis=0, keepdims=True), jnp.sum(dz, axis=0, keepdims=True),
                jnp.sum(ds * rb * k2b, axis=0, keepdims=True))

    dy, dgb, dr_b, dk2_b, dv_b, dgn_g, dgn_b, drk = _rowwise(
        post_bwd, "rwkv_post_bwd", S, _tm(S, D, 28),
        [('row', dzg), ('row', c['y']), ('row', c['r']), ('row', c['k2']), ('row', c['v']), ('row', c['g']),
         ('full', P['gn_g']), ('full', P['gn_b']), ('full', P['rk'])],
        [(D, F32), (D, BF16), (D, F32), (D, F32), (D, F32)], [(1, D)] * 3)
    wk_, kk_, rk_, vrow, ak_, bk_, hist, sarow = c['scan']
    dwk, dkk, drk_s, dvrow, dak, dbk = _scan_bwd("rwkv_scan_bwd", nh, wk_, kk_, rk_, vrow, hist,
                                                 _to_row(dy, nh, C_HEAD), ak_, bk_, sarow)
    fk = functools.partial(_from_k, n_heads=nh, dk=C_HEAD)
    ddecay, dk2_s, dr_s, dA, dB = fk(dwk), fk(dkk), fk(drk_s), fk(dak), fk(dbk)
    dv_s = _from_row(dvrow, nh, C_HEAD)
    dr_t, dk2_t, dv_t = dr_s + dr_b, dk2_s + dk2_b, dv_s + dv_b
    if dvfirst_in is not None:
        dv_t = dv_t + dvfirst_in
    ins = [('row', c['r']), ('row', c['k']), ('row', c['v0']), ('row', c['wpl']), ('row', c['apl']),
           ('full', P['w0']), ('full', P['a0']), ('full', P['kk']), ('full', P['ka'])]
    if vres:
        ins += [('row', c['svl']), ('row', c['vfirst']), ('full', P['v0'])]
    n_fwd = len(ins)
    ins += [('row', t) for t in (dr_t, ddecay, dk2_t, dv_t, dA, dB)]

    def prep_bwd(*xs):
        q = _rwkv_prep(*xs[:n_fwd])
        kb, kkg, kag = xs[1], xs[7], xs[8]
        dr, ddec, dk2, dv, dav, dbv = xs[n_fwd:]
        a, kk, kkp, inv = q['a'], q['kk'], q['kkp'], q['inv']
        dkk = dbv * a - dav
        da = dbv * kk + dk2 * kb * kag
        dk = dk2 * (1.0 + (a - 1.0) * kag)
        pr = _segsum(dkk * kkp, C_HEAD)
        dkkp = dkk * inv - jnp.where(q['nrm'] > 1e-12, kkp * pr * inv * inv * inv, 0.0)
        dk = dk + dkkp * kkg
        dap = da * a * (1.0 - a)
        dwp = ddec * q['decay'] * (-q['ew']) * _sigmoid(-q['wp'])
        outs = [dr, dk]
        accs = [jnp.sum(dwp, axis=0, keepdims=True), jnp.sum(dap, axis=0, keepdims=True),
                jnp.sum(dkkp * kb, axis=0, keepdims=True), jnp.sum(dk2 * kb * (a - 1.0), axis=0, keepdims=True)]
        if vres:
            v0b, vfb, sv = xs[2], xs[10], q['sv']
            dsvp = dv * (vfb - v0b) * sv * (1.0 - sv)
            outs += [dv * (1.0 - sv), dwp, dap, dsvp, dv * sv]
            accs.append(jnp.sum(dsvp, axis=0, keepdims=True))
        else:
            outs += [dv, dwp, dap]
        return tuple(outs + accs)

    outs = [(D, BF16)] * 5 + ([(D, BF16), (D, F32)] if vres else [])
    res = _rowwise(prep_bwd, "rwkv_prep_bwd", S, _tm(S, D, 40), ins, outs, [(1, D)] * (5 if vres else 4))
    drb, dkb, dv0b, dwpb, dapb = res[:5]
    dvfirst_out = res[6] if vres else None
    accs = res[len(outs):]
    G = dict(wo=dwo, w0=accs[0], a0=accs[1], kk=accs[2], ka=accs[3], gn_g=dgn_g, gn_b=dgn_b, rk=drk)
    G['w2'] = _mm_tn(c['tw'], dwpb, P['w2'], "rwkv_dlora_out", F32)
    G['a2'] = _mm_tn(c['la1b'], dapb, P['a2'], "rwkv_dlora_out", F32)
    G['g2'] = _mm_tn(c['sg'], dgb, P['g2'], "rwkv_dlora_out_g", F32)
    dtw = _mm_nt(dwpb, P['w2'], "rwkv_dlora_mid", F32)
    dla1 = _mm_nt(dapb, P['a2'], "rwkv_dlora_mid", F32)
    dsg = _mm_nt(dgb, P['g2'], "rwkv_dlora_mid_g", F32)
    ins = [('row', dtw), ('row', c['tw']), ('row', dla1), ('row', dsg), ('row', c['sg'])]
    outs = [(dtw.shape[1], BF16), (dla1.shape[1], BF16), (dsg.shape[1], BF16)]
    if vres:
        dsvpb = res[5]
        G['v0'] = accs[4]
        G['v2'] = _mm_tn(c['lv1b'], dsvpb, P['v2'], "rwkv_dlora_out_v", F32)
        dlv1 = _mm_nt(dsvpb, P['v2'], "rwkv_dlora_mid_v", F32)
        ins.append(('row', dlv1))
        outs.append((dlv1.shape[1], BF16))

    def lora_act_bwd(dtwb, twb, dla1b_, dsgb, sgb, *rest):
        twb, sgb = twb.astype(F32), sgb.astype(F32)
        return tuple([dtwb * (1.0 - twb * twb), dla1b_, dsgb * sgb * (1.0 - sgb)] + list(rest))

    acts = _rowwise(lora_act_bwd, "rwkv_lora_act_bwd", S, _tm(S, 1024, 6), ins, outs)
    dlw1b, dla1b, dlg1b = acts[0], acts[1], acts[2]
    G['w1'] = _mm_tn(xw, dlw1b, P['w1'], "rwkv_dlora_in", F32)
    G['a1'] = _mm_tn(xa, dla1b, P['a1'], "rwkv_dlora_in", F32)
    G['g1'] = _mm_tn(xg, dlg1b, P['g1'], "rwkv_dlora_in_g", F32)
    G['wr'] = _mm_tn(xr, drb, P['wr'], "rwkv_dwproj", BF16)
    G['wk'] = _mm_tn(xk, dkb, P['wk'], "rwkv_dwproj", BF16)
    G['wv'] = _mm_tn(xv, dv0b, P['wv'], "rwkv_dwproj", BF16)
    dxw = _mm_nt(dlw1b, P['w1'], "rwkv_dx_lora", F32)
    dxa = _mm_nt(dla1b, P['a1'], "rwkv_dx_lora", F32)
    dxg = _mm_nt(dlg1b, P['g1'], "rwkv_dx_lora_g", F32)
    dxr = _mm_nt(drb, P['wr'], "rwkv_dx", F32)
    dxk = _mm_nt(dkb, P['wk'], "rwkv_dx", F32)
    dxv = _mm_nt(dv0b, P['wv'], "rwkv_dx", F32)
    if vres:
        G['v1'] = _mm_tn(xv, acts[3], P['v1'], "rwkv_dlora_in_v", F32)
        dxv = _mm_nt(acts[3], P['v1'], "rwkv_dx_lora_v", F32, res=dxv)

    def mix_bwd(x, xs, m0, m1, m2, m3, m4, m5, d0, d1, d2, d3, d4, d5):
        xx = xs - x
        ds_ = (d0, d1, d2, d3, d4, d5)
        dxx = d0 * m0 + d1 * m1 + d2 * m2 + d3 * m3 + d4 * m4 + d5 * m5
        dsum = d0 + d1 + d2 + d3 + d4 + d5
        return tuple([dsum - dxx, dxx] + [jnp.sum(d * xx, axis=0, keepdims=True) for d in ds_])

    res = _rowwise(mix_bwd, "rwkv_mix_bwd", S, _tm(S, D, 24),
                   [('row', c['hn']), ('row', c['hs'])] + [('full', m) for m in P['mix']]
                   + [('row', d) for d in (dxr, dxw, dxk, dxv, dxa, dxg)], [(D, F32), (D, F32)], [(1, D)] * 6)
    dx_here, dxs = res[0], res[1]
    G['mix'] = res[2:]
    dhn = dx_here + jnp.concatenate([dxs[1:], jnp.zeros((1, D), F32)], axis=0)
    dh2, G['gain'] = _rms_bwd("rms_bwd", h, gain, dhn, dh)
    return dh2, G, dvfirst_out


def _loss_bwd(h, target, gain):
    S, D = h.shape

    def fn(hb, tb, g):
        r = lax.rsqrt(jnp.mean(hb * hb, axis=-1, keepdims=True) + RMS_EPS)
        xh = hb * r
        e = xh * g - tb
        dy = e * (1.0 / D)
        dxh = dy * g
        dx = r * (dxh - xh * jnp.mean(dxh * xh, axis=-1, keepdims=True))
        part = jnp.sum(jnp.sum(e * e, axis=-1, keepdims=True), axis=0, keepdims=True) * (0.5 / D)
        return dx, jnp.sum(dy * xh, axis=0, keepdims=True), jnp.broadcast_to(part, (1, LANES))

    dh, dgain, part = _rowwise(fn, "loss", S, _tm(S, D, 8), [('row', h), ('row', target), ('full', gain)],
                               [(D, F32)], [(1, D), (1, LANES)])
    return part[0, 0], dh, dgain


WEIGHTS = ['norms', 'final_norm', 'ffn_wg', 'ffn_wu', 'ffn_wd', 'ple_wp', 'ple_wg', 'e_w_in', 'e_w_out', 'a_vnorm',
           'a_ws', 'a_bs', 'b_onorm', 'b_lb_logits', 'c_mix', 'c_wr', 'c_wk', 'c_wv', 'c_wo', 'c_w0', 'c_w1', 'c_w2',
           'c_a0', 'c_a1', 'c_a2', 'c_g1', 'c_g2', 'c_kk', 'c_ka', 'c_rk', 'c_gn_g', 'c_gn_b', 'c_v0', 'c_v1', 'c_v2']
BIG = {'ffn_wg': 'col', 'ffn_wu': 'col', 'ffn_wd': 'row', 'ple_wg': 'row', 'e_w_in': 'col', 'e_w_out': 'row',
       'c_wr': 'row', 'c_wk': 'row', 'c_wv': 'row', 'c_wo': 'row'}
SMALL = ['norms', 'ple_wp', 'c_mix', 'c_w0', 'c_w1', 'c_w2', 'c_a0', 'c_a1', 'c_a2', 'c_g1', 'c_g2', 'c_kk', 'c_ka',
         'c_gn_g', 'c_gn_b', 'c_v0', 'c_v1', 'c_v2']
SMALL_MM = {'ple_wp': 'col', 'c_w1': 'row', 'c_w2': 'col', 'c_a1': 'row', 'c_a2': 'col', 'c_g1': 'row', 'c_g2': 'col',
            'c_v1': 'row', 'c_v2': 'col'}
REP = ['final_norm', 'a_vnorm', 'a_ws', 'a_bs', 'b_onorm', 'b_lb_logits', 'c_rk']
PACK_QUANTUM = 8 * LANES


def _pack(arrays, lead=0):
    lead_shape = arrays[0].shape[:lead]
    flat = jnp.concatenate([a.reshape(lead_shape + (-1,)).astype(F32) for a in arrays], axis=-1)
    pad = (-flat.shape[-1]) % PACK_QUANTUM
    if pad:
        flat = jnp.concatenate([flat, jnp.zeros(lead_shape + (pad,), F32)], axis=-1)
    return flat.reshape(lead_shape + (-1, LANES))


def _unpack(packed, names, shapes, lead=0):
    lead_shape = packed.shape[:lead]
    flat = packed.reshape(lead_shape + (-1,))
    out, off = {}, 0
    for n, s in zip(names, shapes):
        size = 1
        for d in s:
            size *= d
        out[n] = flat[..., off:off + size].reshape(lead_shape + tuple(s))
        off += size
    return out


def _full_vec(g):
    return jnp.moveaxis(g, 0, -2).reshape(g.shape[1:-1] + (N_CHIPS * g.shape[-1],))


def _vec_shards(g):
    return jnp.moveaxis(g.reshape(g.shape[:-1] + (N_CHIPS, g.shape[-1] // N_CHIPS)), -2, 0)


def _lower_bounds(logits):
    probs = jax.nn.softmax(logits.astype(F32), axis=0)
    return jnp.cumsum(probs, axis=0) - probs[0]


def kernel(x, p, norms, final_norm, ffn_wg, ffn_wu, ffn_wd, ple_wp, ple_wg, e_w_in, e_w_out, a_vnorm, a_ws, a_bs, b_onorm, b_lb_logits, c_mix, c_wr, c_wk, c_wv, c_wo, c_w0, c_w1, c_w2, c_a0, c_a1, c_a2, c_g1, c_g2, c_kk, c_ka, c_rk, c_gn_g, c_gn_b, c_v0, c_v1, c_v2, loss_target, m_norms, m_final_norm, m_ffn_wg, m_ffn_wu, m_ffn_wd, m_ple_wp, m_ple_wg, m_e_w_in, m_e_w_out, m_a_vnorm, m_a_ws, m_a_bs, m_b_onorm, m_b_lb_logits, m_c_mix, m_c_wr, m_c_wk, m_c_wv, m_c_wo, m_c_w0, m_c_w1, m_c_w2, m_c_a0, m_c_a1, m_c_a2, m_c_g1, m_c_g2, m_c_kk, m_c_ka, m_c_rk, m_c_gn_g, m_c_gn_b, m_c_v0, m_c_v1, m_c_v2, v_norms, v_final_norm, v_ffn_wg, v_ffn_wu, v_ffn_wd, v_ple_wp, v_ple_wg, v_e_w_in, v_e_w_out, v_a_vnorm, v_a_ws, v_a_bs, v_b_onorm, v_b_lb_logits, v_c_mix, v_c_wr, v_c_wk, v_c_wv, v_c_wo, v_c_w0, v_c_w1, v_c_w2, v_c_a0, v_c_a1, v_c_a2, v_c_g1, v_c_g2, v_c_kk, v_c_ka, v_c_rk, v_c_gn_g, v_c_gn_b, v_c_v0, v_c_v1, v_c_v2):
    A = dict(locals())
    assert x.shape[0] == 1, "one example per device"

    big_names = list(BIG)
    gathered = _all_gather("gather_weights", [A[n].astype(BF16).reshape(-1, A[n].shape[-1]) for n in big_names],
                           [_pack([A[n] for n in SMALL])])
    GB = {n: gathered[i].reshape((N_CHIPS,) + A[n].shape) for i, n in enumerate(big_names)}
    GS = _unpack(gathered[-1], SMALL, [A[n].shape for n in SMALL], lead=1)

    part, grad_x, big_entries, sm_pack, rep_pack = _local_step(A, GB, GS)
    loss = lax.psum(part, ("x", "y", "c"))

    c_arr = lax.axis_index("c").astype(jnp.int32).reshape(1)
    grads = [e[2] for e in big_entries]
    swapped = _swap_halves("reduce_swap_cores", grads)
    parts = [_add_own_half("reduce_add_cores", g, s, c_arr) for g, s in zip(grads, swapped)]
    slots = _scatter_chips("reduce_scatter_chips", parts)
    halves = [_sum_slots("reduce_sum_chips", s, (3, 0, 1, 2)) for s in slots]
    places = [(big_names.index(n), idx) for n, idx, _ in big_entries]
    big_grads = _join_halves("reduce_join_cores", halves, places,
                             [jax.ShapeDtypeStruct(A[n].shape, F32) for n in big_names])
    sm_slots, rep_slots = _spread_all("reduce_small", sm_pack, rep_pack)
    sm_grad = _sum_slots("reduce_sum_small", sm_slots, tuple(range(8)))
    rep_grad = _sum_slots("reduce_sum_small", rep_slots, tuple(range(8)))

    outs = {}
    for o, n in enumerate(big_names):
        shp, C = A[n].shape, A[n].shape[-1]
        g = big_grads[o]
        d, nm, nv = _adamw("adamw", g.reshape(-1, C), A[n].reshape(-1, C), A['m_' + n].reshape(-1, C),
                           A['v_' + n].reshape(-1, C))
        outs[n] = (g, d.reshape(shp), nm.reshape(shp), nv.reshape(shp))
    for names, g in ((SMALL, sm_grad), (REP, rep_grad)):
        shapes = [A[n].shape for n in names]
        res = _adamw("adamw_packed", g, _pack([A[n] for n in names]), _pack([A['m_' + n] for n in names]),
                     _pack([A['v_' + n] for n in names]))
        un = [_unpack(t, names, shapes) for t in (g,) + tuple(res)]
        for n in names:
            outs[n] = tuple(u[n] for u in un)
    return (loss, grad_x, *[outs[n][0] for n in WEIGHTS], *[outs[n][1] for n in WEIGHTS],
            *[outs[n][2] for n in WEIGHTS], *[outs[n][3] for n in WEIGHTS])


def _local_step(A, GB, GS):
    x, p, a_vnorm, a_ws, a_bs, b_onorm, b_lb_logits, c_rk = (A[n] for n in (
        'x', 'p', 'a_vnorm', 'a_ws', 'a_bs', 'b_onorm', 'b_lb_logits', 'c_rk'))
    S, D = x.shape[1], x.shape[2]
    depth = A['ffn_wg'].shape[0]
    aw = D // 2
    h = x[0]
    target = A['loss_target'][0]
    final_norm = A['final_norm']
    pb = p[:, 0].astype(BF16)
    GSb = {n: GS[n].astype(BF16) for n in SMALL_MM}

    def bigv(n, *idx):
        return WV(GB[n], idx, BIG[n])

    def smv(n, *idx):
        return WV(GSb[n], idx, SMALL_MM[n])

    def row(v):
        return v.reshape(1, -1)

    vecs = {n: _full_vec(GS[n]) for n in SMALL if n not in SMALL_MM}
    lb_all, lb_vjp = jax.vjp(_lower_bounds, b_lb_logits)

    def rwkv_params(j):
        P = dict(mix=[vecs['c_mix'][j, q:q + 1] for q in range(6)],
                 wr=bigv('c_wr', j), wk=bigv('c_wk', j), wv=bigv('c_wv', j), wo=bigv('c_wo', j),
                 w1=smv('c_w1', j), w2=smv('c_w2', j), a1=smv('c_a1', j), a2=smv('c_a2', j),
                 g1=smv('c_g1', j), g2=smv('c_g2', j),
                 w0=row(vecs['c_w0'][j]), a0=row(vecs['c_a0'][j]), kk=row(vecs['c_kk'][j]), ka=row(vecs['c_ka'][j]),
                 gn_g=row(vecs['c_gn_g'][j]), gn_b=row(vecs['c_gn_b'][j]), rk=c_rk[j].reshape(1, D))
        if j > 0:
            P.update(v0=row(vecs['c_v0'][j - 1]), v1=smv('c_v1', j - 1), v2=smv('c_v2', j - 1))
        return P

    def even_params(i):
        j = i // 2
        return (bigv('e_w_in', j), bigv('e_w_out', j), a_vnorm[j:j + 1], a_ws[j], a_bs[j].T, b_onorm[j:j + 1],
                lb_all[i:i + 1])

    def gain(i, q):
        return row(vecs['norms'][i, q])

    def ffn_views(i, q):
        return bigv('ffn_wg', i, q), bigv('ffn_wu', i, q), bigv('ffn_wd', i, q)

    caches, vfirst = [], None
    for i in range(depth):
        c = {}
        h, c['f1'] = _ffn_fwd(h, gain(i, 0), *ffn_views(i, 0))
        if i % 2 == 0:
            h, c['mix'] = _even_fwd(h, gain(i, 1), *even_params(i))
        else:
            h, c['mix'], vfirst = _rwkv_fwd(h, gain(i, 1), rwkv_params(i // 2), vfirst if i // 2 > 0 else None)
        h, c['f2'] = _ffn_fwd(h, gain(i, 2), *ffn_views(i, 1))
        h, c['ple'] = _ple_fwd(h, gain(i, 3), pb[i], bigv('ple_wg', i), smv('ple_wp', i))
        caches.append(c)

    part, dh, dfinal = _loss_bwd(h, target, final_norm.reshape(1, D))
    big_entries = []
    sm = {n: {} for n in SMALL}
    rep = {n: {} for n in REP}
    dvfirst = None
    for i in reversed(range(depth)):
        j, c = i // 2, caches[i]
        dh, dg, dwgate, dwproj = _ple_bwd(dh, c['ple'], gain(i, 3), pb[i], bigv('ple_wg', i), smv('ple_wp', i))
        sm['norms'][(i, 3)] = dg
        sm['ple_wp'][(i,)] = dwproj
        big_entries.append(('ple_wg', (i,), dwgate))
        for q, key in ((1, 'f2'), (0, 'f1')):
            if key == 'f1':
                if i % 2 == 0:
                    dh, G = _even_bwd(dh, c['mix'], gain(i, 1), *even_params(i))
                    big_entries += [('e_w_in', (j,), G['w_in']), ('e_w_out', (j,), G['w_out'])]
                    rep['a_vnorm'][j], rep['a_ws'][j], rep['a_bs'][j] = G['vgain'][0], G['ws'], G['bs_t'].T
                    rep['b_onorm'][j], rep['b_lb_logits'][i] = G['onorm'][0], G['lb'][0]
                else:
                    dh, G, dvf = _rwkv_bwd(dh, c['mix'], gain(i, 1), rwkv_params(j), dvfirst if j == 0 else None)
                    if dvf is not None:
                        dvfirst = dvf if dvfirst is None else dvfirst + dvf
                    big_entries += [('c_wr', (j,), G['wr']), ('c_wk', (j,), G['wk']), ('c_wv', (j,), G['wv']),
                                    ('c_wo', (j,), G['wo'])]
                    for n in ('w0', 'a0', 'kk', 'ka', 'gn_g', 'gn_b', 'w1', 'w2', 'a1', 'a2', 'g1', 'g2'):
                        sm['c_' + n][(j,)] = G[n]
                    sm['c_mix'][(j,)] = jnp.concatenate(G['mix'], axis=0)
                    rep['c_rk'][j] = G['rk'].reshape(c_rk.shape[1:])
                    if j > 0:
                        for n in ('v0', 'v1', 'v2'):
                            sm['c_' + n][(j - 1,)] = G[n]
                sm['norms'][(i, 1)] = G['gain']
            dh, dg, dwg, dwu, dwd = _ffn_bwd(dh, c[key], gain(i, 2 * q), *ffn_views(i, q))
            sm['norms'][(i, 2 * q)] = dg
            big_entries += [('ffn_wg', (i, q), dwg), ('ffn_wu', (i, q), dwu), ('ffn_wd', (i, q), dwd)]
    grad_x = dh[None]

    def stacked(blocks, lead_shape):
        def rec(prefix, dims):
            if not dims:
                return blocks[prefix]
            return jnp.stack([rec(prefix + (q,), dims[1:]) for q in range(dims[0])], axis=0)
        return rec((), tuple(lead_shape))

    sm_shards = []
    for n in SMALL:
        blk = A[n].shape
        if n in SMALL_MM:
            g = jnp.moveaxis(stacked(sm[n], blk[:-2]), len(blk) - 2, 0)
        else:
            lead = blk[:-1] if n != 'c_mix' else blk[:-2]
            full = stacked(sm[n], lead)
            g = _vec_shards(full.reshape(blk[:-1] + (D,)))
        sm_shards.append(g)
    sm_pack = _pack(sm_shards, lead=1)
    dlb = jnp.stack([rep['b_lb_logits'].get(i, jnp.zeros((aw,), F32)) for i in range(depth)], axis=0)
    rep_grads = dict(final_norm=dfinal[0], a_vnorm=stacked({(k,): v for k, v in rep['a_vnorm'].items()}, a_vnorm.shape[:1]),
                     a_ws=stacked({(k,): v for k, v in rep['a_ws'].items()}, a_ws.shape[:1]),
                     a_bs=stacked({(k,): v for k, v in rep['a_bs'].items()}, a_bs.shape[:1]),
                     b_onorm=stacked({(k,): v for k, v in rep['b_onorm'].items()}, b_onorm.shape[:1]),
                     b_lb_logits=lb_vjp(dlb)[0],
                     c_rk=stacked({(k,): v for k, v in rep['c_rk'].items()}, c_rk.shape[:1]))
    rep_pack = _pack([rep_grads[n] for n in REP])
    return part, grad_x, big_entries, sm_pack, rep_pack
```

```python
import functools

import jax
import jax.numpy as jnp
from jax import lax
from jax.experimental import pallas as pl
from jax.experimental.pallas import tpu as pltpu

F32 = jnp.float32
BF16 = jnp.bfloat16
MESH = pl.DeviceIdType.MESH

LANES = 128
VMEM_LIMIT = 56 * 1024 * 1024
MM_VMEM_BUDGET = 36 * 1024 * 1024
N_CHIPS = 4

RMS_EPS = 1e-6
A_GROUP = 128
A_CHUNK = 128
B_HEAD = 128
B_MIN_F = 1e-30
C_HEAD = 64
C_GN_EPS = 64e-5
ADAM_LR = 0.001
ADAM_B1 = 0.9
ADAM_B2 = 0.999
ADAM_EPS = 1e-08
ADAM_WD = 0.01
ADAM_STEP = 10


def _sigmoid(x):
    return 1.0 / (1.0 + jnp.exp(-x))


def _silu(x):
    return x * _sigmoid(x)


def _dsilu(x):
    s = _sigmoid(x)
    return s * (1.0 + x * (1.0 - s))


_GELU_C = 0.7978845608028654


def _gelu(x):
    return 0.5 * x * (1.0 + jnp.tanh(_GELU_C * (x + 0.044715 * x * x * x)))


def _dgelu(x):
    th = jnp.tanh(_GELU_C * (x + 0.044715 * x * x * x))
    return 0.5 * (1.0 + th) + 0.5 * x * (1.0 - th * th) * _GELU_C * (1.0 + 3.0 * 0.044715 * x * x)


def _softplus(x):
    return jnp.maximum(x, 0.0) + jnp.log(1.0 + jnp.exp(-jnp.abs(x)))


def _seg_ones(seg):
    i = lax.broadcasted_iota(jnp.int32, (LANES, LANES), 0) // seg
    j = lax.broadcasted_iota(jnp.int32, (LANES, LANES), 1) // seg
    return jnp.where(i == j, 1.0, 0.0).astype(BF16)


def _segsum(x, seg):
    ones = _seg_ones(seg)
    outs = []
    for j in range(x.shape[1] // LANES):
        xb = x[:, j * LANES:(j + 1) * LANES]
        hi = xb.astype(BF16)
        r1 = xb - hi.astype(F32)
        mid = r1.astype(BF16)
        lo = (r1 - mid.astype(F32)).astype(BF16)
        acc = jnp.dot(hi, ones, preferred_element_type=F32)
        acc = acc + jnp.dot(mid, ones, preferred_element_type=F32)
        acc = acc + jnp.dot(lo, ones, preferred_element_type=F32)
        outs.append(acc)
    return outs[0] if len(outs) == 1 else jnp.concatenate(outs, axis=1)


def _rowwise(fn, name, rows, tm, ins, outs, accs=()):
    n_in, n_out, n_acc = len(ins), len(outs), len(accs)
    arrays, in_specs = [], []
    for spec in ins:
        kind, arr = spec[0], spec[1]
        arrays.append(arr)
        if kind == 'row':
            in_specs.append(pl.BlockSpec((tm, arr.shape[1]), lambda i: (i, 0)))
        elif kind == 'col':
            in_specs.append(pl.BlockSpec((tm, spec[2]), functools.partial(lambda i, cb: (i, cb), cb=spec[3])))
        else:
            in_specs.append(pl.BlockSpec(arr.shape, functools.partial(lambda i, nd: (0,) * nd, nd=arr.ndim)))
    out_shape = [jax.ShapeDtypeStruct((rows, w), dt) for (w, dt) in outs]
    out_specs = [pl.BlockSpec((tm, w), lambda i: (i, 0)) for (w, _) in outs]
    out_shape += [jax.ShapeDtypeStruct(s, F32) for s in accs]
    out_specs += [pl.BlockSpec(s, functools.partial(lambda i, nd: (0,) * nd, nd=len(s))) for s in accs]

    def body(*refs):
        vals = fn(*[r[...] for r in refs[:n_in]])
        if not isinstance(vals, (tuple, list)):
            vals = (vals,)
        for r, v in zip(refs[n_in:n_in + n_out], vals[:n_out]):
            r[...] = v.astype(r.dtype)
        if n_acc:
            acc_refs = refs[n_in + n_out:]

            @pl.when(pl.program_id(0) == 0)
            def _():
                for r in acc_refs:
                    r[...] = jnp.zeros(r.shape, F32)

            for r, v in zip(acc_refs, vals[n_out:]):
                r[...] += v

    res = pl.pallas_call(
        body, name=name, grid=(rows // tm,), in_specs=in_specs, out_specs=out_specs, out_shape=out_shape,
        compiler_params=pltpu.CompilerParams(
            dimension_semantics=("arbitrary",) if n_acc else ("parallel",), vmem_limit_bytes=VMEM_LIMIT),
    )(*arrays)
    return res


class WV:
    def __init__(self, arr, idx, kind):
        self.arr, self.idx, self.kind = arr, tuple(idx), kind
        self.ns = arr.shape[0]
        self.R, self.C = arr.shape[-2:]
        self.K = self.R * (self.ns if kind == 'row' else 1)
        self.N = self.C * (self.ns if kind == 'col' else 1)

    def spec(self, br, bc, rmap, cmap):
        lead = (None,) * (1 + len(self.idx))
        nrb, ncb = self.R // br, self.C // bc
        idx, kind = self.idx, self.kind

        def index_map(*g):
            ri, ci = rmap(*g), cmap(*g)
            if kind == 'row':
                return (ri // nrb, *idx, ri % nrb, ci)
            return (ci // ncb, *idx, ri, ci % ncb)

        return pl.BlockSpec(lead + (br, bc), index_map)


def _tile_options(n):
    return [n] + [d for d in range(n - LANES, LANES - 1, -LANES) if n % d == 0]


def _pick_tiles(opt_m, opt_n, opt_k, out_bytes, has_res, full_k):
    best, best_key = None, None
    for tm in opt_m:
        for tn in opt_n:
            for tk in opt_k:
                multi = tk != full_k
                est = 2 * (tm * tk * 2 + tk * tn * 2 + tm * tn * out_bytes) + tm * tn * 4 * (2 if multi else 1)
                if has_res:
                    est += 2 * tm * tn * 4
                if est > MM_VMEM_BUDGET:
                    continue
                key = (tm * tn * tk, tk)
                if best is None or key > best_key:
                    best, best_key = (tm, tn, tk), key
    return best


def _mm_call(name, dims, grid, in_specs, out_spec, out_shape, operands, nk, acc_shape, scale, has_res):
    def body(*refs):
        a_ref, b_ref = refs[0], refs[1]
        res_ref = refs[2] if has_res else None
        o_ref = refs[3] if has_res else refs[2]

        def finalize(acc):
            acc = acc * scale if scale != 1.0 else acc
            if has_res:
                acc = acc + res_ref[...]
            o_ref[...] = acc.astype(o_ref.dtype)

        part = lax.dot_general(a_ref[...], b_ref[...], dims, preferred_element_type=F32)
        if nk == 1:
            finalize(part)
        else:
            acc_ref = refs[-1]
            k = pl.program_id(2)

            @pl.when(k == 0)
            def _():
                acc_ref[...] = part

            @pl.when(k > 0)
            def _():
                acc_ref[...] += part

            @pl.when(k == nk - 1)
            def _():
                finalize(acc_ref[...])

    return pl.pallas_call(
        body, name=name, grid=grid, in_specs=in_specs, out_specs=out_spec, out_shape=out_shape,
        scratch_shapes=[pltpu.VMEM(acc_shape, F32)] if nk > 1 else [],
        compiler_params=pltpu.CompilerParams(
            dimension_semantics=("parallel", "parallel", "arbitrary"), vmem_limit_bytes=VMEM_LIMIT),
    )(*operands)


def _mm_nn(a, w, name, out_dtype=F32, res=None, scale=1.0):
    M, K = a.shape
    N = w.N
    opt_n = _tile_options(w.C)
    opt_k = _tile_options(w.R)
    tm, tn, tk = _pick_tiles(_tile_options(M), opt_n, opt_k, jnp.dtype(out_dtype).itemsize, res is not None, K)
    nk = K // tk
    in_specs = [pl.BlockSpec((tm, tk), lambda n, m, k: (m, k)),
                w.spec(tk, tn, lambda n, m, k: k, lambda n, m, k: n)]
    operands = [a, w.arr]
    if res is not None:
        in_specs.append(pl.BlockSpec((tm, tn), lambda n, m, k: (m, n)))
        operands.append(res)
    return _mm_call(name, (((1,), (0,)), ((), ())), (N // tn, M // tm, nk), in_specs,
                    pl.BlockSpec((tm, tn), lambda n, m, k: (m, n)), jax.ShapeDtypeStruct((M, N), out_dtype),
                    operands, nk, (tm, tn), scale, res is not None)


def _mm_nt(a, w, name, out_dtype=F32, res=None, scale=1.0):
    M, C = a.shape
    Ko = w.K
    opt_n = _tile_options(w.R)
    opt_k = _tile_options(w.C)
    tm, tn, tk = _pick_tiles(_tile_options(M), opt_n, opt_k, jnp.dtype(out_dtype).itemsize, res is not None, C)
    nk = C // tk
    in_specs = [pl.BlockSpec((tm, tk), lambda n, m, k: (m, k)),
                w.spec(tn, tk, lambda n, m, k: n, lambda n, m, k: k)]
    operands = [a, w.arr]
    if res is not None:
        in_specs.append(pl.BlockSpec((tm, tn), lambda n, m, k: (m, n)))
        operands.append(res)
    return _mm_call(name, (((1,), (1,)), ((), ())), (Ko // tn, M // tm, nk), in_specs,
                    pl.BlockSpec((tm, tn), lambda n, m, k: (m, n)), jax.ShapeDtypeStruct((M, Ko), out_dtype),
                    operands, nk, (tm, tn), scale, res is not None)


def _mm_tn(a, dy, like, name, out_dtype=BF16, scale=1.0):
    M, K = a.shape
    N = dy.shape[1]
    out = WV(jax.ShapeDtypeStruct((like.ns, like.R, like.C), out_dtype), (), like.kind)
    opt_m = _tile_options(like.R)
    opt_n = _tile_options(like.C)
    tko, tno, tc = _pick_tiles(opt_m, opt_n, _tile_options(M), jnp.dtype(out_dtype).itemsize, False, M)
    nk = M // tc
    in_specs = [pl.BlockSpec((tc, tko), lambda i, j, c: (c, i)),
                pl.BlockSpec((tc, tno), lambda i, j, c: (c, j))]
    return _mm_call(name, (((0,), (0,)), ((), ())), (K // tko, N // tno, nk), in_specs,
                    out.spec(tko, tno, lambda i, j, c: i, lambda i, j, c: j), out.arr,
                    [a, dy], nk, (tko, tno), scale, False)


SCAN_TB = 16


def _allsum(x, n_heads):
    r = x.shape[0]
    while r > 1:
        r //= 2
        x = x + pltpu.roll(x, r, 0)
    sh = LANES // 2
    while sh >= n_heads:
        x = x + pltpu.roll(x, sh, 1)
        sh //= 2
    return x


def _subsum(x):
    r = x.shape[0]
    while r > 1:
        r //= 2
        x = x + pltpu.roll(x, r, 0)
    return x


def _head_ones(n_heads):
    i = lax.broadcasted_iota(jnp.int32, (LANES, LANES), 0) % n_heads
    j = lax.broadcasted_iota(jnp.int32, (LANES, LANES), 1) % n_heads
    return jnp.where(i == j, 1.0, 0.0).astype(BF16)


def _lanesum_mxu(x, ones):
    hi = x.astype(BF16)
    r1 = x - hi.astype(F32)
    mid = r1.astype(BF16)
    lo = (r1 - mid.astype(F32)).astype(BF16)
    acc = jnp.dot(hi, ones, preferred_element_type=F32)
    acc = acc + jnp.dot(mid, ones, preferred_element_type=F32)
    return acc + jnp.dot(lo, ones, preferred_element_type=F32)


def _rows_of(vals):
    n = len(vals)
    if vals[0].shape[0] >= n:
        idx = lax.broadcasted_iota(jnp.int32, (n, LANES), 0)
        out = vals[0][:n]
        for i in range(1, n):
            out = jnp.where(idx == i, vals[i][:n], out)
        return out
    return jnp.concatenate([v[0:1] for v in vals], axis=0)


def _scan_fwd(name, n_heads, wk, kk, rk, vrow, ak=None, bk=None):
    S, R, _ = wk.shape
    dv = vrow.shape[1]
    ab = ak is not None
    tb = min(SCAN_TB, S)
    grp = min(8, dv)

    def body(*refs):
        if ab:
            w_ref, k_ref, r_ref, v_ref, a_ref, b_ref, y_ref, hist_ref, sa_ref, s_ref = refs
        else:
            w_ref, k_ref, r_ref, v_ref, y_ref, hist_ref, s_ref = refs

        @pl.when(pl.program_id(0) == 0)
        def _():
            s_ref[...] = jnp.zeros(s_ref.shape, F32)

        ones = None if ab else _head_ones(n_heads)

        def step(t, carry):
            w, k, r = w_ref[t], k_ref[t], r_ref[t]
            if ab:
                a, b = a_ref[t], b_ref[t]
            tiles = []
            for g0 in range(0, dv, grp):
                ys, sas = [], []
                for i in range(grp):
                    v = g0 + i
                    st = s_ref[v]
                    hist_ref[t, v] = st
                    vr = v_ref[t, pl.ds(v, 1), :]
                    if ab:
                        sa = _allsum(st * a, n_heads)
                        sas.append(sa)
                        st = st * w + vr * k + sa * b
                        ys.append(_allsum(st * r, n_heads))
                    else:
                        st = st * w + vr * k
                        ys.append(_subsum(st * r))
                    s_ref[v] = st
                if ab:
                    y_ref[t, pl.ds(g0, grp), :] = _rows_of(ys)
                    sa_ref[t, pl.ds(g0, grp), :] = _rows_of(sas)
                else:
                    tiles.append(_rows_of(ys))
            if not ab:
                y_ref[t] = _lanesum_mxu(tiles[0] if len(tiles) == 1 else jnp.concatenate(tiles, axis=0), ones)
            return carry

        lax.fori_loop(0, tb, step, 0)

    kspec = pl.BlockSpec((tb, R, LANES), lambda i: (i, 0, 0))
    rspec = pl.BlockSpec((tb, dv, LANES), lambda i: (i, 0, 0))
    operands = [wk, kk, rk, vrow] + ([ak, bk] if ab else [])
    in_specs = [kspec, kspec, kspec, rspec] + ([kspec, kspec] if ab else [])
    out_shape = [jax.ShapeDtypeStruct((S, dv, LANES), F32), jax.ShapeDtypeStruct((S, dv, R, LANES), F32)]
    out_specs = [rspec, pl.BlockSpec((tb, dv, R, LANES), lambda i: (i, 0, 0, 0))]
    if ab:
        out_shape.append(jax.ShapeDtypeStruct((S, dv, LANES), F32))
        out_specs.append(rspec)
    res = pl.pallas_call(
        body, name=name, grid=(S // tb,), in_specs=in_specs, out_specs=out_specs, out_shape=out_shape,
        scratch_shapes=[pltpu.VMEM((dv, R, LANES), F32)],
        compiler_params=pltpu.CompilerParams(dimension_semantics=("arbitrary",), vmem_limit_bytes=VMEM_LIMIT),
    )(*operands)
    return (res[0], res[1], res[2]) if ab else (res[0], res[1], None)


def _scan_bwd(name, n_heads, wk, kk, rk, vrow, hist, dyrow, ak=None, bk=None, sarow=None):
    S, R, _ = wk.shape
    dv = vrow.shape[1]
    ab = ak is not None
    tb = min(SCAN_TB, S)
    nb = S // tb
    grp = min(8, dv)

    def body(*refs):
        if ab:
            (w_ref, k_ref, r_ref, v_ref, hist_ref, dy_ref, a_ref, b_ref, sa_ref,
             dw_ref, dk_ref, dr_ref, dv_ref, da_ref, db_ref, ds_ref) = refs
        else:
            (w_ref, k_ref, r_ref, v_ref, hist_ref, dy_ref,
             dw_ref, dk_ref, dr_ref, dv_ref, ds_ref) = refs

        @pl.when(pl.program_id(0) == 0)
        def _():
            ds_ref[...] = jnp.zeros(ds_ref.shape, F32)

        ones = None if ab else _head_ones(n_heads)

        def step(j, carry):
            t = tb - 1 - j
            w, k, r = w_ref[t], k_ref[t], r_ref[t]
            if ab:
                a, b = a_ref[t], b_ref[t]
            zero = jnp.zeros((R, LANES), F32)
            u, dw, dk, da, db = zero, zero, zero, zero, zero
            vd = jnp.sum(v_ref[t] * dy_ref[t], axis=0, keepdims=True)
            sd = jnp.sum(sa_ref[t] * dy_ref[t], axis=0, keepdims=True) if ab else None
            tiles = []
            for g0 in range(0, dv, grp):
                dvs = []
                for i in range(grp):
                    v = g0 + i
                    sp = hist_ref[t, v]
                    dyr = dy_ref[t, pl.ds(v, 1), :]
                    vr = v_ref[t, pl.ds(v, 1), :]
                    dst = ds_ref[v] + dyr * r
                    u = u + sp * dyr
                    dw = dw + dst * sp
                    dk = dk + dst * vr
                    if ab:
                        dvs.append(_allsum(dst * k, n_heads))
                        db = db + dst * sa_ref[t, pl.ds(v, 1), :]
                        dsa = _allsum(dst * b, n_heads)
                        da = da + sp * dsa
                        dst = dst * w + dsa * a
                    else:
                        dvs.append(_subsum(dst * k))
                        dst = dst * w
                    ds_ref[v] = dst
                if ab:
                    dv_ref[t, pl.ds(g0, grp), :] = _rows_of(dvs)
                else:
                    tiles.append(_rows_of(dvs))
            if not ab:
                dv_ref[t] = _lanesum_mxu(tiles[0] if len(tiles) == 1 else jnp.concatenate(tiles, axis=0), ones)
            dr = w * u + k * vd
            if ab:
                dr = dr + b * sd
                da_ref[t] = da
                db_ref[t] = db
            dw_ref[t] = dw
            dk_ref[t] = dk
            dr_ref[t] = dr
            return carry

        lax.fori_loop(0, tb, step, 0)

    kspec = pl.BlockSpec((tb, R, LANES), lambda i: (nb - 1 - i, 0, 0))
    rspec = pl.BlockSpec((tb, dv, LANES), lambda i: (nb - 1 - i, 0, 0))
    hspec = pl.BlockSpec((tb, dv, R, LANES), lambda i: (nb - 1 - i, 0, 0, 0))
    operands = [wk, kk, rk, vrow, hist, dyrow] + ([ak, bk, sarow] if ab else [])
    in_specs = [kspec, kspec, kspec, rspec, hspec, rspec] + ([kspec, kspec, rspec] if ab else [])
    kshape = jax.ShapeDtypeStruct((S, R, LANES), F32)
    out_shape = [kshape, kshape, kshape, jax.ShapeDtypeStruct((S, dv, LANES), F32)] + ([kshape, kshape] if ab else [])
    out_specs = [kspec, kspec, kspec, rspec] + ([kspec, kspec] if ab else [])
    return pl.pallas_call(
        body, name=name, grid=(nb,), in_specs=in_specs, out_specs=out_specs, out_shape=out_shape,
        scratch_shapes=[pltpu.VMEM((dv, R, LANES), F32)],
        compiler_params=pltpu.CompilerParams(dimension_semantics=("arbitrary",), vmem_limit_bytes=VMEM_LIMIT),
    )(*operands)


def _to_k(x, n_heads, dk):
    S = x.shape[0]
    kl = LANES // n_heads
    return x.reshape(S, n_heads, dk // kl, kl).transpose(0, 2, 3, 1).reshape(S, dk // kl, LANES)


def _from_k(x, n_heads, dk):
    S = x.shape[0]
    kl = LANES // n_heads
    return x.reshape(S, dk // kl, kl, n_heads).transpose(0, 3, 1, 2).reshape(S, n_heads * dk)


def _to_row(x, n_heads, dv):
    S = x.shape[0]
    return jnp.tile(x.reshape(S, n_heads, dv).transpose(0, 2, 1), (1, 1, LANES // n_heads))


def _from_row(x, n_heads, dv):
    S = x.shape[0]
    return x[:, :, :n_heads].transpose(0, 2, 1).reshape(S, n_heads * dv)


def _tril_mask():
    t = lax.broadcasted_iota(jnp.int32, (A_CHUNK, A_CHUNK), 0)
    s = lax.broadcasted_iota(jnp.int32, (A_CHUNK, A_CHUNK), 1)
    return s <= t


def _gmlp_fwd(name, proj, aw, vgain, ws, bs_t):
    S = proj.shape[0]
    G = aw // A_GROUP

    def body(u_ref, v_ref, gain_ref, ws_ref, bs_ref, o_ref):
        mask = _tril_mask()
        lane = lax.broadcasted_iota(jnp.int32, (A_CHUNK, G), 1)
        bs = bs_ref[...]
        for g in range(G):
            seg = slice(g * A_GROUP, (g + 1) * A_GROUP)
            ua = _gelu(u_ref[:, seg])
            va = _gelu(v_ref[:, seg])
            rs = lax.rsqrt(jnp.mean(va * va, axis=-1, keepdims=True) + RMS_EPS)
            vg = (va * rs) * gain_ref[:, seg]
            wm = jnp.where(mask, ws_ref[g], 0.0).astype(BF16)
            bcol = jnp.sum(jnp.where(lane == g, bs, 0.0), axis=1, keepdims=True)
            s = jnp.dot(wm, vg.astype(BF16), preferred_element_type=F32) + bcol
            o_ref[:, seg] = (ua * s).astype(o_ref.dtype)

    return pl.pallas_call(
        body, name=name, grid=(S // A_CHUNK,),
        in_specs=[pl.BlockSpec((A_CHUNK, aw), lambda i: (i, 0)), pl.BlockSpec((A_CHUNK, aw), lambda i: (i, 1)),
                  pl.BlockSpec((1, aw), lambda i: (0, 0)), pl.BlockSpec((G, A_CHUNK, A_CHUNK), lambda i: (0, 0, 0)),
                  pl.BlockSpec((A_CHUNK, G), lambda i: (0, 0))],
        out_specs=pl.BlockSpec((A_CHUNK, aw), lambda i: (i, 0)),
        out_shape=jax.ShapeDtypeStruct((S, aw), BF16),
        compiler_params=pltpu.CompilerParams(dimension_semantics=("parallel",), vmem_limit_bytes=VMEM_LIMIT),
    )(proj, proj, vgain, ws, bs_t)


def _gmlp_bwd(name, proj, dout, dout_cb, aw, vgain, ws, bs_t):
    S = proj.shape[0]
    G = aw // A_GROUP

    def body(u_ref, v_ref, do_ref, gain_ref, ws_ref, bs_ref, du_ref, dv_ref, dws_ref, dbs_ref, dgain_ref):
        @pl.when(pl.program_id(0) == 0)
        def _():
            dws_ref[...] = jnp.zeros(dws_ref.shape, F32)
            dbs_ref[...] = jnp.zeros(dbs_ref.shape, F32)
            dgain_ref[...] = jnp.zeros(dgain_ref.shape, F32)

        mask = _tril_mask()
        lane = lax.broadcasted_iota(jnp.int32, (A_CHUNK, G), 1)
        bs = bs_ref[...]
        dbs = jnp.zeros((A_CHUNK, G), F32)
        for g in range(G):
            seg = slice(g * A_GROUP, (g + 1) * A_GROUP)
            u, v = u_ref[:, seg], v_ref[:, seg]
            do = do_ref[:, seg].astype(F32)
            ua, va = _gelu(u), _gelu(v)
            rs = lax.rsqrt(jnp.mean(va * va, axis=-1, keepdims=True) + RMS_EPS)
            xh = va * rs
            gain = gain_ref[:, seg]
            vg = (xh * gain).astype(BF16)
            wm = jnp.where(mask, ws_ref[g], 0.0).astype(BF16)
            bcol = jnp.sum(jnp.where(lane == g, bs, 0.0), axis=1, keepdims=True)
            s = jnp.dot(wm, vg, preferred_element_type=F32) + bcol
            du_ref[:, seg] = (do * s * _dgelu(u)).astype(du_ref.dtype)
            ds = do * ua
            dsb = ds.astype(BF16)
            dw = lax.dot_general(dsb, vg, (((1,), (1,)), ((), ())), preferred_element_type=F32)
            dws_ref[g] += jnp.where(mask, dw, 0.0)
            dbs = dbs + jnp.where(lane == g, jnp.sum(ds, axis=1, keepdims=True), 0.0)
            dvg = lax.dot_general(wm, dsb, (((0,), (0,)), ((), ())), preferred_element_type=F32)
            dgain_ref[:, seg] += jnp.sum(dvg * xh, axis=0, keepdims=True)
            dxh = dvg * gain
            dva = rs * (dxh - xh * jnp.mean(dxh * xh, axis=-1, keepdims=True))
            dv_ref[:, seg] = (dva * _dgelu(v)).astype(dv_ref.dtype)
        dbs_ref[...] += dbs

    return pl.pallas_call(
        body, name=name, grid=(S // A_CHUNK,),
        in_specs=[pl.BlockSpec((A_CHUNK, aw), lambda i: (i, 0)), pl.BlockSpec((A_CHUNK, aw), lambda i: (i, 1)),
                  pl.BlockSpec((A_CHUNK, aw), functools.partial(lambda i, cb: (i, cb), cb=dout_cb)),
                  pl.BlockSpec((1, aw), lambda i: (0, 0)), pl.BlockSpec((G, A_CHUNK, A_CHUNK), lambda i: (0, 0, 0)),
                  pl.BlockSpec((A_CHUNK, G), lambda i: (0, 0))],
        out_specs=[pl.BlockSpec((A_CHUNK, aw), lambda i: (i, 0)), pl.BlockSpec((A_CHUNK, aw), lambda i: (i, 0)),
                   pl.BlockSpec((G, A_CHUNK, A_CHUNK), lambda i: (0, 0, 0)), pl.BlockSpec((A_CHUNK, G), lambda i: (0, 0)),
                   pl.BlockSpec((1, aw), lambda i: (0, 0))],
        out_shape=[jax.ShapeDtypeStruct((S, aw), BF16), jax.ShapeDtypeStruct((S, aw), BF16),
                   jax.ShapeDtypeStruct((G, A_CHUNK, A_CHUNK), F32), jax.ShapeDtypeStruct((A_CHUNK, G), F32),
                   jax.ShapeDtypeStruct((1, aw), F32)],
        compiler_params=pltpu.CompilerParams(dimension_semantics=("arbitrary",), vmem_limit_bytes=VMEM_LIMIT),
    )(proj, proj, dout, vgain, ws, bs_t)


ANY = pl.BlockSpec(memory_space=pl.ANY)


def _place():
    return lax.axis_index("x"), lax.axis_index("y"), lax.axis_index("c")


def _other_chips(x, y):
    return [(1 - x, y), (x, 1 - y), (1 - x, 1 - y)]


def _remote(src, dst, send_sem, recv_sem, device):
    return pltpu.make_async_remote_copy(src_ref=src, dst_ref=dst, send_sem=send_sem, recv_sem=recv_sem,
                                        device_id=device, device_id_type=MESH)


def _comm_call(body, name, operands, out_shape, n_dma, n_local):
    return pl.pallas_call(
        body, name=name, in_specs=[ANY] * len(operands), out_specs=[ANY] * len(out_shape), out_shape=out_shape,
        scratch_shapes=[pltpu.SemaphoreType.DMA((n_dma,)), pltpu.SemaphoreType.DMA((n_dma,)),
                        pltpu.SemaphoreType.DMA((max(n_local, 1),))],
        compiler_params=pltpu.CompilerParams(has_side_effects=True),
    )(*operands)


def _all_gather(name, bigs, smalls):
    nb, n = len(bigs), len(bigs) + len(smalls)
    arrays = list(bigs) + list(smalls)

    def body(*refs):
        ins, outs = refs[:n], refs[n:2 * n]
        send_sems, recv_sems, local_sems = refs[2 * n:]
        x, y, c = _place()
        me = 2 * x + y
        chips = _other_chips(x, y)
        sibling = (x, y, 1 - c)
        started = []
        for e in range(n):
            if e < nb:
                half = ins[e].shape[0] // 2
                src = ins[e].at[pl.ds(c * half, half)]
                dst = outs[e].at[me, pl.ds(c * half, half)]
            else:
                src, dst = ins[e], outs[e].at[me]
            for j, chip in enumerate(chips):
                cp = _remote(src, dst, send_sems.at[3 * e + j], recv_sems.at[3 * e + j], (*chip, c))
                cp.start()
                started.append(cp)
        fwd = 3 * n
        for e in range(nb):
            half = ins[e].shape[0] // 2
            for j, chip in enumerate(chips):
                landed = outs[e].at[2 * chip[0] + chip[1], pl.ds(c * half, half)]
                _remote(landed, landed, send_sems.at[3 * e + j], recv_sems.at[3 * e + j], sibling).wait_recv()
                cp = _remote(landed, landed, send_sems.at[fwd + 3 * e + j], recv_sems.at[fwd + 3 * e + j], sibling)
                cp.start()
                started.append(cp)
        for e in range(nb, n):
            for j, chip in enumerate(chips):
                landed = outs[e].at[2 * chip[0] + chip[1]]
                _remote(landed, landed, send_sems.at[3 * e + j], recv_sems.at[3 * e + j], sibling).wait_recv()
        for e in range(nb):
            half = ins[e].shape[0] // 2
            for j, chip in enumerate(chips):
                passed = outs[e].at[2 * chip[0] + chip[1], pl.ds((1 - c) * half, half)]
                _remote(passed, passed, send_sems.at[fwd + 3 * e + j], recv_sems.at[fwd + 3 * e + j],
                        sibling).wait_recv()
        for cp in started:
            cp.wait_send()

    out_shape = [jax.ShapeDtypeStruct((N_CHIPS,) + a.shape, a.dtype) for a in arrays]
    gathered = _comm_call(body, name, arrays, out_shape, 3 * n + 3 * nb, 0)
    me = 2 * lax.axis_index("x") + lax.axis_index("y")
    return [lax.dynamic_update_slice(g, a[None], (me,) + (0,) * a.ndim) for g, a in zip(gathered, arrays)]


def _swap_halves(name, grads):
    n = len(grads)

    def body(*refs):
        ins, outs = refs[:n], refs[n:2 * n]
        send_sems, recv_sems, _ = refs[2 * n:]
        x, y, c = _place()
        cps = []
        for e in range(n):
            half = ins[e].shape[1] // 2
            src = ins[e].at[pl.ds(0, N_CHIPS), pl.ds((1 - c) * half, half)]
            cp = _remote(src, outs[e], send_sems.at[e], recv_sems.at[e], (x, y, 1 - c))
            cp.start()
            cps.append(cp)
        for cp in cps:
            cp.wait()

    out_shape = [jax.ShapeDtypeStruct((N_CHIPS, g.shape[1] // 2, g.shape[2]), g.dtype) for g in grads]
    return _comm_call(body, name, list(grads), out_shape, n, 0)


def _scatter_chips(name, parts):
    n = len(parts)

    def body(*refs):
        ins, outs = refs[:n], refs[n:2 * n]
        send_sems, recv_sems, local_sems = refs[2 * n:]
        x, y, c = _place()
        me = 2 * x + y
        cps = []
        for e in range(n):
            cp = pltpu.make_async_copy(ins[e].at[me], outs[e].at[3], local_sems.at[e])
            cp.start()
            cps.append(cp)
            for j, chip in enumerate(_other_chips(x, y)):
                cp = _remote(ins[e].at[2 * chip[0] + chip[1]], outs[e].at[j], send_sems.at[3 * e + j],
                             recv_sems.at[3 * e + j], (*chip, c))
                cp.start()
                cps.append(cp)
        for cp in cps:
            cp.wait()

    out_shape = [jax.ShapeDtypeStruct(p.shape, p.dtype) for p in parts]
    return _comm_call(body, name, list(parts), out_shape, 3 * n, n)


def _join_halves(name, halves, places, out_shapes):
    n, n_out = len(halves), len(out_shapes)

    def body(*refs):
        ins, outs = refs[:n], refs[n:n + n_out]
        send_sems, recv_sems, local_sems = refs[n + n_out:]
        x, y, c = _place()
        cps = []
        for e in range(n):
            o, idx = places[e]
            half = ins[e].shape[0]
            dst = outs[o].at[(*idx, pl.ds(c * half, half))]
            cp = _remote(ins[e], dst, send_sems.at[e], recv_sems.at[e], (x, y, 1 - c))
            cp.start()
            cps.append(cp)
        for e, cp in enumerate(cps):
            o, idx = places[e]
            half = ins[e].shape[0]
            landed = outs[o].at[(*idx, pl.ds((1 - c) * half, half))]
            cp.wait_send()
            _remote(ins[e], landed, send_sems.at[e], recv_sems.at[e], (x, y, 1 - c)).wait_recv()

    joined = list(_comm_call(body, name, list(halves), list(out_shapes), n, 0))
    c = lax.axis_index("c")
    for e, hv in enumerate(halves):
        o, idx = places[e]
        start = tuple(idx) + (c * hv.shape[0], 0)
        joined[o] = lax.dynamic_update_slice(joined[o], hv.reshape((1,) * len(idx) + hv.shape), start)
    return joined


def _spread_all(name, per_chip, everywhere):
    def body(pc_ref, ev_ref, pc_out, ev_out, send_sems, recv_sems, local_sems):
        x, y, c = _place()
        me = 4 * x + 2 * y + c
        cps = [pltpu.make_async_copy(pc_ref.at[2 * x + y], pc_out.at[me], local_sems.at[0]),
               pltpu.make_async_copy(ev_ref, ev_out.at[me], local_sems.at[1])]
        for f in range(1, 8):
            fx, fy, fc = f // 4, (f // 2) % 2, f % 2
            tx = 1 - x if fx else x
            ty = 1 - y if fy else y
            tc = 1 - c if fc else c
            cps.append(_remote(pc_ref.at[2 * tx + ty], pc_out.at[me], send_sems.at[2 * f], recv_sems.at[2 * f],
                               (tx, ty, tc)))
            cps.append(_remote(ev_ref, ev_out.at[me], send_sems.at[2 * f + 1], recv_sems.at[2 * f + 1],
                               (tx, ty, tc)))
        for cp in cps:
            cp.start()
        for cp in cps:
            cp.wait()

    out_shape = [jax.ShapeDtypeStruct((8,) + per_chip.shape[1:], F32), jax.ShapeDtypeStruct((8,) + everywhere.shape, F32)]
    return _comm_call(body, name, [per_chip, everywhere], out_shape, 16, 2)


def _row_tile(rows, width, itemsize, n_arrays):
    tm = 1
    while rows % (tm * 2) == 0 and (tm * 2) * width * itemsize * n_arrays * 2 <= 24 * 1024 * 1024 and tm * 2 <= 1024:
        tm *= 2
    return tm


def _add_own_half(name, grad, swapped, c_arr):
    ns, R, C = grad.shape
    half = R // 2
    th = _row_tile(half, C, 2, 3)
    g4 = grad.reshape(ns, 2, half, C)

    def body(c_ref, g_ref, s_ref, o_ref):
        o_ref[...] = (g_ref[...].astype(F32) + s_ref[...].astype(F32)).astype(o_ref.dtype)

    return pl.pallas_call(
        body, name=name,
        grid_spec=pltpu.PrefetchScalarGridSpec(
            num_scalar_prefetch=1, grid=(ns, half // th),
            in_specs=[pl.BlockSpec((None, None, th, C), lambda k, i, c_ref: (k, c_ref[0], i, 0)),
                      pl.BlockSpec((None, th, C), lambda k, i, c_ref: (k, i, 0))],
            out_specs=pl.BlockSpec((None, th, C), lambda k, i, c_ref: (k, i, 0))),
        out_shape=jax.ShapeDtypeStruct((ns, half, C), grad.dtype),
        compiler_params=pltpu.CompilerParams(dimension_semantics=("parallel", "parallel"), vmem_limit_bytes=VMEM_LIMIT),
    )(c_arr, g4, swapped)


def _sum_slots(name, slots, order, out_dtype=F32):
    n, rows, C = slots.shape
    th = _row_tile(rows, C, 4, n + 1)

    def body(s_ref, o_ref):
        acc = s_ref[order[0]].astype(F32)
        for k in order[1:]:
            acc = acc + s_ref[k].astype(F32)
        o_ref[...] = acc.astype(o_ref.dtype)

    return pl.pallas_call(
        body, name=name, grid=(rows // th,),
        in_specs=[pl.BlockSpec((n, th, C), lambda i: (0, i, 0))],
        out_specs=pl.BlockSpec((th, C), lambda i: (i, 0)),
        out_shape=jax.ShapeDtypeStruct((rows, C), out_dtype),
        compiler_params=pltpu.CompilerParams(dimension_semantics=("parallel",), vmem_limit_bytes=VMEM_LIMIT),
    )(slots)


def _adamw(name, g, w, m, v):
    rows, C = g.shape
    tm = _row_tile(rows, C, 4, 7)
    c1 = 1.0 - ADAM_B1 ** ADAM_STEP
    c2 = 1.0 - ADAM_B2 ** ADAM_STEP

    def fn(g, w, m, v):
        m = ADAM_B1 * m + (1.0 - ADAM_B1) * g
        v = ADAM_B2 * v + (1.0 - ADAM_B2) * (g * g)
        delta = -ADAM_LR * ((m / c1) / (jnp.sqrt(v / c2) + ADAM_EPS) + ADAM_WD * w)
        return delta, m, v

    return _rowwise(fn, name, rows, tm, [('row', g), ('row', w), ('row', m), ('row', v)],
                    [(C, F32), (C, F32), (C, F32)])


def _tm(S, width, n_arrays):
    return _row_tile(S, width, 4, n_arrays)


def _rms_fwd(name, h, gain, out_dtype):
    S, D = h.shape

    def fn(hb, g):
        r = lax.rsqrt(jnp.mean(hb * hb, axis=-1, keepdims=True) + RMS_EPS)
        return ((hb * r) * g,)

    return _rowwise(fn, name, S, _tm(S, D, 4), [('row', h), ('full', gain)], [(D, out_dtype)])[0]


def _rms_bwd(name, h_in, gain, dn, dh):
    S, D = h_in.shape

    def fn(hb, g, dnb, dhb):
        r = lax.rsqrt(jnp.mean(hb * hb, axis=-1, keepdims=True) + RMS_EPS)
        xh = hb * r
        dnb = dnb.astype(F32)
        dxh = dnb * g
        dx = r * (dxh - xh * jnp.mean(dxh * xh, axis=-1, keepdims=True))
        return dhb + dx, jnp.sum(dnb * xh, axis=0, keepdims=True)

    return _rowwise(fn, name, S, _tm(S, D, 8), [('row', h_in), ('full', gain), ('row', dn), ('row', dh)],
                    [(D, F32)], [(1, D)])


def _ffn_fwd(h, gain, wg, wu, wd):
    S, D = h.shape
    n = _rms_fwd("rms_fwd_bf16", h, gain, BF16)
    g = _mm_nn(n, wg, "ffn_up", BF16)
    u = _mm_nn(n, wu, "ffn_up", BF16)
    FF = g.shape[1]

    def fn(gb, ub):
        return (_silu(gb.astype(F32)) * ub.astype(F32),)

    act = _rowwise(fn, "ffn_act", S, _tm(S, FF, 6), [('row', g), ('row', u)], [(FF, BF16)])[0]
    return _mm_nn(act, wd, "ffn_down", F32, res=h, scale=0.5), (h, n, g, u)


def _ffn_bwd(dh, cache, gain, wg, wu, wd):
    h, n, g, u = cache
    S, FF = g.shape
    dhb = dh.astype(BF16)
    dact = _mm_nt(dhb, wd, "ffn_dact", BF16, scale=0.5)

    def fn(gb, ub, db):
        gb, ub, db = gb.astype(F32), ub.astype(F32), db.astype(F32)
        sg = _sigmoid(gb)
        sl = gb * sg
        return db * ub * (sg * (1.0 + gb * (1.0 - sg))), db * sl, sl * ub

    dg, du, act = _rowwise(fn, "ffn_act_bwd", S, _tm(S, FF, 10), [('row', g), ('row', u), ('row', dact)],
                           [(FF, BF16), (FF, BF16), (FF, BF16)])
    dwd = _mm_tn(act, dhb, wd, "ffn_dwd", BF16, scale=0.5)
    dwg = _mm_tn(n, dg, wg, "ffn_dwup", BF16)
    dwu = _mm_tn(n, du, wu, "ffn_dwup", BF16)
    dn = _mm_nt(dg, wg, "ffn_dn", F32)
    dn = _mm_nt(du, wu, "ffn_dn_acc", F32, res=dn)
    dh2, dgain = _rms_bwd("rms_bwd", h, gain, dn, dh)
    return dh2, dgain, dwg, dwu, dwd


def _ple_fwd(h, gain, pb, wgate, wproj):
    S, D = h.shape
    n = _rms_fwd("rms_fwd_bf16", h, gain, BF16)
    pre = _mm_nn(n, wgate, "ple_gate", F32)
    e = _mm_nn(pb, wproj, "ple_proj", F32)

    def fn(hb, pr, eb):
        return (hb + _sigmoid(pr) * eb,)

    h2 = _rowwise(fn, "ple_add", S, _tm(S, D, 6), [('row', h), ('row', pre), ('row', e)], [(D, F32)])[0]
    return h2, (h, n, pre, e)


def _ple_bwd(dh, cache, gain, pb, wgate, wproj):
    h, n, pre, e = cache
    S, D = h.shape

    def fn(db, pr, eb):
        gt = _sigmoid(pr)
        return db * eb * gt * (1.0 - gt), db * gt

    dpre, de = _rowwise(fn, "ple_bwd", S, _tm(S, D, 6), [('row', dh), ('row', pre), ('row', e)],
                        [(D, BF16), (D, BF16)])
    dwproj = _mm_tn(pb, de, wproj, "ple_dwproj", F32)
    dwgate = _mm_tn(n, dpre, wgate, "ple_dwgate", BF16)
    dn = _mm_nt(dpre, wgate, "ple_dn", F32)
    dh2, dgain = _rms_bwd("rms_bwd", h, gain, dn, dh)
    return dh2, dgain, dwgate, dwproj


def _even_fwd(h, gain, w_in, w_out, vgain, ws, bs_t, onorm, lb):
    S, D = h.shape
    aw = D // 2
    nh = aw // B_HEAD
    hn = _rms_fwd("rms_fwd_bf16", h, gain, BF16)
    proj = _mm_nn(hn, w_in, "even_in", F32)
    a_out = _gmlp_fwd("gmlp_fwd", proj, aw, vgain, ws, bs_t)

    def pre(bq, bf, lbv):
        f = lbv + (1.0 - lbv) * _sigmoid(bf)
        return _silu(bq), jnp.maximum(f, B_MIN_F), 1.0 - f

    q, w, k = _rowwise(pre, "hgrn_pre", S, _tm(S, aw, 8), [('col', proj, aw, 2), ('col', proj, aw, 3), ('full', lb)],
                       [(aw, F32), (aw, F32), (aw, F32)])
    wk, kk, qk = _to_k(w, nh, B_HEAD), _to_k(k, nh, B_HEAD), _to_k(q, nh, B_HEAD)
    vrow = _to_row(proj[:, 4 * aw:5 * aw], nh, B_HEAD)
    yrow, hist, _ = _scan_fwd("hgrn_scan_fwd", nh, wk, kk, qk, vrow)
    o = _from_row(yrow, nh, B_HEAD)

    def post(ob, bg, on):
        rs = lax.rsqrt(_segsum(ob * ob, B_HEAD) * (1.0 / B_HEAD) + RMS_EPS)
        return ((ob * rs * on) * _silu(bg),)

    b_out = _rowwise(post, "hgrn_post", S, _tm(S, aw, 8), [('row', o), ('col', proj, aw, 5), ('full', onorm)],
                     [(aw, BF16)])[0]
    cat = jnp.concatenate([a_out, b_out], axis=1)
    h2 = _mm_nn(cat, w_out, "even_out", F32, res=h)
    return h2, (h, hn, proj, wk, kk, qk, vrow, hist, o, cat)


def _even_bwd(dh, cache, gain, w_in, w_out, vgain, ws, bs_t, onorm, lb):
    h, hn, proj, wk, kk, qk, vrow, hist, o, cat = cache
    S, D = h.shape
    aw = D // 2
    nh = aw // B_HEAD
    dhb = dh.astype(BF16)
    dw_out = _mm_tn(cat, dhb, w_out, "even_dwout", BF16)
    dcat = _mm_nt(dhb, w_out, "even_dcat", F32)

    def post_bwd(ob, bg, on, db):
        rs = lax.rsqrt(_segsum(ob * ob, B_HEAD) * (1.0 / B_HEAD) + RMS_EPS)
        xh = ob * rs
        dy = db * _silu(bg)
        dbg = db * (xh * on) * _dsilu(bg)
        dxh = dy * on
        do = rs * (dxh - xh * (_segsum(dxh * xh, B_HEAD) * (1.0 / B_HEAD)))
        return do, dbg, jnp.sum(dy * xh, axis=0, keepdims=True)

    do, dbg, donorm = _rowwise(post_bwd, "hgrn_post_bwd", S, _tm(S, aw, 10),
                               [('row', o), ('col', proj, aw, 5), ('full', onorm), ('col', dcat, aw, 1)],
                               [(aw, F32), (aw, BF16)], [(1, aw)])
    dwk, dkk, dqk, dvrow = _scan_bwd("hgrn_scan_bwd", nh, wk, kk, qk, vrow, hist, _to_row(do, nh, B_HEAD))
    dq, dw, dk = _from_k(dqk, nh, B_HEAD), _from_k(dwk, nh, B_HEAD), _from_k(dkk, nh, B_HEAD)
    dbi = _from_row(dvrow, nh, B_HEAD).astype(BF16)

    def pre_bwd(bq, bf, lbv, dqb, dwb, dkb):
        sig = _sigmoid(bf)
        f = lbv + (1.0 - lbv) * sig
        df = jnp.where(f > B_MIN_F, dwb, 0.0) - dkb
        return dqb * _dsilu(bq), df * (1.0 - lbv) * sig * (1.0 - sig), jnp.sum(df * (1.0 - sig), axis=0, keepdims=True)

    dbq, dbf, dlb = _rowwise(pre_bwd, "hgrn_pre_bwd", S, _tm(S, aw, 12),
                             [('col', proj, aw, 2), ('col', proj, aw, 3), ('full', lb), ('row', dq), ('row', dw),
                              ('row', dk)], [(aw, BF16), (aw, BF16)], [(1, aw)])
    dau, dav, dws, dbs_t, dvgain = _gmlp_bwd("gmlp_bwd", proj, dcat, 0, aw, vgain, ws, bs_t)
    dproj = jnp.concatenate([dau, dav, dbq, dbf, dbi, dbg], axis=1)
    dw_in = _mm_tn(hn, dproj, w_in, "even_dwin", BF16)
    dn = _mm_nt(dproj, w_in, "even_dn", F32)
    dh2, dgain = _rms_bwd("rms_bwd", h, gain, dn, dh)
    return dh2, dict(gain=dgain, w_in=dw_in, w_out=dw_out, vgain=dvgain, ws=dws, bs_t=dbs_t, onorm=donorm, lb=dlb)


def _rwkv_prep(r, k, v0, wpl, apl, w0, a0, kkg, kag, svl=None, vf=None, v0p=None):
    wp = w0 + wpl
    w = -_softplus(-wp) - 0.5
    ew = jnp.exp(w)
    decay = jnp.exp(-ew)
    a = _sigmoid(a0 + apl)
    if svl is not None:
        sv = _sigmoid(v0p + svl)
        v = v0 + (vf - v0) * sv
    else:
        sv, v = None, v0
    kkp = k * kkg
    nrm = jnp.sqrt(_segsum(kkp * kkp, C_HEAD))
    inv = 1.0 / jnp.maximum(nrm, 1e-12)
    kk = kkp * inv
    k2 = k * (1.0 + (a - 1.0) * kag)
    return dict(wp=wp, ew=ew, decay=decay, a=a, sv=sv, v=v, kkp=kkp, nrm=nrm, inv=inv, kk=kk, k2=k2)


def _rwkv_post(y, r, k2, v, gn_g, gn_b, rk):
    mu = _segsum(y, C_HEAD) * (1.0 / C_HEAD)
    yc = y - mu
    rstd = lax.rsqrt(_segsum(yc * yc, C_HEAD) * (1.0 / C_HEAD) + C_GN_EPS)
    yh = yc * rstd
    s = _segsum(r * k2 * rk, C_HEAD)
    return yh, rstd, s, yh * gn_g + gn_b + s * v


def _rwkv_fwd(h, gain, P, vfirst):
    S, D = h.shape
    nh = D // C_HEAD
    vres = vfirst is not None
    hn = _rms_fwd("rms_fwd_f32", h, gain, F32)
    hs = jnp.concatenate([jnp.zeros((1, D), F32), hn[:-1]], axis=0)

    def mixf(x, xs, m0, m1, m2, m3, m4, m5):
        xx = xs - x
        return tuple(x + xx * m for m in (m0, m1, m2, m3, m4, m5))

    xr, xw, xk, xv, xa, xg = _rowwise(mixf, "rwkv_mix", S, _tm(S, D, 8),
                                      [('row', hn), ('row', hs)] + [('full', m) for m in P['mix']], [(D, BF16)] * 6)
    r = _mm_nn(xr, P['wr'], "rwkv_proj", F32)
    k = _mm_nn(xk, P['wk'], "rwkv_proj", F32)
    v0 = _mm_nn(xv, P['wv'], "rwkv_proj", F32)
    lw1 = _mm_nn(xw, P['w1'], "rwkv_lora_in", F32)
    la1 = _mm_nn(xa, P['a1'], "rwkv_lora_in", F32)
    lg1 = _mm_nn(xg, P['g1'], "rwkv_lora_in_g", F32)
    ins = [('row', lw1), ('row', la1), ('row', lg1)]
    outs = [(lw1.shape[1], BF16), (la1.shape[1], BF16), (lg1.shape[1], BF16)]
    if vres:
        lv1 = _mm_nn(xv, P['v1'], "rwkv_lora_in_v", F32)
        ins.append(('row', lv1))
        outs.append((lv1.shape[1], BF16))

    def lora_act(*xs):
        res = [jnp.tanh(xs[0]), xs[1], _sigmoid(xs[2])]
        return tuple(res + list(xs[3:]))

    acts = _rowwise(lora_act, "rwkv_lora_act", S, _tm(S, 1024, 4), ins, outs)
    tw, la1b, sg = acts[0], acts[1], acts[2]
    wpl = _mm_nn(tw, P['w2'], "rwkv_lora_out", F32)
    apl = _mm_nn(la1b, P['a2'], "rwkv_lora_out", F32)
    g = _mm_nn(sg, P['g2'], "rwkv_lora_out_g", F32)
    prep_ins = [('row', r), ('row', k), ('row', v0), ('row', wpl), ('row', apl),
                ('full', P['w0']), ('full', P['a0']), ('full', P['kk']), ('full', P['ka'])]
    svl = lv1b = None
    if vres:
        lv1b = acts[3]
        svl = _mm_nn(lv1b, P['v2'], "rwkv_lora_out_v", F32)
        prep_ins += [('row', svl), ('row', vfirst), ('full', P['v0'])]

    def prep(*xs):
        q = _rwkv_prep(*xs)
        return q['decay'], q['k2'], q['v'], -q['kk'], q['kk'] * q['a']

    decay, k2, v, av, bv = _rowwise(prep, "rwkv_prep", S, _tm(S, D, 24), prep_ins, [(D, F32)] * 5)
    tk = functools.partial(_to_k, n_heads=nh, dk=C_HEAD)
    wk_, kk_, rk_, ak_, bk_ = tk(decay), tk(k2), tk(r), tk(av), tk(bv)
    vrow = _to_row(v, nh, C_HEAD)
    yrow, hist, sarow = _scan_fwd("rwkv_scan_fwd", nh, wk_, kk_, rk_, vrow, ak_, bk_)
    y = _from_row(yrow, nh, C_HEAD)

    def post(yb, rb, k2b, vb, gb, gn_g, gn_b, rkf):
        return (_rwkv_post(yb, rb, k2b, vb, gn_g, gn_b, rkf)[3] * gb,)

    zg = _rowwise(post, "rwkv_post", S, _tm(S, D, 16),
                  [('row', y), ('row', r), ('row', k2), ('row', v), ('row', g),
                   ('full', P['gn_g']), ('full', P['gn_b']), ('full', P['rk'])], [(D, BF16)])[0]
    h2 = _mm_nn(zg, P['wo'], "rwkv_out", F32, res=h)
    cache = dict(h=h, hn=hn, hs=hs, x=(xr, xw, xk, xv, xa, xg), r=r, k=k, v0=v0, tw=tw, la1b=la1b, sg=sg, lv1b=lv1b,
                 wpl=wpl, apl=apl, svl=svl, g=g, k2=k2, v=v, scan=(wk_, kk_, rk_, vrow, ak_, bk_, hist, sarow), y=y,
                 zg=zg, vfirst=vfirst)
    return h2, cache, (v if not vres else vfirst)


def _rwkv_bwd(dh, cache, gain, P, dvfirst_in):
    c = cache
    h = c['h']
    S, D = h.shape
    nh = D // C_HEAD
    vres = c['vfirst'] is not None
    xr, xw, xk, xv, xa, xg = c['x']
    dhb = dh.astype(BF16)
    dwo = _mm_tn(c['zg'], dhb, P['wo'], "rwkv_dwo", BF16)
    dzg = _mm_nt(dhb, P['wo'], "rwkv_dzg", F32)

    def post_bwd(dzgb, yb, rb, k2b, vb, gb, gn_g, gn_b, rkf):
        yh, rstd, s, z = _rwkv_post(yb, rb, k2b, vb, gn_g, gn_b, rkf)
        dz = dzgb * gb
        dyh = dz * gn_g
        m1 = _segsum(dyh, C_HEAD) * (1.0 / C_HEAD)
        m2 = _segsum(dyh * yh, C_HEAD) * (1.0 / C_HEAD)
        dy = rstd * (dyh - m1 - yh * m2)
        ds = _segsum(dz * vb, C_HEAD)
        return (dy, dzgb * z, ds * k2b * rkf, ds * rb * rkf, dz * s,
                jnp.sum(dz * yh, axis=0, keepdims=True), jnp.sum(dz, axis=0, keepdims=True),
                jnp.sum(ds * rb * k2b, axis=0, keepdims=True))

    dy, dgb, dr_b, dk2_b, dv_b, dgn_g, dgn_b, drk = _rowwise(
        post_bwd, "rwkv_post_bwd", S, _tm(S, D, 28),
        [('row', dzg), ('row', c['y']), ('row', c['r']), ('row', c['k2']), ('row', c['v']), ('row', c['g']),
         ('full', P['gn_g']), ('full', P['gn_b']), ('full', P['rk'])],
        [(D, F32), (D, BF16), (D, F32), (D, F32), (D, F32)], [(1, D)] * 3)
    wk_, kk_, rk_, vrow, ak_, bk_, hist, sarow = c['scan']
    dwk, dkk, drk_s, dvrow, dak, dbk = _scan_bwd("rwkv_scan_bwd", nh, wk_, kk_, rk_, vrow, hist,
                                                 _to_row(dy, nh, C_HEAD), ak_, bk_, sarow)
    fk = functools.partial(_from_k, n_heads=nh, dk=C_HEAD)
    ddecay, dk2_s, dr_s, dA, dB = fk(dwk), fk(dkk), fk(drk_s), fk(dak), fk(dbk)
    dv_s = _from_row(dvrow, nh, C_HEAD)
    dr_t, dk2_t, dv_t = dr_s + dr_b, dk2_s + dk2_b, dv_s + dv_b
    if dvfirst_in is not None:
        dv_t = dv_t + dvfirst_in
    ins = [('row', c['r']), ('row', c['k']), ('row', c['v0']), ('row', c['wpl']), ('row', c['apl']),
           ('full', P['w0']), ('full', P['a0']), ('full', P['kk']), ('full', P['ka'])]
    if vres:
        ins += [('row', c['svl']), ('row', c['vfirst']), ('full', P['v0'])]
    n_fwd = len(ins)
    ins += [('row', t) for t in (dr_t, ddecay, dk2_t, dv_t, dA, dB)]

    def prep_bwd(*xs):
        q = _rwkv_prep(*xs[:n_fwd])
        kb, kkg, kag = xs[1], xs[7], xs[8]
        dr, ddec, dk2, dv, dav, dbv = xs[n_fwd:]
        a, kk, kkp, inv = q['a'], q['kk'], q['kkp'], q['inv']
        dkk = dbv * a - dav
        da = dbv * kk + dk2 * kb * kag
        dk = dk2 * (1.0 + (a - 1.0) * kag)
        pr = _segsum(dkk * kkp, C_HEAD)
        dkkp = dkk * inv - jnp.where(q['nrm'] > 1e-12, kkp * pr * inv * inv * inv, 0.0)
        dk = dk + dkkp * kkg
        dap = da * a * (1.0 - a)
        dwp = ddec * q['decay'] * (-q['ew']) * _sigmoid(-q['wp'])
        outs = [dr, dk]
        accs = [jnp.sum(dwp, axis=0, keepdims=True), jnp.sum(dap, axis=0, keepdims=True),
                jnp.sum(dkkp * kb, axis=0, keepdims=True), jnp.sum(dk2 * kb * (a - 1.0), axis=0, keepdims=True)]
        if vres:
            v0b, vfb, sv = xs[2], xs[10], q['sv']
            dsvp = dv * (vfb - v0b) * sv * (1.0 - sv)
            outs += [dv * (1.0 - sv), dwp, dap, dsvp, dv * sv]
            accs.append(jnp.sum(dsvp, axis=0, keepdims=True))
        else:
            outs += [dv, dwp, dap]
        return tuple(outs + accs)

    outs = [(D, BF16)] * 5 + ([(D, BF16), (D, F32)] if vres else [])
    res = _rowwise(prep_bwd, "rwkv_prep_bwd", S, _tm(S, D, 40), ins, outs, [(1, D)] * (5 if vres else 4))
    drb, dkb, dv0b, dwpb, dapb = res[:5]
    dvfirst_out = res[6] if vres else None
    accs = res[len(outs):]
    G = dict(wo=dwo, w0=accs[0], a0=accs[1], kk=accs[2], ka=accs[3], gn_g=dgn_g, gn_b=dgn_b, rk=drk)
    G['w2'] = _mm_tn(c['tw'], dwpb, P['w2'], "rwkv_dlora_out", F32)
    G['a2'] = _mm_tn(c['la1b'], dapb, P['a2'], "rwkv_dlora_out", F32)
    G['g2'] = _mm_tn(c['sg'], dgb, P['g2'], "rwkv_dlora_out_g", F32)
    dtw = _mm_nt(dwpb, P['w2'], "rwkv_dlora_mid", F32)
    dla1 = _mm_nt(dapb, P['a2'], "rwkv_dlora_mid", F32)
    dsg = _mm_nt(dgb, P['g2'], "rwkv_dlora_mid_g", F32)
    ins = [('row', dtw), ('row', c['tw']), ('row', dla1), ('row', dsg), ('row', c['sg'])]
    outs = [(dtw.shape[1], BF16), (dla1.shape[1], BF16), (dsg.shape[1], BF16)]
    if vres:
        dsvpb = res[5]
        G['v0'] = accs[4]
        G['v2'] = _mm_tn(c['lv1b'], dsvpb, P['v2'], "rwkv_dlora_out_v", F32)
        dlv1 = _mm_nt(dsvpb, P['v2'], "rwkv_dlora_mid_v", F32)
        ins.append(('row', dlv1))
        outs.append((dlv1.shape[1], BF16))

    def lora_act_bwd(dtwb, twb, dla1b_, dsgb, sgb, *rest):
        twb, sgb = twb.astype(F32), sgb.astype(F32)
        return tuple([dtwb * (1.0 - twb * twb), dla1b_, dsgb * sgb * (1.0 - sgb)] + list(rest))

    acts = _rowwise(lora_act_bwd, "rwkv_lora_act_bwd", S, _tm(S, 1024, 6), ins, outs)
    dlw1b, dla1b, dlg1b = acts[0], acts[1], acts[2]
    G['w1'] = _mm_tn(xw, dlw1b, P['w1'], "rwkv_dlora_in", F32)
    G['a1'] = _mm_tn(xa, dla1b, P['a1'], "rwkv_dlora_in", F32)
    G['g1'] = _mm_tn(xg, dlg1b, P['g1'], "rwkv_dlora_in_g", F32)
    G['wr'] = _mm_tn(xr, drb, P['wr'], "rwkv_dwproj", BF16)
    G['wk'] = _mm_tn(xk, dkb, P['wk'], "rwkv_dwproj", BF16)
    G['wv'] = _mm_tn(xv, dv0b, P['wv'], "rwkv_dwproj", BF16)
    dxw = _mm_nt(dlw1b, P['w1'], "rwkv_dx_lora", F32)
    dxa = _mm_nt(dla1b, P['a1'], "rwkv_dx_lora", F32)
    dxg = _mm_nt(dlg1b, P['g1'], "rwkv_dx_lora_g", F32)
    dxr = _mm_nt(drb, P['wr'], "rwkv_dx", F32)
    dxk = _mm_nt(dkb, P['wk'], "rwkv_dx", F32)
    dxv = _mm_nt(dv0b, P['wv'], "rwkv_dx", F32)
    if vres:
        G['v1'] = _mm_tn(xv, acts[3], P['v1'], "rwkv_dlora_in_v", F32)
        dxv = _mm_nt(acts[3], P['v1'], "rwkv_dx_lora_v", F32, res=dxv)

    def mix_bwd(x, xs, m0, m1, m2, m3, m4, m5, d0, d1, d2, d3, d4, d5):
        xx = xs - x
        ds_ = (d0, d1, d2, d3, d4, d5)
        dxx = d0 * m0 + d1 * m1 + d2 * m2 + d3 * m3 + d4 * m4 + d5 * m5
        dsum = d0 + d1 + d2 + d3 + d4 + d5
        return tuple([dsum - dxx, dxx] + [jnp.sum(d * xx, axis=0, keepdims=True) for d in ds_])

    res = _rowwise(mix_bwd, "rwkv_mix_bwd", S, _tm(S, D, 24),
                   [('row', c['hn']), ('row', c['hs'])] + [('full', m) for m in P['mix']]
                   + [('row', d) for d in (dxr, dxw, dxk, dxv, dxa, dxg)], [(D, F32), (D, F32)], [(1, D)] * 6)
    dx_here, dxs = res[0], res[1]
    G['mix'] = res[2:]
    dhn = dx_here + jnp.concatenate([dxs[1:], jnp.zeros((1, D), F32)], axis=0)
    dh2, G['gain'] = _rms_bwd("rms_bwd", h, gain, dhn, dh)
    return dh2, G, dvfirst_out


def _loss_bwd(h, target, gain):
    S, D = h.shape

    def fn(hb, tb, g):
        r = lax.rsqrt(jnp.mean(hb * hb, axis=-1, keepdims=True) + RMS_EPS)
        xh = hb * r
        e = xh * g - tb
        dy = e * (1.0 / D)
        dxh = dy * g
        dx = r * (dxh - xh * jnp.mean(dxh * xh, axis=-1, keepdims=True))
        part = jnp.sum(jnp.sum(e * e, axis=-1, keepdims=True), axis=0, keepdims=True) * (0.5 / D)
        return dx, jnp.sum(dy * xh, axis=0, keepdims=True), jnp.broadcast_to(part, (1, LANES))

    dh, dgain, part = _rowwise(fn, "loss", S, _tm(S, D, 8), [('row', h), ('row', target), ('full', gain)],
                               [(D, F32)], [(1, D), (1, LANES)])
    return part[0, 0], dh, dgain


WEIGHTS = ['norms', 'final_norm', 'ffn_wg', 'ffn_wu', 'ffn_wd', 'ple_wp', 'ple_wg', 'e_w_in', 'e_w_out', 'a_vnorm',
           'a_ws', 'a_bs', 'b_onorm', 'b_lb_logits', 'c_mix', 'c_wr', 'c_wk', 'c_wv', 'c_wo', 'c_w0', 'c_w1', 'c_w2',
           'c_a0', 'c_a1', 'c_a2', 'c_g1', 'c_g2', 'c_kk', 'c_ka', 'c_rk', 'c_gn_g', 'c_gn_b', 'c_v0', 'c_v1', 'c_v2']
BIG = {'ffn_wg': 'col', 'ffn_wu': 'col', 'ffn_wd': 'row', 'ple_wg': 'row', 'e_w_in': 'col', 'e_w_out': 'row',
       'c_wr': 'row', 'c_wk': 'row', 'c_wv': 'row', 'c_wo': 'row'}
SMALL = ['norms', 'ple_wp', 'c_mix', 'c_w0', 'c_w1', 'c_w2', 'c_a0', 'c_a1', 'c_a2', 'c_g1', 'c_g2', 'c_kk', 'c_ka',
         'c_gn_g', 'c_gn_b', 'c_v0', 'c_v1', 'c_v2']
SMALL_MM = {'ple_wp': 'col', 'c_w1': 'row', 'c_w2': 'col', 'c_a1': 'row', 'c_a2': 'col', 'c_g1': 'row', 'c_g2': 'col',
            'c_v1': 'row', 'c_v2': 'col'}
REP = ['final_norm', 'a_vnorm', 'a_ws', 'a_bs', 'b_onorm', 'b_lb_logits', 'c_rk']
PACK_QUANTUM = 1024 * LANES


def _pack(arrays, lead=0):
    lead_shape = arrays[0].shape[:lead]
    flat = jnp.concatenate([a.reshape(lead_shape + (-1,)).astype(F32) for a in arrays], axis=-1)
    pad = (-flat.shape[-1]) % PACK_QUANTUM
    if pad:
        flat = jnp.concatenate([flat, jnp.zeros(lead_shape + (pad,), F32)], axis=-1)
    return flat.reshape(lead_shape + (-1, LANES))


def _unpack(packed, names, shapes, lead=0):
    lead_shape = packed.shape[:lead]
    flat = packed.reshape(lead_shape + (-1,))
    out, off = {}, 0
    for n, s in zip(names, shapes):
        size = 1
        for d in s:
            size *= d
        out[n] = flat[..., off:off + size].reshape(lead_shape + tuple(s))
        off += size
    return out


def _full_vec(g):
    return jnp.moveaxis(g, 0, -2).reshape(g.shape[1:-1] + (N_CHIPS * g.shape[-1],))


def _vec_shards(g):
    return jnp.moveaxis(g.reshape(g.shape[:-1] + (N_CHIPS, g.shape[-1] // N_CHIPS)), -2, 0)


def _lower_bounds(logits):
    probs = jax.nn.softmax(logits.astype(F32), axis=0)
    return jnp.cumsum(probs, axis=0) - probs[0]


def kernel(x, p, norms, final_norm, ffn_wg, ffn_wu, ffn_wd, ple_wp, ple_wg, e_w_in, e_w_out, a_vnorm, a_ws, a_bs, b_onorm, b_lb_logits, c_mix, c_wr, c_wk, c_wv, c_wo, c_w0, c_w1, c_w2, c_a0, c_a1, c_a2, c_g1, c_g2, c_kk, c_ka, c_rk, c_gn_g, c_gn_b, c_v0, c_v1, c_v2, loss_target, m_norms, m_final_norm, m_ffn_wg, m_ffn_wu, m_ffn_wd, m_ple_wp, m_ple_wg, m_e_w_in, m_e_w_out, m_a_vnorm, m_a_ws, m_a_bs, m_b_onorm, m_b_lb_logits, m_c_mix, m_c_wr, m_c_wk, m_c_wv, m_c_wo, m_c_w0, m_c_w1, m_c_w2, m_c_a0, m_c_a1, m_c_a2, m_c_g1, m_c_g2, m_c_kk, m_c_ka, m_c_rk, m_c_gn_g, m_c_gn_b, m_c_v0, m_c_v1, m_c_v2, v_norms, v_final_norm, v_ffn_wg, v_ffn_wu, v_ffn_wd, v_ple_wp, v_ple_wg, v_e_w_in, v_e_w_out, v_a_vnorm, v_a_ws, v_a_bs, v_b_onorm, v_b_lb_logits, v_c_mix, v_c_wr, v_c_wk, v_c_wv, v_c_wo, v_c_w0, v_c_w1, v_c_w2, v_c_a0, v_c_a1, v_c_a2, v_c_g1, v_c_g2, v_c_kk, v_c_ka, v_c_rk, v_c_gn_g, v_c_gn_b, v_c_v0, v_c_v1, v_c_v2):
    A = dict(locals())
    assert x.shape[0] == 1, "one example per device"

    big_names = list(BIG)
    gathered = _all_gather("gather_weights", [A[n].astype(BF16).reshape(-1, A[n].shape[-1]) for n in big_names],
                           [_pack([A[n] for n in SMALL])])
    GB = {n: gathered[i].reshape((N_CHIPS,) + A[n].shape) for i, n in enumerate(big_names)}
    GS = _unpack(gathered[-1], SMALL, [A[n].shape for n in SMALL], lead=1)

    part, grad_x, big_entries, sm_pack, rep_pack = _local_step(A, GB, GS)
    loss = lax.psum(part, ("x", "y", "c"))

    c_arr = lax.axis_index("c").astype(jnp.int32).reshape(1)
    grads = [e[2] for e in big_entries]
    swapped = _swap_halves("reduce_swap_cores", grads)
    parts = [_add_own_half("reduce_add_cores", g, s, c_arr) for g, s in zip(grads, swapped)]
    slots = _scatter_chips("reduce_scatter_chips", parts)
    halves = [_sum_slots("reduce_sum_chips", s, (3, 0, 1, 2)) for s in slots]
    places = [(big_names.index(n), idx) for n, idx, _ in big_entries]
    big_grads = _join_halves("reduce_join_cores", halves, places,
                             [jax.ShapeDtypeStruct(A[n].shape, F32) for n in big_names])
    sm_slots, rep_slots = _spread_all("reduce_small", sm_pack, rep_pack)
    sm_grad = _sum_slots("reduce_sum_small", sm_slots, tuple(range(8)))
    rep_grad = _sum_slots("reduce_sum_small", rep_slots, tuple(range(8)))

    outs = {}
    for o, n in enumerate(big_names):
        shp, C = A[n].shape, A[n].shape[-1]
        g = big_grads[o]
        d, nm, nv = _adamw("adamw", g.reshape(-1, C), A[n].reshape(-1, C), A['m_' + n].reshape(-1, C),
                           A['v_' + n].reshape(-1, C))
        outs[n] = (g, d.reshape(shp), nm.reshape(shp), nv.reshape(shp))
    for names, g in ((SMALL, sm_grad), (REP, rep_grad)):
        shapes = [A[n].shape for n in names]
        res = _adamw("adamw_packed", g, _pack([A[n] for n in names]), _pack([A['m_' + n] for n in names]),
                     _pack([A['v_' + n] for n in names]))
        un = [_unpack(t, names, shapes) for t in (g,) + tuple(res)]
        for n in names:
            outs[n] = tuple(u[n] for u in un)
    return (loss, grad_x, *[outs[n][0] for n in WEIGHTS], *[outs[n][1] for n in WEIGHTS],
            *[outs[n][2] for n in WEIGHTS], *[outs[n][3] for n in WEIGHTS])


def _local_step(A, GB, GS):
    x, p, a_vnorm, a_ws, a_bs, b_onorm, b_lb_logits, c_rk = (A[n] for n in (
        'x', 'p', 'a_vnorm', 'a_ws', 'a_bs', 'b_onorm', 'b_lb_logits', 'c_rk'))
    S, D = x.shape[1], x.shape[2]
    depth = A['ffn_wg'].shape[0]
    aw = D // 2
    h = x[0]
    target = A['loss_target'][0]
    final_norm = A['final_norm']
    pb = p[:, 0].astype(BF16)
    GSb = {n: GS[n].astype(BF16) for n in SMALL_MM}

    def bigv(n, *idx):
        return WV(GB[n], idx, BIG[n])

    def smv(n, *idx):
        return WV(GSb[n], idx, SMALL_MM[n])

    def row(v):
        return v.reshape(1, -1)

    vecs = {n: _full_vec(GS[n]) for n in SMALL if n not in SMALL_MM}
    lb_all, lb_vjp = jax.vjp(_lower_bounds, b_lb_logits)

    def rwkv_params(j):
        P = dict(mix=[vecs['c_mix'][j, q:q + 1] for q in range(6)],
                 wr=bigv('c_wr', j), wk=bigv('c_wk', j), wv=bigv('c_wv', j), wo=bigv('c_wo', j),
                 w1=smv('c_w1', j), w2=smv('c_w2', j), a1=smv('c_a1', j), a2=smv('c_a2', j),
                 g1=smv('c_g1', j), g2=smv('c_g2', j),
                 w0=row(vecs['c_w0'][j]), a0=row(vecs['c_a0'][j]), kk=row(vecs['c_kk'][j]), ka=row(vecs['c_ka'][j]),
                 gn_g=row(vecs['c_gn_g'][j]), gn_b=row(vecs['c_gn_b'][j]), rk=c_rk[j].reshape(1, D))
        if j > 0:
            P.update(v0=row(vecs['c_v0'][j - 1]), v1=smv('c_v1', j - 1), v2=smv('c_v2', j - 1))
        return P

    def even_params(i):
        j = i // 2
        return (bigv('e_w_in', j), bigv('e_w_out', j), a_vnorm[j:j + 1], a_ws[j], a_bs[j].T, b_onorm[j:j + 1],
                lb_all[i:i + 1])

    def gain(i, q):
        return row(vecs['norms'][i, q])

    def ffn_views(i, q):
        return bigv('ffn_wg', i, q), bigv('ffn_wu', i, q), bigv('ffn_wd', i, q)

    caches, vfirst = [], None
    for i in range(depth):
        c = {}
        h, c['f1'] = _ffn_fwd(h, gain(i, 0), *ffn_views(i, 0))
        if i % 2 == 0:
            h, c['mix'] = _even_fwd(h, gain(i, 1), *even_params(i))
        else:
            h, c['mix'], vfirst = _rwkv_fwd(h, gain(i, 1), rwkv_params(i // 2), vfirst if i // 2 > 0 else None)
        h, c['f2'] = _ffn_fwd(h, gain(i, 2), *ffn_views(i, 1))
        h, c['ple'] = _ple_fwd(h, gain(i, 3), pb[i], bigv('ple_wg', i), smv('ple_wp', i))
        caches.append(c)

    part, dh, dfinal = _loss_bwd(h, target, final_norm.reshape(1, D))
    big_entries = []
    sm = {n: {} for n in SMALL}
    rep = {n: {} for n in REP}
    dvfirst = None
    for i in reversed(range(depth)):
        j, c = i // 2, caches[i]
        dh, dg, dwgate, dwproj = _ple_bwd(dh, c['ple'], gain(i, 3), pb[i], bigv('ple_wg', i), smv('ple_wp', i))
        sm['norms'][(i, 3)] = dg
        sm['ple_wp'][(i,)] = dwproj
        big_entries.append(('ple_wg', (i,), dwgate))
        for q, key in ((1, 'f2'), (0, 'f1')):
            if key == 'f1':
                if i % 2 == 0:
                    dh, G = _even_bwd(dh, c['mix'], gain(i, 1), *even_params(i))
                    big_entries += [('e_w_in', (j,), G['w_in']), ('e_w_out', (j,), G['w_out'])]
                    rep['a_vnorm'][j], rep['a_ws'][j], rep['a_bs'][j] = G['vgain'][0], G['ws'], G['bs_t'].T
                    rep['b_onorm'][j], rep['b_lb_logits'][i] = G['onorm'][0], G['lb'][0]
                else:
                    dh, G, dvf = _rwkv_bwd(dh, c['mix'], gain(i, 1), rwkv_params(j), dvfirst if j == 0 else None)
                    if dvf is not None:
                        dvfirst = dvf if dvfirst is None else dvfirst + dvf
                    big_entries += [('c_wr', (j,), G['wr']), ('c_wk', (j,), G['wk']), ('c_wv', (j,), G['wv']),
                                    ('c_wo', (j,), G['wo'])]
                    for n in ('w0', 'a0', 'kk', 'ka', 'gn_g', 'gn_b', 'w1', 'w2', 'a1', 'a2', 'g1', 'g2'):
                        sm['c_' + n][(j,)] = G[n]
                    sm['c_mix'][(j,)] = jnp.concatenate(G['mix'], axis=0)
                    rep['c_rk'][j] = G['rk'].reshape(c_rk.shape[1:])
                    if j > 0:
                        for n in ('v0', 'v1', 'v2'):
                            sm['c_' + n][(j - 1,)] = G[n]
                sm['norms'][(i, 1)] = G['gain']
            dh, dg, dwg, dwu, dwd = _ffn_bwd(dh, c[key], gain(i, 2 * q), *ffn_views(i, q))
            sm['norms'][(i, 2 * q)] = dg
            big_entries += [('ffn_wg', (i, q), dwg), ('ffn_wu', (i, q), dwu), ('ffn_wd', (i, q), dwd)]
    grad_x = dh[None]

    def stacked(blocks, lead_shape):
        def rec(prefix, dims):
            if not dims:
                return blocks[prefix]
            return jnp.stack([rec(prefix + (q,), dims[1:]) for q in range(dims[0])], axis=0)
        return rec((), tuple(lead_shape))

    sm_shards = []
    for n in SMALL:
        blk = A[n].shape
        if n in SMALL_MM:
            g = jnp.moveaxis(stacked(sm[n], blk[:-2]), len(blk) - 2, 0)
        else:
            lead = blk[:-1] if n != 'c_mix' else blk[:-2]
            full = stacked(sm[n], lead)
            g = _vec_shards(full.reshape(blk[:-1] + (D,)))
        sm_shards.append(g)
    sm_pack = _pack(sm_shards, lead=1)
    dlb = jnp.stack([rep['b_lb_logits'].get(i, jnp.zeros((aw,), F32)) for i in range(depth)], axis=0)
    rep_grads = dict(final_norm=dfinal[0], a_vnorm=stacked({(k,): v for k, v in rep['a_vnorm'].items()}, a_vnorm.shape[:1]),
                     a_ws=stacked({(k,): v for k, v in rep['a_ws'].items()}, a_ws.shape[:1]),
                     a_bs=stacked({(k,): v for k, v in rep['a_bs'].items()}, a_bs.shape[:1]),
                     b_onorm=stacked({(k,): v for k, v in rep['b_onorm'].items()}, b_onorm.shape[:1]),
                     b_lb_logits=lb_vjp(dlb)[0],
                     c_rk=stacked({(k,): v for k, v in rep['c_rk'].items()}, c_rk.shape[:1]))
    rep_pack = _pack([rep_grads[n] for n in REP])
    return part, grad_x, big_entries, sm_pack, rep_pack
```

```python
import functools

import jax
import jax.numpy as jnp
from jax import lax
from jax.experimental import pallas as pl
from jax.experimental.pallas import tpu as pltpu

F32 = jnp.float32
BF16 = jnp.bfloat16
MESH = pl.DeviceIdType.MESH

LANES = 128
VMEM_LIMIT = 56 * 1024 * 1024
MM_VMEM_BUDGET = 36 * 1024 * 1024
N_CHIPS = 4

RMS_EPS = 1e-6
A_GROUP = 128
A_CHUNK = 128
B_HEAD = 128
B_MIN_F = 1e-30
C_HEAD = 64
C_GN_EPS = 64e-5
ADAM_LR = 0.001
ADAM_B1 = 0.9
ADAM_B2 = 0.999
ADAM_EPS = 1e-08
ADAM_WD = 0.01
ADAM_STEP = 10


def _sigmoid(x):
    return 1.0 / (1.0 + jnp.exp(-x))


def _silu(x):
    return x * _sigmoid(x)


def _dsilu(x):
    s = _sigmoid(x)
    return s * (1.0 + x * (1.0 - s))


_GELU_C = 0.7978845608028654


def _gelu(x):
    return 0.5 * x * (1.0 + jnp.tanh(_GELU_C * (x + 0.044715 * x * x * x)))


def _dgelu(x):
    th = jnp.tanh(_GELU_C * (x + 0.044715 * x * x * x))
    return 0.5 * (1.0 + th) + 0.5 * x * (1.0 - th * th) * _GELU_C * (1.0 + 3.0 * 0.044715 * x * x)


def _softplus(x):
    return jnp.maximum(x, 0.0) + jnp.log(1.0 + jnp.exp(-jnp.abs(x)))


def _seg_ones(seg):
    i = lax.broadcasted_iota(jnp.int32, (LANES, LANES), 0) // seg
    j = lax.broadcasted_iota(jnp.int32, (LANES, LANES), 1) // seg
    return jnp.where(i == j, 1.0, 0.0).astype(BF16)


def _segsum(x, seg):
    ones = _seg_ones(seg)
    outs = []
    for j in range(x.shape[1] // LANES):
        xb = x[:, j * LANES:(j + 1) * LANES]
        hi = xb.astype(BF16)
        r1 = xb - hi.astype(F32)
        mid = r1.astype(BF16)
        lo = (r1 - mid.astype(F32)).astype(BF16)
        acc = jnp.dot(hi, ones, preferred_element_type=F32)
        acc = acc + jnp.dot(mid, ones, preferred_element_type=F32)
        acc = acc + jnp.dot(lo, ones, preferred_element_type=F32)
        outs.append(acc)
    return outs[0] if len(outs) == 1 else jnp.concatenate(outs, axis=1)


def _rowwise(fn, name, rows, tm, ins, outs, accs=()):
    n_in, n_out, n_acc = len(ins), len(outs), len(accs)
    arrays, in_specs = [], []
    for spec in ins:
        kind, arr = spec[0], spec[1]
        arrays.append(arr)
        if kind == 'row':
            in_specs.append(pl.BlockSpec((tm, arr.shape[1]), lambda i: (i, 0)))
        elif kind == 'col':
            in_specs.append(pl.BlockSpec((tm, spec[2]), functools.partial(lambda i, cb: (i, cb), cb=spec[3])))
        else:
            in_specs.append(pl.BlockSpec(arr.shape, functools.partial(lambda i, nd: (0,) * nd, nd=arr.ndim)))
    out_shape = [jax.ShapeDtypeStruct((rows, w), dt) for (w, dt) in outs]
    out_specs = [pl.BlockSpec((tm, w), lambda i: (i, 0)) for (w, _) in outs]
    out_shape += [jax.ShapeDtypeStruct(s, F32) for s in accs]
    out_specs += [pl.BlockSpec(s, functools.partial(lambda i, nd: (0,) * nd, nd=len(s))) for s in accs]

    def body(*refs):
        vals = fn(*[r[...] for r in refs[:n_in]])
        if not isinstance(vals, (tuple, list)):
            vals = (vals,)
        for r, v in zip(refs[n_in:n_in + n_out], vals[:n_out]):
            r[...] = v.astype(r.dtype)
        if n_acc:
            acc_refs = refs[n_in + n_out:]

            @pl.when(pl.program_id(0) == 0)
            def _():
                for r in acc_refs:
                    r[...] = jnp.zeros(r.shape, F32)

            for r, v in zip(acc_refs, vals[n_out:]):
                r[...] += v

    res = pl.pallas_call(
        body, name=name, grid=(rows // tm,), in_specs=in_specs, out_specs=out_specs, out_shape=out_shape,
        compiler_params=pltpu.CompilerParams(
            dimension_semantics=("arbitrary",) if n_acc else ("parallel",), vmem_limit_bytes=VMEM_LIMIT),
    )(*arrays)
    return res


class WV:
    def __init__(self, arr, idx, kind):
        self.arr, self.idx, self.kind = arr, tuple(idx), kind
        self.ns = arr.shape[0]
        self.R, self.C = arr.shape[-2:]
        self.K = self.R * (self.ns if kind == 'row' else 1)
        self.N = self.C * (self.ns if kind == 'col' else 1)

    def spec(self, br, bc, rmap, cmap):
        lead = (None,) * (1 + len(self.idx))
        nrb, ncb = self.R // br, self.C // bc
        idx, kind = self.idx, self.kind

        def index_map(*g):
            ri, ci = rmap(*g), cmap(*g)
            if kind == 'row':
                return (ri // nrb, *idx, ri % nrb, ci)
            return (ci // ncb, *idx, ri, ci % ncb)

        return pl.BlockSpec(lead + (br, bc), index_map)


def _tile_options(n):
    return [n] + [d for d in range(n - LANES, LANES - 1, -LANES) if n % d == 0]


def _pick_tiles(opt_m, opt_n, opt_k, out_bytes, has_res, full_k):
    best, best_key = None, None
    for tm in opt_m:
        for tn in opt_n:
            for tk in opt_k:
                multi = tk != full_k
                est = 2 * (tm * tk * 2 + tk * tn * 2 + tm * tn * out_bytes) + tm * tn * 4 * (2 if multi else 1)
                if has_res:
                    est += 2 * tm * tn * 4
                if est > MM_VMEM_BUDGET:
                    continue
                key = (tm * tn * tk, tk)
                if best is None or key > best_key:
                    best, best_key = (tm, tn, tk), key
    return best


def _mm_call(name, dims, grid, in_specs, out_spec, out_shape, operands, nk, acc_shape, scale, has_res):
    def body(*refs):
        a_ref, b_ref = refs[0], refs[1]
        res_ref = refs[2] if has_res else None
        o_ref = refs[3] if has_res else refs[2]

        def finalize(acc):
            acc = acc * scale if scale != 1.0 else acc
            if has_res:
                acc = acc + res_ref[...]
            o_ref[...] = acc.astype(o_ref.dtype)

        part = lax.dot_general(a_ref[...], b_ref[...], dims, preferred_element_type=F32)
        if nk == 1:
            finalize(part)
        else:
            acc_ref = refs[-1]
            k = pl.program_id(2)

            @pl.when(k == 0)
            def _():
                acc_ref[...] = part

            @pl.when(k > 0)
            def _():
                acc_ref[...] += part

            @pl.when(k == nk - 1)
            def _():
                finalize(acc_ref[...])

    return pl.pallas_call(
        body, name=name, grid=grid, in_specs=in_specs, out_specs=out_spec, out_shape=out_shape,
        scratch_shapes=[pltpu.VMEM(acc_shape, F32)] if nk > 1 else [],
        compiler_params=pltpu.CompilerParams(
            dimension_semantics=("parallel", "parallel", "arbitrary"), vmem_limit_bytes=VMEM_LIMIT),
    )(*operands)


def _mm_nn(a, w, name, out_dtype=F32, res=None, scale=1.0):
    M, K = a.shape
    N = w.N
    opt_n = _tile_options(w.C)
    opt_k = _tile_options(w.R)
    tm, tn, tk = _pick_tiles(_tile_options(M), opt_n, opt_k, jnp.dtype(out_dtype).itemsize, res is not None, K)
    nk = K // tk
    in_specs = [pl.BlockSpec((tm, tk), lambda n, m, k: (m, k)),
                w.spec(tk, tn, lambda n, m, k: k, lambda n, m, k: n)]
    operands = [a, w.arr]
    if res is not None:
        in_specs.append(pl.BlockSpec((tm, tn), lambda n, m, k: (m, n)))
        operands.append(res)
    return _mm_call(name, (((1,), (0,)), ((), ())), (N // tn, M // tm, nk), in_specs,
                    pl.BlockSpec((tm, tn), lambda n, m, k: (m, n)), jax.ShapeDtypeStruct((M, N), out_dtype),
                    operands, nk, (tm, tn), scale, res is not None)


def _mm_nt(a, w, name, out_dtype=F32, res=None, scale=1.0):
    M, C = a.shape
    Ko = w.K
    opt_n = _tile_options(w.R)
    opt_k = _tile_options(w.C)
    tm, tn, tk = _pick_tiles(_tile_options(M), opt_n, opt_k, jnp.dtype(out_dtype).itemsize, res is not None, C)
    nk = C // tk
    in_specs = [pl.BlockSpec((tm, tk), lambda n, m, k: (m, k)),
                w.spec(tn, tk, lambda n, m, k: n, lambda n, m, k: k)]
    operands = [a, w.arr]
    if res is not None:
        in_specs.append(pl.BlockSpec((tm, tn), lambda n, m, k: (m, n)))
        operands.append(res)
    return _mm_call(name, (((1,), (1,)), ((), ())), (Ko // tn, M // tm, nk), in_specs,
                    pl.BlockSpec((tm, tn), lambda n, m, k: (m, n)), jax.ShapeDtypeStruct((M, Ko), out_dtype),
                    operands, nk, (tm, tn), scale, res is not None)


def _mm_tn(a, dy, like, name, out_dtype=BF16, scale=1.0):
    M, K = a.shape
    N = dy.shape[1]
    out = WV(jax.ShapeDtypeStruct((like.ns, like.R, like.C), out_dtype), (), like.kind)
    opt_m = _tile_options(like.R)
    opt_n = _tile_options(like.C)
    tko, tno, tc = _pick_tiles(opt_m, opt_n, _tile_options(M), jnp.dtype(out_dtype).itemsize, False, M)
    nk = M // tc
    in_specs = [pl.BlockSpec((tc, tko), lambda i, j, c: (c, i)),
                pl.BlockSpec((tc, tno), lambda i, j, c: (c, j))]
    return _mm_call(name, (((0,), (0,)), ((), ())), (K // tko, N // tno, nk), in_specs,
                    out.spec(tko, tno, lambda i, j, c: i, lambda i, j, c: j), out.arr,
                    [a, dy], nk, (tko, tno), scale, False)


SCAN_TB = 16


def _subsum(x):
    r = x.shape[0]
    while r > 8:
        r //= 2
        x = x[:r] + x[r:]
    while r > 1:
        r //= 2
        x = x + pltpu.roll(x, r, 0)
    return x


def _head_ones(n_heads):
    i = lax.broadcasted_iota(jnp.int32, (LANES, LANES), 0) % n_heads
    j = lax.broadcasted_iota(jnp.int32, (LANES, LANES), 1) % n_heads
    return jnp.where(i == j, 1.0, 0.0).astype(BF16)


def _lanesum_mxu(x, ones):
    hi = x.astype(BF16)
    r1 = x - hi.astype(F32)
    mid = r1.astype(BF16)
    lo = (r1 - mid.astype(F32)).astype(BF16)
    acc = jnp.dot(hi, ones, preferred_element_type=F32)
    acc = acc + jnp.dot(mid, ones, preferred_element_type=F32)
    return acc + jnp.dot(lo, ones, preferred_element_type=F32)


def _rows_of(vals):
    n = len(vals)
    if vals[0].shape[0] >= n:
        idx = lax.broadcasted_iota(jnp.int32, (n, LANES), 0)
        out = vals[0][:n]
        for i in range(1, n):
            out = jnp.where(idx == i, vals[i][:n], out)
        return out
    return jnp.concatenate([v[0:1] for v in vals], axis=0)


def _scan_fwd(name, n_heads, wk, kk, rk, vrow, ak=None, bk=None):
    S, R, _ = wk.shape
    dv = vrow.shape[1]
    ab = ak is not None
    tb = min(SCAN_TB, S)
    grp = min(8, dv)

    def body(*refs):
        if ab:
            w_ref, k_ref, r_ref, v_ref, a_ref, b_ref, y_ref, hist_ref, sa_ref, s_ref = refs
        else:
            w_ref, k_ref, r_ref, v_ref, y_ref, hist_ref, s_ref = refs

        @pl.when(pl.program_id(0) == 0)
        def _():
            s_ref[...] = jnp.zeros(s_ref.shape, F32)

        ones = _head_ones(n_heads)

        def lanesum(tiles):
            return _lanesum_mxu(tiles[0] if len(tiles) == 1 else jnp.concatenate(tiles, axis=0), ones)

        def step(t, carry):
            w, k, r = w_ref[t], k_ref[t], r_ref[t]
            if ab:
                a, b = a_ref[t], b_ref[t]
                tiles = []
                for g0 in range(0, dv, grp):
                    tiles.append(_rows_of([_subsum(s_ref[g0 + i] * a) for i in range(grp)]))
                sa_ref[t] = lanesum(tiles)
            tiles = []
            for g0 in range(0, dv, grp):
                ys = []
                for i in range(grp):
                    v = g0 + i
                    st = s_ref[v]
                    hist_ref[t, v] = st
                    st = st * w + v_ref[t, pl.ds(v, 1), :] * k
                    if ab:
                        st = st + sa_ref[t, pl.ds(v, 1), :] * b
                    s_ref[v] = st
                    ys.append(_subsum(st * r))
                tiles.append(_rows_of(ys))
            y_ref[t] = lanesum(tiles)
            return carry

        lax.fori_loop(0, tb, step, 0, unroll=2)

    kspec = pl.BlockSpec((tb, R, LANES), lambda i: (i, 0, 0))
    rspec = pl.BlockSpec((tb, dv, LANES), lambda i: (i, 0, 0))
    operands = [wk, kk, rk, vrow] + ([ak, bk] if ab else [])
    in_specs = [kspec, kspec, kspec, rspec] + ([kspec, kspec] if ab else [])
    out_shape = [jax.ShapeDtypeStruct((S, dv, LANES), F32), jax.ShapeDtypeStruct((S, dv, R, LANES), F32)]
    out_specs = [rspec, pl.BlockSpec((tb, dv, R, LANES), lambda i: (i, 0, 0, 0))]
    if ab:
        out_shape.append(jax.ShapeDtypeStruct((S, dv, LANES), F32))
        out_specs.append(rspec)
    res = pl.pallas_call(
        body, name=name, grid=(S // tb,), in_specs=in_specs, out_specs=out_specs, out_shape=out_shape,
        scratch_shapes=[pltpu.VMEM((dv, R, LANES), F32)],
        compiler_params=pltpu.CompilerParams(dimension_semantics=("arbitrary",), vmem_limit_bytes=VMEM_LIMIT),
    )(*operands)
    return (res[0], res[1], res[2]) if ab else (res[0], res[1], None)


def _scan_bwd(name, n_heads, wk, kk, rk, vrow, hist, dyrow, ak=None, bk=None, sarow=None):
    S, R, _ = wk.shape
    dv = vrow.shape[1]
    ab = ak is not None
    tb = min(SCAN_TB, S)
    nb = S // tb
    grp = min(8, dv)

    def body(*refs):
        if ab:
            (w_ref, k_ref, r_ref, v_ref, hist_ref, dy_ref, a_ref, b_ref, sa_ref,
             dw_ref, dk_ref, dr_ref, dv_ref, da_ref, db_ref, ds_ref, dsa_ref) = refs
        else:
            (w_ref, k_ref, r_ref, v_ref, hist_ref, dy_ref,
             dw_ref, dk_ref, dr_ref, dv_ref, ds_ref) = refs

        @pl.when(pl.program_id(0) == 0)
        def _():
            ds_ref[...] = jnp.zeros(ds_ref.shape, F32)

        ones = _head_ones(n_heads)

        def lanesum(tiles):
            return _lanesum_mxu(tiles[0] if len(tiles) == 1 else jnp.concatenate(tiles, axis=0), ones)

        def step(j, carry):
            t = tb - 1 - j
            w, k, r = w_ref[t], k_ref[t], r_ref[t]
            zero = jnp.zeros((R, LANES), F32)
            u, dw, dk, da, db = zero, zero, zero, zero, zero
            vd = jnp.sum(v_ref[t] * dy_ref[t], axis=0, keepdims=True)
            sd = jnp.sum(sa_ref[t] * dy_ref[t], axis=0, keepdims=True) if ab else None
            if ab:
                a, b = a_ref[t], b_ref[t]
                tiles = []
                for g0 in range(0, dv, grp):
                    ps = []
                    for i in range(grp):
                        v = g0 + i
                        dst = ds_ref[v] + dy_ref[t, pl.ds(v, 1), :] * r
                        ds_ref[v] = dst
                        ps.append(_subsum(dst * b))
                    tiles.append(_rows_of(ps))
                dsa_ref[...] = lanesum(tiles)
            tiles = []
            for g0 in range(0, dv, grp):
                dvs = []
                for i in range(grp):
                    v = g0 + i
                    sp = hist_ref[t, v]
                    dyr = dy_ref[t, pl.ds(v, 1), :]
                    vr = v_ref[t, pl.ds(v, 1), :]
                    dst = ds_ref[v] if ab else ds_ref[v] + dyr * r
                    u = u + sp * dyr
                    dw = dw + dst * sp
                    dk = dk + dst * vr
                    dvs.append(_subsum(dst * k))
                    if ab:
                        db = db + dst * sa_ref[t, pl.ds(v, 1), :]
                        dsa = dsa_ref[pl.ds(v, 1), :]
                        da = da + sp * dsa
                        dst = dst * w + dsa * a
                    else:
                        dst = dst * w
                    ds_ref[v] = dst
                tiles.append(_rows_of(dvs))
            dv_ref[t] = lanesum(tiles)
            dr = w * u + k * vd
            if ab:
                dr = dr + b * sd
                da_ref[t] = da
                db_ref[t] = db
            dw_ref[t] = dw
            dk_ref[t] = dk
            dr_ref[t] = dr
            return carry

        lax.fori_loop(0, tb, step, 0, unroll=2)

    kspec = pl.BlockSpec((tb, R, LANES), lambda i: (nb - 1 - i, 0, 0))
    rspec = pl.BlockSpec((tb, dv, LANES), lambda i: (nb - 1 - i, 0, 0))
    hspec = pl.BlockSpec((tb, dv, R, LANES), lambda i: (nb - 1 - i, 0, 0, 0))
    operands = [wk, kk, rk, vrow, hist, dyrow] + ([ak, bk, sarow] if ab else [])
    in_specs = [kspec, kspec, kspec, rspec, hspec, rspec] + ([kspec, kspec, rspec] if ab else [])
    kshape = jax.ShapeDtypeStruct((S, R, LANES), F32)
    out_shape = [kshape, kshape, kshape, jax.ShapeDtypeStruct((S, dv, LANES), F32)] + ([kshape, kshape] if ab else [])
    out_specs = [kspec, kspec, kspec, rspec] + ([kspec, kspec] if ab else [])
    return pl.pallas_call(
        body, name=name, grid=(nb,), in_specs=in_specs, out_specs=out_specs, out_shape=out_shape,
        scratch_shapes=[pltpu.VMEM((dv, R, LANES), F32)] + ([pltpu.VMEM((dv, LANES), F32)] if ab else []),
        compiler_params=pltpu.CompilerParams(dimension_semantics=("arbitrary",), vmem_limit_bytes=VMEM_LIMIT),
    )(*operands)


def _to_k(x, n_heads, dk):
    S = x.shape[0]
    kl = LANES // n_heads
    return x.reshape(S, n_heads, dk // kl, kl).transpose(0, 2, 3, 1).reshape(S, dk // kl, LANES)


def _from_k(x, n_heads, dk):
    S = x.shape[0]
    kl = LANES // n_heads
    return x.reshape(S, dk // kl, kl, n_heads).transpose(0, 3, 1, 2).reshape(S, n_heads * dk)


def _to_row(x, n_heads, dv):
    S = x.shape[0]
    return jnp.tile(x.reshape(S, n_heads, dv).transpose(0, 2, 1), (1, 1, LANES // n_heads))


def _from_row(x, n_heads, dv):
    S = x.shape[0]
    return x[:, :, :n_heads].transpose(0, 2, 1).reshape(S, n_heads * dv)


def _tril_mask():
    t = lax.broadcasted_iota(jnp.int32, (A_CHUNK, A_CHUNK), 0)
    s = lax.broadcasted_iota(jnp.int32, (A_CHUNK, A_CHUNK), 1)
    return s <= t


def _gmlp_fwd(name, proj, aw, vgain, ws, bs_t):
    S = proj.shape[0]
    G = aw // A_GROUP

    def body(u_ref, v_ref, gain_ref, ws_ref, bs_ref, o_ref):
        mask = _tril_mask()
        lane = lax.broadcasted_iota(jnp.int32, (A_CHUNK, G), 1)
        bs = bs_ref[...]
        for g in range(G):
            seg = slice(g * A_GROUP, (g + 1) * A_GROUP)
            ua = _gelu(u_ref[:, seg])
            va = _gelu(v_ref[:, seg])
            rs = lax.rsqrt(jnp.mean(va * va, axis=-1, keepdims=True) + RMS_EPS)
            vg = (va * rs) * gain_ref[:, seg]
            wm = jnp.where(mask, ws_ref[g], 0.0).astype(BF16)
            bcol = jnp.sum(jnp.where(lane == g, bs, 0.0), axis=1, keepdims=True)
            s = jnp.dot(wm, vg.astype(BF16), preferred_element_type=F32) + bcol
            o_ref[:, seg] = (ua * s).astype(o_ref.dtype)

    return pl.pallas_call(
        body, name=name, grid=(S // A_CHUNK,),
        in_specs=[pl.BlockSpec((A_CHUNK, aw), lambda i: (i, 0)), pl.BlockSpec((A_CHUNK, aw), lambda i: (i, 1)),
                  pl.BlockSpec((1, aw), lambda i: (0, 0)), pl.BlockSpec((G, A_CHUNK, A_CHUNK), lambda i: (0, 0, 0)),
                  pl.BlockSpec((A_CHUNK, G), lambda i: (0, 0))],
        out_specs=pl.BlockSpec((A_CHUNK, aw), lambda i: (i, 0)),
        out_shape=jax.ShapeDtypeStruct((S, aw), BF16),
        compiler_params=pltpu.CompilerParams(dimension_semantics=("parallel",), vmem_limit_bytes=VMEM_LIMIT),
    )(proj, proj, vgain, ws, bs_t)


def _gmlp_bwd(name, proj, dout, dout_cb, aw, vgain, ws, bs_t):
    S = proj.shape[0]
    G = aw // A_GROUP

    def body(u_ref, v_ref, do_ref, gain_ref, ws_ref, bs_ref, du_ref, dv_ref, dws_ref, dbs_ref, dgain_ref):
        @pl.when(pl.program_id(0) == 0)
        def _():
            dws_ref[...] = jnp.zeros(dws_ref.shape, F32)
            dbs_ref[...] = jnp.zeros(dbs_ref.shape, F32)
            dgain_ref[...] = jnp.zeros(dgain_ref.shape, F32)

        mask = _tril_mask()
        lane = lax.broadcasted_iota(jnp.int32, (A_CHUNK, G), 1)
        bs = bs_ref[...]
        dbs = jnp.zeros((A_CHUNK, G), F32)
        for g in range(G):
            seg = slice(g * A_GROUP, (g + 1) * A_GROUP)
            u, v = u_ref[:, seg], v_ref[:, seg]
            do = do_ref[:, seg].astype(F32)
            ua, va = _gelu(u), _gelu(v)
            rs = lax.rsqrt(jnp.mean(va * va, axis=-1, keepdims=True) + RMS_EPS)
            xh = va * rs
            gain = gain_ref[:, seg]
            vg = (xh * gain).astype(BF16)
            wm = jnp.where(mask, ws_ref[g], 0.0).astype(BF16)
            bcol = jnp.sum(jnp.where(lane == g, bs, 0.0), axis=1, keepdims=True)
            s = jnp.dot(wm, vg, preferred_element_type=F32) + bcol
            du_ref[:, seg] = (do * s * _dgelu(u)).astype(du_ref.dtype)
            ds = do * ua
            dsb = ds.astype(BF16)
            dw = lax.dot_general(dsb, vg, (((1,), (1,)), ((), ())), preferred_element_type=F32)
            dws_ref[g] += jnp.where(mask, dw, 0.0)
            dbs = dbs + jnp.where(lane == g, jnp.sum(ds, axis=1, keepdims=True), 0.0)
            dvg = lax.dot_general(wm, dsb, (((0,), (0,)), ((), ())), preferred_element_type=F32)
            dgain_ref[:, seg] += jnp.sum(dvg * xh, axis=0, keepdims=True)
            dxh = dvg * gain
            dva = rs * (dxh - xh * jnp.mean(dxh * xh, axis=-1, keepdims=True))
            dv_ref[:, seg] = (dva * _dgelu(v)).astype(dv_ref.dtype)
        dbs_ref[...] += dbs

    return pl.pallas_call(
        body, name=name, grid=(S // A_CHUNK,),
        in_specs=[pl.BlockSpec((A_CHUNK, aw), lambda i: (i, 0)), pl.BlockSpec((A_CHUNK, aw), lambda i: (i, 1)),
                  pl.BlockSpec((A_CHUNK, aw), functools.partial(lambda i, cb: (i, cb), cb=dout_cb)),
                  pl.BlockSpec((1, aw), lambda i: (0, 0)), pl.BlockSpec((G, A_CHUNK, A_CHUNK), lambda i: (0, 0, 0)),
                  pl.BlockSpec((A_CHUNK, G), lambda i: (0, 0))],
        out_specs=[pl.BlockSpec((A_CHUNK, aw), lambda i: (i, 0)), pl.BlockSpec((A_CHUNK, aw), lambda i: (i, 0)),
                   pl.BlockSpec((G, A_CHUNK, A_CHUNK), lambda i: (0, 0, 0)), pl.BlockSpec((A_CHUNK, G), lambda i: (0, 0)),
                   pl.BlockSpec((1, aw), lambda i: (0, 0))],
        out_shape=[jax.ShapeDtypeStruct((S, aw), BF16), jax.ShapeDtypeStruct((S, aw), BF16),
                   jax.ShapeDtypeStruct((G, A_CHUNK, A_CHUNK), F32), jax.ShapeDtypeStruct((A_CHUNK, G), F32),
                   jax.ShapeDtypeStruct((1, aw), F32)],
        compiler_params=pltpu.CompilerParams(dimension_semantics=("arbitrary",), vmem_limit_bytes=VMEM_LIMIT),
    )(proj, proj, dout, vgain, ws, bs_t)


ANY = pl.BlockSpec(memory_space=pl.ANY)


def _place():
    return lax.axis_index("x"), lax.axis_index("y"), lax.axis_index("c")


def _other_chips(x, y):
    return [(1 - x, y), (x, 1 - y), (1 - x, 1 - y)]


def _remote(src, dst, send_sem, recv_sem, device):
    return pltpu.make_async_remote_copy(src_ref=src, dst_ref=dst, send_sem=send_sem, recv_sem=recv_sem,
                                        device_id=device, device_id_type=MESH)


def _comm_call(body, name, operands, out_shape, n_dma, n_local):
    return pl.pallas_call(
        body, name=name, in_specs=[ANY] * len(operands), out_specs=[ANY] * len(out_shape), out_shape=out_shape,
        scratch_shapes=[pltpu.SemaphoreType.DMA((n_dma,)), pltpu.SemaphoreType.DMA((n_dma,)),
                        pltpu.SemaphoreType.DMA((max(n_local, 1),))],
        compiler_params=pltpu.CompilerParams(has_side_effects=True),
    )(*operands)


def _all_gather(name, bigs, smalls):
    nb, n = len(bigs), len(bigs) + len(smalls)
    arrays = list(bigs) + list(smalls)

    def body(*refs):
        ins, outs = refs[:n], refs[n:2 * n]
        send_sems, recv_sems, local_sems = refs[2 * n:]
        x, y, c = _place()
        me = 2 * x + y
        chip_x, chip_y = (1 - x, y), (x, 1 - y)
        k_x, k_y, k_d = 2 * (1 - x) + y, 2 * x + (1 - y), 2 * (1 - x) + (1 - y)
        sibling = (x, y, 1 - c)
        started = []

        def go(src, dst, s, device):
            cp = _remote(src, dst, send_sems.at[s], recv_sems.at[s], device)
            cp.start()
            started.append(cp)

        def landed(ref, s):
            _remote(ref, ref, send_sems.at[s], recv_sems.at[s], sibling).wait_recv()

        for e in range(nb):
            half = ins[e].shape[0] // 2
            src = ins[e].at[pl.ds(c * half, half)]
            dst = outs[e].at[me, pl.ds(c * half, half)]
            go(src, dst, 8 * e, (*chip_x, c))
            go(src, dst, 8 * e + 1, (*chip_y, c))
        small = 8 * nb
        for e in range(nb, n):
            for j, chip in enumerate(_other_chips(x, y)):
                go(ins[e], outs[e].at[me], small + 3 * (e - nb) + j, (*chip, c))
        for e in range(nb):
            half = ins[e].shape[0] // 2
            q = half // 2
            from_x = outs[e].at[k_x, pl.ds(c * half, half)]
            landed(from_x, 8 * e)
            first = outs[e].at[k_x, pl.ds(c * half, q)]
            go(first, first, 8 * e + 2, (*chip_y, c))
            go(from_x, from_x, 8 * e + 4, sibling)
            from_y = outs[e].at[k_y, pl.ds(c * half, half)]
            landed(from_y, 8 * e + 1)
            second = outs[e].at[k_y, pl.ds(c * half + q, q)]
            go(second, second, 8 * e + 3, (*chip_x, c))
            go(from_y, from_y, 8 * e + 5, sibling)
        for e in range(nb):
            half = ins[e].shape[0] // 2
            q = half // 2
            first = outs[e].at[k_d, pl.ds(c * half, q)]
            landed(first, 8 * e + 2)
            go(first, first, 8 * e + 6, sibling)
            second = outs[e].at[k_d, pl.ds(c * half + q, q)]
            landed(second, 8 * e + 3)
            go(second, second, 8 * e + 7, sibling)
        for e in range(nb, n):
            for j, chip in enumerate(_other_chips(x, y)):
                landed(outs[e].at[2 * chip[0] + chip[1]], small + 3 * (e - nb) + j)
        for e in range(nb):
            half = ins[e].shape[0] // 2
            q = half // 2
            o = (1 - c) * half
            landed(outs[e].at[k_x, pl.ds(o, half)], 8 * e + 4)
            landed(outs[e].at[k_y, pl.ds(o, half)], 8 * e + 5)
            landed(outs[e].at[k_d, pl.ds(o, q)], 8 * e + 6)
            landed(outs[e].at[k_d, pl.ds(o + q, q)], 8 * e + 7)
        for cp in started:
            cp.wait_send()

    out_shape = [jax.ShapeDtypeStruct((N_CHIPS,) + a.shape, a.dtype) for a in arrays]
    gathered = _comm_call(body, name, arrays, out_shape, 8 * nb + 3 * len(smalls), 0)
    me = 2 * lax.axis_index("x") + lax.axis_index("y")
    return [lax.dynamic_update_slice(g, a[None], (me,) + (0,) * a.ndim) for g, a in zip(gathered, arrays)]


def _swap_halves(name, grads):
    n = len(grads)

    def body(*refs):
        ins, outs = refs[:n], refs[n:2 * n]
        send_sems, recv_sems, _ = refs[2 * n:]
        x, y, c = _place()
        cps = []
        for e in range(n):
            half = ins[e].shape[1] // 2
            src = ins[e].at[pl.ds(0, N_CHIPS), pl.ds((1 - c) * half, half)]
            cp = _remote(src, outs[e], send_sems.at[e], recv_sems.at[e], (x, y, 1 - c))
            cp.start()
            cps.append(cp)
        for cp in cps:
            cp.wait()

    out_shape = [jax.ShapeDtypeStruct((N_CHIPS, g.shape[1] // 2, g.shape[2]), g.dtype) for g in grads]
    return _comm_call(body, name, list(grads), out_shape, n, 0)


def _scatter_chips(name, parts):
    n = len(parts)

    def body(*refs):
        ins, outs = refs[:n], refs[n:2 * n]
        send_sems, recv_sems, local_sems = refs[2 * n:]
        x, y, c = _place()
        me = 2 * x + y
        cps = []
        for e in range(n):
            cp = pltpu.make_async_copy(ins[e].at[me], outs[e].at[3], local_sems.at[e])
            cp.start()
            cps.append(cp)
            for j, chip in enumerate(_other_chips(x, y)):
                cp = _remote(ins[e].at[2 * chip[0] + chip[1]], outs[e].at[j], send_sems.at[3 * e + j],
                             recv_sems.at[3 * e + j], (*chip, c))
                cp.start()
                cps.append(cp)
        for cp in cps:
            cp.wait()

    out_shape = [jax.ShapeDtypeStruct(p.shape, p.dtype) for p in parts]
    return _comm_call(body, name, list(parts), out_shape, 3 * n, n)


def _join_halves(name, halves, places, out_shapes):
    n, n_out = len(halves), len(out_shapes)

    def body(*refs):
        ins, outs = refs[:n], refs[n:n + n_out]
        send_sems, recv_sems, local_sems = refs[n + n_out:]
        x, y, c = _place()
        cps = []
        for e in range(n):
            o, idx = places[e]
            half = ins[e].shape[0]
            dst = outs[o].at[(*idx, pl.ds(c * half, half))]
            cp = _remote(ins[e], dst, send_sems.at[e], recv_sems.at[e], (x, y, 1 - c))
            cp.start()
            cps.append(cp)
        for e, cp in enumerate(cps):
            o, idx = places[e]
            half = ins[e].shape[0]
            landed = outs[o].at[(*idx, pl.ds((1 - c) * half, half))]
            cp.wait_send()
            _remote(ins[e], landed, send_sems.at[e], recv_sems.at[e], (x, y, 1 - c)).wait_recv()

    joined = list(_comm_call(body, name, list(halves), list(out_shapes), n, 0))
    c = lax.axis_index("c")
    for e, hv in enumerate(halves):
        o, idx = places[e]
        start = tuple(idx) + (c * hv.shape[0], 0)
        joined[o] = lax.dynamic_update_slice(joined[o], hv.reshape((1,) * len(idx) + hv.shape), start)
    return joined


def _spread_all(name, per_chip, everywhere):
    def body(pc_ref, ev_ref, pc_out, ev_out, send_sems, recv_sems, local_sems):
        x, y, c = _place()
        me = 4 * x + 2 * y + c
        cps = [pltpu.make_async_copy(pc_ref.at[2 * x + y], pc_out.at[me], local_sems.at[0]),
               pltpu.make_async_copy(ev_ref, ev_out.at[me], local_sems.at[1])]
        for f in range(1, 8):
            fx, fy, fc = f // 4, (f // 2) % 2, f % 2
            tx = 1 - x if fx else x
            ty = 1 - y if fy else y
            tc = 1 - c if fc else c
            cps.append(_remote(pc_ref.at[2 * tx + ty], pc_out.at[me], send_sems.at[2 * f], recv_sems.at[2 * f],
                               (tx, ty, tc)))
            cps.append(_remote(ev_ref, ev_out.at[me], send_sems.at[2 * f + 1], recv_sems.at[2 * f + 1],
                               (tx, ty, tc)))
        for cp in cps:
            cp.start()
        for cp in cps:
            cp.wait()

    out_shape = [jax.ShapeDtypeStruct((8,) + per_chip.shape[1:], F32), jax.ShapeDtypeStruct((8,) + everywhere.shape, F32)]
    return _comm_call(body, name, [per_chip, everywhere], out_shape, 16, 2)


def _row_tile(rows, width, itemsize, n_arrays):
    tm = 1
    while rows % (tm * 2) == 0 and (tm * 2) * width * itemsize * n_arrays * 2 <= 24 * 1024 * 1024 and tm * 2 <= 1024:
        tm *= 2
    return tm


def _add_own_half(name, grad, swapped, c_arr):
    ns, R, C = grad.shape
    half = R // 2
    th = _row_tile(half, C, 2, 3)
    g4 = grad.reshape(ns, 2, half, C)

    def body(c_ref, g_ref, s_ref, o_ref):
        o_ref[...] = (g_ref[...].astype(F32) + s_ref[...].astype(F32)).astype(o_ref.dtype)

    return pl.pallas_call(
        body, name=name,
        grid_spec=pltpu.PrefetchScalarGridSpec(
            num_scalar_prefetch=1, grid=(ns, half // th),
            in_specs=[pl.BlockSpec((None, None, th, C), lambda k, i, c_ref: (k, c_ref[0], i, 0)),
                      pl.BlockSpec((None, th, C), lambda k, i, c_ref: (k, i, 0))],
            out_specs=pl.BlockSpec((None, th, C), lambda k, i, c_ref: (k, i, 0))),
        out_shape=jax.ShapeDtypeStruct((ns, half, C), grad.dtype),
        compiler_params=pltpu.CompilerParams(dimension_semantics=("parallel", "parallel"), vmem_limit_bytes=VMEM_LIMIT),
    )(c_arr, g4, swapped)


def _sum_slots(name, slots, order, out_dtype=F32):
    n, rows, C = slots.shape
    th = _row_tile(rows, C, 4, n + 1)

    def body(s_ref, o_ref):
        acc = s_ref[order[0]].astype(F32)
        for k in order[1:]:
            acc = acc + s_ref[k].astype(F32)
        o_ref[...] = acc.astype(o_ref.dtype)

    return pl.pallas_call(
        body, name=name, grid=(rows // th,),
        in_specs=[pl.BlockSpec((n, th, C), lambda i: (0, i, 0))],
        out_specs=pl.BlockSpec((th, C), lambda i: (i, 0)),
        out_shape=jax.ShapeDtypeStruct((rows, C), out_dtype),
        compiler_params=pltpu.CompilerParams(dimension_semantics=("parallel",), vmem_limit_bytes=VMEM_LIMIT),
    )(slots)


def _adamw(name, g, w, m, v):
    rows, C = g.shape
    tm = _row_tile(rows, C, 4, 7)
    c1 = 1.0 - ADAM_B1 ** ADAM_STEP
    c2 = 1.0 - ADAM_B2 ** ADAM_STEP

    def fn(g, w, m, v):
        m = ADAM_B1 * m + (1.0 - ADAM_B1) * g
        v = ADAM_B2 * v + (1.0 - ADAM_B2) * (g * g)
        delta = -ADAM_LR * ((m / c1) / (jnp.sqrt(v / c2) + ADAM_EPS) + ADAM_WD * w)
        return delta, m, v

    return _rowwise(fn, name, rows, tm, [('row', g), ('row', w), ('row', m), ('row', v)],
                    [(C, F32), (C, F32), (C, F32)])


def _tm(S, width, n_arrays):
    return _row_tile(S, width, 4, n_arrays)


def _rms_fwd(name, h, gain, out_dtype):
    S, D = h.shape

    def fn(hb, g):
        r = lax.rsqrt(jnp.mean(hb * hb, axis=-1, keepdims=True) + RMS_EPS)
        return ((hb * r) * g,)

    return _rowwise(fn, name, S, _tm(S, D, 4), [('row', h), ('full', gain)], [(D, out_dtype)])[0]


def _rms_bwd(name, h_in, gain, dn, dh):
    S, D = h_in.shape

    def fn(hb, g, dnb, dhb):
        r = lax.rsqrt(jnp.mean(hb * hb, axis=-1, keepdims=True) + RMS_EPS)
        xh = hb * r
        dnb = dnb.astype(F32)
        dxh = dnb * g
        dx = r * (dxh - xh * jnp.mean(dxh * xh, axis=-1, keepdims=True))
        return dhb + dx, jnp.sum(dnb * xh, axis=0, keepdims=True)

    return _rowwise(fn, name, S, _tm(S, D, 8), [('row', h_in), ('full', gain), ('row', dn), ('row', dh)],
                    [(D, F32)], [(1, D)])


def _ffn_fwd(h, gain, wg, wu, wd):
    S, D = h.shape
    n = _rms_fwd("rms_fwd_bf16", h, gain, BF16)
    g = _mm_nn(n, wg, "ffn_up", BF16)
    u = _mm_nn(n, wu, "ffn_up", BF16)
    FF = g.shape[1]

    def fn(gb, ub):
        return (_silu(gb.astype(F32)) * ub.astype(F32),)

    act = _rowwise(fn, "ffn_act", S, _tm(S, FF, 6), [('row', g), ('row', u)], [(FF, BF16)])[0]
    return _mm_nn(act, wd, "ffn_down", F32, res=h, scale=0.5), (h, n, g, u)


def _ffn_bwd(dh, cache, gain, wg, wu, wd):
    h, n, g, u = cache
    S, FF = g.shape
    dhb = dh.astype(BF16)
    dact = _mm_nt(dhb, wd, "ffn_dact", BF16, scale=0.5)

    def fn(gb, ub, db):
        gb, ub, db = gb.astype(F32), ub.astype(F32), db.astype(F32)
        sg = _sigmoid(gb)
        sl = gb * sg
        return db * ub * (sg * (1.0 + gb * (1.0 - sg))), db * sl, sl * ub

    dg, du, act = _rowwise(fn, "ffn_act_bwd", S, _tm(S, FF, 10), [('row', g), ('row', u), ('row', dact)],
                           [(FF, BF16), (FF, BF16), (FF, BF16)])
    dwd = _mm_tn(act, dhb, wd, "ffn_dwd", BF16, scale=0.5)
    dwg = _mm_tn(n, dg, wg, "ffn_dwup", BF16)
    dwu = _mm_tn(n, du, wu, "ffn_dwup", BF16)
    dn = _mm_nt(dg, wg, "ffn_dn", F32)
    dn = _mm_nt(du, wu, "ffn_dn_acc", F32, res=dn)
    dh2, dgain = _rms_bwd("rms_bwd", h, gain, dn, dh)
    return dh2, dgain, dwg, dwu, dwd


def _ple_fwd(h, gain, pb, wgate, wproj):
    S, D = h.shape
    n = _rms_fwd("rms_fwd_bf16", h, gain, BF16)
    pre = _mm_nn(n, wgate, "ple_gate", F32)
    e = _mm_nn(pb, wproj, "ple_proj", F32)

    def fn(hb, pr, eb):
        return (hb + _sigmoid(pr) * eb,)

    h2 = _rowwise(fn, "ple_add", S, _tm(S, D, 6), [('row', h), ('row', pre), ('row', e)], [(D, F32)])[0]
    return h2, (h, n, pre, e)


def _ple_bwd(dh, cache, gain, pb, wgate, wproj):
    h, n, pre, e = cache
    S, D = h.shape

    def fn(db, pr, eb):
        gt = _sigmoid(pr)
        return db * eb * gt * (1.0 - gt), db * gt

    dpre, de = _rowwise(fn, "ple_bwd", S, _tm(S, D, 6), [('row', dh), ('row', pre), ('row', e)],
                        [(D, BF16), (D, BF16)])
    dwproj = _mm_tn(pb, de, wproj, "ple_dwproj", F32)
    dwgate = _mm_tn(n, dpre, wgate, "ple_dwgate", BF16)
    dn = _mm_nt(dpre, wgate, "ple_dn", F32)
    dh2, dgain = _rms_bwd("rms_bwd", h, gain, dn, dh)
    return dh2, dgain, dwgate, dwproj


def _even_fwd(h, gain, w_in, w_out, vgain, ws, bs_t, onorm, lb):
    S, D = h.shape
    aw = D // 2
    nh = aw // B_HEAD
    hn = _rms_fwd("rms_fwd_bf16", h, gain, BF16)
    proj = _mm_nn(hn, w_in, "even_in", F32)
    a_out = _gmlp_fwd("gmlp_fwd", proj, aw, vgain, ws, bs_t)

    def pre(bq, bf, lbv):
        f = lbv + (1.0 - lbv) * _sigmoid(bf)
        return _silu(bq), jnp.maximum(f, B_MIN_F), 1.0 - f

    q, w, k = _rowwise(pre, "hgrn_pre", S, _tm(S, aw, 8), [('col', proj, aw, 2), ('col', proj, aw, 3), ('full', lb)],
                       [(aw, F32), (aw, F32), (aw, F32)])
    wk, kk, qk = _to_k(w, nh, B_HEAD), _to_k(k, nh, B_HEAD), _to_k(q, nh, B_HEAD)
    vrow = _to_row(proj[:, 4 * aw:5 * aw], nh, B_HEAD)
    yrow, hist, _ = _scan_fwd("hgrn_scan_fwd", nh, wk, kk, qk, vrow)
    o = _from_row(yrow, nh, B_HEAD)

    def post(ob, bg, on):
        rs = lax.rsqrt(_segsum(ob * ob, B_HEAD) * (1.0 / B_HEAD) + RMS_EPS)
        return ((ob * rs * on) * _silu(bg),)

    b_out = _rowwise(post, "hgrn_post", S, _tm(S, aw, 8), [('row', o), ('col', proj, aw, 5), ('full', onorm)],
                     [(aw, BF16)])[0]
    cat = jnp.concatenate([a_out, b_out], axis=1)
    h2 = _mm_nn(cat, w_out, "even_out", F32, res=h)
    return h2, (h, hn, proj, wk, kk, qk, vrow, hist, o, cat)


def _even_bwd(dh, cache, gain, w_in, w_out, vgain, ws, bs_t, onorm, lb):
    h, hn, proj, wk, kk, qk, vrow, hist, o, cat = cache
    S, D = h.shape
    aw = D // 2
    nh = aw // B_HEAD
    dhb = dh.astype(BF16)
    dw_out = _mm_tn(cat, dhb, w_out, "even_dwout", BF16)
    dcat = _mm_nt(dhb, w_out, "even_dcat", F32)

    def post_bwd(ob, bg, on, db):
        rs = lax.rsqrt(_segsum(ob * ob, B_HEAD) * (1.0 / B_HEAD) + RMS_EPS)
        xh = ob * rs
        dy = db * _silu(bg)
        dbg = db * (xh * on) * _dsilu(bg)
        dxh = dy * on
        do = rs * (dxh - xh * (_segsum(dxh * xh, B_HEAD) * (1.0 / B_HEAD)))
        return do, dbg, jnp.sum(dy * xh, axis=0, keepdims=True)

    do, dbg, donorm = _rowwise(post_bwd, "hgrn_post_bwd", S, _tm(S, aw, 10),
                               [('row', o), ('col', proj, aw, 5), ('full', onorm), ('col', dcat, aw, 1)],
                               [(aw, F32), (aw, BF16)], [(1, aw)])
    dwk, dkk, dqk, dvrow = _scan_bwd("hgrn_scan_bwd", nh, wk, kk, qk, vrow, hist, _to_row(do, nh, B_HEAD))
    dq, dw, dk = _from_k(dqk, nh, B_HEAD), _from_k(dwk, nh, B_HEAD), _from_k(dkk, nh, B_HEAD)
    dbi = _from_row(dvrow, nh, B_HEAD).astype(BF16)

    def pre_bwd(bq, bf, lbv, dqb, dwb, dkb):
        sig = _sigmoid(bf)
        f = lbv + (1.0 - lbv) * sig
        df = jnp.where(f > B_MIN_F, dwb, 0.0) - dkb
        return dqb * _dsilu(bq), df * (1.0 - lbv) * sig * (1.0 - sig), jnp.sum(df * (1.0 - sig), axis=0, keepdims=True)

    dbq, dbf, dlb = _rowwise(pre_bwd, "hgrn_pre_bwd", S, _tm(S, aw, 12),
                             [('col', proj, aw, 2), ('col', proj, aw, 3), ('full', lb), ('row', dq), ('row', dw),
                              ('row', dk)], [(aw, BF16), (aw, BF16)], [(1, aw)])
    dau, dav, dws, dbs_t, dvgain = _gmlp_bwd("gmlp_bwd", proj, dcat, 0, aw, vgain, ws, bs_t)
    dproj = jnp.concatenate([dau, dav, dbq, dbf, dbi, dbg], axis=1)
    dw_in = _mm_tn(hn, dproj, w_in, "even_dwin", BF16)
    dn = _mm_nt(dproj, w_in, "even_dn", F32)
    dh2, dgain = _rms_bwd("rms_bwd", h, gain, dn, dh)
    return dh2, dict(gain=dgain, w_in=dw_in, w_out=dw_out, vgain=dvgain, ws=dws, bs_t=dbs_t, onorm=donorm, lb=dlb)


def _rwkv_prep(r, k, v0, wpl, apl, w0, a0, kkg, kag, svl=None, vf=None, v0p=None):
    wp = w0 + wpl
    w = -_softplus(-wp) - 0.5
    ew = jnp.exp(w)
    decay = jnp.exp(-ew)
    a = _sigmoid(a0 + apl)
    if svl is not None:
        sv = _sigmoid(v0p + svl)
        v = v0 + (vf - v0) * sv
    else:
        sv, v = None, v0
    kkp = k * kkg
    nrm = jnp.sqrt(_segsum(kkp * kkp, C_HEAD))
    inv = 1.0 / jnp.maximum(nrm, 1e-12)
    kk = kkp * inv
    k2 = k * (1.0 + (a - 1.0) * kag)
    return dict(wp=wp, ew=ew, decay=decay, a=a, sv=sv, v=v, kkp=kkp, nrm=nrm, inv=inv, kk=kk, k2=k2)


def _rwkv_post(y, r, k2, v, gn_g, gn_b, rk):
    mu = _segsum(y, C_HEAD) * (1.0 / C_HEAD)
    yc = y - mu
    rstd = lax.rsqrt(_segsum(yc * yc, C_HEAD) * (1.0 / C_HEAD) + C_GN_EPS)
    yh = yc * rstd
    s = _segsum(r * k2 * rk, C_HEAD)
    return yh, rstd, s, yh * gn_g + gn_b + s * v


def _rwkv_fwd(h, gain, P, vfirst):
    S, D = h.shape
    nh = D // C_HEAD
    vres = vfirst is not None
    hn = _rms_fwd("rms_fwd_f32", h, gain, F32)
    hs = jnp.concatenate([jnp.zeros((1, D), F32), hn[:-1]], axis=0)

    def mixf(x, xs, m0, m1, m2, m3, m4, m5):
        xx = xs - x
        return tuple(x + xx * m for m in (m0, m1, m2, m3, m4, m5))

    xr, xw, xk, xv, xa, xg = _rowwise(mixf, "rwkv_mix", S, _tm(S, D, 8),
                                      [('row', hn), ('row', hs)] + [('full', m) for m in P['mix']], [(D, BF16)] * 6)
    r = _mm_nn(xr, P['wr'], "rwkv_proj", F32)
    k = _mm_nn(xk, P['wk'], "rwkv_proj", F32)
    v0 = _mm_nn(xv, P['wv'], "rwkv_proj", F32)
    lw1 = _mm_nn(xw, P['w1'], "rwkv_lora_in", F32)
    la1 = _mm_nn(xa, P['a1'], "rwkv_lora_in", F32)
    lg1 = _mm_nn(xg, P['g1'], "rwkv_lora_in_g", F32)
    ins = [('row', lw1), ('row', la1), ('row', lg1)]
    outs = [(lw1.shape[1], BF16), (la1.shape[1], BF16), (lg1.shape[1], BF16)]
    if vres:
        lv1 = _mm_nn(xv, P['v1'], "rwkv_lora_in_v", F32)
        ins.append(('row', lv1))
        outs.append((lv1.shape[1], BF16))

    def lora_act(*xs):
        res = [jnp.tanh(xs[0]), xs[1], _sigmoid(xs[2])]
        return tuple(res + list(xs[3:]))

    acts = _rowwise(lora_act, "rwkv_lora_act", S, _tm(S, 1024, 4), ins, outs)
    tw, la1b, sg = acts[0], acts[1], acts[2]
    wpl = _mm_nn(tw, P['w2'], "rwkv_lora_out", F32)
    apl = _mm_nn(la1b, P['a2'], "rwkv_lora_out", F32)
    g = _mm_nn(sg, P['g2'], "rwkv_lora_out_g", F32)
    prep_ins = [('row', r), ('row', k), ('row', v0), ('row', wpl), ('row', apl),
                ('full', P['w0']), ('full', P['a0']), ('full', P['kk']), ('full', P['ka'])]
    svl = lv1b = None
    if vres:
        lv1b = acts[3]
        svl = _mm_nn(lv1b, P['v2'], "rwkv_lora_out_v", F32)
        prep_ins += [('row', svl), ('row', vfirst), ('full', P['v0'])]

    def prep(*xs):
        q = _rwkv_prep(*xs)
        return q['decay'], q['k2'], q['v'], -q['kk'], q['kk'] * q['a']

    decay, k2, v, av, bv = _rowwise(prep, "rwkv_prep", S, _tm(S, D, 24), prep_ins, [(D, F32)] * 5)
    tk = functools.partial(_to_k, n_heads=nh, dk=C_HEAD)
    wk_, kk_, rk_, ak_, bk_ = tk(decay), tk(k2), tk(r), tk(av), tk(bv)
    vrow = _to_row(v, nh, C_HEAD)
    yrow, hist, sarow = _scan_fwd("rwkv_scan_fwd", nh, wk_, kk_, rk_, vrow, ak_, bk_)
    y = _from_row(yrow, nh, C_HEAD)

    def post(yb, rb, k2b, vb, gb, gn_g, gn_b, rkf):
        return (_rwkv_post(yb, rb, k2b, vb, gn_g, gn_b, rkf)[3] * gb,)

    zg = _rowwise(post, "rwkv_post", S, _tm(S, D, 16),
                  [('row', y), ('row', r), ('row', k2), ('row', v), ('row', g),
                   ('full', P['gn_g']), ('full', P['gn_b']), ('full', P['rk'])], [(D, BF16)])[0]
    h2 = _mm_nn(zg, P['wo'], "rwkv_out", F32, res=h)
    cache = dict(h=h, hn=hn, hs=hs, x=(xr, xw, xk, xv, xa, xg), r=r, k=k, v0=v0, tw=tw, la1b=la1b, sg=sg, lv1b=lv1b,
                 wpl=wpl, apl=apl, svl=svl, g=g, k2=k2, v=v, scan=(wk_, kk_, rk_, vrow, ak_, bk_, hist, sarow), y=y,
                 zg=zg, vfirst=vfirst)
    return h2, cache, (v if not vres else vfirst)


def _rwkv_bwd(dh, cache, gain, P, dvfirst_in):
    c = cache
    h = c['h']
    S, D = h.shape
    nh = D // C_HEAD
    vres = c['vfirst'] is not None
    xr, xw, xk, xv, xa, xg = c['x']
    dhb = dh.astype(BF16)
    dwo = _mm_tn(c['zg'], dhb, P['wo'], "rwkv_dwo", BF16)
    dzg = _mm_nt(dhb, P['wo'], "rwkv_dzg", F32)

    def post_bwd(dzgb, yb, rb, k2b, vb, gb, gn_g, gn_b, rkf):
        yh, rstd, s, z = _rwkv_post(yb, rb, k2b, vb, gn_g, gn_b, rkf)
        dz = dzgb * gb
        dyh = dz * gn_g
        m1 = _segsum(dyh, C_HEAD) * (1.0 / C_HEAD)
        m2 = _segsum(dyh * yh, C_HEAD) * (1.0 / C_HEAD)
        dy = rstd * (dyh - m1 - yh * m2)
        ds = _segsum(dz * vb, C_HEAD)
        return (dy, dzgb * z, ds * k2b * rkf, ds * rb * rkf, dz * s,
                jnp.sum(dz * yh, axis=0, keepdims=True), jnp.sum(dz, axis=0, keepdims=True),
                jnp.sum(ds * rb * k2b, axis=0, keepdims=True))

    dy, dgb, dr_b, dk2_b, dv_b, dgn_g, dgn_b, drk = _rowwise(
        post_bwd, "rwkv_post_bwd", S, _tm(S, D, 28),
        [('row', dzg), ('row', c['y']), ('row', c['r']), ('row', c['k2']), ('row', c['v']), ('row', c['g']),
         ('full', P['gn_g']), ('full', P['gn_b']), ('full', P['rk'])],
        [(D, F32), (D, BF16), (D, F32), (D, F32), (D, F32)], [(1, D)] * 3)
    wk_, kk_, rk_, vrow, ak_, bk_, hist, sarow = c['scan']
    dwk, dkk, drk_s, dvrow, dak, dbk = _scan_bwd("rwkv_scan_bwd", nh, wk_, kk_, rk_, vrow, hist,
                                                 _to_row(dy, nh, C_HEAD), ak_, bk_, sarow)
    fk = functools.partial(_from_k, n_heads=nh, dk=C_HEAD)
    ddecay, dk2_s, dr_s, dA, dB = fk(dwk), fk(dkk), fk(drk_s), fk(dak), fk(dbk)
    dv_s = _from_row(dvrow, nh, C_HEAD)
    dr_t, dk2_t, dv_t = dr_s + dr_b, dk2_s + dk2_b, dv_s + dv_b
    if dvfirst_in is not None:
        dv_t = dv_t + dvfirst_in
    ins = [('row', c['r']), ('row', c['k']), ('row', c['v0']), ('row', c['wpl']), ('row', c['apl']),
           ('full', P['w0']), ('full', P['a0']), ('full', P['kk']), ('full', P['ka'])]
    if vres:
        ins += [('row', c['svl']), ('row', c['vfirst']), ('full', P['v0'])]
    n_fwd = len(ins)
    ins += [('row', t) for t in (dr_t, ddecay, dk2_t, dv_t, dA, dB)]

    def prep_bwd(*xs):
        q = _rwkv_prep(*xs[:n_fwd])
        kb, kkg, kag = xs[1], xs[7], xs[8]
        dr, ddec, dk2, dv, dav, dbv = xs[n_fwd:]
        a, kk, kkp, inv = q['a'], q['kk'], q['kkp'], q['inv']
        dkk = dbv * a - dav
        da = dbv * kk + dk2 * kb * kag
        dk = dk2 * (1.0 + (a - 1.0) * kag)
        pr = _segsum(dkk * kkp, C_HEAD)
        dkkp = dkk * inv - jnp.where(q['nrm'] > 1e-12, kkp * pr * inv * inv * inv, 0.0)
        dk = dk + dkkp * kkg
        dap = da * a * (1.0 - a)
        dwp = ddec * q['decay'] * (-q['ew']) * _sigmoid(-q['wp'])
        outs = [dr, dk]
        accs = [jnp.sum(dwp, axis=0, keepdims=True), jnp.sum(dap, axis=0, keepdims=True),
                jnp.sum(dkkp * kb, axis=0, keepdims=True), jnp.sum(dk2 * kb * (a - 1.0), axis=0, keepdims=True)]
        if vres:
            v0b, vfb, sv = xs[2], xs[10], q['sv']
            dsvp = dv * (vfb - v0b) * sv * (1.0 - sv)
            outs += [dv * (1.0 - sv), dwp, dap, dsvp, dv * sv]
            accs.append(jnp.sum(dsvp, axis=0, keepdims=True))
        else:
            outs += [dv, dwp, dap]
        return tuple(outs + accs)

    outs = [(D, BF16)] * 5 + ([(D, BF16), (D, F32)] if vres else [])
    res = _rowwise(prep_bwd, "rwkv_prep_bwd", S, _tm(S, D, 40), ins, outs, [(1, D)] * (5 if vres else 4))
    drb, dkb, dv0b, dwpb, dapb = res[:5]
    dvfirst_out = res[6] if vres else None
    accs = res[len(outs):]
    G = dict(wo=dwo, w0=accs[0], a0=accs[1], kk=accs[2], ka=accs[3], gn_g=dgn_g, gn_b=dgn_b, rk=drk)
    G['w2'] = _mm_tn(c['tw'], dwpb, P['w2'], "rwkv_dlora_out", F32)
    G['a2'] = _mm_tn(c['la1b'], dapb, P['a2'], "rwkv_dlora_out", F32)
    G['g2'] = _mm_tn(c['sg'], dgb, P['g2'], "rwkv_dlora_out_g", F32)
    dtw = _mm_nt(dwpb, P['w2'], "rwkv_dlora_mid", F32)
    dla1 = _mm_nt(dapb, P['a2'], "rwkv_dlora_mid", F32)
    dsg = _mm_nt(dgb, P['g2'], "rwkv_dlora_mid_g", F32)
    ins = [('row', dtw), ('row', c['tw']), ('row', dla1), ('row', dsg), ('row', c['sg'])]
    outs = [(dtw.shape[1], BF16), (dla1.shape[1], BF16), (dsg.shape[1], BF16)]
    if vres:
        dsvpb = res[5]
        G['v0'] = accs[4]
        G['v2'] = _mm_tn(c['lv1b'], dsvpb, P['v2'], "rwkv_dlora_out_v", F32)
        dlv1 = _mm_nt(dsvpb, P['v2'], "rwkv_dlora_mid_v", F32)
        ins.append(('row', dlv1))
        outs.append((dlv1.shape[1], BF16))

    def lora_act_bwd(dtwb, twb, dla1b_, dsgb, sgb, *rest):
        twb, sgb = twb.astype(F32), sgb.astype(F32)
        return tuple([dtwb * (1.0 - twb * twb), dla1b_, dsgb * sgb * (1.0 - sgb)] + list(rest))

    acts = _rowwise(lora_act_bwd, "rwkv_lora_act_bwd", S, _tm(S, 1024, 6), ins, outs)
    dlw1b, dla1b, dlg1b = acts[0], acts[1], acts[2]
    G['w1'] = _mm_tn(xw, dlw1b, P['w1'], "rwkv_dlora_in", F32)
    G['a1'] = _mm_tn(xa, dla1b, P['a1'], "rwkv_dlora_in", F32)
    G['g1'] = _mm_tn(xg, dlg1b, P['g1'], "rwkv_dlora_in_g", F32)
    G['wr'] = _mm_tn(xr, drb, P['wr'], "rwkv_dwproj", BF16)
    G['wk'] = _mm_tn(xk, dkb, P['wk'], "rwkv_dwproj", BF16)
    G['wv'] = _mm_tn(xv, dv0b, P['wv'], "rwkv_dwproj", BF16)
    dxw = _mm_nt(dlw1b, P['w1'], "rwkv_dx_lora", F32)
    dxa = _mm_nt(dla1b, P['a1'], "rwkv_dx_lora", F32)
    dxg = _mm_nt(dlg1b, P['g1'], "rwkv_dx_lora_g", F32)
    dxr = _mm_nt(drb, P['wr'], "rwkv_dx", F32)
    dxk = _mm_nt(dkb, P['wk'], "rwkv_dx", F32)
    dxv = _mm_nt(dv0b, P['wv'], "rwkv_dx", F32)
    if vres:
        G['v1'] = _mm_tn(xv, acts[3], P['v1'], "rwkv_dlora_in_v", F32)
        dxv = _mm_nt(acts[3], P['v1'], "rwkv_dx_lora_v", F32, res=dxv)

    def mix_bwd(x, xs, m0, m1, m2, m3, m4, m5, d0, d1, d2, d3, d4, d5):
        xx = xs - x
        ds_ = (d0, d1, d2, d3, d4, d5)
        dxx = d0 * m0 + d1 * m1 + d2 * m2 + d3 * m3 + d4 * m4 + d5 * m5
        dsum = d0 + d1 + d2 + d3 + d4 + d5
        return tuple([dsum - dxx, dxx] + [jnp.sum(d * xx, axis=0, keepdims=True) for d in ds_])

    res = _rowwise(mix_bwd, "rwkv_mix_bwd", S, _tm(S, D, 24),
                   [('row', c['hn']), ('row', c['hs'])] + [('full', m) for m in P['mix']]
                   + [('row', d) for d in (dxr, dxw, dxk, dxv, dxa, dxg)], [(D, F32), (D, F32)], [(1, D)] * 6)
    dx_here, dxs = res[0], res[1]
    G['mix'] = res[2:]
    dhn = dx_here + jnp.concatenate([dxs[1:], jnp.zeros((1, D), F32)], axis=0)
    dh2, G['gain'] = _rms_bwd("rms_bwd", h, gain, dhn, dh)
    return dh2, G, dvfirst_out


def _loss_bwd(h, target, gain):
    S, D = h.shape

    def fn(hb, tb, g):
        r = lax.rsqrt(jnp.mean(hb * hb, axis=-1, keepdims=True) + RMS_EPS)
        xh = hb * r
        e = xh * g - tb
        dy = e * (1.0 / D)
        dxh = dy * g
        dx = r * (dxh - xh * jnp.mean(dxh * xh, axis=-1, keepdims=True))
        part = jnp.sum(jnp.sum(e * e, axis=-1, keepdims=True), axis=0, keepdims=True) * (0.5 / D)
        return dx, jnp.sum(dy * xh, axis=0, keepdims=True), jnp.broadcast_to(part, (1, LANES))

    dh, dgain, part = _rowwise(fn, "loss", S, _tm(S, D, 8), [('row', h), ('row', target), ('full', gain)],
                               [(D, F32)], [(1, D), (1, LANES)])
    return part[0, 0], dh, dgain


WEIGHTS = ['norms', 'final_norm', 'ffn_wg', 'ffn_wu', 'ffn_wd', 'ple_wp', 'ple_wg', 'e_w_in', 'e_w_out', 'a_vnorm',
           'a_ws', 'a_bs', 'b_onorm', 'b_lb_logits', 'c_mix', 'c_wr', 'c_wk', 'c_wv', 'c_wo', 'c_w0', 'c_w1', 'c_w2',
           'c_a0', 'c_a1', 'c_a2', 'c_g1', 'c_g2', 'c_kk', 'c_ka', 'c_rk', 'c_gn_g', 'c_gn_b', 'c_v0', 'c_v1', 'c_v2']
BIG = {'ffn_wg': 'col', 'ffn_wu': 'col', 'ffn_wd': 'row', 'ple_wg': 'row', 'e_w_in': 'col', 'e_w_out': 'row',
       'c_wr': 'row', 'c_wk': 'row', 'c_wv': 'row', 'c_wo': 'row'}
SMALL = ['norms', 'ple_wp', 'c_mix', 'c_w0', 'c_w1', 'c_w2', 'c_a0', 'c_a1', 'c_a2', 'c_g1', 'c_g2', 'c_kk', 'c_ka',
         'c_gn_g', 'c_gn_b', 'c_v0', 'c_v1', 'c_v2']
SMALL_MM = {'ple_wp': 'col', 'c_w1': 'row', 'c_w2': 'col', 'c_a1': 'row', 'c_a2': 'col', 'c_g1': 'row', 'c_g2': 'col',
            'c_v1': 'row', 'c_v2': 'col'}
REP = ['final_norm', 'a_vnorm', 'a_ws', 'a_bs', 'b_onorm', 'b_lb_logits', 'c_rk']
PACK_QUANTUM = 1024 * LANES


def _pack(arrays, lead=0):
    lead_shape = arrays[0].shape[:lead]
    flat = jnp.concatenate([a.reshape(lead_shape + (-1,)).astype(F32) for a in arrays], axis=-1)
    pad = (-flat.shape[-1]) % PACK_QUANTUM
    if pad:
        flat = jnp.concatenate([flat, jnp.zeros(lead_shape + (pad,), F32)], axis=-1)
    return flat.reshape(lead_shape + (-1, LANES))


def _unpack(packed, names, shapes, lead=0):
    lead_shape = packed.shape[:lead]
    flat = packed.reshape(lead_shape + (-1,))
    out, off = {}, 0
    for n, s in zip(names, shapes):
        size = 1
        for d in s:
            size *= d
        out[n] = flat[..., off:off + size].reshape(lead_shape + tuple(s))
        off += size
    return out


def _full_vec(g):
    return jnp.moveaxis(g, 0, -2).reshape(g.shape[1:-1] + (N_CHIPS * g.shape[-1],))


def _vec_shards(g):
    return jnp.moveaxis(g.reshape(g.shape[:-1] + (N_CHIPS, g.shape[-1] // N_CHIPS)), -2, 0)


def _lower_bounds(logits):
    probs = jax.nn.softmax(logits.astype(F32), axis=0)
    return jnp.cumsum(probs, axis=0) - probs[0]


def kernel(x, p, norms, final_norm, ffn_wg, ffn_wu, ffn_wd, ple_wp, ple_wg, e_w_in, e_w_out, a_vnorm, a_ws, a_bs, b_onorm, b_lb_logits, c_mix, c_wr, c_wk, c_wv, c_wo, c_w0, c_w1, c_w2, c_a0, c_a1, c_a2, c_g1, c_g2, c_kk, c_ka, c_rk, c_gn_g, c_gn_b, c_v0, c_v1, c_v2, loss_target, m_norms, m_final_norm, m_ffn_wg, m_ffn_wu, m_ffn_wd, m_ple_wp, m_ple_wg, m_e_w_in, m_e_w_out, m_a_vnorm, m_a_ws, m_a_bs, m_b_onorm, m_b_lb_logits, m_c_mix, m_c_wr, m_c_wk, m_c_wv, m_c_wo, m_c_w0, m_c_w1, m_c_w2, m_c_a0, m_c_a1, m_c_a2, m_c_g1, m_c_g2, m_c_kk, m_c_ka, m_c_rk, m_c_gn_g, m_c_gn_b, m_c_v0, m_c_v1, m_c_v2, v_norms, v_final_norm, v_ffn_wg, v_ffn_wu, v_ffn_wd, v_ple_wp, v_ple_wg, v_e_w_in, v_e_w_out, v_a_vnorm, v_a_ws, v_a_bs, v_b_onorm, v_b_lb_logits, v_c_mix, v_c_wr, v_c_wk, v_c_wv, v_c_wo, v_c_w0, v_c_w1, v_c_w2, v_c_a0, v_c_a1, v_c_a2, v_c_g1, v_c_g2, v_c_kk, v_c_ka, v_c_rk, v_c_gn_g, v_c_gn_b, v_c_v0, v_c_v1, v_c_v2):
    A = dict(locals())
    assert x.shape[0] == 1, "one example per device"

    big_names = list(BIG)
    gathered = _all_gather("gather_weights", [A[n].astype(BF16).reshape(-1, A[n].shape[-1]) for n in big_names],
                           [_pack([A[n] for n in SMALL])])
    GB = {n: gathered[i].reshape((N_CHIPS,) + A[n].shape) for i, n in enumerate(big_names)}
    GS = _unpack(gathered[-1], SMALL, [A[n].shape for n in SMALL], lead=1)

    part, grad_x, big_entries, sm_pack, rep_pack = _local_step(A, GB, GS)
    loss = lax.psum(part, ("x", "y", "c"))

    c_arr = lax.axis_index("c").astype(jnp.int32).reshape(1)
    grads = [e[2] for e in big_entries]
    swapped = _swap_halves("reduce_swap_cores", grads)
    parts = [_add_own_half("reduce_add_cores", g, s, c_arr) for g, s in zip(grads, swapped)]
    slots = _scatter_chips("reduce_scatter_chips", parts)
    halves = [_sum_slots("reduce_sum_chips", s, (3, 0, 1, 2)) for s in slots]
    places = [(big_names.index(n), idx) for n, idx, _ in big_entries]
    big_grads = _join_halves("reduce_join_cores", halves, places,
                             [jax.ShapeDtypeStruct(A[n].shape, F32) for n in big_names])
    sm_slots, rep_slots = _spread_all("reduce_small", sm_pack, rep_pack)
    sm_grad = _sum_slots("reduce_sum_small", sm_slots, tuple(range(8)))
    rep_grad = _sum_slots("reduce_sum_small", rep_slots, tuple(range(8)))

    outs = {}
    for o, n in enumerate(big_names):
        shp, C = A[n].shape, A[n].shape[-1]
        g = big_grads[o]
        d, nm, nv = _adamw("adamw", g.reshape(-1, C), A[n].reshape(-1, C), A['m_' + n].reshape(-1, C),
                           A['v_' + n].reshape(-1, C))
        outs[n] = (g, d.reshape(shp), nm.reshape(shp), nv.reshape(shp))
    for names, g in ((SMALL, sm_grad), (REP, rep_grad)):
        shapes = [A[n].shape for n in names]
        res = _adamw("adamw_packed", g, _pack([A[n] for n in names]), _pack([A['m_' + n] for n in names]),
                     _pack([A['v_' + n] for n in names]))
        un = [_unpack(t, names, shapes) for t in (g,) + tuple(res)]
        for n in names:
            outs[n] = tuple(u[n] for u in un)
    return (loss, grad_x, *[outs[n][0] for n in WEIGHTS], *[outs[n][1] for n in WEIGHTS],
            *[outs[n][2] for n in WEIGHTS], *[outs[n][3] for n in WEIGHTS])


def _local_step(A, GB, GS):
    x, p, a_vnorm, a_ws, a_bs, b_onorm, b_lb_logits, c_rk = (A[n] for n in (
        'x', 'p', 'a_vnorm', 'a_ws', 'a_bs', 'b_onorm', 'b_lb_logits', 'c_rk'))
    S, D = x.shape[1], x.shape[2]
    depth = A['ffn_wg'].shape[0]
    aw = D // 2
    h = x[0]
    target = A['loss_target'][0]
    final_norm = A['final_norm']
    pb = p[:, 0].astype(BF16)
    GSb = {n: GS[n].astype(BF16) for n in SMALL_MM}

    def bigv(n, *idx):
        return WV(GB[n], idx, BIG[n])

    def smv(n, *idx):
        return WV(GSb[n], idx, SMALL_MM[n])

    def row(v):
        return v.reshape(1, -1)

    vecs = {n: _full_vec(GS[n]) for n in SMALL if n not in SMALL_MM}
    lb_all, lb_vjp = jax.vjp(_lower_bounds, b_lb_logits)

    def rwkv_params(j):
        P = dict(mix=[vecs['c_mix'][j, q:q + 1] for q in range(6)],
                 wr=bigv('c_wr', j), wk=bigv('c_wk', j), wv=bigv('c_wv', j), wo=bigv('c_wo', j),
                 w1=smv('c_w1', j), w2=smv('c_w2', j), a1=smv('c_a1', j), a2=smv('c_a2', j),
                 g1=smv('c_g1', j), g2=smv('c_g2', j),
                 w0=row(vecs['c_w0'][j]), a0=row(vecs['c_a0'][j]), kk=row(vecs['c_kk'][j]), ka=row(vecs['c_ka'][j]),
                 gn_g=row(vecs['c_gn_g'][j]), gn_b=row(vecs['c_gn_b'][j]), rk=c_rk[j].reshape(1, D))
        if j > 0:
            P.update(v0=row(vecs['c_v0'][j - 1]), v1=smv('c_v1', j - 1), v2=smv('c_v2', j - 1))
        return P

    def even_params(i):
        j = i // 2
        return (bigv('e_w_in', j), bigv('e_w_out', j), a_vnorm[j:j + 1], a_ws[j], a_bs[j].T, b_onorm[j:j + 1],
                lb_all[i:i + 1])

    def gain(i, q):
        return row(vecs['norms'][i, q])

    def ffn_views(i, q):
        return bigv('ffn_wg', i, q), bigv('ffn_wu', i, q), bigv('ffn_wd', i, q)

    caches, vfirst = [], None
    for i in range(depth):
        c = {}
        h, c['f1'] = _ffn_fwd(h, gain(i, 0), *ffn_views(i, 0))
        if i % 2 == 0:
            h, c['mix'] = _even_fwd(h, gain(i, 1), *even_params(i))
        else:
            h, c['mix'], vfirst = _rwkv_fwd(h, gain(i, 1), rwkv_params(i // 2), vfirst if i // 2 > 0 else None)
        h, c['f2'] = _ffn_fwd(h, gain(i, 2), *ffn_views(i, 1))
        h, c['ple'] = _ple_fwd(h, gain(i, 3), pb[i], bigv('ple_wg', i), smv('ple_wp', i))
        caches.append(c)

    part, dh, dfinal = _loss_bwd(h, target, final_norm.reshape(1, D))
    big_entries = []
    sm = {n: {} for n in SMALL}
    rep = {n: {} for n in REP}
    dvfirst = None
    for i in reversed(range(depth)):
        j, c = i // 2, caches[i]
        dh, dg, dwgate, dwproj = _ple_bwd(dh, c['ple'], gain(i, 3), pb[i], bigv('ple_wg', i), smv('ple_wp', i))
        sm['norms'][(i, 3)] = dg
        sm['ple_wp'][(i,)] = dwproj
        big_entries.append(('ple_wg', (i,), dwgate))
        for q, key in ((1, 'f2'), (0, 'f1')):
            if key == 'f1':
                if i % 2 == 0:
                    dh, G = _even_bwd(dh, c['mix'], gain(i, 1), *even_params(i))
                    big_entries += [('e_w_in', (j,), G['w_in']), ('e_w_out', (j,), G['w_out'])]
                    rep['a_vnorm'][j], rep['a_ws'][j], rep['a_bs'][j] = G['vgain'][0], G['ws'], G['bs_t'].T
                    rep['b_onorm'][j], rep['b_lb_logits'][i] = G['onorm'][0], G['lb'][0]
                else:
                    dh, G, dvf = _rwkv_bwd(dh, c['mix'], gain(i, 1), rwkv_params(j), dvfirst if j == 0 else None)
                    if dvf is not None:
                        dvfirst = dvf if dvfirst is None else dvfirst + dvf
                    big_entries += [('c_wr', (j,), G['wr']), ('c_wk', (j,), G['wk']), ('c_wv', (j,), G['wv']),
                                    ('c_wo', (j,), G['wo'])]
                    for n in ('w0', 'a0', 'kk', 'ka', 'gn_g', 'gn_b', 'w1', 'w2', 'a1', 'a2', 'g1', 'g2'):
                        sm['c_' + n][(j,)] = G[n]
                    sm['c_mix'][(j,)] = jnp.concatenate(G['mix'], axis=0)
                    rep['c_rk'][j] = G['rk'].reshape(c_rk.shape[1:])
                    if j > 0:
                        for n in ('v0', 'v1', 'v2'):
                            sm['c_' + n][(j - 1,)] = G[n]
                sm['norms'][(i, 1)] = G['gain']
            dh, dg, dwg, dwu, dwd = _ffn_bwd(dh, c[key], gain(i, 2 * q), *ffn_views(i, q))
            sm['norms'][(i, 2 * q)] = dg
            big_entries += [('ffn_wg', (i, q), dwg), ('ffn_wu', (i, q), dwu), ('ffn_wd', (i, q), dwd)]
    grad_x = dh[None]

    def stacked(blocks, lead_shape):
        def rec(prefix, dims):
            if not dims:
                return blocks[prefix]
            return jnp.stack([rec(prefix + (q,), dims[1:]) for q in range(dims[0])], axis=0)
        return rec((), tuple(lead_shape))

    sm_shards = []
    for n in SMALL:
        blk = A[n].shape
        if n in SMALL_MM:
            g = jnp.moveaxis(stacked(sm[n], blk[:-2]), len(blk) - 2, 0)
        else:
            lead = blk[:-1] if n != 'c_mix' else blk[:-2]
            full = stacked(sm[n], lead)
            g = _vec_shards(full.reshape(blk[:-1] + (D,)))
        sm_shards.append(g)
    sm_pack = _pack(sm_shards, lead=1)
    dlb = jnp.stack([rep['b_lb_logits'].get(i, jnp.zeros((aw,), F32)) for i in range(depth)], axis=0)
    rep_grads = dict(final_norm=dfinal[0], a_vnorm=stacked({(k,): v for k, v in rep['a_vnorm'].items()}, a_vnorm.shape[:1]),
                     a_ws=stacked({(k,): v for k, v in rep['a_ws'].items()}, a_ws.shape[:1]),
                     a_bs=stacked({(k,): v for k, v in rep['a_bs'].items()}, a_bs.shape[:1]),
                     b_onorm=stacked({(k,): v for k, v in rep['b_onorm'].items()}, b_onorm.shape[:1]),
                     b_lb_logits=lb_vjp(dlb)[0],
                     c_rk=stacked({(k,): v for k, v in rep['c_rk'].items()}, c_rk.shape[:1]))
    rep_pack = _pack([rep_grads[n] for n in REP])
    return part, grad_x, big_entries, sm_pack, rep_pack
```

```python
import functools

import jax
import jax.numpy as jnp
from jax import lax
from jax.experimental import pallas as pl
from jax.experimental.pallas import tpu as pltpu

F32 = jnp.float32
BF16 = jnp.bfloat16
MESH = pl.DeviceIdType.MESH

LANES = 128
VMEM_LIMIT = 56 * 1024 * 1024
MM_VMEM_BUDGET = 36 * 1024 * 1024
N_CHIPS = 4

RMS_EPS = 1e-6
A_GROUP = 128
A_CHUNK = 128
B_HEAD = 128
B_MIN_F = 1e-30
C_HEAD = 64
C_GN_EPS = 64e-5
ADAM_LR = 0.001
ADAM_B1 = 0.9
ADAM_B2 = 0.999
ADAM_EPS = 1e-08
ADAM_WD = 0.01
ADAM_STEP = 10


def _sigmoid(x):
    return 1.0 / (1.0 + jnp.exp(-x))


def _silu(x):
    return x * _sigmoid(x)


def _dsilu(x):
    s = _sigmoid(x)
    return s * (1.0 + x * (1.0 - s))


_GELU_C = 0.7978845608028654


def _gelu(x):
    return 0.5 * x * (1.0 + jnp.tanh(_GELU_C * (x + 0.044715 * x * x * x)))


def _dgelu(x):
    th = jnp.tanh(_GELU_C * (x + 0.044715 * x * x * x))
    return 0.5 * (1.0 + th) + 0.5 * x * (1.0 - th * th) * _GELU_C * (1.0 + 3.0 * 0.044715 * x * x)


def _softplus(x):
    return jnp.maximum(x, 0.0) + jnp.log(1.0 + jnp.exp(-jnp.abs(x)))


def _seg_ones(seg):
    i = lax.broadcasted_iota(jnp.int32, (LANES, LANES), 0) // seg
    j = lax.broadcasted_iota(jnp.int32, (LANES, LANES), 1) // seg
    return jnp.where(i == j, 1.0, 0.0).astype(BF16)


def _segsum(x, seg):
    ones = _seg_ones(seg)
    outs = []
    for j in range(x.shape[1] // LANES):
        xb = x[:, j * LANES:(j + 1) * LANES]
        hi = xb.astype(BF16)
        r1 = xb - hi.astype(F32)
        mid = r1.astype(BF16)
        lo = (r1 - mid.astype(F32)).astype(BF16)
        acc = jnp.dot(hi, ones, preferred_element_type=F32)
        acc = acc + jnp.dot(mid, ones, preferred_element_type=F32)
        acc = acc + jnp.dot(lo, ones, preferred_element_type=F32)
        outs.append(acc)
    return outs[0] if len(outs) == 1 else jnp.concatenate(outs, axis=1)


def _rowwise(fn, name, rows, tm, ins, outs, accs=()):
    n_in, n_out, n_acc = len(ins), len(outs), len(accs)
    arrays, in_specs = [], []
    for spec in ins:
        kind, arr = spec[0], spec[1]
        arrays.append(arr)
        if kind == 'row':
            in_specs.append(pl.BlockSpec((tm, arr.shape[1]), lambda i: (i, 0)))
        elif kind == 'col':
            in_specs.append(pl.BlockSpec((tm, spec[2]), functools.partial(lambda i, cb: (i, cb), cb=spec[3])))
        else:
            in_specs.append(pl.BlockSpec(arr.shape, functools.partial(lambda i, nd: (0,) * nd, nd=arr.ndim)))
    out_shape = [jax.ShapeDtypeStruct((rows, w), dt) for (w, dt) in outs]
    out_specs = [pl.BlockSpec((tm, w), lambda i: (i, 0)) for (w, _) in outs]
    out_shape += [jax.ShapeDtypeStruct(s, F32) for s in accs]
    out_specs += [pl.BlockSpec(s, functools.partial(lambda i, nd: (0,) * nd, nd=len(s))) for s in accs]

    def body(*refs):
        vals = fn(*[r[...] for r in refs[:n_in]])
        if not isinstance(vals, (tuple, list)):
            vals = (vals,)
        for r, v in zip(refs[n_in:n_in + n_out], vals[:n_out]):
            r[...] = v.astype(r.dtype)
        if n_acc:
            acc_refs = refs[n_in + n_out:]

            @pl.when(pl.program_id(0) == 0)
            def _():
                for r in acc_refs:
                    r[...] = jnp.zeros(r.shape, F32)

            for r, v in zip(acc_refs, vals[n_out:]):
                r[...] += v

    res = pl.pallas_call(
        body, name=name, grid=(rows // tm,), in_specs=in_specs, out_specs=out_specs, out_shape=out_shape,
        compiler_params=pltpu.CompilerParams(
            dimension_semantics=("arbitrary",) if n_acc else ("parallel",), vmem_limit_bytes=VMEM_LIMIT),
    )(*arrays)
    return res


class WV:
    def __init__(self, arr, idx, kind):
        self.arr, self.idx, self.kind = arr, tuple(idx), kind
        self.ns = arr.shape[0]
        self.R, self.C = arr.shape[-2:]
        self.K = self.R * (self.ns if kind == 'row' else 1)
        self.N = self.C * (self.ns if kind == 'col' else 1)

    def spec(self, br, bc, rmap, cmap):
        lead = (None,) * (1 + len(self.idx))
        nrb, ncb = self.R // br, self.C // bc
        idx, kind = self.idx, self.kind

        def index_map(*g):
            ri, ci = rmap(*g), cmap(*g)
            if kind == 'row':
                return (ri // nrb, *idx, ri % nrb, ci)
            return (ci // ncb, *idx, ri, ci % ncb)

        return pl.BlockSpec(lead + (br, bc), index_map)


def _tile_options(n):
    return [n] + [d for d in range(n - LANES, LANES - 1, -LANES) if n % d == 0]


def _pick_tiles(opt_m, opt_n, opt_k, out_bytes, has_res, full_k):
    best, best_key = None, None
    for tm in opt_m:
        for tn in opt_n:
            for tk in opt_k:
                multi = tk != full_k
                est = 2 * (tm * tk * 2 + tk * tn * 2 + tm * tn * out_bytes) + tm * tn * 4 * (2 if multi else 1)
                if has_res:
                    est += 2 * tm * tn * 4
                if est > MM_VMEM_BUDGET:
                    continue
                key = (tm * tn * tk, tk)
                if best is None or key > best_key:
                    best, best_key = (tm, tn, tk), key
    return best


def _mm_call(name, dims, grid, in_specs, out_spec, out_shape, operands, nk, acc_shape, scale, has_res):
    def body(*refs):
        a_ref, b_ref = refs[0], refs[1]
        res_ref = refs[2] if has_res else None
        o_ref = refs[3] if has_res else refs[2]

        def finalize(acc):
            acc = acc * scale if scale != 1.0 else acc
            if has_res:
                acc = acc + res_ref[...]
            o_ref[...] = acc.astype(o_ref.dtype)

        part = lax.dot_general(a_ref[...], b_ref[...], dims, preferred_element_type=F32)
        if nk == 1:
            finalize(part)
        else:
            acc_ref = refs[-1]
            k = pl.program_id(2)

            @pl.when(k == 0)
            def _():
                acc_ref[...] = part

            @pl.when(k > 0)
            def _():
                acc_ref[...] += part

            @pl.when(k == nk - 1)
            def _():
                finalize(acc_ref[...])

    return pl.pallas_call(
        body, name=name, grid=grid, in_specs=in_specs, out_specs=out_spec, out_shape=out_shape,
        scratch_shapes=[pltpu.VMEM(acc_shape, F32)] if nk > 1 else [],
        compiler_params=pltpu.CompilerParams(
            dimension_semantics=("parallel", "parallel", "arbitrary"), vmem_limit_bytes=VMEM_LIMIT),
    )(*operands)


def _mm_nn(a, w, name, out_dtype=F32, res=None, scale=1.0):
    M, K = a.shape
    N = w.N
    opt_n = _tile_options(w.C)
    opt_k = _tile_options(w.R)
    tm, tn, tk = _pick_tiles(_tile_options(M), opt_n, opt_k, jnp.dtype(out_dtype).itemsize, res is not None, K)
    nk = K // tk
    in_specs = [pl.BlockSpec((tm, tk), lambda n, m, k: (m, k)),
                w.spec(tk, tn, lambda n, m, k: k, lambda n, m, k: n)]
    operands = [a, w.arr]
    if res is not None:
        in_specs.append(pl.BlockSpec((tm, tn), lambda n, m, k: (m, n)))
        operands.append(res)
    return _mm_call(name, (((1,), (0,)), ((), ())), (N // tn, M // tm, nk), in_specs,
                    pl.BlockSpec((tm, tn), lambda n, m, k: (m, n)), jax.ShapeDtypeStruct((M, N), out_dtype),
                    operands, nk, (tm, tn), scale, res is not None)


def _mm_nt(a, w, name, out_dtype=F32, res=None, scale=1.0):
    M, C = a.shape
    Ko = w.K
    opt_n = _tile_options(w.R)
    opt_k = _tile_options(w.C)
    tm, tn, tk = _pick_tiles(_tile_options(M), opt_n, opt_k, jnp.dtype(out_dtype).itemsize, res is not None, C)
    nk = C // tk
    in_specs = [pl.BlockSpec((tm, tk), lambda n, m, k: (m, k)),
                w.spec(tn, tk, lambda n, m, k: n, lambda n, m, k: k)]
    operands = [a, w.arr]
    if res is not None:
        in_specs.append(pl.BlockSpec((tm, tn), lambda n, m, k: (m, n)))
        operands.append(res)
    return _mm_call(name, (((1,), (1,)), ((), ())), (Ko // tn, M // tm, nk), in_specs,
                    pl.BlockSpec((tm, tn), lambda n, m, k: (m, n)), jax.ShapeDtypeStruct((M, Ko), out_dtype),
                    operands, nk, (tm, tn), scale, res is not None)


def _mm_tn(a, dy, like, name, out_dtype=BF16, scale=1.0):
    M, K = a.shape
    N = dy.shape[1]
    out = WV(jax.ShapeDtypeStruct((like.ns, like.R, like.C), out_dtype), (), like.kind)
    opt_m = _tile_options(like.R)
    opt_n = _tile_options(like.C)
    tko, tno, tc = _pick_tiles(opt_m, opt_n, _tile_options(M), jnp.dtype(out_dtype).itemsize, False, M)
    nk = M // tc
    in_specs = [pl.BlockSpec((tc, tko), lambda i, j, c: (c, i)),
                pl.BlockSpec((tc, tno), lambda i, j, c: (c, j))]
    return _mm_call(name, (((0,), (0,)), ((), ())), (K // tko, N // tno, nk), in_specs,
                    out.spec(tko, tno, lambda i, j, c: i, lambda i, j, c: j), out.arr,
                    [a, dy], nk, (tko, tno), scale, False)


SCAN_TB = 16


def _subsum(x):
    r = x.shape[0]
    while r > 8:
        r //= 2
        x = x[:r] + x[r:]
    while r > 1:
        r //= 2
        x = x + pltpu.roll(x, r, 0)
    return x


def _head_ones(n_heads):
    i = lax.broadcasted_iota(jnp.int32, (LANES, LANES), 0) % n_heads
    j = lax.broadcasted_iota(jnp.int32, (LANES, LANES), 1) % n_heads
    return jnp.where(i == j, 1.0, 0.0).astype(BF16)


def _lanesum_mxu(x, ones):
    hi = x.astype(BF16)
    r1 = x - hi.astype(F32)
    mid = r1.astype(BF16)
    lo = (r1 - mid.astype(F32)).astype(BF16)
    acc = jnp.dot(hi, ones, preferred_element_type=F32)
    acc = acc + jnp.dot(mid, ones, preferred_element_type=F32)
    return acc + jnp.dot(lo, ones, preferred_element_type=F32)


def _head_sel(n_heads):
    h = lax.broadcasted_iota(jnp.int32, (n_heads, LANES), 0)
    lane = lax.broadcasted_iota(jnp.int32, (n_heads, LANES), 1) % n_heads
    return jnp.where(h == lane, 1.0, 0.0).astype(BF16)


def _split3(x):
    hi = x.astype(BF16)
    r1 = x - hi.astype(F32)
    mid = r1.astype(BF16)
    return hi, mid, (r1 - mid.astype(F32)).astype(BF16)


def _rows_from_nat(vn, sel):
    dn = (((0,), (0,)), ((), ()))
    hi, mid, lo = _split3(vn)
    acc = lax.dot_general(hi, sel, dn, preferred_element_type=F32)
    acc = acc + lax.dot_general(mid, sel, dn, preferred_element_type=F32)
    return acc + lax.dot_general(lo, sel, dn, preferred_element_type=F32)


def _nat_from_rows(rows, sel):
    dn = (((1,), (1,)), ((), ()))
    hi, mid, lo = _split3(rows)
    acc = lax.dot_general(sel, hi, dn, preferred_element_type=F32)
    acc = acc + lax.dot_general(sel, mid, dn, preferred_element_type=F32)
    return acc + lax.dot_general(sel, lo, dn, preferred_element_type=F32)


def _rows_of(vals):
    n = len(vals)
    if vals[0].shape[0] >= n:
        idx = lax.broadcasted_iota(jnp.int32, (n, LANES), 0)
        out = vals[0][:n]
        for i in range(1, n):
            out = jnp.where(idx == i, vals[i][:n], out)
        return out
    return jnp.concatenate([v[0:1] for v in vals], axis=0)


def _scan_fwd(name, wk, kk, rk, vnat, ak=None, bk=None):
    S, R, _ = wk.shape
    n_heads, dv = vnat.shape[1:]
    ab = ak is not None
    tb = min(SCAN_TB, S)
    grp = min(8, dv)

    def body(*refs):
        if ab:
            w_ref, k_ref, r_ref, v_ref, a_ref, b_ref, y_ref, hist_ref, sa_ref, s_ref, vt_ref = refs
        else:
            w_ref, k_ref, r_ref, v_ref, y_ref, hist_ref, s_ref, vt_ref = refs

        @pl.when(pl.program_id(0) == 0)
        def _():
            s_ref[...] = jnp.zeros(s_ref.shape, F32)

        sel = _head_sel(n_heads)
        ones = _head_ones(n_heads) if ab else None

        def joined(tiles):
            return tiles[0] if len(tiles) == 1 else jnp.concatenate(tiles, axis=0)

        def step(t, carry):
            w, k, r = w_ref[t], k_ref[t], r_ref[t]
            vt_ref[...] = _rows_from_nat(v_ref[t], sel)
            if ab:
                a, b = a_ref[t], b_ref[t]
                tiles = []
                for g0 in range(0, dv, grp):
                    tiles.append(_rows_of([_subsum(s_ref[g0 + i] * a) for i in range(grp)]))
                sa_ref[t] = _lanesum_mxu(joined(tiles), ones)
            tiles = []
            for g0 in range(0, dv, grp):
                ys = []
                for i in range(grp):
                    v = g0 + i
                    st = s_ref[v]
                    hist_ref[t, v] = st
                    st = st * w + vt_ref[pl.ds(v, 1), :] * k
                    if ab:
                        st = st + sa_ref[t, pl.ds(v, 1), :] * b
                    s_ref[v] = st
                    ys.append(_subsum(st * r))
                tiles.append(_rows_of(ys))
            y_ref[t] = _nat_from_rows(joined(tiles), sel)
            return carry

        lax.fori_loop(0, tb, step, 0, unroll=2)

    kspec = pl.BlockSpec((tb, R, LANES), lambda i: (i, 0, 0))
    nspec = pl.BlockSpec((tb, n_heads, dv), lambda i: (i, 0, 0))
    operands = [wk, kk, rk, vnat] + ([ak, bk] if ab else [])
    in_specs = [kspec, kspec, kspec, nspec] + ([kspec, kspec] if ab else [])
    out_shape = [jax.ShapeDtypeStruct((S, n_heads, dv), F32), jax.ShapeDtypeStruct((S, dv, R, LANES), F32)]
    out_specs = [nspec, pl.BlockSpec((tb, dv, R, LANES), lambda i: (i, 0, 0, 0))]
    if ab:
        out_shape.append(jax.ShapeDtypeStruct((S, dv, LANES), F32))
        out_specs.append(pl.BlockSpec((tb, dv, LANES), lambda i: (i, 0, 0)))
    res = pl.pallas_call(
        body, name=name, grid=(S // tb,), in_specs=in_specs, out_specs=out_specs, out_shape=out_shape,
        scratch_shapes=[pltpu.VMEM((dv, R, LANES), F32), pltpu.VMEM((dv, LANES), F32)],
        compiler_params=pltpu.CompilerParams(dimension_semantics=("arbitrary",), vmem_limit_bytes=VMEM_LIMIT),
    )(*operands)
    return (res[0], res[1], res[2]) if ab else (res[0], res[1], None)


def _scan_bwd(name, wk, kk, rk, vnat, hist, dynat, ak=None, bk=None, sarow=None):
    S, R, _ = wk.shape
    n_heads, dv = vnat.shape[1:]
    ab = ak is not None
    tb = min(SCAN_TB, S)
    nb = S // tb
    grp = min(8, dv)

    def body(*refs):
        if ab:
            (w_ref, k_ref, r_ref, v_ref, hist_ref, dy_ref, a_ref, b_ref, sa_ref,
             dw_ref, dk_ref, dr_ref, dv_ref, da_ref, db_ref, ds_ref, vt_ref, dyt_ref, dsa_ref) = refs
        else:
            (w_ref, k_ref, r_ref, v_ref, hist_ref, dy_ref,
             dw_ref, dk_ref, dr_ref, dv_ref, ds_ref, vt_ref, dyt_ref) = refs

        @pl.when(pl.program_id(0) == 0)
        def _():
            ds_ref[...] = jnp.zeros(ds_ref.shape, F32)

        sel = _head_sel(n_heads)
        ones = _head_ones(n_heads) if ab else None

        def joined(tiles):
            return tiles[0] if len(tiles) == 1 else jnp.concatenate(tiles, axis=0)

        def step(j, carry):
            t = tb - 1 - j
            w, k, r = w_ref[t], k_ref[t], r_ref[t]
            zero = jnp.zeros((R, LANES), F32)
            u, dw, dk, da, db = zero, zero, zero, zero, zero
            vt = _rows_from_nat(v_ref[t], sel)
            dyt = _rows_from_nat(dy_ref[t], sel)
            vt_ref[...] = vt
            dyt_ref[...] = dyt
            vd = jnp.sum(vt * dyt, axis=0, keepdims=True)
            sd = jnp.sum(sa_ref[t] * dyt, axis=0, keepdims=True) if ab else None
            if ab:
                a, b = a_ref[t], b_ref[t]
                tiles = []
                for g0 in range(0, dv, grp):
                    ps = []
                    for i in range(grp):
                        v = g0 + i
                        dst = ds_ref[v] + dyt_ref[pl.ds(v, 1), :] * r
                        ds_ref[v] = dst
                        ps.append(_subsum(dst * b))
                    tiles.append(_rows_of(ps))
                dsa_ref[...] = _lanesum_mxu(joined(tiles), ones)
            tiles = []
            for g0 in range(0, dv, grp):
                dvs = []
                for i in range(grp):
                    v = g0 + i
                    sp = hist_ref[t, v]
                    dyr = dyt_ref[pl.ds(v, 1), :]
                    vr = vt_ref[pl.ds(v, 1), :]
                    dst = ds_ref[v] if ab else ds_ref[v] + dyr * r
                    u = u + sp * dyr
                    dw = dw + dst * sp
                    dk = dk + dst * vr
                    dvs.append(_subsum(dst * k))
                    if ab:
                        db = db + dst * sa_ref[t, pl.ds(v, 1), :]
                        dsa = dsa_ref[pl.ds(v, 1), :]
                        da = da + sp * dsa
                        dst = dst * w + dsa * a
                    else:
                        dst = dst * w
                    ds_ref[v] = dst
                tiles.append(_rows_of(dvs))
            dv_ref[t] = _nat_from_rows(joined(tiles), sel)
            dr = w * u + k * vd
            if ab:
                dr = dr + b * sd
                da_ref[t] = da
                db_ref[t] = db
            dw_ref[t] = dw
            dk_ref[t] = dk
            dr_ref[t] = dr
            return carry

        lax.fori_loop(0, tb, step, 0, unroll=2)

    kspec = pl.BlockSpec((tb, R, LANES), lambda i: (nb - 1 - i, 0, 0))
    rspec = pl.BlockSpec((tb, dv, LANES), lambda i: (nb - 1 - i, 0, 0))
    hspec = pl.BlockSpec((tb, dv, R, LANES), lambda i: (nb - 1 - i, 0, 0, 0))
    nspec = pl.BlockSpec((tb, n_heads, dv), lambda i: (nb - 1 - i, 0, 0))
    operands = [wk, kk, rk, vnat, hist, dynat] + ([ak, bk, sarow] if ab else [])
    in_specs = [kspec, kspec, kspec, nspec, hspec, nspec] + ([kspec, kspec, rspec] if ab else [])
    kshape = jax.ShapeDtypeStruct((S, R, LANES), F32)
    out_shape = [kshape, kshape, kshape, jax.ShapeDtypeStruct((S, n_heads, dv), F32)] + ([kshape, kshape] if ab else [])
    out_specs = [kspec, kspec, kspec, nspec] + ([kspec, kspec] if ab else [])
    return pl.pallas_call(
        body, name=name, grid=(nb,), in_specs=in_specs, out_specs=out_specs, out_shape=out_shape,
        scratch_shapes=[pltpu.VMEM((dv, R, LANES), F32)] + [pltpu.VMEM((dv, LANES), F32)] * (3 if ab else 2),
        compiler_params=pltpu.CompilerParams(dimension_semantics=("arbitrary",), vmem_limit_bytes=VMEM_LIMIT),
    )(*operands)


def _to_k(x, n_heads, dk):
    S = x.shape[0]
    kl = LANES // n_heads
    return x.reshape(S, n_heads, dk // kl, kl).transpose(0, 2, 3, 1).reshape(S, dk // kl, LANES)


def _from_k(x, n_heads, dk):
    S = x.shape[0]
    kl = LANES // n_heads
    return x.reshape(S, dk // kl, kl, n_heads).transpose(0, 3, 1, 2).reshape(S, n_heads * dk)


def _tril_mask():
    t = lax.broadcasted_iota(jnp.int32, (A_CHUNK, A_CHUNK), 0)
    s = lax.broadcasted_iota(jnp.int32, (A_CHUNK, A_CHUNK), 1)
    return s <= t


def _gmlp_fwd(name, proj, aw, vgain, ws, bs_t):
    S = proj.shape[0]
    G = aw // A_GROUP

    def body(u_ref, v_ref, gain_ref, ws_ref, bs_ref, o_ref):
        mask = _tril_mask()
        lane = lax.broadcasted_iota(jnp.int32, (A_CHUNK, G), 1)
        bs = bs_ref[...]
        for g in range(G):
            seg = slice(g * A_GROUP, (g + 1) * A_GROUP)
            ua = _gelu(u_ref[:, seg])
            va = _gelu(v_ref[:, seg])
            rs = lax.rsqrt(jnp.mean(va * va, axis=-1, keepdims=True) + RMS_EPS)
            vg = (va * rs) * gain_ref[:, seg]
            wm = jnp.where(mask, ws_ref[g], 0.0).astype(BF16)
            bcol = jnp.sum(jnp.where(lane == g, bs, 0.0), axis=1, keepdims=True)
            s = jnp.dot(wm, vg.astype(BF16), preferred_element_type=F32) + bcol
            o_ref[:, seg] = (ua * s).astype(o_ref.dtype)

    return pl.pallas_call(
        body, name=name, grid=(S // A_CHUNK,),
        in_specs=[pl.BlockSpec((A_CHUNK, aw), lambda i: (i, 0)), pl.BlockSpec((A_CHUNK, aw), lambda i: (i, 1)),
                  pl.BlockSpec((1, aw), lambda i: (0, 0)), pl.BlockSpec((G, A_CHUNK, A_CHUNK), lambda i: (0, 0, 0)),
                  pl.BlockSpec((A_CHUNK, G), lambda i: (0, 0))],
        out_specs=pl.BlockSpec((A_CHUNK, aw), lambda i: (i, 0)),
        out_shape=jax.ShapeDtypeStruct((S, aw), BF16),
        compiler_params=pltpu.CompilerParams(dimension_semantics=("parallel",), vmem_limit_bytes=VMEM_LIMIT),
    )(proj, proj, vgain, ws, bs_t)


def _gmlp_bwd(name, proj, dout, dout_cb, aw, vgain, ws, bs_t):
    S = proj.shape[0]
    G = aw // A_GROUP

    def body(u_ref, v_ref, do_ref, gain_ref, ws_ref, bs_ref, du_ref, dv_ref, dws_ref, dbs_ref, dgain_ref):
        @pl.when(pl.program_id(0) == 0)
        def _():
            dws_ref[...] = jnp.zeros(dws_ref.shape, F32)
            dbs_ref[...] = jnp.zeros(dbs_ref.shape, F32)
            dgain_ref[...] = jnp.zeros(dgain_ref.shape, F32)

        mask = _tril_mask()
        lane = lax.broadcasted_iota(jnp.int32, (A_CHUNK, G), 1)
        bs = bs_ref[...]
        dbs = jnp.zeros((A_CHUNK, G), F32)
        for g in range(G):
            seg = slice(g * A_GROUP, (g + 1) * A_GROUP)
            u, v = u_ref[:, seg], v_ref[:, seg]
            do = do_ref[:, seg].astype(F32)
            ua, va = _gelu(u), _gelu(v)
            rs = lax.rsqrt(jnp.mean(va * va, axis=-1, keepdims=True) + RMS_EPS)
            xh = va * rs
            gain = gain_ref[:, seg]
            vg = (xh * gain).astype(BF16)
            wm = jnp.where(mask, ws_ref[g], 0.0).astype(BF16)
            bcol = jnp.sum(jnp.where(lane == g, bs, 0.0), axis=1, keepdims=True)
            s = jnp.dot(wm, vg, preferred_element_type=F32) + bcol
            du_ref[:, seg] = (do * s * _dgelu(u)).astype(du_ref.dtype)
            ds = do * ua
            dsb = ds.astype(BF16)
            dw = lax.dot_general(dsb, vg, (((1,), (1,)), ((), ())), preferred_element_type=F32)
            dws_ref[g] += jnp.where(mask, dw, 0.0)
            dbs = dbs + jnp.where(lane == g, jnp.sum(ds, axis=1, keepdims=True), 0.0)
            dvg = lax.dot_general(wm, dsb, (((0,), (0,)), ((), ())), preferred_element_type=F32)
            dgain_ref[:, seg] += jnp.sum(dvg * xh, axis=0, keepdims=True)
            dxh = dvg * gain
            dva = rs * (dxh - xh * jnp.mean(dxh * xh, axis=-1, keepdims=True))
            dv_ref[:, seg] = (dva * _dgelu(v)).astype(dv_ref.dtype)
        dbs_ref[...] += dbs

    return pl.pallas_call(
        body, name=name, grid=(S // A_CHUNK,),
        in_specs=[pl.BlockSpec((A_CHUNK, aw), lambda i: (i, 0)), pl.BlockSpec((A_CHUNK, aw), lambda i: (i, 1)),
                  pl.BlockSpec((A_CHUNK, aw), functools.partial(lambda i, cb: (i, cb), cb=dout_cb)),
                  pl.BlockSpec((1, aw), lambda i: (0, 0)), pl.BlockSpec((G, A_CHUNK, A_CHUNK), lambda i: (0, 0, 0)),
                  pl.BlockSpec((A_CHUNK, G), lambda i: (0, 0))],
        out_specs=[pl.BlockSpec((A_CHUNK, aw), lambda i: (i, 0)), pl.BlockSpec((A_CHUNK, aw), lambda i: (i, 0)),
                   pl.BlockSpec((G, A_CHUNK, A_CHUNK), lambda i: (0, 0, 0)), pl.BlockSpec((A_CHUNK, G), lambda i: (0, 0)),
                   pl.BlockSpec((1, aw), lambda i: (0, 0))],
        out_shape=[jax.ShapeDtypeStruct((S, aw), BF16), jax.ShapeDtypeStruct((S, aw), BF16),
                   jax.ShapeDtypeStruct((G, A_CHUNK, A_CHUNK), F32), jax.ShapeDtypeStruct((A_CHUNK, G), F32),
                   jax.ShapeDtypeStruct((1, aw), F32)],
        compiler_params=pltpu.CompilerParams(dimension_semantics=("arbitrary",), vmem_limit_bytes=VMEM_LIMIT),
    )(proj, proj, dout, vgain, ws, bs_t)


ANY = pl.BlockSpec(memory_space=pl.ANY)


def _place():
    return lax.axis_index("x"), lax.axis_index("y"), lax.axis_index("c")


def _other_chips(x, y):
    return [(1 - x, y), (x, 1 - y), (1 - x, 1 - y)]


def _remote(src, dst, send_sem, recv_sem, device):
    return pltpu.make_async_remote_copy(src_ref=src, dst_ref=dst, send_sem=send_sem, recv_sem=recv_sem,
                                        device_id=device, device_id_type=MESH)


def _comm_call(body, name, operands, out_shape, n_dma, n_local):
    return pl.pallas_call(
        body, name=name, in_specs=[ANY] * len(operands), out_specs=[ANY] * len(out_shape), out_shape=out_shape,
        scratch_shapes=[pltpu.SemaphoreType.DMA((n_dma,)), pltpu.SemaphoreType.DMA((n_dma,)),
                        pltpu.SemaphoreType.DMA((max(n_local, 1),))],
        compiler_params=pltpu.CompilerParams(has_side_effects=True),
    )(*operands)


def _all_gather(name, bigs, smalls):
    nb, n = len(bigs), len(bigs) + len(smalls)
    arrays = list(bigs) + list(smalls)

    def body(*refs):
        ins, outs = refs[:n], refs[n:2 * n]
        send_sems, recv_sems, local_sems = refs[2 * n:]
        x, y, c = _place()
        me = 2 * x + y
        chip_x, chip_y = (1 - x, y), (x, 1 - y)
        k_x, k_y, k_d = 2 * (1 - x) + y, 2 * x + (1 - y), 2 * (1 - x) + (1 - y)
        sibling = (x, y, 1 - c)
        started = []

        def go(src, dst, s, device):
            cp = _remote(src, dst, send_sems.at[s], recv_sems.at[s], device)
            cp.start()
            started.append(cp)

        def landed(ref, s):
            _remote(ref, ref, send_sems.at[s], recv_sems.at[s], sibling).wait_recv()

        for e in range(nb):
            half = ins[e].shape[0] // 2
            src = ins[e].at[pl.ds(c * half, half)]
            dst = outs[e].at[me, pl.ds(c * half, half)]
            go(src, dst, 8 * e, (*chip_x, c))
            go(src, dst, 8 * e + 1, (*chip_y, c))
        small = 8 * nb
        for e in range(nb, n):
            for j, chip in enumerate(_other_chips(x, y)):
                go(ins[e], outs[e].at[me], small + 3 * (e - nb) + j, (*chip, c))
        for e in range(nb):
            half = ins[e].shape[0] // 2
            q = half // 2
            from_x = outs[e].at[k_x, pl.ds(c * half, half)]
            landed(from_x, 8 * e)
            first = outs[e].at[k_x, pl.ds(c * half, q)]
            go(first, first, 8 * e + 2, (*chip_y, c))
            go(from_x, from_x, 8 * e + 4, sibling)
            from_y = outs[e].at[k_y, pl.ds(c * half, half)]
            landed(from_y, 8 * e + 1)
            second = outs[e].at[k_y, pl.ds(c * half + q, q)]
            go(second, second, 8 * e + 3, (*chip_x, c))
            go(from_y, from_y, 8 * e + 5, sibling)
        for e in range(nb):
            half = ins[e].shape[0] // 2
            q = half // 2
            first = outs[e].at[k_d, pl.ds(c * half, q)]
            landed(first, 8 * e + 2)
            go(first, first, 8 * e + 6, sibling)
            second = outs[e].at[k_d, pl.ds(c * half + q, q)]
            landed(second, 8 * e + 3)
            go(second, second, 8 * e + 7, sibling)
        for e in range(nb, n):
            for j, chip in enumerate(_other_chips(x, y)):
                landed(outs[e].at[2 * chip[0] + chip[1]], small + 3 * (e - nb) + j)
        for e in range(nb):
            half = ins[e].shape[0] // 2
            q = half // 2
            o = (1 - c) * half
            landed(outs[e].at[k_x, pl.ds(o, half)], 8 * e + 4)
            landed(outs[e].at[k_y, pl.ds(o, half)], 8 * e + 5)
            landed(outs[e].at[k_d, pl.ds(o, q)], 8 * e + 6)
            landed(outs[e].at[k_d, pl.ds(o + q, q)], 8 * e + 7)
        for cp in started:
            cp.wait_send()

    out_shape = [jax.ShapeDtypeStruct((N_CHIPS,) + a.shape, a.dtype) for a in arrays]
    gathered = _comm_call(body, name, arrays, out_shape, 8 * nb + 3 * len(smalls), 0)
    me = 2 * lax.axis_index("x") + lax.axis_index("y")
    return [lax.dynamic_update_slice(g, a[None], (me,) + (0,) * a.ndim) for g, a in zip(gathered, arrays)]


def _swap_halves(name, grads):
    n = len(grads)

    def body(*refs):
        ins, outs = refs[:n], refs[n:2 * n]
        send_sems, recv_sems, _ = refs[2 * n:]
        x, y, c = _place()
        cps = []
        for e in range(n):
            half = ins[e].shape[1] // 2
            src = ins[e].at[pl.ds(0, N_CHIPS), pl.ds((1 - c) * half, half)]
            cp = _remote(src, outs[e], send_sems.at[e], recv_sems.at[e], (x, y, 1 - c))
            cp.start()
            cps.append(cp)
        for cp in cps:
            cp.wait()

    out_shape = [jax.ShapeDtypeStruct((N_CHIPS, g.shape[1] // 2, g.shape[2]), g.dtype) for g in grads]
    return _comm_call(body, name, list(grads), out_shape, n, 0)


def _scatter_chips(name, parts):
    n = len(parts)

    def body(*refs):
        ins, outs = refs[:n], refs[n:2 * n]
        send_sems, recv_sems, local_sems = refs[2 * n:]
        x, y, c = _place()
        me = 2 * x + y
        cps = []
        for e in range(n):
            cp = pltpu.make_async_copy(ins[e].at[me], outs[e].at[3], local_sems.at[e])
            cp.start()
            cps.append(cp)
            for j, chip in enumerate(_other_chips(x, y)):
                cp = _remote(ins[e].at[2 * chip[0] + chip[1]], outs[e].at[j], send_sems.at[3 * e + j],
                             recv_sems.at[3 * e + j], (*chip, c))
                cp.start()
                cps.append(cp)
        for cp in cps:
            cp.wait()

    out_shape = [jax.ShapeDtypeStruct(p.shape, p.dtype) for p in parts]
    return _comm_call(body, name, list(parts), out_shape, 3 * n, n)


def _join_halves(name, halves, places, out_shapes):
    n, n_out = len(halves), len(out_shapes)

    def body(*refs):
        ins, outs = refs[:n], refs[n:n + n_out]
        send_sems, recv_sems, local_sems = refs[n + n_out:]
        x, y, c = _place()
        cps = []
        for e in range(n):
            o, idx = places[e]
            half = ins[e].shape[0]
            dst = outs[o].at[(*idx, pl.ds(c * half, half))]
            cp = _remote(ins[e], dst, send_sems.at[e], recv_sems.at[e], (x, y, 1 - c))
            cp.start()
            cps.append(cp)
        for e, cp in enumerate(cps):
            o, idx = places[e]
            half = ins[e].shape[0]
            landed = outs[o].at[(*idx, pl.ds((1 - c) * half, half))]
            cp.wait_send()
            _remote(ins[e], landed, send_sems.at[e], recv_sems.at[e], (x, y, 1 - c)).wait_recv()

    joined = list(_comm_call(body, name, list(halves), list(out_shapes), n, 0))
    c = lax.axis_index("c")
    for e, hv in enumerate(halves):
        o, idx = places[e]
        start = tuple(idx) + (c * hv.shape[0], 0)
        joined[o] = lax.dynamic_update_slice(joined[o], hv.reshape((1,) * len(idx) + hv.shape), start)
    return joined


def _spread_all(name, per_chip, everywhere):
    def body(pc_ref, ev_ref, pc_out, ev_out, send_sems, recv_sems, local_sems):
        x, y, c = _place()
        me = 4 * x + 2 * y + c
        cps = [pltpu.make_async_copy(pc_ref.at[2 * x + y], pc_out.at[me], local_sems.at[0]),
               pltpu.make_async_copy(ev_ref, ev_out.at[me], local_sems.at[1])]
        for f in range(1, 8):
            fx, fy, fc = f // 4, (f // 2) % 2, f % 2
            tx = 1 - x if fx else x
            ty = 1 - y if fy else y
            tc = 1 - c if fc else c
            cps.append(_remote(pc_ref.at[2 * tx + ty], pc_out.at[me], send_sems.at[2 * f], recv_sems.at[2 * f],
                               (tx, ty, tc)))
            cps.append(_remote(ev_ref, ev_out.at[me], send_sems.at[2 * f + 1], recv_sems.at[2 * f + 1],
                               (tx, ty, tc)))
        for cp in cps:
            cp.start()
        for cp in cps:
            cp.wait()

    out_shape = [jax.ShapeDtypeStruct((8,) + per_chip.shape[1:], F32), jax.ShapeDtypeStruct((8,) + everywhere.shape, F32)]
    return _comm_call(body, name, [per_chip, everywhere], out_shape, 16, 2)


def _row_tile(rows, width, itemsize, n_arrays):
    tm = 1
    while rows % (tm * 2) == 0 and (tm * 2) * width * itemsize * n_arrays * 2 <= 24 * 1024 * 1024 and tm * 2 <= 1024:
        tm *= 2
    return tm


def _add_own_half(name, grad, swapped, c_arr):
    ns, R, C = grad.shape
    half = R // 2
    th = _row_tile(half, C, 2, 3)
    g4 = grad.reshape(ns, 2, half, C)

    def body(c_ref, g_ref, s_ref, o_ref):
        o_ref[...] = (g_ref[...].astype(F32) + s_ref[...].astype(F32)).astype(o_ref.dtype)

    return pl.pallas_call(
        body, name=name,
        grid_spec=pltpu.PrefetchScalarGridSpec(
            num_scalar_prefetch=1, grid=(ns, half // th),
            in_specs=[pl.BlockSpec((None, None, th, C), lambda k, i, c_ref: (k, c_ref[0], i, 0)),
                      pl.BlockSpec((None, th, C), lambda k, i, c_ref: (k, i, 0))],
            out_specs=pl.BlockSpec((None, th, C), lambda k, i, c_ref: (k, i, 0))),
        out_shape=jax.ShapeDtypeStruct((ns, half, C), grad.dtype),
        compiler_params=pltpu.CompilerParams(dimension_semantics=("parallel", "parallel"), vmem_limit_bytes=VMEM_LIMIT),
    )(c_arr, g4, swapped)


def _sum_slots(name, slots, order, out_dtype=F32):
    n, rows, C = slots.shape
    th = _row_tile(rows, C, 4, n + 1)

    def body(s_ref, o_ref):
        acc = s_ref[order[0]].astype(F32)
        for k in order[1:]:
            acc = acc + s_ref[k].astype(F32)
        o_ref[...] = acc.astype(o_ref.dtype)

    return pl.pallas_call(
        body, name=name, grid=(rows // th,),
        in_specs=[pl.BlockSpec((n, th, C), lambda i: (0, i, 0))],
        out_specs=pl.BlockSpec((th, C), lambda i: (i, 0)),
        out_shape=jax.ShapeDtypeStruct((rows, C), out_dtype),
        compiler_params=pltpu.CompilerParams(dimension_semantics=("parallel",), vmem_limit_bytes=VMEM_LIMIT),
    )(slots)


def _adamw(name, g, w, m, v):
    rows, C = g.shape
    tm = _row_tile(rows, C, 4, 7)
    c1 = 1.0 - ADAM_B1 ** ADAM_STEP
    c2 = 1.0 - ADAM_B2 ** ADAM_STEP

    def fn(g, w, m, v):
        m = ADAM_B1 * m + (1.0 - ADAM_B1) * g
        v = ADAM_B2 * v + (1.0 - ADAM_B2) * (g * g)
        delta = -ADAM_LR * ((m / c1) / (jnp.sqrt(v / c2) + ADAM_EPS) + ADAM_WD * w)
        return delta, m, v

    return _rowwise(fn, name, rows, tm, [('row', g), ('row', w), ('row', m), ('row', v)],
                    [(C, F32), (C, F32), (C, F32)])


def _tm(S, width, n_arrays):
    return _row_tile(S, width, 4, n_arrays)


def _rms_fwd(name, h, gain, out_dtype):
    S, D = h.shape

    def fn(hb, g):
        r = lax.rsqrt(jnp.mean(hb * hb, axis=-1, keepdims=True) + RMS_EPS)
        return ((hb * r) * g,)

    return _rowwise(fn, name, S, _tm(S, D, 4), [('row', h), ('full', gain)], [(D, out_dtype)])[0]


def _rms_bwd(name, h_in, gain, dn, dh):
    S, D = h_in.shape

    def fn(hb, g, dnb, dhb):
        r = lax.rsqrt(jnp.mean(hb * hb, axis=-1, keepdims=True) + RMS_EPS)
        xh = hb * r
        dnb = dnb.astype(F32)
        dxh = dnb * g
        dx = r * (dxh - xh * jnp.mean(dxh * xh, axis=-1, keepdims=True))
        return dhb + dx, jnp.sum(dnb * xh, axis=0, keepdims=True)

    return _rowwise(fn, name, S, _tm(S, D, 8), [('row', h_in), ('full', gain), ('row', dn), ('row', dh)],
                    [(D, F32)], [(1, D)])


def _ffn_fwd(h, gain, wg, wu, wd):
    S, D = h.shape
    n = _rms_fwd("rms_fwd_bf16", h, gain, BF16)
    g = _mm_nn(n, wg, "ffn_up", BF16)
    u = _mm_nn(n, wu, "ffn_up", BF16)
    FF = g.shape[1]

    def fn(gb, ub):
        return (_silu(gb.astype(F32)) * ub.astype(F32),)

    act = _rowwise(fn, "ffn_act", S, _tm(S, FF, 6), [('row', g), ('row', u)], [(FF, BF16)])[0]
    return _mm_nn(act, wd, "ffn_down", F32, res=h, scale=0.5), (h, n, g, u)


def _ffn_bwd(dh, cache, gain, wg, wu, wd):
    h, n, g, u = cache
    S, FF = g.shape
    dhb = dh.astype(BF16)
    dact = _mm_nt(dhb, wd, "ffn_dact", BF16, scale=0.5)

    def fn(gb, ub, db):
        gb, ub, db = gb.astype(F32), ub.astype(F32), db.astype(F32)
        sg = _sigmoid(gb)
        sl = gb * sg
        return db * ub * (sg * (1.0 + gb * (1.0 - sg))), db * sl, sl * ub

    dg, du, act = _rowwise(fn, "ffn_act_bwd", S, _tm(S, FF, 10), [('row', g), ('row', u), ('row', dact)],
                           [(FF, BF16), (FF, BF16), (FF, BF16)])
    dwd = _mm_tn(act, dhb, wd, "ffn_dwd", BF16, scale=0.5)
    dwg = _mm_tn(n, dg, wg, "ffn_dwup", BF16)
    dwu = _mm_tn(n, du, wu, "ffn_dwup", BF16)
    dn = _mm_nt(dg, wg, "ffn_dn", F32)
    dn = _mm_nt(du, wu, "ffn_dn_acc", F32, res=dn)
    dh2, dgain = _rms_bwd("rms_bwd", h, gain, dn, dh)
    return dh2, dgain, dwg, dwu, dwd


def _ple_fwd(h, gain, pb, wgate, wproj):
    S, D = h.shape
    n = _rms_fwd("rms_fwd_bf16", h, gain, BF16)
    pre = _mm_nn(n, wgate, "ple_gate", F32)
    e = _mm_nn(pb, wproj, "ple_proj", F32)

    def fn(hb, pr, eb):
        return (hb + _sigmoid(pr) * eb,)

    h2 = _rowwise(fn, "ple_add", S, _tm(S, D, 6), [('row', h), ('row', pre), ('row', e)], [(D, F32)])[0]
    return h2, (h, n, pre, e)


def _ple_bwd(dh, cache, gain, pb, wgate, wproj):
    h, n, pre, e = cache
    S, D = h.shape

    def fn(db, pr, eb):
        gt = _sigmoid(pr)
        return db * eb * gt * (1.0 - gt), db * gt

    dpre, de = _rowwise(fn, "ple_bwd", S, _tm(S, D, 6), [('row', dh), ('row', pre), ('row', e)],
                        [(D, BF16), (D, BF16)])
    dwproj = _mm_tn(pb, de, wproj, "ple_dwproj", BF16)
    dwgate = _mm_tn(n, dpre, wgate, "ple_dwgate", BF16)
    dn = _mm_nt(dpre, wgate, "ple_dn", F32)
    dh2, dgain = _rms_bwd("rms_bwd", h, gain, dn, dh)
    return dh2, dgain, dwgate, dwproj


def _even_fwd(h, gain, w_in, w_out, vgain, ws, bs_t, onorm, lb):
    S, D = h.shape
    aw = D // 2
    nh = aw // B_HEAD
    hn = _rms_fwd("rms_fwd_bf16", h, gain, BF16)
    proj = _mm_nn(hn, w_in, "even_in", F32)
    a_out = _gmlp_fwd("gmlp_fwd", proj, aw, vgain, ws, bs_t)

    def pre(bq, bf, lbv):
        f = lbv + (1.0 - lbv) * _sigmoid(bf)
        return _silu(bq), jnp.maximum(f, B_MIN_F), 1.0 - f

    q, w, k = _rowwise(pre, "hgrn_pre", S, _tm(S, aw, 8), [('col', proj, aw, 2), ('col', proj, aw, 3), ('full', lb)],
                       [(aw, F32), (aw, F32), (aw, F32)])
    wk, kk, qk = _to_k(w, nh, B_HEAD), _to_k(k, nh, B_HEAD), _to_k(q, nh, B_HEAD)
    vrow = proj[:, 4 * aw:5 * aw].reshape(S, nh, B_HEAD)
    ynat, hist, _ = _scan_fwd("hgrn_scan_fwd", wk, kk, qk, vrow)
    o = ynat.reshape(S, aw)

    def post(ob, bg, on):
        rs = lax.rsqrt(_segsum(ob * ob, B_HEAD) * (1.0 / B_HEAD) + RMS_EPS)
        return ((ob * rs * on) * _silu(bg),)

    b_out = _rowwise(post, "hgrn_post", S, _tm(S, aw, 8), [('row', o), ('col', proj, aw, 5), ('full', onorm)],
                     [(aw, BF16)])[0]
    cat = jnp.concatenate([a_out, b_out], axis=1)
    h2 = _mm_nn(cat, w_out, "even_out", F32, res=h)
    return h2, (h, hn, proj, wk, kk, qk, vrow, hist, o, cat)


def _even_bwd(dh, cache, gain, w_in, w_out, vgain, ws, bs_t, onorm, lb):
    h, hn, proj, wk, kk, qk, vrow, hist, o, cat = cache
    S, D = h.shape
    aw = D // 2
    nh = aw // B_HEAD
    dhb = dh.astype(BF16)
    dw_out = _mm_tn(cat, dhb, w_out, "even_dwout", BF16)
    dcat = _mm_nt(dhb, w_out, "even_dcat", F32)

    def post_bwd(ob, bg, on, db):
        rs = lax.rsqrt(_segsum(ob * ob, B_HEAD) * (1.0 / B_HEAD) + RMS_EPS)
        xh = ob * rs
        dy = db * _silu(bg)
        dbg = db * (xh * on) * _dsilu(bg)
        dxh = dy * on
        do = rs * (dxh - xh * (_segsum(dxh * xh, B_HEAD) * (1.0 / B_HEAD)))
        return do, dbg, jnp.sum(dy * xh, axis=0, keepdims=True)

    do, dbg, donorm = _rowwise(post_bwd, "hgrn_post_bwd", S, _tm(S, aw, 10),
                               [('row', o), ('col', proj, aw, 5), ('full', onorm), ('col', dcat, aw, 1)],
                               [(aw, F32), (aw, BF16)], [(1, aw)])
    dwk, dkk, dqk, dvnat = _scan_bwd("hgrn_scan_bwd", wk, kk, qk, vrow, hist, do.reshape(S, nh, B_HEAD))
    dq, dw, dk = _from_k(dqk, nh, B_HEAD), _from_k(dwk, nh, B_HEAD), _from_k(dkk, nh, B_HEAD)
    dbi = dvnat.reshape(S, aw).astype(BF16)

    def pre_bwd(bq, bf, lbv, dqb, dwb, dkb):
        sig = _sigmoid(bf)
        f = lbv + (1.0 - lbv) * sig
        df = jnp.where(f > B_MIN_F, dwb, 0.0) - dkb
        return dqb * _dsilu(bq), df * (1.0 - lbv) * sig * (1.0 - sig), jnp.sum(df * (1.0 - sig), axis=0, keepdims=True)

    dbq, dbf, dlb = _rowwise(pre_bwd, "hgrn_pre_bwd", S, _tm(S, aw, 12),
                             [('col', proj, aw, 2), ('col', proj, aw, 3), ('full', lb), ('row', dq), ('row', dw),
                              ('row', dk)], [(aw, BF16), (aw, BF16)], [(1, aw)])
    dau, dav, dws, dbs_t, dvgain = _gmlp_bwd("gmlp_bwd", proj, dcat, 0, aw, vgain, ws, bs_t)
    dproj = jnp.concatenate([dau, dav, dbq, dbf, dbi, dbg], axis=1)
    dw_in = _mm_tn(hn, dproj, w_in, "even_dwin", BF16)
    dn = _mm_nt(dproj, w_in, "even_dn", F32)
    dh2, dgain = _rms_bwd("rms_bwd", h, gain, dn, dh)
    return dh2, dict(gain=dgain, w_in=dw_in, w_out=dw_out, vgain=dvgain, ws=dws, bs_t=dbs_t, onorm=donorm, lb=dlb)


def _rwkv_prep(r, k, v0, wpl, apl, w0, a0, kkg, kag, svl=None, vf=None, v0p=None):
    wp = w0 + wpl
    w = -_softplus(-wp) - 0.5
    ew = jnp.exp(w)
    decay = jnp.exp(-ew)
    a = _sigmoid(a0 + apl)
    if svl is not None:
        sv = _sigmoid(v0p + svl)
        v = v0 + (vf - v0) * sv
    else:
        sv, v = None, v0
    kkp = k * kkg
    nrm = jnp.sqrt(_segsum(kkp * kkp, C_HEAD))
    inv = 1.0 / jnp.maximum(nrm, 1e-12)
    kk = kkp * inv
    k2 = k * (1.0 + (a - 1.0) * kag)
    return dict(wp=wp, ew=ew, decay=decay, a=a, sv=sv, v=v, kkp=kkp, nrm=nrm, inv=inv, kk=kk, k2=k2)


def _rwkv_post(y, r, k2, v, gn_g, gn_b, rk):
    mu = _segsum(y, C_HEAD) * (1.0 / C_HEAD)
    yc = y - mu
    rstd = lax.rsqrt(_segsum(yc * yc, C_HEAD) * (1.0 / C_HEAD) + C_GN_EPS)
    yh = yc * rstd
    s = _segsum(r * k2 * rk, C_HEAD)
    return yh, rstd, s, yh * gn_g + gn_b + s * v


def _rwkv_fwd(h, gain, P, vfirst):
    S, D = h.shape
    nh = D // C_HEAD
    vres = vfirst is not None
    hn = _rms_fwd("rms_fwd_f32", h, gain, F32)
    hs = jnp.concatenate([jnp.zeros((1, D), F32), hn[:-1]], axis=0)

    def mixf(x, xs, m0, m1, m2, m3, m4, m5):
        xx = xs - x
        return tuple(x + xx * m for m in (m0, m1, m2, m3, m4, m5))

    xr, xw, xk, xv, xa, xg = _rowwise(mixf, "rwkv_mix", S, _tm(S, D, 8),
                                      [('row', hn), ('row', hs)] + [('full', m) for m in P['mix']], [(D, BF16)] * 6)
    r = _mm_nn(xr, P['wr'], "rwkv_proj", F32)
    k = _mm_nn(xk, P['wk'], "rwkv_proj", F32)
    v0 = _mm_nn(xv, P['wv'], "rwkv_proj", F32)
    lw1 = _mm_nn(xw, P['w1'], "rwkv_lora_in", F32)
    la1 = _mm_nn(xa, P['a1'], "rwkv_lora_in", F32)
    lg1 = _mm_nn(xg, P['g1'], "rwkv_lora_in_g", F32)
    ins = [('row', lw1), ('row', la1), ('row', lg1)]
    outs = [(lw1.shape[1], BF16), (la1.shape[1], BF16), (lg1.shape[1], BF16)]
    if vres:
        lv1 = _mm_nn(xv, P['v1'], "rwkv_lora_in_v", F32)
        ins.append(('row', lv1))
        outs.append((lv1.shape[1], BF16))

    def lora_act(*xs):
        res = [jnp.tanh(xs[0]), xs[1], _sigmoid(xs[2])]
        return tuple(res + list(xs[3:]))

    acts = _rowwise(lora_act, "rwkv_lora_act", S, _tm(S, 1024, 4), ins, outs)
    tw, la1b, sg = acts[0], acts[1], acts[2]
    wpl = _mm_nn(tw, P['w2'], "rwkv_lora_out", F32)
    apl = _mm_nn(la1b, P['a2'], "rwkv_lora_out", F32)
    g = _mm_nn(sg, P['g2'], "rwkv_lora_out_g", F32)
    prep_ins = [('row', r), ('row', k), ('row', v0), ('row', wpl), ('row', apl),
                ('full', P['w0']), ('full', P['a0']), ('full', P['kk']), ('full', P['ka'])]
    svl = lv1b = None
    if vres:
        lv1b = acts[3]
        svl = _mm_nn(lv1b, P['v2'], "rwkv_lora_out_v", F32)
        prep_ins += [('row', svl), ('row', vfirst), ('full', P['v0'])]

    def prep(*xs):
        q = _rwkv_prep(*xs)
        return q['decay'], q['k2'], q['v'], -q['kk'], q['kk'] * q['a']

    decay, k2, v, av, bv = _rowwise(prep, "rwkv_prep", S, _tm(S, D, 24), prep_ins, [(D, F32)] * 5)
    tk = functools.partial(_to_k, n_heads=nh, dk=C_HEAD)
    wk_, kk_, rk_, ak_, bk_ = tk(decay), tk(k2), tk(r), tk(av), tk(bv)
    vrow = v.reshape(S, nh, C_HEAD)
    ynat, hist, sarow = _scan_fwd("rwkv_scan_fwd", wk_, kk_, rk_, vrow, ak_, bk_)
    y = ynat.reshape(S, D)

    def post(yb, rb, k2b, vb, gb, gn_g, gn_b, rkf):
        return (_rwkv_post(yb, rb, k2b, vb, gn_g, gn_b, rkf)[3] * gb,)

    zg = _rowwise(post, "rwkv_post", S, _tm(S, D, 16),
                  [('row', y), ('row', r), ('row', k2), ('row', v), ('row', g),
                   ('full', P['gn_g']), ('full', P['gn_b']), ('full', P['rk'])], [(D, BF16)])[0]
    h2 = _mm_nn(zg, P['wo'], "rwkv_out", F32, res=h)
    cache = dict(h=h, hn=hn, hs=hs, x=(xr, xw, xk, xv, xa, xg), r=r, k=k, v0=v0, tw=tw, la1b=la1b, sg=sg, lv1b=lv1b,
                 wpl=wpl, apl=apl, svl=svl, g=g, k2=k2, v=v, scan=(wk_, kk_, rk_, vrow, ak_, bk_, hist, sarow), y=y,
                 zg=zg, vfirst=vfirst)
    return h2, cache, (v if not vres else vfirst)


def _rwkv_bwd(dh, cache, gain, P, dvfirst_in):
    c = cache
    h = c['h']
    S, D = h.shape
    nh = D // C_HEAD
    vres = c['vfirst'] is not None
    xr, xw, xk, xv, xa, xg = c['x']
    dhb = dh.astype(BF16)
    dwo = _mm_tn(c['zg'], dhb, P['wo'], "rwkv_dwo", BF16)
    dzg = _mm_nt(dhb, P['wo'], "rwkv_dzg", F32)

    def post_bwd(dzgb, yb, rb, k2b, vb, gb, gn_g, gn_b, rkf):
        yh, rstd, s, z = _rwkv_post(yb, rb, k2b, vb, gn_g, gn_b, rkf)
        dz = dzgb * gb
        dyh = dz * gn_g
        m1 = _segsum(dyh, C_HEAD) * (1.0 / C_HEAD)
        m2 = _segsum(dyh * yh, C_HEAD) * (1.0 / C_HEAD)
        dy = rstd * (dyh - m1 - yh * m2)
        ds = _segsum(dz * vb, C_HEAD)
        return (dy, dzgb * z, ds * k2b * rkf, ds * rb * rkf, dz * s,
                jnp.sum(dz * yh, axis=0, keepdims=True), jnp.sum(dz, axis=0, keepdims=True),
                jnp.sum(ds * rb * k2b, axis=0, keepdims=True))

    dy, dgb, dr_b, dk2_b, dv_b, dgn_g, dgn_b, drk = _rowwise(
        post_bwd, "rwkv_post_bwd", S, _tm(S, D, 28),
        [('row', dzg), ('row', c['y']), ('row', c['r']), ('row', c['k2']), ('row', c['v']), ('row', c['g']),
         ('full', P['gn_g']), ('full', P['gn_b']), ('full', P['rk'])],
        [(D, F32), (D, BF16), (D, F32), (D, F32), (D, F32)], [(1, D)] * 3)
    wk_, kk_, rk_, vrow, ak_, bk_, hist, sarow = c['scan']
    dwk, dkk, drk_s, dvnat, dak, dbk = _scan_bwd("rwkv_scan_bwd", wk_, kk_, rk_, vrow, hist,
                                                 dy.reshape(S, nh, C_HEAD), ak_, bk_, sarow)
    fk = functools.partial(_from_k, n_heads=nh, dk=C_HEAD)
    ddecay, dk2_s, dr_s, dA, dB = fk(dwk), fk(dkk), fk(drk_s), fk(dak), fk(dbk)
    dv_s = dvnat.reshape(S, D)
    dr_t, dk2_t, dv_t = dr_s + dr_b, dk2_s + dk2_b, dv_s + dv_b
    if dvfirst_in is not None:
        dv_t = dv_t + dvfirst_in
    ins = [('row', c['r']), ('row', c['k']), ('row', c['v0']), ('row', c['wpl']), ('row', c['apl']),
           ('full', P['w0']), ('full', P['a0']), ('full', P['kk']), ('full', P['ka'])]
    if vres:
        ins += [('row', c['svl']), ('row', c['vfirst']), ('full', P['v0'])]
    n_fwd = len(ins)
    ins += [('row', t) for t in (dr_t, ddecay, dk2_t, dv_t, dA, dB)]

    def prep_bwd(*xs):
        q = _rwkv_prep(*xs[:n_fwd])
        kb, kkg, kag = xs[1], xs[7], xs[8]
        dr, ddec, dk2, dv, dav, dbv = xs[n_fwd:]
        a, kk, kkp, inv = q['a'], q['kk'], q['kkp'], q['inv']
        dkk = dbv * a - dav
        da = dbv * kk + dk2 * kb * kag
        dk = dk2 * (1.0 + (a - 1.0) * kag)
        pr = _segsum(dkk * kkp, C_HEAD)
        dkkp = dkk * inv - jnp.where(q['nrm'] > 1e-12, kkp * pr * inv * inv * inv, 0.0)
        dk = dk + dkkp * kkg
        dap = da * a * (1.0 - a)
        dwp = ddec * q['decay'] * (-q['ew']) * _sigmoid(-q['wp'])
        outs = [dr, dk]
        accs = [jnp.sum(dwp, axis=0, keepdims=True), jnp.sum(dap, axis=0, keepdims=True),
                jnp.sum(dkkp * kb, axis=0, keepdims=True), jnp.sum(dk2 * kb * (a - 1.0), axis=0, keepdims=True)]
        if vres:
            v0b, vfb, sv = xs[2], xs[10], q['sv']
            dsvp = dv * (vfb - v0b) * sv * (1.0 - sv)
            outs += [dv * (1.0 - sv), dwp, dap, dsvp, dv * sv]
            accs.append(jnp.sum(dsvp, axis=0, keepdims=True))
        else:
            outs += [dv, dwp, dap]
        return tuple(outs + accs)

    outs = [(D, BF16)] * 5 + ([(D, BF16), (D, F32)] if vres else [])
    res = _rowwise(prep_bwd, "rwkv_prep_bwd", S, _tm(S, D, 40), ins, outs, [(1, D)] * (5 if vres else 4))
    drb, dkb, dv0b, dwpb, dapb = res[:5]
    dvfirst_out = res[6] if vres else None
    accs = res[len(outs):]
    G = dict(wo=dwo, w0=accs[0], a0=accs[1], kk=accs[2], ka=accs[3], gn_g=dgn_g, gn_b=dgn_b, rk=drk)
    G['w2'] = _mm_tn(c['tw'], dwpb, P['w2'], "rwkv_dlora_out", BF16)
    G['a2'] = _mm_tn(c['la1b'], dapb, P['a2'], "rwkv_dlora_out", BF16)
    G['g2'] = _mm_tn(c['sg'], dgb, P['g2'], "rwkv_dlora_out_g", BF16)
    dtw = _mm_nt(dwpb, P['w2'], "rwkv_dlora_mid", F32)
    dla1 = _mm_nt(dapb, P['a2'], "rwkv_dlora_mid", F32)
    dsg = _mm_nt(dgb, P['g2'], "rwkv_dlora_mid_g", F32)
    ins = [('row', dtw), ('row', c['tw']), ('row', dla1), ('row', dsg), ('row', c['sg'])]
    outs = [(dtw.shape[1], BF16), (dla1.shape[1], BF16), (dsg.shape[1], BF16)]
    if vres:
        dsvpb = res[5]
        G['v0'] = accs[4]
        G['v2'] = _mm_tn(c['lv1b'], dsvpb, P['v2'], "rwkv_dlora_out_v", BF16)
        dlv1 = _mm_nt(dsvpb, P['v2'], "rwkv_dlora_mid_v", F32)
        ins.append(('row', dlv1))
        outs.append((dlv1.shape[1], BF16))

    def lora_act_bwd(dtwb, twb, dla1b_, dsgb, sgb, *rest):
        twb, sgb = twb.astype(F32), sgb.astype(F32)
        return tuple([dtwb * (1.0 - twb * twb), dla1b_, dsgb * sgb * (1.0 - sgb)] + list(rest))

    acts = _rowwise(lora_act_bwd, "rwkv_lora_act_bwd", S, _tm(S, 1024, 6), ins, outs)
    dlw1b, dla1b, dlg1b = acts[0], acts[1], acts[2]
    G['w1'] = _mm_tn(xw, dlw1b, P['w1'], "rwkv_dlora_in", BF16)
    G['a1'] = _mm_tn(xa, dla1b, P['a1'], "rwkv_dlora_in", BF16)
    G['g1'] = _mm_tn(xg, dlg1b, P['g1'], "rwkv_dlora_in_g", BF16)
    G['wr'] = _mm_tn(xr, drb, P['wr'], "rwkv_dwproj", BF16)
    G['wk'] = _mm_tn(xk, dkb, P['wk'], "rwkv_dwproj", BF16)
    G['wv'] = _mm_tn(xv, dv0b, P['wv'], "rwkv_dwproj", BF16)
    dxw = _mm_nt(dlw1b, P['w1'], "rwkv_dx_lora", F32)
    dxa = _mm_nt(dla1b, P['a1'], "rwkv_dx_lora", F32)
    dxg = _mm_nt(dlg1b, P['g1'], "rwkv_dx_lora_g", F32)
    dxr = _mm_nt(drb, P['wr'], "rwkv_dx", F32)
    dxk = _mm_nt(dkb, P['wk'], "rwkv_dx", F32)
    dxv = _mm_nt(dv0b, P['wv'], "rwkv_dx", F32)
    if vres:
        G['v1'] = _mm_tn(xv, acts[3], P['v1'], "rwkv_dlora_in_v", BF16)
        dxv = _mm_nt(acts[3], P['v1'], "rwkv_dx_lora_v", F32, res=dxv)

    def mix_bwd(x, xs, m0, m1, m2, m3, m4, m5, d0, d1, d2, d3, d4, d5):
        xx = xs - x
        ds_ = (d0, d1, d2, d3, d4, d5)
        dxx = d0 * m0 + d1 * m1 + d2 * m2 + d3 * m3 + d4 * m4 + d5 * m5
        dsum = d0 + d1 + d2 + d3 + d4 + d5
        return tuple([dsum - dxx, dxx] + [jnp.sum(d * xx, axis=0, keepdims=True) for d in ds_])

    res = _rowwise(mix_bwd, "rwkv_mix_bwd", S, _tm(S, D, 24),
                   [('row', c['hn']), ('row', c['hs'])] + [('full', m) for m in P['mix']]
                   + [('row', d) for d in (dxr, dxw, dxk, dxv, dxa, dxg)], [(D, F32), (D, F32)], [(1, D)] * 6)
    dx_here, dxs = res[0], res[1]
    G['mix'] = res[2:]
    dhn = dx_here + jnp.concatenate([dxs[1:], jnp.zeros((1, D), F32)], axis=0)
    dh2, G['gain'] = _rms_bwd("rms_bwd", h, gain, dhn, dh)
    return dh2, G, dvfirst_out


def _loss_bwd(h, target, gain):
    S, D = h.shape

    def fn(hb, tb, g):
        r = lax.rsqrt(jnp.mean(hb * hb, axis=-1, keepdims=True) + RMS_EPS)
        xh = hb * r
        e = xh * g - tb
        dy = e * (1.0 / D)
        dxh = dy * g
        dx = r * (dxh - xh * jnp.mean(dxh * xh, axis=-1, keepdims=True))
        part = jnp.sum(jnp.sum(e * e, axis=-1, keepdims=True), axis=0, keepdims=True) * (0.5 / D)
        return dx, jnp.sum(dy * xh, axis=0, keepdims=True), jnp.broadcast_to(part, (1, LANES))

    dh, dgain, part = _rowwise(fn, "loss", S, _tm(S, D, 8), [('row', h), ('row', target), ('full', gain)],
                               [(D, F32)], [(1, D), (1, LANES)])
    return part[0, 0], dh, dgain


WEIGHTS = ['norms', 'final_norm', 'ffn_wg', 'ffn_wu', 'ffn_wd', 'ple_wp', 'ple_wg', 'e_w_in', 'e_w_out', 'a_vnorm',
           'a_ws', 'a_bs', 'b_onorm', 'b_lb_logits', 'c_mix', 'c_wr', 'c_wk', 'c_wv', 'c_wo', 'c_w0', 'c_w1', 'c_w2',
           'c_a0', 'c_a1', 'c_a2', 'c_g1', 'c_g2', 'c_kk', 'c_ka', 'c_rk', 'c_gn_g', 'c_gn_b', 'c_v0', 'c_v1', 'c_v2']
BIG = {'ffn_wg': 'col', 'ffn_wu': 'col', 'ffn_wd': 'row', 'ple_wg': 'row', 'e_w_in': 'col', 'e_w_out': 'row',
       'c_wr': 'row', 'c_wk': 'row', 'c_wv': 'row', 'c_wo': 'row', 'ple_wp': 'col', 'c_w1': 'row', 'c_w2': 'col',
       'c_a1': 'row', 'c_a2': 'col', 'c_g1': 'row', 'c_g2': 'col', 'c_v1': 'row', 'c_v2': 'col'}
SMALL = ['norms', 'c_mix', 'c_w0', 'c_a0', 'c_kk', 'c_ka', 'c_gn_g', 'c_gn_b', 'c_v0']
REP = ['final_norm', 'a_vnorm', 'a_ws', 'a_bs', 'b_onorm', 'b_lb_logits', 'c_rk']
PACK_QUANTUM = 1024 * LANES


def _pack(arrays, lead=0):
    lead_shape = arrays[0].shape[:lead]
    flat = jnp.concatenate([a.reshape(lead_shape + (-1,)).astype(F32) for a in arrays], axis=-1)
    pad = (-flat.shape[-1]) % PACK_QUANTUM
    if pad:
        flat = jnp.concatenate([flat, jnp.zeros(lead_shape + (pad,), F32)], axis=-1)
    return flat.reshape(lead_shape + (-1, LANES))


def _unpack(packed, names, shapes, lead=0):
    lead_shape = packed.shape[:lead]
    flat = packed.reshape(lead_shape + (-1,))
    out, off = {}, 0
    for n, s in zip(names, shapes):
        size = 1
        for d in s:
            size *= d
        out[n] = flat[..., off:off + size].reshape(lead_shape + tuple(s))
        off += size
    return out


def _full_vec(g):
    return jnp.moveaxis(g, 0, -2).reshape(g.shape[1:-1] + (N_CHIPS * g.shape[-1],))


def _vec_shards(g):
    return jnp.moveaxis(g.reshape(g.shape[:-1] + (N_CHIPS, g.shape[-1] // N_CHIPS)), -2, 0)


def _lower_bounds(logits):
    probs = jax.nn.softmax(logits.astype(F32), axis=0)
    return jnp.cumsum(probs, axis=0) - probs[0]


def kernel(x, p, norms, final_norm, ffn_wg, ffn_wu, ffn_wd, ple_wp, ple_wg, e_w_in, e_w_out, a_vnorm, a_ws, a_bs, b_onorm, b_lb_logits, c_mix, c_wr, c_wk, c_wv, c_wo, c_w0, c_w1, c_w2, c_a0, c_a1, c_a2, c_g1, c_g2, c_kk, c_ka, c_rk, c_gn_g, c_gn_b, c_v0, c_v1, c_v2, loss_target, m_norms, m_final_norm, m_ffn_wg, m_ffn_wu, m_ffn_wd, m_ple_wp, m_ple_wg, m_e_w_in, m_e_w_out, m_a_vnorm, m_a_ws, m_a_bs, m_b_onorm, m_b_lb_logits, m_c_mix, m_c_wr, m_c_wk, m_c_wv, m_c_wo, m_c_w0, m_c_w1, m_c_w2, m_c_a0, m_c_a1, m_c_a2, m_c_g1, m_c_g2, m_c_kk, m_c_ka, m_c_rk, m_c_gn_g, m_c_gn_b, m_c_v0, m_c_v1, m_c_v2, v_norms, v_final_norm, v_ffn_wg, v_ffn_wu, v_ffn_wd, v_ple_wp, v_ple_wg, v_e_w_in, v_e_w_out, v_a_vnorm, v_a_ws, v_a_bs, v_b_onorm, v_b_lb_logits, v_c_mix, v_c_wr, v_c_wk, v_c_wv, v_c_wo, v_c_w0, v_c_w1, v_c_w2, v_c_a0, v_c_a1, v_c_a2, v_c_g1, v_c_g2, v_c_kk, v_c_ka, v_c_rk, v_c_gn_g, v_c_gn_b, v_c_v0, v_c_v1, v_c_v2):
    A = dict(locals())
    assert x.shape[0] == 1, "one example per device"

    big_names = list(BIG)
    gathered = _all_gather("gather_weights", [A[n].astype(BF16).reshape(-1, A[n].shape[-1]) for n in big_names],
                           [_pack([A[n] for n in SMALL])])
    GB = {n: gathered[i].reshape((N_CHIPS,) + A[n].shape) for i, n in enumerate(big_names)}
    GS = _unpack(gathered[-1], SMALL, [A[n].shape for n in SMALL], lead=1)

    part, grad_x, big_entries, sm_pack, rep_pack = _local_step(A, GB, GS)
    loss = lax.psum(part, ("x", "y", "c"))

    c_arr = lax.axis_index("c").astype(jnp.int32).reshape(1)
    grads = [e[2] for e in big_entries]
    swapped = _swap_halves("reduce_swap_cores", grads)
    parts = [_add_own_half("reduce_add_cores", g, s, c_arr) for g, s in zip(grads, swapped)]
    slots = _scatter_chips("reduce_scatter_chips", parts)
    halves = [_sum_slots("reduce_sum_chips", s, (3, 0, 1, 2)) for s in slots]
    places = [(big_names.index(n), idx) for n, idx, _ in big_entries]
    big_grads = _join_halves("reduce_join_cores", halves, places,
                             [jax.ShapeDtypeStruct(A[n].shape, F32) for n in big_names])
    sm_slots, rep_slots = _spread_all("reduce_small", sm_pack, rep_pack)
    sm_grad = _sum_slots("reduce_sum_small", sm_slots, tuple(range(8)))
    rep_grad = _sum_slots("reduce_sum_small", rep_slots, tuple(range(8)))

    outs = {}
    for o, n in enumerate(big_names):
        shp, C = A[n].shape, A[n].shape[-1]
        g = big_grads[o]
        d, nm, nv = _adamw("adamw", g.reshape(-1, C), A[n].reshape(-1, C), A['m_' + n].reshape(-1, C),
                           A['v_' + n].reshape(-1, C))
        outs[n] = (g, d.reshape(shp), nm.reshape(shp), nv.reshape(shp))
    for names, g in ((SMALL, sm_grad), (REP, rep_grad)):
        shapes = [A[n].shape for n in names]
        res = _adamw("adamw_packed", g, _pack([A[n] for n in names]), _pack([A['m_' + n] for n in names]),
                     _pack([A['v_' + n] for n in names]))
        un = [_unpack(t, names, shapes) for t in (g,) + tuple(res)]
        for n in names:
            outs[n] = tuple(u[n] for u in un)
    return (loss, grad_x, *[outs[n][0] for n in WEIGHTS], *[outs[n][1] for n in WEIGHTS],
            *[outs[n][2] for n in WEIGHTS], *[outs[n][3] for n in WEIGHTS])


def _local_step(A, GB, GS):
    x, p, a_vnorm, a_ws, a_bs, b_onorm, b_lb_logits, c_rk = (A[n] for n in (
        'x', 'p', 'a_vnorm', 'a_ws', 'a_bs', 'b_onorm', 'b_lb_logits', 'c_rk'))
    S, D = x.shape[1], x.shape[2]
    depth = A['ffn_wg'].shape[0]
    aw = D // 2
    h = x[0]
    target = A['loss_target'][0]
    final_norm = A['final_norm']
    pb = p[:, 0].astype(BF16)

    def bigv(n, *idx):
        return WV(GB[n], idx, BIG[n])

    smv = bigv

    def row(v):
        return v.reshape(1, -1)

    vecs = {n: _full_vec(GS[n]) for n in SMALL}
    lb_all, lb_vjp = jax.vjp(_lower_bounds, b_lb_logits)

    def rwkv_params(j):
        P = dict(mix=[vecs['c_mix'][j, q:q + 1] for q in range(6)],
                 wr=bigv('c_wr', j), wk=bigv('c_wk', j), wv=bigv('c_wv', j), wo=bigv('c_wo', j),
                 w1=smv('c_w1', j), w2=smv('c_w2', j), a1=smv('c_a1', j), a2=smv('c_a2', j),
                 g1=smv('c_g1', j), g2=smv('c_g2', j),
                 w0=row(vecs['c_w0'][j]), a0=row(vecs['c_a0'][j]), kk=row(vecs['c_kk'][j]), ka=row(vecs['c_ka'][j]),
                 gn_g=row(vecs['c_gn_g'][j]), gn_b=row(vecs['c_gn_b'][j]), rk=c_rk[j].reshape(1, D))
        if j > 0:
            P.update(v0=row(vecs['c_v0'][j - 1]), v1=smv('c_v1', j - 1), v2=smv('c_v2', j - 1))
        return P

    def even_params(i):
        j = i // 2
        return (bigv('e_w_in', j), bigv('e_w_out', j), a_vnorm[j:j + 1], a_ws[j], a_bs[j].T, b_onorm[j:j + 1],
                lb_all[i:i + 1])

    def gain(i, q):
        return row(vecs['norms'][i, q])

    def ffn_views(i, q):
        return bigv('ffn_wg', i, q), bigv('ffn_wu', i, q), bigv('ffn_wd', i, q)

    caches, vfirst = [], None
    for i in range(depth):
        c = {}
        h, c['f1'] = _ffn_fwd(h, gain(i, 0), *ffn_views(i, 0))
        if i % 2 == 0:
            h, c['mix'] = _even_fwd(h, gain(i, 1), *even_params(i))
        else:
            h, c['mix'], vfirst = _rwkv_fwd(h, gain(i, 1), rwkv_params(i // 2), vfirst if i // 2 > 0 else None)
        h, c['f2'] = _ffn_fwd(h, gain(i, 2), *ffn_views(i, 1))
        h, c['ple'] = _ple_fwd(h, gain(i, 3), pb[i], bigv('ple_wg', i), smv('ple_wp', i))
        caches.append(c)

    part, dh, dfinal = _loss_bwd(h, target, final_norm.reshape(1, D))
    big_entries = []
    sm = {n: {} for n in SMALL}
    rep = {n: {} for n in REP}
    dvfirst = None
    for i in reversed(range(depth)):
        j, c = i // 2, caches[i]
        dh, dg, dwgate, dwproj = _ple_bwd(dh, c['ple'], gain(i, 3), pb[i], bigv('ple_wg', i), smv('ple_wp', i))
        sm['norms'][(i, 3)] = dg
        big_entries += [('ple_wg', (i,), dwgate), ('ple_wp', (i,), dwproj)]
        for q, key in ((1, 'f2'), (0, 'f1')):
            if key == 'f1':
                if i % 2 == 0:
                    dh, G = _even_bwd(dh, c['mix'], gain(i, 1), *even_params(i))
                    big_entries += [('e_w_in', (j,), G['w_in']), ('e_w_out', (j,), G['w_out'])]
                    rep['a_vnorm'][j], rep['a_ws'][j], rep['a_bs'][j] = G['vgain'][0], G['ws'], G['bs_t'].T
                    rep['b_onorm'][j], rep['b_lb_logits'][i] = G['onorm'][0], G['lb'][0]
                else:
                    dh, G, dvf = _rwkv_bwd(dh, c['mix'], gain(i, 1), rwkv_params(j), dvfirst if j == 0 else None)
                    if dvf is not None:
                        dvfirst = dvf if dvfirst is None else dvfirst + dvf
                    big_entries += [('c_wr', (j,), G['wr']), ('c_wk', (j,), G['wk']), ('c_wv', (j,), G['wv']),
                                    ('c_wo', (j,), G['wo'])]
                    big_entries += [('c_' + n, (j,), G[n]) for n in ('w1', 'w2', 'a1', 'a2', 'g1', 'g2')]
                    for n in ('w0', 'a0', 'kk', 'ka', 'gn_g', 'gn_b'):
                        sm['c_' + n][(j,)] = G[n]
                    sm['c_mix'][(j,)] = jnp.concatenate(G['mix'], axis=0)
                    rep['c_rk'][j] = G['rk'].reshape(c_rk.shape[1:])
                    if j > 0:
                        sm['c_v0'][(j - 1,)] = G['v0']
                        big_entries += [('c_v1', (j - 1,), G['v1']), ('c_v2', (j - 1,), G['v2'])]
                sm['norms'][(i, 1)] = G['gain']
            dh, dg, dwg, dwu, dwd = _ffn_bwd(dh, c[key], gain(i, 2 * q), *ffn_views(i, q))
            sm['norms'][(i, 2 * q)] = dg
            big_entries += [('ffn_wg', (i, q), dwg), ('ffn_wu', (i, q), dwu), ('ffn_wd', (i, q), dwd)]
    grad_x = dh[None]

    def stacked(blocks, lead_shape):
        def rec(prefix, dims):
            if not dims:
                return blocks[prefix]
            return jnp.stack([rec(prefix + (q,), dims[1:]) for q in range(dims[0])], axis=0)
        return rec((), tuple(lead_shape))

    sm_shards = []
    for n in SMALL:
        blk = A[n].shape
        full = stacked(sm[n], blk[:-1] if n != 'c_mix' else blk[:-2])
        sm_shards.append(_vec_shards(full.reshape(blk[:-1] + (D,))))
    sm_pack = _pack(sm_shards, lead=1)
    dlb = jnp.stack([rep['b_lb_logits'].get(i, jnp.zeros((aw,), F32)) for i in range(depth)], axis=0)
    rep_grads = dict(final_norm=dfinal[0], a_vnorm=stacked({(k,): v for k, v in rep['a_vnorm'].items()}, a_vnorm.shape[:1]),
                     a_ws=stacked({(k,): v for k, v in rep['a_ws'].items()}, a_ws.shape[:1]),
                     a_bs=stacked({(k,): v for k, v in rep['a_bs'].items()}, a_bs.shape[:1]),
                     b_onorm=stacked({(k,): v for k, v in rep['b_onorm'].items()}, b_onorm.shape[:1]),
                     b_lb_logits=lb_vjp(dlb)[0],
                     c_rk=stacked({(k,): v for k, v in rep['c_rk'].items()}, c_rk.shape[:1]))
    rep_pack = _pack([rep_grads[n] for n in REP])
    return part, grad_x, big_entries, sm_pack, rep_pack
```

```python
import functools

import jax
import jax.numpy as jnp
from jax import lax
from jax.experimental import pallas as pl
from jax.experimental.pallas import tpu as pltpu

F32 = jnp.float32
BF16 = jnp.bfloat16
MESH = pl.DeviceIdType.MESH

LANES = 128
VMEM_LIMIT = 56 * 1024 * 1024
MM_VMEM_BUDGET = 36 * 1024 * 1024
N_CHIPS = 4

RMS_EPS = 1e-6
A_GROUP = 128
A_CHUNK = 128
B_HEAD = 128
B_MIN_F = 1e-30
C_HEAD = 64
C_GN_EPS = 64e-5
ADAM_LR = 0.001
ADAM_B1 = 0.9
ADAM_B2 = 0.999
ADAM_EPS = 1e-08
ADAM_WD = 0.01
ADAM_STEP = 10


def _sigmoid(x):
    return 1.0 / (1.0 + jnp.exp(-x))


def _silu(x):
    return x * _sigmoid(x)


def _dsilu(x):
    s = _sigmoid(x)
    return s * (1.0 + x * (1.0 - s))


_GELU_C = 0.7978845608028654


def _gelu(x):
    return 0.5 * x * (1.0 + jnp.tanh(_GELU_C * (x + 0.044715 * x * x * x)))


def _dgelu(x):
    th = jnp.tanh(_GELU_C * (x + 0.044715 * x * x * x))
    return 0.5 * (1.0 + th) + 0.5 * x * (1.0 - th * th) * _GELU_C * (1.0 + 3.0 * 0.044715 * x * x)


def _softplus(x):
    return jnp.maximum(x, 0.0) + jnp.log(1.0 + jnp.exp(-jnp.abs(x)))


def _seg_ones(seg):
    i = lax.broadcasted_iota(jnp.int32, (LANES, LANES), 0) // seg
    j = lax.broadcasted_iota(jnp.int32, (LANES, LANES), 1) // seg
    return jnp.where(i == j, 1.0, 0.0).astype(BF16)


def _segsum(x, seg):
    ones = _seg_ones(seg)
    outs = []
    for j in range(x.shape[1] // LANES):
        xb = x[:, j * LANES:(j + 1) * LANES]
        hi = xb.astype(BF16)
        r1 = xb - hi.astype(F32)
        mid = r1.astype(BF16)
        lo = (r1 - mid.astype(F32)).astype(BF16)
        acc = jnp.dot(hi, ones, preferred_element_type=F32)
        acc = acc + jnp.dot(mid, ones, preferred_element_type=F32)
        acc = acc + jnp.dot(lo, ones, preferred_element_type=F32)
        outs.append(acc)
    return outs[0] if len(outs) == 1 else jnp.concatenate(outs, axis=1)


def _rowwise(fn, name, rows, tm, ins, outs, accs=()):
    n_in, n_out, n_acc = len(ins), len(outs), len(accs)
    arrays, in_specs = [], []
    for spec in ins:
        kind, arr = spec[0], spec[1]
        arrays.append(arr)
        if kind == 'row':
            in_specs.append(pl.BlockSpec((tm, arr.shape[1]), lambda i: (i, 0)))
        elif kind == 'col':
            in_specs.append(pl.BlockSpec((tm, spec[2]), functools.partial(lambda i, cb: (i, cb), cb=spec[3])))
        else:
            in_specs.append(pl.BlockSpec(arr.shape, functools.partial(lambda i, nd: (0,) * nd, nd=arr.ndim)))
    out_shape = [jax.ShapeDtypeStruct((rows, w), dt) for (w, dt) in outs]
    out_specs = [pl.BlockSpec((tm, w), lambda i: (i, 0)) for (w, _) in outs]
    out_shape += [jax.ShapeDtypeStruct(s, F32) for s in accs]
    out_specs += [pl.BlockSpec(s, functools.partial(lambda i, nd: (0,) * nd, nd=len(s))) for s in accs]

    def body(*refs):
        vals = fn(*[r[...] for r in refs[:n_in]])
        if not isinstance(vals, (tuple, list)):
            vals = (vals,)
        for r, v in zip(refs[n_in:n_in + n_out], vals[:n_out]):
            r[...] = v.astype(r.dtype)
        if n_acc:
            acc_refs = refs[n_in + n_out:]

            @pl.when(pl.program_id(0) == 0)
            def _():
                for r in acc_refs:
                    r[...] = jnp.zeros(r.shape, F32)

            for r, v in zip(acc_refs, vals[n_out:]):
                r[...] += v

    res = pl.pallas_call(
        body, name=name, grid=(rows // tm,), in_specs=in_specs, out_specs=out_specs, out_shape=out_shape,
        compiler_params=pltpu.CompilerParams(
            dimension_semantics=("arbitrary",) if n_acc else ("parallel",), vmem_limit_bytes=VMEM_LIMIT),
    )(*arrays)
    return res


class WV:
    def __init__(self, arr, idx, kind):
        self.arr, self.idx, self.kind = arr, tuple(idx), kind
        self.ns = arr.shape[0]
        self.R, self.C = arr.shape[-2:]
        self.K = self.R * (self.ns if kind == 'row' else 1)
        self.N = self.C * (self.ns if kind == 'col' else 1)

    def spec(self, br, bc, rmap, cmap):
        lead = (None,) * (1 + len(self.idx))
        nrb, ncb = self.R // br, self.C // bc
        idx, kind = self.idx, self.kind

        def index_map(*g):
            ri, ci = rmap(*g), cmap(*g)
            if kind == 'row':
                return (ri // nrb, *idx, ri % nrb, ci)
            return (ci // ncb, *idx, ri, ci % ncb)

        return pl.BlockSpec(lead + (br, bc), index_map)


def _tile_options(n):
    return [n] + [d for d in range(n - LANES, LANES - 1, -LANES) if n % d == 0]


def _pick_tiles(opt_m, opt_n, opt_k, out_bytes, has_res, full_k):
    best, best_key = None, None
    for tm in opt_m:
        for tn in opt_n:
            for tk in opt_k:
                multi = tk != full_k
                est = 2 * (tm * tk * 2 + tk * tn * 2 + tm * tn * out_bytes) + tm * tn * 4 * (2 if multi else 1)
                if has_res:
                    est += 2 * tm * tn * 4
                if est > MM_VMEM_BUDGET:
                    continue
                key = (tm * tn * tk, tk)
                if best is None or key > best_key:
                    best, best_key = (tm, tn, tk), key
    return best


def _mm_call(name, dims, grid, in_specs, out_spec, out_shape, operands, nk, acc_shape, scale, has_res):
    def body(*refs):
        a_ref, b_ref = refs[0], refs[1]
        res_ref = refs[2] if has_res else None
        o_ref = refs[3] if has_res else refs[2]

        def finalize(acc):
            acc = acc * scale if scale != 1.0 else acc
            if has_res:
                acc = acc + res_ref[...]
            o_ref[...] = acc.astype(o_ref.dtype)

        part = lax.dot_general(a_ref[...], b_ref[...], dims, preferred_element_type=F32)
        if nk == 1:
            finalize(part)
        else:
            acc_ref = refs[-1]
            k = pl.program_id(2)

            @pl.when(k == 0)
            def _():
                acc_ref[...] = part

            @pl.when(k > 0)
            def _():
                acc_ref[...] += part

            @pl.when(k == nk - 1)
            def _():
                finalize(acc_ref[...])

    return pl.pallas_call(
        body, name=name, grid=grid, in_specs=in_specs, out_specs=out_spec, out_shape=out_shape,
        scratch_shapes=[pltpu.VMEM(acc_shape, F32)] if nk > 1 else [],
        compiler_params=pltpu.CompilerParams(
            dimension_semantics=("parallel", "parallel", "arbitrary"), vmem_limit_bytes=VMEM_LIMIT),
    )(*operands)


def _mm_nn(a, w, name, out_dtype=F32, res=None, scale=1.0):
    M, K = a.shape
    N = w.N
    opt_n = _tile_options(w.C)
    opt_k = _tile_options(w.R)
    tm, tn, tk = _pick_tiles(_tile_options(M), opt_n, opt_k, jnp.dtype(out_dtype).itemsize, res is not None, K)
    nk = K // tk
    in_specs = [pl.BlockSpec((tm, tk), lambda n, m, k: (m, k)),
                w.spec(tk, tn, lambda n, m, k: k, lambda n, m, k: n)]
    operands = [a, w.arr]
    if res is not None:
        in_specs.append(pl.BlockSpec((tm, tn), lambda n, m, k: (m, n)))
        operands.append(res)
    return _mm_call(name, (((1,), (0,)), ((), ())), (N // tn, M // tm, nk), in_specs,
                    pl.BlockSpec((tm, tn), lambda n, m, k: (m, n)), jax.ShapeDtypeStruct((M, N), out_dtype),
                    operands, nk, (tm, tn), scale, res is not None)


def _mm_nt(a, w, name, out_dtype=F32, res=None, scale=1.0):
    M, C = a.shape
    Ko = w.K
    opt_n = _tile_options(w.R)
    opt_k = _tile_options(w.C)
    tm, tn, tk = _pick_tiles(_tile_options(M), opt_n, opt_k, jnp.dtype(out_dtype).itemsize, res is not None, C)
    nk = C // tk
    in_specs = [pl.BlockSpec((tm, tk), lambda n, m, k: (m, k)),
                w.spec(tn, tk, lambda n, m, k: n, lambda n, m, k: k)]
    operands = [a, w.arr]
    if res is not None:
        in_specs.append(pl.BlockSpec((tm, tn), lambda n, m, k: (m, n)))
        operands.append(res)
    return _mm_call(name, (((1,), (1,)), ((), ())), (Ko // tn, M // tm, nk), in_specs,
                    pl.BlockSpec((tm, tn), lambda n, m, k: (m, n)), jax.ShapeDtypeStruct((M, Ko), out_dtype),
                    operands, nk, (tm, tn), scale, res is not None)


def _mm_tn(a, dy, like, name, out_dtype=BF16, scale=1.0):
    M, K = a.shape
    N = dy.shape[1]
    out = WV(jax.ShapeDtypeStruct((like.ns, like.R, like.C), out_dtype), (), like.kind)
    opt_m = _tile_options(like.R)
    opt_n = _tile_options(like.C)
    tko, tno, tc = _pick_tiles(opt_m, opt_n, _tile_options(M), jnp.dtype(out_dtype).itemsize, False, M)
    nk = M // tc
    in_specs = [pl.BlockSpec((tc, tko), lambda i, j, c: (c, i)),
                pl.BlockSpec((tc, tno), lambda i, j, c: (c, j))]
    return _mm_call(name, (((0,), (0,)), ((), ())), (K // tko, N // tno, nk), in_specs,
                    out.spec(tko, tno, lambda i, j, c: i, lambda i, j, c: j), out.arr,
                    [a, dy], nk, (tko, tno), scale, False)


SCAN_TB = 16


def _subsum(x):
    r = x.shape[0]
    while r > 8:
        r //= 2
        x = x[:r] + x[r:]
    while r > 1:
        r //= 2
        x = x + pltpu.roll(x, r, 0)
    return x


def _head_ones(n_heads):
    i = lax.broadcasted_iota(jnp.int32, (LANES, LANES), 0) % n_heads
    j = lax.broadcasted_iota(jnp.int32, (LANES, LANES), 1) % n_heads
    return jnp.where(i == j, 1.0, 0.0).astype(BF16)


def _lanesum_mxu(x, ones):
    hi = x.astype(BF16)
    r1 = x - hi.astype(F32)
    mid = r1.astype(BF16)
    lo = (r1 - mid.astype(F32)).astype(BF16)
    acc = jnp.dot(hi, ones, preferred_element_type=F32)
    acc = acc + jnp.dot(mid, ones, preferred_element_type=F32)
    return acc + jnp.dot(lo, ones, preferred_element_type=F32)


def _head_sel(n_heads):
    h = lax.broadcasted_iota(jnp.int32, (n_heads, LANES), 0)
    lane = lax.broadcasted_iota(jnp.int32, (n_heads, LANES), 1) % n_heads
    return jnp.where(h == lane, 1.0, 0.0).astype(BF16)


def _split3(x):
    hi = x.astype(BF16)
    r1 = x - hi.astype(F32)
    mid = r1.astype(BF16)
    return hi, mid, (r1 - mid.astype(F32)).astype(BF16)


def _nat_from_rows(rows, sel):
    dn = (((1,), (1,)), ((), ()))
    hi, mid, lo = _split3(rows)
    acc = lax.dot_general(sel, hi, dn, preferred_element_type=F32)
    acc = acc + lax.dot_general(sel, mid, dn, preferred_element_type=F32)
    return acc + lax.dot_general(sel, lo, dn, preferred_element_type=F32)


def _rows_of(vals):
    n = len(vals)
    if vals[0].shape[0] >= n:
        idx = lax.broadcasted_iota(jnp.int32, (n, LANES), 0)
        out = vals[0][:n]
        for i in range(1, n):
            out = jnp.where(idx == i, vals[i][:n], out)
        return out
    return jnp.concatenate([v[0:1] for v in vals], axis=0)


def _scan_fwd(name, n_heads, wk, kk, rk, vrow, ak=None, bk=None):
    S, R, _ = wk.shape
    dv = vrow.shape[1]
    ab = ak is not None
    tb = min(SCAN_TB, S)
    grp = min(8, dv)

    def body(*refs):
        if ab:
            w_ref, k_ref, r_ref, v_ref, a_ref, b_ref, y_ref, hist_ref, sa_ref, s_ref = refs
        else:
            w_ref, k_ref, r_ref, v_ref, y_ref, hist_ref, s_ref = refs

        @pl.when(pl.program_id(0) == 0)
        def _():
            s_ref[...] = jnp.zeros(s_ref.shape, F32)

        sel = _head_sel(n_heads)
        ones = _head_ones(n_heads) if ab else None

        def joined(tiles):
            return tiles[0] if len(tiles) == 1 else jnp.concatenate(tiles, axis=0)

        def step(t, carry):
            w, k, r = w_ref[t], k_ref[t], r_ref[t]
            if ab:
                a, b = a_ref[t], b_ref[t]
                tiles = []
                for g0 in range(0, dv, grp):
                    tiles.append(_rows_of([_subsum(s_ref[g0 + i] * a) for i in range(grp)]))
                sa_ref[t] = _lanesum_mxu(joined(tiles), ones)
            tiles = []
            for g0 in range(0, dv, grp):
                ys = []
                for i in range(grp):
                    v = g0 + i
                    st = s_ref[v]
                    hist_ref[t, v] = st
                    st = st * w + v_ref[t, pl.ds(v, 1), :] * k
                    if ab:
                        st = st + sa_ref[t, pl.ds(v, 1), :] * b
                    s_ref[v] = st
                    ys.append(_subsum(st * r))
                tiles.append(_rows_of(ys))
            y_ref[t] = _nat_from_rows(joined(tiles), sel)
            return carry

        lax.fori_loop(0, tb, step, 0, unroll=2)

    kspec = pl.BlockSpec((tb, R, LANES), lambda i: (i, 0, 0))
    nspec = pl.BlockSpec((tb, n_heads, dv), lambda i: (i, 0, 0))
    rspec = pl.BlockSpec((tb, dv, LANES), lambda i: (i, 0, 0))
    operands = [wk, kk, rk, vrow] + ([ak, bk] if ab else [])
    in_specs = [kspec, kspec, kspec, rspec] + ([kspec, kspec] if ab else [])
    out_shape = [jax.ShapeDtypeStruct((S, n_heads, dv), F32), jax.ShapeDtypeStruct((S, dv, R, LANES), F32)]
    out_specs = [nspec, pl.BlockSpec((tb, dv, R, LANES), lambda i: (i, 0, 0, 0))]
    if ab:
        out_shape.append(jax.ShapeDtypeStruct((S, dv, LANES), F32))
        out_specs.append(rspec)
    res = pl.pallas_call(
        body, name=name, grid=(S // tb,), in_specs=in_specs, out_specs=out_specs, out_shape=out_shape,
        scratch_shapes=[pltpu.VMEM((dv, R, LANES), F32)],
        compiler_params=pltpu.CompilerParams(dimension_semantics=("arbitrary",), vmem_limit_bytes=VMEM_LIMIT),
    )(*operands)
    return (res[0], res[1], res[2]) if ab else (res[0], res[1], None)


def _scan_bwd(name, n_heads, wk, kk, rk, vrow, hist, dyrow, ak=None, bk=None, sarow=None):
    S, R, _ = wk.shape
    dv = vrow.shape[1]
    ab = ak is not None
    tb = min(SCAN_TB, S)
    nb = S // tb
    grp = min(8, dv)

    def body(*refs):
        if ab:
            (w_ref, k_ref, r_ref, v_ref, hist_ref, dy_ref, a_ref, b_ref, sa_ref,
             dw_ref, dk_ref, dr_ref, dv_ref, da_ref, db_ref, ds_ref, dsa_ref) = refs
        else:
            (w_ref, k_ref, r_ref, v_ref, hist_ref, dy_ref,
             dw_ref, dk_ref, dr_ref, dv_ref, ds_ref) = refs

        @pl.when(pl.program_id(0) == 0)
        def _():
            ds_ref[...] = jnp.zeros(ds_ref.shape, F32)

        sel = _head_sel(n_heads)
        ones = _head_ones(n_heads) if ab else None

        def joined(tiles):
            return tiles[0] if len(tiles) == 1 else jnp.concatenate(tiles, axis=0)

        def step(j, carry):
            t = tb - 1 - j
            w, k, r = w_ref[t], k_ref[t], r_ref[t]
            zero = jnp.zeros((R, LANES), F32)
            u, dw, dk, da, db = zero, zero, zero, zero, zero
            vd = jnp.sum(v_ref[t] * dy_ref[t], axis=0, keepdims=True)
            sd = jnp.sum(sa_ref[t] * dy_ref[t], axis=0, keepdims=True) if ab else None
            if ab:
                a, b = a_ref[t], b_ref[t]
                tiles = []
                for g0 in range(0, dv, grp):
                    ps = []
                    for i in range(grp):
                        v = g0 + i
                        dst = ds_ref[v] + dy_ref[t, pl.ds(v, 1), :] * r
                        ds_ref[v] = dst
                        ps.append(_subsum(dst * b))
                    tiles.append(_rows_of(ps))
                dsa_ref[...] = _lanesum_mxu(joined(tiles), ones)
            tiles = []
            for g0 in range(0, dv, grp):
                dvs = []
                for i in range(grp):
                    v = g0 + i
                    sp = hist_ref[t, v]
                    dyr = dy_ref[t, pl.ds(v, 1), :]
                    vr = v_ref[t, pl.ds(v, 1), :]
                    dst = ds_ref[v] if ab else ds_ref[v] + dyr * r
                    u = u + sp * dyr
                    dw = dw + dst * sp
                    dk = dk + dst * vr
                    dvs.append(_subsum(dst * k))
                    if ab:
                        db = db + dst * sa_ref[t, pl.ds(v, 1), :]
                        dsa = dsa_ref[pl.ds(v, 1), :]
                        da = da + sp * dsa
                        dst = dst * w + dsa * a
                    else:
                        dst = dst * w
                    ds_ref[v] = dst
                tiles.append(_rows_of(dvs))
            dv_ref[t] = _nat_from_rows(joined(tiles), sel)
            dr = w * u + k * vd
            if ab:
                dr = dr + b * sd
                da_ref[t] = da
                db_ref[t] = db
            dw_ref[t] = dw
            dk_ref[t] = dk
            dr_ref[t] = dr
            return carry

        lax.fori_loop(0, tb, step, 0, unroll=2)

    kspec = pl.BlockSpec((tb, R, LANES), lambda i: (nb - 1 - i, 0, 0))
    rspec = pl.BlockSpec((tb, dv, LANES), lambda i: (nb - 1 - i, 0, 0))
    hspec = pl.BlockSpec((tb, dv, R, LANES), lambda i: (nb - 1 - i, 0, 0, 0))
    nspec = pl.BlockSpec((tb, n_heads, dv), lambda i: (nb - 1 - i, 0, 0))
    operands = [wk, kk, rk, vrow, hist, dyrow] + ([ak, bk, sarow] if ab else [])
    in_specs = [kspec, kspec, kspec, rspec, hspec, rspec] + ([kspec, kspec, rspec] if ab else [])
    kshape = jax.ShapeDtypeStruct((S, R, LANES), F32)
    out_shape = [kshape, kshape, kshape, jax.ShapeDtypeStruct((S, n_heads, dv), F32)] + ([kshape, kshape] if ab else [])
    out_specs = [kspec, kspec, kspec, nspec] + ([kspec, kspec] if ab else [])
    return pl.pallas_call(
        body, name=name, grid=(nb,), in_specs=in_specs, out_specs=out_specs, out_shape=out_shape,
        scratch_shapes=[pltpu.VMEM((dv, R, LANES), F32)] + ([pltpu.VMEM((dv, LANES), F32)] if ab else []),
        compiler_params=pltpu.CompilerParams(dimension_semantics=("arbitrary",), vmem_limit_bytes=VMEM_LIMIT),
    )(*operands)


def _to_k(x, n_heads, dk):
    S = x.shape[0]
    kl = LANES // n_heads
    return x.reshape(S, n_heads, dk // kl, kl).transpose(0, 2, 3, 1).reshape(S, dk // kl, LANES)


def _from_k(x, n_heads, dk):
    S = x.shape[0]
    kl = LANES // n_heads
    return x.reshape(S, dk // kl, kl, n_heads).transpose(0, 3, 1, 2).reshape(S, n_heads * dk)


def _to_row(x, n_heads, dv):
    S = x.shape[0]
    return jnp.tile(x.reshape(S, n_heads, dv).transpose(0, 2, 1), (1, 1, LANES // n_heads))


def _tril_mask():
    t = lax.broadcasted_iota(jnp.int32, (A_CHUNK, A_CHUNK), 0)
    s = lax.broadcasted_iota(jnp.int32, (A_CHUNK, A_CHUNK), 1)
    return s <= t


def _gmlp_fwd(name, proj, aw, vgain, ws, bs_t):
    S = proj.shape[0]
    G = aw // A_GROUP

    def body(u_ref, v_ref, gain_ref, ws_ref, bs_ref, o_ref):
        mask = _tril_mask()
        lane = lax.broadcasted_iota(jnp.int32, (A_CHUNK, G), 1)
        bs = bs_ref[...]
        for g in range(G):
            seg = slice(g * A_GROUP, (g + 1) * A_GROUP)
            ua = _gelu(u_ref[:, seg])
            va = _gelu(v_ref[:, seg])
            rs = lax.rsqrt(jnp.mean(va * va, axis=-1, keepdims=True) + RMS_EPS)
            vg = (va * rs) * gain_ref[:, seg]
            wm = jnp.where(mask, ws_ref[g], 0.0).astype(BF16)
            bcol = jnp.sum(jnp.where(lane == g, bs, 0.0), axis=1, keepdims=True)
            s = jnp.dot(wm, vg.astype(BF16), preferred_element_type=F32) + bcol
            o_ref[:, seg] = (ua * s).astype(o_ref.dtype)

    return pl.pallas_call(
        body, name=name, grid=(S // A_CHUNK,),
        in_specs=[pl.BlockSpec((A_CHUNK, aw), lambda i: (i, 0)), pl.BlockSpec((A_CHUNK, aw), lambda i: (i, 1)),
                  pl.BlockSpec((1, aw), lambda i: (0, 0)), pl.BlockSpec((G, A_CHUNK, A_CHUNK), lambda i: (0, 0, 0)),
                  pl.BlockSpec((A_CHUNK, G), lambda i: (0, 0))],
        out_specs=pl.BlockSpec((A_CHUNK, aw), lambda i: (i, 0)),
        out_shape=jax.ShapeDtypeStruct((S, aw), BF16),
        compiler_params=pltpu.CompilerParams(dimension_semantics=("parallel",), vmem_limit_bytes=VMEM_LIMIT),
    )(proj, proj, vgain, ws, bs_t)


def _gmlp_bwd(name, proj, dout, dout_cb, aw, vgain, ws, bs_t):
    S = proj.shape[0]
    G = aw // A_GROUP

    def body(u_ref, v_ref, do_ref, gain_ref, ws_ref, bs_ref, du_ref, dv_ref, dws_ref, dbs_ref, dgain_ref):
        @pl.when(pl.program_id(0) == 0)
        def _():
            dws_ref[...] = jnp.zeros(dws_ref.shape, F32)
            dbs_ref[...] = jnp.zeros(dbs_ref.shape, F32)
            dgain_ref[...] = jnp.zeros(dgain_ref.shape, F32)

        mask = _tril_mask()
        lane = lax.broadcasted_iota(jnp.int32, (A_CHUNK, G), 1)
        bs = bs_ref[...]
        dbs = jnp.zeros((A_CHUNK, G), F32)
        for g in range(G):
            seg = slice(g * A_GROUP, (g + 1) * A_GROUP)
            u, v = u_ref[:, seg], v_ref[:, seg]
            do = do_ref[:, seg].astype(F32)
            ua, va = _gelu(u), _gelu(v)
            rs = lax.rsqrt(jnp.mean(va * va, axis=-1, keepdims=True) + RMS_EPS)
            xh = va * rs
            gain = gain_ref[:, seg]
            vg = (xh * gain).astype(BF16)
            wm = jnp.where(mask, ws_ref[g], 0.0).astype(BF16)
            bcol = jnp.sum(jnp.where(lane == g, bs, 0.0), axis=1, keepdims=True)
            s = jnp.dot(wm, vg, preferred_element_type=F32) + bcol
            du_ref[:, seg] = (do * s * _dgelu(u)).astype(du_ref.dtype)
            ds = do * ua
            dsb = ds.astype(BF16)
            dw = lax.dot_general(dsb, vg, (((1,), (1,)), ((), ())), preferred_element_type=F32)
            dws_ref[g] += jnp.where(mask, dw, 0.0)
            dbs = dbs + jnp.where(lane == g, jnp.sum(ds, axis=1, keepdims=True), 0.0)
            dvg = lax.dot_general(wm, dsb, (((0,), (0,)), ((), ())), preferred_element_type=F32)
            dgain_ref[:, seg] += jnp.sum(dvg * xh, axis=0, keepdims=True)
            dxh = dvg * gain
            dva = rs * (dxh - xh * jnp.mean(dxh * xh, axis=-1, keepdims=True))
            dv_ref[:, seg] = (dva * _dgelu(v)).astype(dv_ref.dtype)
        dbs_ref[...] += dbs

    return pl.pallas_call(
        body, name=name, grid=(S // A_CHUNK,),
        in_specs=[pl.BlockSpec((A_CHUNK, aw), lambda i: (i, 0)), pl.BlockSpec((A_CHUNK, aw), lambda i: (i, 1)),
                  pl.BlockSpec((A_CHUNK, aw), functools.partial(lambda i, cb: (i, cb), cb=dout_cb)),
                  pl.BlockSpec((1, aw), lambda i: (0, 0)), pl.BlockSpec((G, A_CHUNK, A_CHUNK), lambda i: (0, 0, 0)),
                  pl.BlockSpec((A_CHUNK, G), lambda i: (0, 0))],
        out_specs=[pl.BlockSpec((A_CHUNK, aw), lambda i: (i, 0)), pl.BlockSpec((A_CHUNK, aw), lambda i: (i, 0)),
                   pl.BlockSpec((G, A_CHUNK, A_CHUNK), lambda i: (0, 0, 0)), pl.BlockSpec((A_CHUNK, G), lambda i: (0, 0)),
                   pl.BlockSpec((1, aw), lambda i: (0, 0))],
        out_shape=[jax.ShapeDtypeStruct((S, aw), BF16), jax.ShapeDtypeStruct((S, aw), BF16),
                   jax.ShapeDtypeStruct((G, A_CHUNK, A_CHUNK), F32), jax.ShapeDtypeStruct((A_CHUNK, G), F32),
                   jax.ShapeDtypeStruct((1, aw), F32)],
        compiler_params=pltpu.CompilerParams(dimension_semantics=("arbitrary",), vmem_limit_bytes=VMEM_LIMIT),
    )(proj, proj, dout, vgain, ws, bs_t)


ANY = pl.BlockSpec(memory_space=pl.ANY)


def _place():
    return lax.axis_index("x"), lax.axis_index("y"), lax.axis_index("c")


def _other_chips(x, y):
    return [(1 - x, y), (x, 1 - y), (1 - x, 1 - y)]


def _remote(src, dst, send_sem, recv_sem, device):
    return pltpu.make_async_remote_copy(src_ref=src, dst_ref=dst, send_sem=send_sem, recv_sem=recv_sem,
                                        device_id=device, device_id_type=MESH)


def _comm_call(body, name, operands, out_shape, n_dma, n_local):
    return pl.pallas_call(
        body, name=name, in_specs=[ANY] * len(operands), out_specs=[ANY] * len(out_shape), out_shape=out_shape,
        scratch_shapes=[pltpu.SemaphoreType.DMA((n_dma,)), pltpu.SemaphoreType.DMA((n_dma,)),
                        pltpu.SemaphoreType.DMA((max(n_local, 1),))],
        compiler_params=pltpu.CompilerParams(has_side_effects=True),
    )(*operands)


def _all_gather(name, bigs, smalls):
    nb, n = len(bigs), len(bigs) + len(smalls)
    arrays = list(bigs) + list(smalls)

    def body(*refs):
        ins, outs = refs[:n], refs[n:2 * n]
        send_sems, recv_sems, local_sems = refs[2 * n:]
        x, y, c = _place()
        me = 2 * x + y
        chip_x, chip_y = (1 - x, y), (x, 1 - y)
        k_x, k_y, k_d = 2 * (1 - x) + y, 2 * x + (1 - y), 2 * (1 - x) + (1 - y)
        sibling = (x, y, 1 - c)
        started = []

        def go(src, dst, s, device):
            cp = _remote(src, dst, send_sems.at[s], recv_sems.at[s], device)
            cp.start()
            started.append(cp)

        def landed(ref, s):
            _remote(ref, ref, send_sems.at[s], recv_sems.at[s], sibling).wait_recv()

        for e in range(nb):
            half = ins[e].shape[0] // 2
            src = ins[e].at[pl.ds(c * half, half)]
            dst = outs[e].at[me, pl.ds(c * half, half)]
            go(src, dst, 8 * e, (*chip_x, c))
            go(src, dst, 8 * e + 1, (*chip_y, c))
        small = 8 * nb
        for e in range(nb, n):
            for j, chip in enumerate(_other_chips(x, y)):
                go(ins[e], outs[e].at[me], small + 3 * (e - nb) + j, (*chip, c))
        for e in range(nb):
            half = ins[e].shape[0] // 2
            q = half // 2
            from_x = outs[e].at[k_x, pl.ds(c * half, half)]
            landed(from_x, 8 * e)
            first = outs[e].at[k_x, pl.ds(c * half, q)]
            go(first, first, 8 * e + 2, (*chip_y, c))
            go(from_x, from_x, 8 * e + 4, sibling)
            from_y = outs[e].at[k_y, pl.ds(c * half, half)]
            landed(from_y, 8 * e + 1)
            second = outs[e].at[k_y, pl.ds(c * half + q, q)]
            go(second, second, 8 * e + 3, (*chip_x, c))
            go(from_y, from_y, 8 * e + 5, sibling)
        for e in range(nb):
            half = ins[e].shape[0] // 2
            q = half // 2
            first = outs[e].at[k_d, pl.ds(c * half, q)]
            landed(first, 8 * e + 2)
            go(first, first, 8 * e + 6, sibling)
            second = outs[e].at[k_d, pl.ds(c * half + q, q)]
            landed(second, 8 * e + 3)
            go(second, second, 8 * e + 7, sibling)
        for e in range(nb, n):
            for j, chip in enumerate(_other_chips(x, y)):
                landed(outs[e].at[2 * chip[0] + chip[1]], small + 3 * (e - nb) + j)
        for e in range(nb):
            half = ins[e].shape[0] // 2
            q = half // 2
            o = (1 - c) * half
            landed(outs[e].at[k_x, pl.ds(o, half)], 8 * e + 4)
            landed(outs[e].at[k_y, pl.ds(o, half)], 8 * e + 5)
            landed(outs[e].at[k_d, pl.ds(o, q)], 8 * e + 6)
            landed(outs[e].at[k_d, pl.ds(o + q, q)], 8 * e + 7)
        for cp in started:
            cp.wait_send()

    out_shape = [jax.ShapeDtypeStruct((N_CHIPS,) + a.shape, a.dtype) for a in arrays]
    gathered = _comm_call(body, name, arrays, out_shape, 8 * nb + 3 * len(smalls), 0)
    me = 2 * lax.axis_index("x") + lax.axis_index("y")
    return [lax.dynamic_update_slice(g, a[None], (me,) + (0,) * a.ndim) for g, a in zip(gathered, arrays)]


def _swap_halves(name, grads):
    n = len(grads)

    def body(*refs):
        ins, outs = refs[:n], refs[n:2 * n]
        send_sems, recv_sems, _ = refs[2 * n:]
        x, y, c = _place()
        cps = []
        for e in range(n):
            half = ins[e].shape[1] // 2
            src = ins[e].at[pl.ds(0, N_CHIPS), pl.ds((1 - c) * half, half)]
            cp = _remote(src, outs[e], send_sems.at[e], recv_sems.at[e], (x, y, 1 - c))
            cp.start()
            cps.append(cp)
        for cp in cps:
            cp.wait()

    out_shape = [jax.ShapeDtypeStruct((N_CHIPS, g.shape[1] // 2, g.shape[2]), g.dtype) for g in grads]
    return _comm_call(body, name, list(grads), out_shape, n, 0)


def _scatter_chips(name, parts):
    n = len(parts)

    def body(*refs):
        ins, outs = refs[:n], refs[n:2 * n]
        send_sems, recv_sems, local_sems = refs[2 * n:]
        x, y, c = _place()
        me = 2 * x + y
        cps = []
        for e in range(n):
            cp = pltpu.make_async_copy(ins[e].at[me], outs[e].at[3], local_sems.at[e])
            cp.start()
            cps.append(cp)
            for j, chip in enumerate(_other_chips(x, y)):
                cp = _remote(ins[e].at[2 * chip[0] + chip[1]], outs[e].at[j], send_sems.at[3 * e + j],
                             recv_sems.at[3 * e + j], (*chip, c))
                cp.start()
                cps.append(cp)
        for cp in cps:
            cp.wait()

    out_shape = [jax.ShapeDtypeStruct(p.shape, p.dtype) for p in parts]
    return _comm_call(body, name, list(parts), out_shape, 3 * n, n)


def _join_halves(name, halves, places, out_shapes):
    n, n_out = len(halves), len(out_shapes)

    def body(*refs):
        ins, outs = refs[:n], refs[n:n + n_out]
        send_sems, recv_sems, local_sems = refs[n + n_out:]
        x, y, c = _place()
        cps = []
        for e in range(n):
            o, idx = places[e]
            half = ins[e].shape[0]
            dst = outs[o].at[(*idx, pl.ds(c * half, half))]
            cp = _remote(ins[e], dst, send_sems.at[e], recv_sems.at[e], (x, y, 1 - c))
            cp.start()
            cps.append(cp)
        for e, cp in enumerate(cps):
            o, idx = places[e]
            half = ins[e].shape[0]
            landed = outs[o].at[(*idx, pl.ds((1 - c) * half, half))]
            cp.wait_send()
            _remote(ins[e], landed, send_sems.at[e], recv_sems.at[e], (x, y, 1 - c)).wait_recv()

    joined = list(_comm_call(body, name, list(halves), list(out_shapes), n, 0))
    c = lax.axis_index("c")
    for e, hv in enumerate(halves):
        o, idx = places[e]
        start = tuple(idx) + (c * hv.shape[0], 0)
        joined[o] = lax.dynamic_update_slice(joined[o], hv.reshape((1,) * len(idx) + hv.shape), start)
    return joined


def _spread_all(name, per_chip, everywhere):
    def body(pc_ref, ev_ref, pc_out, ev_out, send_sems, recv_sems, local_sems):
        x, y, c = _place()
        me = 4 * x + 2 * y + c
        cps = [pltpu.make_async_copy(pc_ref.at[2 * x + y], pc_out.at[me], local_sems.at[0]),
               pltpu.make_async_copy(ev_ref, ev_out.at[me], local_sems.at[1])]
        for f in range(1, 8):
            fx, fy, fc = f // 4, (f // 2) % 2, f % 2
            tx = 1 - x if fx else x
            ty = 1 - y if fy else y
            tc = 1 - c if fc else c
            cps.append(_remote(pc_ref.at[2 * tx + ty], pc_out.at[me], send_sems.at[2 * f], recv_sems.at[2 * f],
                               (tx, ty, tc)))
            cps.append(_remote(ev_ref, ev_out.at[me], send_sems.at[2 * f + 1], recv_sems.at[2 * f + 1],
                               (tx, ty, tc)))
        for cp in cps:
            cp.start()
        for cp in cps:
            cp.wait()

    out_shape = [jax.ShapeDtypeStruct((8,) + per_chip.shape[1:], F32), jax.ShapeDtypeStruct((8,) + everywhere.shape, F32)]
    return _comm_call(body, name, [per_chip, everywhere], out_shape, 16, 2)


def _row_tile(rows, width, itemsize, n_arrays):
    tm = 1
    while rows % (tm * 2) == 0 and (tm * 2) * width * itemsize * n_arrays * 2 <= 24 * 1024 * 1024 and tm * 2 <= 1024:
        tm *= 2
    return tm


def _add_own_half(name, grad, swapped, c_arr):
    ns, R, C = grad.shape
    half = R // 2
    th = _row_tile(half, C, 2, 3)
    g4 = grad.reshape(ns, 2, half, C)

    def body(c_ref, g_ref, s_ref, o_ref):
        o_ref[...] = (g_ref[...].astype(F32) + s_ref[...].astype(F32)).astype(o_ref.dtype)

    return pl.pallas_call(
        body, name=name,
        grid_spec=pltpu.PrefetchScalarGridSpec(
            num_scalar_prefetch=1, grid=(ns, half // th),
            in_specs=[pl.BlockSpec((None, None, th, C), lambda k, i, c_ref: (k, c_ref[0], i, 0)),
                      pl.BlockSpec((None, th, C), lambda k, i, c_ref: (k, i, 0))],
            out_specs=pl.BlockSpec((None, th, C), lambda k, i, c_ref: (k, i, 0))),
        out_shape=jax.ShapeDtypeStruct((ns, half, C), grad.dtype),
        compiler_params=pltpu.CompilerParams(dimension_semantics=("parallel", "parallel"), vmem_limit_bytes=VMEM_LIMIT),
    )(c_arr, g4, swapped)


def _sum_slots(name, slots, order, out_dtype=F32):
    n, rows, C = slots.shape
    th = _row_tile(rows, C, 4, n + 1)

    def body(s_ref, o_ref):
        acc = s_ref[order[0]].astype(F32)
        for k in order[1:]:
            acc = acc + s_ref[k].astype(F32)
        o_ref[...] = acc.astype(o_ref.dtype)

    return pl.pallas_call(
        body, name=name, grid=(rows // th,),
        in_specs=[pl.BlockSpec((n, th, C), lambda i: (0, i, 0))],
        out_specs=pl.BlockSpec((th, C), lambda i: (i, 0)),
        out_shape=jax.ShapeDtypeStruct((rows, C), out_dtype),
        compiler_params=pltpu.CompilerParams(dimension_semantics=("parallel",), vmem_limit_bytes=VMEM_LIMIT),
    )(slots)


def _adamw(name, g, w, m, v):
    rows, C = g.shape
    tm = _row_tile(rows, C, 4, 7)
    c1 = 1.0 - ADAM_B1 ** ADAM_STEP
    c2 = 1.0 - ADAM_B2 ** ADAM_STEP

    def fn(g, w, m, v):
        m = ADAM_B1 * m + (1.0 - ADAM_B1) * g
        v = ADAM_B2 * v + (1.0 - ADAM_B2) * (g * g)
        delta = -ADAM_LR * ((m / c1) / (jnp.sqrt(v / c2) + ADAM_EPS) + ADAM_WD * w)
        return delta, m, v

    return _rowwise(fn, name, rows, tm, [('row', g), ('row', w), ('row', m), ('row', v)],
                    [(C, F32), (C, F32), (C, F32)])


def _tm(S, width, n_arrays):
    return _row_tile(S, width, 4, n_arrays)


def _rms_fwd(name, h, gain, out_dtype):
    S, D = h.shape

    def fn(hb, g):
        r = lax.rsqrt(jnp.mean(hb * hb, axis=-1, keepdims=True) + RMS_EPS)
        return ((hb * r) * g,)

    return _rowwise(fn, name, S, _tm(S, D, 4), [('row', h), ('full', gain)], [(D, out_dtype)])[0]


def _rms_bwd(name, h_in, gain, dn, dh):
    S, D = h_in.shape

    def fn(hb, g, dnb, dhb):
        r = lax.rsqrt(jnp.mean(hb * hb, axis=-1, keepdims=True) + RMS_EPS)
        xh = hb * r
        dnb = dnb.astype(F32)
        dxh = dnb * g
        dx = r * (dxh - xh * jnp.mean(dxh * xh, axis=-1, keepdims=True))
        return dhb + dx, jnp.sum(dnb * xh, axis=0, keepdims=True)

    return _rowwise(fn, name, S, _tm(S, D, 8), [('row', h_in), ('full', gain), ('row', dn), ('row', dh)],
                    [(D, F32)], [(1, D)])


def _ffn_fwd(h, gain, wg, wu, wd):
    S, D = h.shape
    n = _rms_fwd("rms_fwd_bf16", h, gain, BF16)
    g = _mm_nn(n, wg, "ffn_up", BF16)
    u = _mm_nn(n, wu, "ffn_up", BF16)
    FF = g.shape[1]

    def fn(gb, ub):
        return (_silu(gb.astype(F32)) * ub.astype(F32),)

    act = _rowwise(fn, "ffn_act", S, _tm(S, FF, 6), [('row', g), ('row', u)], [(FF, BF16)])[0]
    return _mm_nn(act, wd, "ffn_down", F32, res=h, scale=0.5), (h, n, g, u)


def _ffn_bwd(dh, cache, gain, wg, wu, wd):
    h, n, g, u = cache
    S, FF = g.shape
    dhb = dh.astype(BF16)
    dact = _mm_nt(dhb, wd, "ffn_dact", BF16, scale=0.5)

    def fn(gb, ub, db):
        gb, ub, db = gb.astype(F32), ub.astype(F32), db.astype(F32)
        sg = _sigmoid(gb)
        sl = gb * sg
        return db * ub * (sg * (1.0 + gb * (1.0 - sg))), db * sl, sl * ub

    dg, du, act = _rowwise(fn, "ffn_act_bwd", S, _tm(S, FF, 10), [('row', g), ('row', u), ('row', dact)],
                           [(FF, BF16), (FF, BF16), (FF, BF16)])
    dwd = _mm_tn(act, dhb, wd, "ffn_dwd", BF16, scale=0.5)
    dwg = _mm_tn(n, dg, wg, "ffn_dwup", BF16)
    dwu = _mm_tn(n, du, wu, "ffn_dwup", BF16)
    dn = _mm_nt(dg, wg, "ffn_dn", F32)
    dn = _mm_nt(du, wu, "ffn_dn_acc", F32, res=dn)
    dh2, dgain = _rms_bwd("rms_bwd", h, gain, dn, dh)
    return dh2, dgain, dwg, dwu, dwd


def _ple_fwd(h, gain, pb, wgate, wproj):
    S, D = h.shape
    n = _rms_fwd("rms_fwd_bf16", h, gain, BF16)
    pre = _mm_nn(n, wgate, "ple_gate", F32)
    e = _mm_nn(pb, wproj, "ple_proj", F32)

    def fn(hb, pr, eb):
        return (hb + _sigmoid(pr) * eb,)

    h2 = _rowwise(fn, "ple_add", S, _tm(S, D, 6), [('row', h), ('row', pre), ('row', e)], [(D, F32)])[0]
    return h2, (h, n, pre, e)


def _ple_bwd(dh, cache, gain, pb, wgate, wproj):
    h, n, pre, e = cache
    S, D = h.shape

    def fn(db, pr, eb):
        gt = _sigmoid(pr)
        return db * eb * gt * (1.0 - gt), db * gt

    dpre, de = _rowwise(fn, "ple_bwd", S, _tm(S, D, 6), [('row', dh), ('row', pre), ('row', e)],
                        [(D, BF16), (D, BF16)])
    dwproj = _mm_tn(pb, de, wproj, "ple_dwproj", BF16)
    dwgate = _mm_tn(n, dpre, wgate, "ple_dwgate", BF16)
    dn = _mm_nt(dpre, wgate, "ple_dn", F32)
    dh2, dgain = _rms_bwd("rms_bwd", h, gain, dn, dh)
    return dh2, dgain, dwgate, dwproj


def _even_fwd(h, gain, w_in, w_out, vgain, ws, bs_t, onorm, lb):
    S, D = h.shape
    aw = D // 2
    nh = aw // B_HEAD
    hn = _rms_fwd("rms_fwd_bf16", h, gain, BF16)
    proj = _mm_nn(hn, w_in, "even_in", F32)
    a_out = _gmlp_fwd("gmlp_fwd", proj, aw, vgain, ws, bs_t)

    def pre(bq, bf, lbv):
        f = lbv + (1.0 - lbv) * _sigmoid(bf)
        return _silu(bq), jnp.maximum(f, B_MIN_F), 1.0 - f

    q, w, k = _rowwise(pre, "hgrn_pre", S, _tm(S, aw, 8), [('col', proj, aw, 2), ('col', proj, aw, 3), ('full', lb)],
                       [(aw, F32), (aw, F32), (aw, F32)])
    wk, kk, qk = _to_k(w, nh, B_HEAD), _to_k(k, nh, B_HEAD), _to_k(q, nh, B_HEAD)
    vrow = _to_row(proj[:, 4 * aw:5 * aw], nh, B_HEAD)
    ynat, hist, _ = _scan_fwd("hgrn_scan_fwd", nh, wk, kk, qk, vrow)
    o = ynat.reshape(S, aw)

    def post(ob, bg, on):
        rs = lax.rsqrt(_segsum(ob * ob, B_HEAD) * (1.0 / B_HEAD) + RMS_EPS)
        return ((ob * rs * on) * _silu(bg),)

    b_out = _rowwise(post, "hgrn_post", S, _tm(S, aw, 8), [('row', o), ('col', proj, aw, 5), ('full', onorm)],
                     [(aw, BF16)])[0]
    cat = jnp.concatenate([a_out, b_out], axis=1)
    h2 = _mm_nn(cat, w_out, "even_out", F32, res=h)
    return h2, (h, hn, proj, wk, kk, qk, vrow, hist, o, cat)


def _even_bwd(dh, cache, gain, w_in, w_out, vgain, ws, bs_t, onorm, lb):
    h, hn, proj, wk, kk, qk, vrow, hist, o, cat = cache
    S, D = h.shape
    aw = D // 2
    nh = aw // B_HEAD
    dhb = dh.astype(BF16)
    dw_out = _mm_tn(cat, dhb, w_out, "even_dwout", BF16)
    dcat = _mm_nt(dhb, w_out, "even_dcat", F32)

    def post_bwd(ob, bg, on, db):
        rs = lax.rsqrt(_segsum(ob * ob, B_HEAD) * (1.0 / B_HEAD) + RMS_EPS)
        xh = ob * rs
        dy = db * _silu(bg)
        dbg = db * (xh * on) * _dsilu(bg)
        dxh = dy * on
        do = rs * (dxh - xh * (_segsum(dxh * xh, B_HEAD) * (1.0 / B_HEAD)))
        return do, dbg, jnp.sum(dy * xh, axis=0, keepdims=True)

    do, dbg, donorm = _rowwise(post_bwd, "hgrn_post_bwd", S, _tm(S, aw, 10),
                               [('row', o), ('col', proj, aw, 5), ('full', onorm), ('col', dcat, aw, 1)],
                               [(aw, F32), (aw, BF16)], [(1, aw)])
    dwk, dkk, dqk, dvnat = _scan_bwd("hgrn_scan_bwd", nh, wk, kk, qk, vrow, hist, _to_row(do, nh, B_HEAD))
    dq, dw, dk = _from_k(dqk, nh, B_HEAD), _from_k(dwk, nh, B_HEAD), _from_k(dkk, nh, B_HEAD)
    dbi = dvnat.reshape(S, aw).astype(BF16)

    def pre_bwd(bq, bf, lbv, dqb, dwb, dkb):
        sig = _sigmoid(bf)
        f = lbv + (1.0 - lbv) * sig
        df = jnp.where(f > B_MIN_F, dwb, 0.0) - dkb
        return dqb * _dsilu(bq), df * (1.0 - lbv) * sig * (1.0 - sig), jnp.sum(df * (1.0 - sig), axis=0, keepdims=True)

    dbq, dbf, dlb = _rowwise(pre_bwd, "hgrn_pre_bwd", S, _tm(S, aw, 12),
                             [('col', proj, aw, 2), ('col', proj, aw, 3), ('full', lb), ('row', dq), ('row', dw),
                              ('row', dk)], [(aw, BF16), (aw, BF16)], [(1, aw)])
    dau, dav, dws, dbs_t, dvgain = _gmlp_bwd("gmlp_bwd", proj, dcat, 0, aw, vgain, ws, bs_t)
    dproj = jnp.concatenate([dau, dav, dbq, dbf, dbi, dbg], axis=1)
    dw_in = _mm_tn(hn, dproj, w_in, "even_dwin", BF16)
    dn = _mm_nt(dproj, w_in, "even_dn", F32)
    dh2, dgain = _rms_bwd("rms_bwd", h, gain, dn, dh)
    return dh2, dict(gain=dgain, w_in=dw_in, w_out=dw_out, vgain=dvgain, ws=dws, bs_t=dbs_t, onorm=donorm, lb=dlb)


def _rwkv_prep(r, k, v0, wpl, apl, w0, a0, kkg, kag, svl=None, vf=None, v0p=None):
    wp = w0 + wpl
    w = -_softplus(-wp) - 0.5
    ew = jnp.exp(w)
    decay = jnp.exp(-ew)
    a = _sigmoid(a0 + apl)
    if svl is not None:
        sv = _sigmoid(v0p + svl)
        v = v0 + (vf - v0) * sv
    else:
        sv, v = None, v0
    kkp = k * kkg
    nrm = jnp.sqrt(_segsum(kkp * kkp, C_HEAD))
    inv = 1.0 / jnp.maximum(nrm, 1e-12)
    kk = kkp * inv
    k2 = k * (1.0 + (a - 1.0) * kag)
    return dict(wp=wp, ew=ew, decay=decay, a=a, sv=sv, v=v, kkp=kkp, nrm=nrm, inv=inv, kk=kk, k2=k2)


def _rwkv_post(y, r, k2, v, gn_g, gn_b, rk):
    mu = _segsum(y, C_HEAD) * (1.0 / C_HEAD)
    yc = y - mu
    rstd = lax.rsqrt(_segsum(yc * yc, C_HEAD) * (1.0 / C_HEAD) + C_GN_EPS)
    yh = yc * rstd
    s = _segsum(r * k2 * rk, C_HEAD)
    return yh, rstd, s, yh * gn_g + gn_b + s * v


def _rwkv_fwd(h, gain, P, vfirst):
    S, D = h.shape
    nh = D // C_HEAD
    vres = vfirst is not None
    hn = _rms_fwd("rms_fwd_f32", h, gain, F32)
    hs = jnp.concatenate([jnp.zeros((1, D), F32), hn[:-1]], axis=0)

    def mixf(x, xs, m0, m1, m2, m3, m4, m5):
        xx = xs - x
        return tuple(x + xx * m for m in (m0, m1, m2, m3, m4, m5))

    xr, xw, xk, xv, xa, xg = _rowwise(mixf, "rwkv_mix", S, _tm(S, D, 8),
                                      [('row', hn), ('row', hs)] + [('full', m) for m in P['mix']], [(D, BF16)] * 6)
    r = _mm_nn(xr, P['wr'], "rwkv_proj", F32)
    k = _mm_nn(xk, P['wk'], "rwkv_proj", F32)
    v0 = _mm_nn(xv, P['wv'], "rwkv_proj", F32)
    lw1 = _mm_nn(xw, P['w1'], "rwkv_lora_in", F32)
    la1 = _mm_nn(xa, P['a1'], "rwkv_lora_in", F32)
    lg1 = _mm_nn(xg, P['g1'], "rwkv_lora_in_g", F32)
    ins = [('row', lw1), ('row', la1), ('row', lg1)]
    outs = [(lw1.shape[1], BF16), (la1.shape[1], BF16), (lg1.shape[1], BF16)]
    if vres:
        lv1 = _mm_nn(xv, P['v1'], "rwkv_lora_in_v", F32)
        ins.append(('row', lv1))
        outs.append((lv1.shape[1], BF16))

    def lora_act(*xs):
        res = [jnp.tanh(xs[0]), xs[1], _sigmoid(xs[2])]
        return tuple(res + list(xs[3:]))

    acts = _rowwise(lora_act, "rwkv_lora_act", S, _tm(S, 1024, 4), ins, outs)
    tw, la1b, sg = acts[0], acts[1], acts[2]
    wpl = _mm_nn(tw, P['w2'], "rwkv_lora_out", F32)
    apl = _mm_nn(la1b, P['a2'], "rwkv_lora_out", F32)
    g = _mm_nn(sg, P['g2'], "rwkv_lora_out_g", F32)
    prep_ins = [('row', r), ('row', k), ('row', v0), ('row', wpl), ('row', apl),
                ('full', P['w0']), ('full', P['a0']), ('full', P['kk']), ('full', P['ka'])]
    svl = lv1b = None
    if vres:
        lv1b = acts[3]
        svl = _mm_nn(lv1b, P['v2'], "rwkv_lora_out_v", F32)
        prep_ins += [('row', svl), ('row', vfirst), ('full', P['v0'])]

    def prep(*xs):
        q = _rwkv_prep(*xs)
        return q['decay'], q['k2'], q['v'], -q['kk'], q['kk'] * q['a']

    decay, k2, v, av, bv = _rowwise(prep, "rwkv_prep", S, _tm(S, D, 24), prep_ins, [(D, F32)] * 5)
    tk = functools.partial(_to_k, n_heads=nh, dk=C_HEAD)
    wk_, kk_, rk_, ak_, bk_ = tk(decay), tk(k2), tk(r), tk(av), tk(bv)
    vrow = _to_row(v, nh, C_HEAD)
    ynat, hist, sarow = _scan_fwd("rwkv_scan_fwd", nh, wk_, kk_, rk_, vrow, ak_, bk_)
    y = ynat.reshape(S, D)

    def post(yb, rb, k2b, vb, gb, gn_g, gn_b, rkf):
        return (_rwkv_post(yb, rb, k2b, vb, gn_g, gn_b, rkf)[3] * gb,)

    zg = _rowwise(post, "rwkv_post", S, _tm(S, D, 16),
                  [('row', y), ('row', r), ('row', k2), ('row', v), ('row', g),
                   ('full', P['gn_g']), ('full', P['gn_b']), ('full', P['rk'])], [(D, BF16)])[0]
    h2 = _mm_nn(zg, P['wo'], "rwkv_out", F32, res=h)
    cache = dict(h=h, hn=hn, hs=hs, x=(xr, xw, xk, xv, xa, xg), r=r, k=k, v0=v0, tw=tw, la1b=la1b, sg=sg, lv1b=lv1b,
                 wpl=wpl, apl=apl, svl=svl, g=g, k2=k2, v=v, scan=(wk_, kk_, rk_, vrow, ak_, bk_, hist, sarow), y=y,
                 zg=zg, vfirst=vfirst)
    return h2, cache, (v if not vres else vfirst)


def _rwkv_bwd(dh, cache, gain, P, dvfirst_in):
    c = cache
    h = c['h']
    S, D = h.shape
    nh = D // C_HEAD
    vres = c['vfirst'] is not None
    xr, xw, xk, xv, xa, xg = c['x']
    dhb = dh.astype(BF16)
    dwo = _mm_tn(c['zg'], dhb, P['wo'], "rwkv_dwo", BF16)
    dzg = _mm_nt(dhb, P['wo'], "rwkv_dzg", F32)

    def post_bwd(dzgb, yb, rb, k2b, vb, gb, gn_g, gn_b, rkf):
        yh, rstd, s, z = _rwkv_post(yb, rb, k2b, vb, gn_g, gn_b, rkf)
        dz = dzgb * gb
        dyh = dz * gn_g
        m1 = _segsum(dyh, C_HEAD) * (1.0 / C_HEAD)
        m2 = _segsum(dyh * yh, C_HEAD) * (1.0 / C_HEAD)
        dy = rstd * (dyh - m1 - yh * m2)
        ds = _segsum(dz * vb, C_HEAD)
        return (dy, dzgb * z, ds * k2b * rkf, ds * rb * rkf, dz * s,
                jnp.sum(dz * yh, axis=0, keepdims=True), jnp.sum(dz, axis=0, keepdims=True),
                jnp.sum(ds * rb * k2b, axis=0, keepdims=True))

    dy, dgb, dr_b, dk2_b, dv_b, dgn_g, dgn_b, drk = _rowwise(
        post_bwd, "rwkv_post_bwd", S, _tm(S, D, 28),
        [('row', dzg), ('row', c['y']), ('row', c['r']), ('row', c['k2']), ('row', c['v']), ('row', c['g']),
         ('full', P['gn_g']), ('full', P['gn_b']), ('full', P['rk'])],
        [(D, F32), (D, BF16), (D, F32), (D, F32), (D, F32)], [(1, D)] * 3)
    wk_, kk_, rk_, vrow, ak_, bk_, hist, sarow = c['scan']
    dwk, dkk, drk_s, dvnat, dak, dbk = _scan_bwd("rwkv_scan_bwd", nh, wk_, kk_, rk_, vrow, hist,
                                                 _to_row(dy, nh, C_HEAD), ak_, bk_, sarow)
    fk = functools.partial(_from_k, n_heads=nh, dk=C_HEAD)
    ddecay, dk2_s, dr_s, dA, dB = fk(dwk), fk(dkk), fk(drk_s), fk(dak), fk(dbk)
    dv_s = dvnat.reshape(S, D)
    dr_t, dk2_t, dv_t = dr_s + dr_b, dk2_s + dk2_b, dv_s + dv_b
    if dvfirst_in is not None:
        dv_t = dv_t + dvfirst_in
    ins = [('row', c['r']), ('row', c['k']), ('row', c['v0']), ('row', c['wpl']), ('row', c['apl']),
           ('full', P['w0']), ('full', P['a0']), ('full', P['kk']), ('full', P['ka'])]
    if vres:
        ins += [('row', c['svl']), ('row', c['vfirst']), ('full', P['v0'])]
    n_fwd = len(ins)
    ins += [('row', t) for t in (dr_t, ddecay, dk2_t, dv_t, dA, dB)]

    def prep_bwd(*xs):
        q = _rwkv_prep(*xs[:n_fwd])
        kb, kkg, kag = xs[1], xs[7], xs[8]
        dr, ddec, dk2, dv, dav, dbv = xs[n_fwd:]
        a, kk, kkp, inv = q['a'], q['kk'], q['kkp'], q['inv']
        dkk = dbv * a - dav
        da = dbv * kk + dk2 * kb * kag
        dk = dk2 * (1.0 + (a - 1.0) * kag)
        pr = _segsum(dkk * kkp, C_HEAD)
        dkkp = dkk * inv - jnp.where(q['nrm'] > 1e-12, kkp * pr * inv * inv * inv, 0.0)
        dk = dk + dkkp * kkg
        dap = da * a * (1.0 - a)
        dwp = ddec * q['decay'] * (-q['ew']) * _sigmoid(-q['wp'])
        outs = [dr, dk]
        accs = [jnp.sum(dwp, axis=0, keepdims=True), jnp.sum(dap, axis=0, keepdims=True),
                jnp.sum(dkkp * kb, axis=0, keepdims=True), jnp.sum(dk2 * kb * (a - 1.0), axis=0, keepdims=True)]
        if vres:
            v0b, vfb, sv = xs[2], xs[10], q['sv']
            dsvp = dv * (vfb - v0b) * sv * (1.0 - sv)
            outs += [dv * (1.0 - sv), dwp, dap, dsvp, dv * sv]
            accs.append(jnp.sum(dsvp, axis=0, keepdims=True))
        else:
            outs += [dv, dwp, dap]
        return tuple(outs + accs)

    outs = [(D, BF16)] * 5 + ([(D, BF16), (D, F32)] if vres else [])
    res = _rowwise(prep_bwd, "rwkv_prep_bwd", S, _tm(S, D, 40), ins, outs, [(1, D)] * (5 if vres else 4))
    drb, dkb, dv0b, dwpb, dapb = res[:5]
    dvfirst_out = res[6] if vres else None
    accs = res[len(outs):]
    G = dict(wo=dwo, w0=accs[0], a0=accs[1], kk=accs[2], ka=accs[3], gn_g=dgn_g, gn_b=dgn_b, rk=drk)
    G['w2'] = _mm_tn(c['tw'], dwpb, P['w2'], "rwkv_dlora_out", BF16)
    G['a2'] = _mm_tn(c['la1b'], dapb, P['a2'], "rwkv_dlora_out", BF16)
    G['g2'] = _mm_tn(c['sg'], dgb, P['g2'], "rwkv_dlora_out_g", BF16)
    dtw = _mm_nt(dwpb, P['w2'], "rwkv_dlora_mid", F32)
    dla1 = _mm_nt(dapb, P['a2'], "rwkv_dlora_mid", F32)
    dsg = _mm_nt(dgb, P['g2'], "rwkv_dlora_mid_g", F32)
    ins = [('row', dtw), ('row', c['tw']), ('row', dla1), ('row', dsg), ('row', c['sg'])]
    outs = [(dtw.shape[1], BF16), (dla1.shape[1], BF16), (dsg.shape[1], BF16)]
    if vres:
        dsvpb = res[5]
        G['v0'] = accs[4]
        G['v2'] = _mm_tn(c['lv1b'], dsvpb, P['v2'], "rwkv_dlora_out_v", BF16)
        dlv1 = _mm_nt(dsvpb, P['v2'], "rwkv_dlora_mid_v", F32)
        ins.append(('row', dlv1))
        outs.append((dlv1.shape[1], BF16))

    def lora_act_bwd(dtwb, twb, dla1b_, dsgb, sgb, *rest):
        twb, sgb = twb.astype(F32), sgb.astype(F32)
        return tuple([dtwb * (1.0 - twb * twb), dla1b_, dsgb * sgb * (1.0 - sgb)] + list(rest))

    acts = _rowwise(lora_act_bwd, "rwkv_lora_act_bwd", S, _tm(S, 1024, 6), ins, outs)
    dlw1b, dla1b, dlg1b = acts[0], acts[1], acts[2]
    G['w1'] = _mm_tn(xw, dlw1b, P['w1'], "rwkv_dlora_in", BF16)
    G['a1'] = _mm_tn(xa, dla1b, P['a1'], "rwkv_dlora_in", BF16)
    G['g1'] = _mm_tn(xg, dlg1b, P['g1'], "rwkv_dlora_in_g", BF16)
    G['wr'] = _mm_tn(xr, drb, P['wr'], "rwkv_dwproj", BF16)
    G['wk'] = _mm_tn(xk, dkb, P['wk'], "rwkv_dwproj", BF16)
    G['wv'] = _mm_tn(xv, dv0b, P['wv'], "rwkv_dwproj", BF16)
    dxw = _mm_nt(dlw1b, P['w1'], "rwkv_dx_lora", F32)
    dxa = _mm_nt(dla1b, P['a1'], "rwkv_dx_lora", F32)
    dxg = _mm_nt(dlg1b, P['g1'], "rwkv_dx_lora_g", F32)
    dxr = _mm_nt(drb, P['wr'], "rwkv_dx", F32)
    dxk = _mm_nt(dkb, P['wk'], "rwkv_dx", F32)
    dxv = _mm_nt(dv0b, P['wv'], "rwkv_dx", F32)
    if vres:
        G['v1'] = _mm_tn(xv, acts[3], P['v1'], "rwkv_dlora_in_v", BF16)
        dxv = _mm_nt(acts[3], P['v1'], "rwkv_dx_lora_v", F32, res=dxv)

    def mix_bwd(x, xs, m0, m1, m2, m3, m4, m5, d0, d1, d2, d3, d4, d5):
        xx = xs - x
        ds_ = (d0, d1, d2, d3, d4, d5)
        dxx = d0 * m0 + d1 * m1 + d2 * m2 + d3 * m3 + d4 * m4 + d5 * m5
        dsum = d0 + d1 + d2 + d3 + d4 + d5
        return tuple([dsum - dxx, dxx] + [jnp.sum(d * xx, axis=0, keepdims=True) for d in ds_])

    res = _rowwise(mix_bwd, "rwkv_mix_bwd", S, _tm(S, D, 24),
                   [('row', c['hn']), ('row', c['hs'])] + [('full', m) for m in P['mix']]
                   + [('row', d) for d in (dxr, dxw, dxk, dxv, dxa, dxg)], [(D, F32), (D, F32)], [(1, D)] * 6)
    dx_here, dxs = res[0], res[1]
    G['mix'] = res[2:]
    dhn = dx_here + jnp.concatenate([dxs[1:], jnp.zeros((1, D), F32)], axis=0)
    dh2, G['gain'] = _rms_bwd("rms_bwd", h, gain, dhn, dh)
    return dh2, G, dvfirst_out


def _loss_bwd(h, target, gain):
    S, D = h.shape

    def fn(hb, tb, g):
        r = lax.rsqrt(jnp.mean(hb * hb, axis=-1, keepdims=True) + RMS_EPS)
        xh = hb * r
        e = xh * g - tb
        dy = e * (1.0 / D)
        dxh = dy * g
        dx = r * (dxh - xh * jnp.mean(dxh * xh, axis=-1, keepdims=True))
        part = jnp.sum(jnp.sum(e * e, axis=-1, keepdims=True), axis=0, keepdims=True) * (0.5 / D)
        return dx, jnp.sum(dy * xh, axis=0, keepdims=True), jnp.broadcast_to(part, (1, LANES))

    dh, dgain, part = _rowwise(fn, "loss", S, _tm(S, D, 8), [('row', h), ('row', target), ('full', gain)],
                               [(D, F32)], [(1, D), (1, LANES)])
    return part[0, 0], dh, dgain


WEIGHTS = ['norms', 'final_norm', 'ffn_wg', 'ffn_wu', 'ffn_wd', 'ple_wp', 'ple_wg', 'e_w_in', 'e_w_out', 'a_vnorm',
           'a_ws', 'a_bs', 'b_onorm', 'b_lb_logits', 'c_mix', 'c_wr', 'c_wk', 'c_wv', 'c_wo', 'c_w0', 'c_w1', 'c_w2',
           'c_a0', 'c_a1', 'c_a2', 'c_g1', 'c_g2', 'c_kk', 'c_ka', 'c_rk', 'c_gn_g', 'c_gn_b', 'c_v0', 'c_v1', 'c_v2']
BIG = {'ffn_wg': 'col', 'ffn_wu': 'col', 'ffn_wd': 'row', 'ple_wg': 'row', 'e_w_in': 'col', 'e_w_out': 'row',
       'c_wr': 'row', 'c_wk': 'row', 'c_wv': 'row', 'c_wo': 'row', 'ple_wp': 'col', 'c_w1': 'row', 'c_w2': 'col',
       'c_a1': 'row', 'c_a2': 'col', 'c_g1': 'row', 'c_g2': 'col', 'c_v1': 'row', 'c_v2': 'col'}
SMALL = ['norms', 'c_mix', 'c_w0', 'c_a0', 'c_kk', 'c_ka', 'c_gn_g', 'c_gn_b', 'c_v0']
REP = ['final_norm', 'a_vnorm', 'a_ws', 'a_bs', 'b_onorm', 'b_lb_logits', 'c_rk']
PACK_QUANTUM = 1024 * LANES


def _pack(arrays, lead=0):
    lead_shape = arrays[0].shape[:lead]
    flat = jnp.concatenate([a.reshape(lead_shape + (-1,)).astype(F32) for a in arrays], axis=-1)
    pad = (-flat.shape[-1]) % PACK_QUANTUM
    if pad:
        flat = jnp.concatenate([flat, jnp.zeros(lead_shape + (pad,), F32)], axis=-1)
    return flat.reshape(lead_shape + (-1, LANES))


def _unpack(packed, names, shapes, lead=0):
    lead_shape = packed.shape[:lead]
    flat = packed.reshape(lead_shape + (-1,))
    out, off = {}, 0
    for n, s in zip(names, shapes):
        size = 1
        for d in s:
            size *= d
        out[n] = flat[..., off:off + size].reshape(lead_shape + tuple(s))
        off += size
    return out


def _full_vec(g):
    return jnp.moveaxis(g, 0, -2).reshape(g.shape[1:-1] + (N_CHIPS * g.shape[-1],))


def _vec_shards(g):
    return jnp.moveaxis(g.reshape(g.shape[:-1] + (N_CHIPS, g.shape[-1] // N_CHIPS)), -2, 0)


def _lower_bounds(logits):
    probs = jax.nn.softmax(logits.astype(F32), axis=0)
    return jnp.cumsum(probs, axis=0) - probs[0]


def kernel(x, p, norms, final_norm, ffn_wg, ffn_wu, ffn_wd, ple_wp, ple_wg, e_w_in, e_w_out, a_vnorm, a_ws, a_bs, b_onorm, b_lb_logits, c_mix, c_wr, c_wk, c_wv, c_wo, c_w0, c_w1, c_w2, c_a0, c_a1, c_a2, c_g1, c_g2, c_kk, c_ka, c_rk, c_gn_g, c_gn_b, c_v0, c_v1, c_v2, loss_target, m_norms, m_final_norm, m_ffn_wg, m_ffn_wu, m_ffn_wd, m_ple_wp, m_ple_wg, m_e_w_in, m_e_w_out, m_a_vnorm, m_a_ws, m_a_bs, m_b_onorm, m_b_lb_logits, m_c_mix, m_c_wr, m_c_wk, m_c_wv, m_c_wo, m_c_w0, m_c_w1, m_c_w2, m_c_a0, m_c_a1, m_c_a2, m_c_g1, m_c_g2, m_c_kk, m_c_ka, m_c_rk, m_c_gn_g, m_c_gn_b, m_c_v0, m_c_v1, m_c_v2, v_norms, v_final_norm, v_ffn_wg, v_ffn_wu, v_ffn_wd, v_ple_wp, v_ple_wg, v_e_w_in, v_e_w_out, v_a_vnorm, v_a_ws, v_a_bs, v_b_onorm, v_b_lb_logits, v_c_mix, v_c_wr, v_c_wk, v_c_wv, v_c_wo, v_c_w0, v_c_w1, v_c_w2, v_c_a0, v_c_a1, v_c_a2, v_c_g1, v_c_g2, v_c_kk, v_c_ka, v_c_rk, v_c_gn_g, v_c_gn_b, v_c_v0, v_c_v1, v_c_v2):
    A = dict(locals())
    assert x.shape[0] == 1, "one example per device"

    big_names = list(BIG)
    gathered = _all_gather("gather_weights", [A[n].astype(BF16).reshape(-1, A[n].shape[-1]) for n in big_names],
                           [_pack([A[n] for n in SMALL])])
    GB = {n: gathered[i].reshape((N_CHIPS,) + A[n].shape) for i, n in enumerate(big_names)}
    GS = _unpack(gathered[-1], SMALL, [A[n].shape for n in SMALL], lead=1)

    part, grad_x, big_entries, sm_pack, rep_pack = _local_step(A, GB, GS)
    loss = lax.psum(part, ("x", "y", "c"))

    c_arr = lax.axis_index("c").astype(jnp.int32).reshape(1)
    grads = [e[2] for e in big_entries]
    swapped = _swap_halves("reduce_swap_cores", grads)
    parts = [_add_own_half("reduce_add_cores", g, s, c_arr) for g, s in zip(grads, swapped)]
    slots = _scatter_chips("reduce_scatter_chips", parts)
    halves = [_sum_slots("reduce_sum_chips", s, (3, 0, 1, 2)) for s in slots]
    places = [(big_names.index(n), idx) for n, idx, _ in big_entries]
    big_grads = _join_halves("reduce_join_cores", halves, places,
                             [jax.ShapeDtypeStruct(A[n].shape, F32) for n in big_names])
    sm_slots, rep_slots = _spread_all("reduce_small", sm_pack, rep_pack)
    sm_grad = _sum_slots("reduce_sum_small", sm_slots, tuple(range(8)))
    rep_grad = _sum_slots("reduce_sum_small", rep_slots, tuple(range(8)))

    outs = {}
    for o, n in enumerate(big_names):
        shp, C = A[n].shape, A[n].shape[-1]
        g = big_grads[o]
        d, nm, nv = _adamw("adamw", g.reshape(-1, C), A[n].reshape(-1, C), A['m_' + n].reshape(-1, C),
                           A['v_' + n].reshape(-1, C))
        outs[n] = (g, d.reshape(shp), nm.reshape(shp), nv.reshape(shp))
    for names, g in ((SMALL, sm_grad), (REP, rep_grad)):
        shapes = [A[n].shape for n in names]
        res = _adamw("adamw_packed", g, _pack([A[n] for n in names]), _pack([A['m_' + n] for n in names]),
                     _pack([A['v_' + n] for n in names]))
        un = [_unpack(t, names, shapes) for t in (g,) + tuple(res)]
        for n in names:
            outs[n] = tuple(u[n] for u in un)
    return (loss, grad_x, *[outs[n][0] for n in WEIGHTS], *[outs[n][1] for n in WEIGHTS],
            *[outs[n][2] for n in WEIGHTS], *[outs[n][3] for n in WEIGHTS])


def _local_step(A, GB, GS):
    x, p, a_vnorm, a_ws, a_bs, b_onorm, b_lb_logits, c_rk = (A[n] for n in (
        'x', 'p', 'a_vnorm', 'a_ws', 'a_bs', 'b_onorm', 'b_lb_logits', 'c_rk'))
    S, D = x.shape[1], x.shape[2]
    depth = A['ffn_wg'].shape[0]
    aw = D // 2
    h = x[0]
    target = A['loss_target'][0]
    final_norm = A['final_norm']
    pb = p[:, 0].astype(BF16)

    def bigv(n, *idx):
        return WV(GB[n], idx, BIG[n])

    smv = bigv

    def row(v):
        return v.reshape(1, -1)

    vecs = {n: _full_vec(GS[n]) for n in SMALL}
    lb_all, lb_vjp = jax.vjp(_lower_bounds, b_lb_logits)

    def rwkv_params(j):
        P = dict(mix=[vecs['c_mix'][j, q:q + 1] for q in range(6)],
                 wr=bigv('c_wr', j), wk=bigv('c_wk', j), wv=bigv('c_wv', j), wo=bigv('c_wo', j),
                 w1=smv('c_w1', j), w2=smv('c_w2', j), a1=smv('c_a1', j), a2=smv('c_a2', j),
                 g1=smv('c_g1', j), g2=smv('c_g2', j),
                 w0=row(vecs['c_w0'][j]), a0=row(vecs['c_a0'][j]), kk=row(vecs['c_kk'][j]), ka=row(vecs['c_ka'][j]),
                 gn_g=row(vecs['c_gn_g'][j]), gn_b=row(vecs['c_gn_b'][j]), rk=c_rk[j].reshape(1, D))
        if j > 0:
            P.update(v0=row(vecs['c_v0'][j - 1]), v1=smv('c_v1', j - 1), v2=smv('c_v2', j - 1))
        return P

    def even_params(i):
        j = i // 2
        return (bigv('e_w_in', j), bigv('e_w_out', j), a_vnorm[j:j + 1], a_ws[j], a_bs[j].T, b_onorm[j:j + 1],
                lb_all[i:i + 1])

    def gain(i, q):
        return row(vecs['norms'][i, q])

    def ffn_views(i, q):
        return bigv('ffn_wg', i, q), bigv('ffn_wu', i, q), bigv('ffn_wd', i, q)

    caches, vfirst = [], None
    for i in range(depth):
        c = {}
        h, c['f1'] = _ffn_fwd(h, gain(i, 0), *ffn_views(i, 0))
        if i % 2 == 0:
            h, c['mix'] = _even_fwd(h, gain(i, 1), *even_params(i))
        else:
            h, c['mix'], vfirst = _rwkv_fwd(h, gain(i, 1), rwkv_params(i // 2), vfirst if i // 2 > 0 else None)
        h, c['f2'] = _ffn_fwd(h, gain(i, 2), *ffn_views(i, 1))
        h, c['ple'] = _ple_fwd(h, gain(i, 3), pb[i], bigv('ple_wg', i), smv('ple_wp', i))
        caches.append(c)

    part, dh, dfinal = _loss_bwd(h, target, final_norm.reshape(1, D))
    big_entries = []
    sm = {n: {} for n in SMALL}
    rep = {n: {} for n in REP}
    dvfirst = None
    for i in reversed(range(depth)):
        j, c = i // 2, caches[i]
        dh, dg, dwgate, dwproj = _ple_bwd(dh, c['ple'], gain(i, 3), pb[i], bigv('ple_wg', i), smv('ple_wp', i))
        sm['norms'][(i, 3)] = dg
        big_entries += [('ple_wg', (i,), dwgate), ('ple_wp', (i,), dwproj)]
        for q, key in ((1, 'f2'), (0, 'f1')):
            if key == 'f1':
                if i % 2 == 0:
                    dh, G = _even_bwd(dh, c['mix'], gain(i, 1), *even_params(i))
                    big_entries += [('e_w_in', (j,), G['w_in']), ('e_w_out', (j,), G['w_out'])]
                    rep['a_vnorm'][j], rep['a_ws'][j], rep['a_bs'][j] = G['vgain'][0], G['ws'], G['bs_t'].T
                    rep['b_onorm'][j], rep['b_lb_logits'][i] = G['onorm'][0], G['lb'][0]
                else:
                    dh, G, dvf = _rwkv_bwd(dh, c['mix'], gain(i, 1), rwkv_params(j), dvfirst if j == 0 else None)
                    if dvf is not None:
                        dvfirst = dvf if dvfirst is None else dvfirst + dvf
                    big_entries += [('c_wr', (j,), G['wr']), ('c_wk', (j,), G['wk']), ('c_wv', (j,), G['wv']),
                                    ('c_wo', (j,), G['wo'])]
                    big_entries += [('c_' + n, (j,), G[n]) for n in ('w1', 'w2', 'a1', 'a2', 'g1', 'g2')]
                    for n in ('w0', 'a0', 'kk', 'ka', 'gn_g', 'gn_b'):
                        sm['c_' + n][(j,)] = G[n]
                    sm['c_mix'][(j,)] = jnp.concatenate(G['mix'], axis=0)
                    rep['c_rk'][j] = G['rk'].reshape(c_rk.shape[1:])
                    if j > 0:
                        sm['c_v0'][(j - 1,)] = G['v0']
                        big_entries += [('c_v1', (j - 1,), G['v1']), ('c_v2', (j - 1,), G['v2'])]
                sm['norms'][(i, 1)] = G['gain']
            dh, dg, dwg, dwu, dwd = _ffn_bwd(dh, c[key], gain(i, 2 * q), *ffn_views(i, q))
            sm['norms'][(i, 2 * q)] = dg
            big_entries += [('ffn_wg', (i, q), dwg), ('ffn_wu', (i, q), dwu), ('ffn_wd', (i, q), dwd)]
    grad_x = dh[None]

    def stacked(blocks, lead_shape):
        def rec(prefix, dims):
            if not dims:
                return blocks[prefix]
            return jnp.stack([rec(prefix + (q,), dims[1:]) for q in range(dims[0])], axis=0)
        return rec((), tuple(lead_shape))

    sm_shards = []
    for n in SMALL:
        blk = A[n].shape
        full = stacked(sm[n], blk[:-1] if n != 'c_mix' else blk[:-2])
        sm_shards.append(_vec_shards(full.reshape(blk[:-1] + (D,))))
    sm_pack = _pack(sm_shards, lead=1)
    dlb = jnp.stack([rep['b_lb_logits'].get(i, jnp.zeros((aw,), F32)) for i in range(depth)], axis=0)
    rep_grads = dict(final_norm=dfinal[0], a_vnorm=stacked({(k,): v for k, v in rep['a_vnorm'].items()}, a_vnorm.shape[:1]),
                     a_ws=stacked({(k,): v for k, v in rep['a_ws'].items()}, a_ws.shape[:1]),
                     a_bs=stacked({(k,): v for k, v in rep['a_bs'].items()}, a_bs.shape[:1]),
                     b_onorm=stacked({(k,): v for k, v in rep['b_onorm'].items()}, b_onorm.shape[:1]),
                     b_lb_logits=lb_vjp(dlb)[0],
                     c_rk=stacked({(k,): v for k, v in rep['c_rk'].items()}, c_rk.shape[:1]))
    rep_pack = _pack([rep_grads[n] for n in REP])
    return part, grad_x, big_entries, sm_pack, rep_pack
```

```python
import functools

import jax
import jax.numpy as jnp
from jax import lax
from jax.experimental import pallas as pl
from jax.experimental.pallas import tpu as pltpu

F32 = jnp.float32
BF16 = jnp.bfloat16
MESH = pl.DeviceIdType.MESH

LANES = 128
VMEM_LIMIT = 56 * 1024 * 1024
MM_VMEM_BUDGET = 36 * 1024 * 1024
N_CHIPS = 4

RMS_EPS = 1e-6
A_GROUP = 128
A_CHUNK = 128
B_HEAD = 128
B_MIN_F = 1e-30
C_HEAD = 64
C_GN_EPS = 64e-5
ADAM_LR = 0.001
ADAM_B1 = 0.9
ADAM_B2 = 0.999
ADAM_EPS = 1e-08
ADAM_WD = 0.01
ADAM_STEP = 10


def _sigmoid(x):
    return 1.0 / (1.0 + jnp.exp(-x))


def _silu(x):
    return x * _sigmoid(x)


def _dsilu(x):
    s = _sigmoid(x)
    return s * (1.0 + x * (1.0 - s))


_GELU_C = 0.7978845608028654


def _gelu(x):
    return 0.5 * x * (1.0 + jnp.tanh(_GELU_C * (x + 0.044715 * x * x * x)))


def _dgelu(x):
    th = jnp.tanh(_GELU_C * (x + 0.044715 * x * x * x))
    return 0.5 * (1.0 + th) + 0.5 * x * (1.0 - th * th) * _GELU_C * (1.0 + 3.0 * 0.044715 * x * x)


def _softplus(x):
    return jnp.maximum(x, 0.0) + jnp.log(1.0 + jnp.exp(-jnp.abs(x)))


def _seg_ones(seg):
    i = lax.broadcasted_iota(jnp.int32, (LANES, LANES), 0) // seg
    j = lax.broadcasted_iota(jnp.int32, (LANES, LANES), 1) // seg
    return jnp.where(i == j, 1.0, 0.0).astype(BF16)


def _segsum(x, seg):
    ones = _seg_ones(seg)
    outs = []
    for j in range(x.shape[1] // LANES):
        xb = x[:, j * LANES:(j + 1) * LANES]
        hi = xb.astype(BF16)
        r1 = xb - hi.astype(F32)
        mid = r1.astype(BF16)
        lo = (r1 - mid.astype(F32)).astype(BF16)
        acc = jnp.dot(hi, ones, preferred_element_type=F32)
        acc = acc + jnp.dot(mid, ones, preferred_element_type=F32)
        acc = acc + jnp.dot(lo, ones, preferred_element_type=F32)
        outs.append(acc)
    return outs[0] if len(outs) == 1 else jnp.concatenate(outs, axis=1)


def _rowwise(fn, name, rows, tm, ins, outs, accs=()):
    n_in, n_out, n_acc = len(ins), len(outs), len(accs)
    arrays, in_specs = [], []
    for spec in ins:
        kind, arr = spec[0], spec[1]
        arrays.append(arr)
        if kind == 'row':
            in_specs.append(pl.BlockSpec((tm, arr.shape[1]), lambda i: (i, 0)))
        elif kind == 'col':
            in_specs.append(pl.BlockSpec((tm, spec[2]), functools.partial(lambda i, cb: (i, cb), cb=spec[3])))
        else:
            in_specs.append(pl.BlockSpec(arr.shape, functools.partial(lambda i, nd: (0,) * nd, nd=arr.ndim)))
    out_shape = [jax.ShapeDtypeStruct((rows, w), dt) for (w, dt) in outs]
    out_specs = [pl.BlockSpec((tm, w), lambda i: (i, 0)) for (w, _) in outs]
    out_shape += [jax.ShapeDtypeStruct(s, F32) for s in accs]
    out_specs += [pl.BlockSpec(s, functools.partial(lambda i, nd: (0,) * nd, nd=len(s))) for s in accs]

    def body(*refs):
        vals = fn(*[r[...] for r in refs[:n_in]])
        if not isinstance(vals, (tuple, list)):
            vals = (vals,)
        for r, v in zip(refs[n_in:n_in + n_out], vals[:n_out]):
            r[...] = v.astype(r.dtype)
        if n_acc:
            acc_refs = refs[n_in + n_out:]

            @pl.when(pl.program_id(0) == 0)
            def _():
                for r in acc_refs:
                    r[...] = jnp.zeros(r.shape, F32)

            for r, v in zip(acc_refs, vals[n_out:]):
                r[...] += v

    res = pl.pallas_call(
        body, name=name, grid=(rows // tm,), in_specs=in_specs, out_specs=out_specs, out_shape=out_shape,
        compiler_params=pltpu.CompilerParams(
            dimension_semantics=("arbitrary",) if n_acc else ("parallel",), vmem_limit_bytes=VMEM_LIMIT),
    )(*arrays)
    return res


class WV:
    def __init__(self, arr, idx, kind):
        self.arr, self.idx, self.kind = arr, tuple(idx), kind
        self.ns = arr.shape[0]
        self.R, self.C = arr.shape[-2:]
        self.K = self.R * (self.ns if kind == 'row' else 1)
        self.N = self.C * (self.ns if kind == 'col' else 1)

    def spec(self, br, bc, rmap, cmap):
        lead = (None,) * (1 + len(self.idx))
        nrb, ncb = self.R // br, self.C // bc
        idx, kind = self.idx, self.kind

        def index_map(*g):
            ri, ci = rmap(*g), cmap(*g)
            if kind == 'row':
                return (ri // nrb, *idx, ri % nrb, ci)
            return (ci // ncb, *idx, ri, ci % ncb)

        return pl.BlockSpec(lead + (br, bc), index_map)


def _tile_options(n):
    return [n] + [d for d in range(n - LANES, LANES - 1, -LANES) if n % d == 0]


def _pick_tiles(opt_m, opt_n, opt_k, out_bytes, has_res, full_k):
    best, best_key = None, None
    for tm in opt_m:
        for tn in opt_n:
            for tk in opt_k:
                multi = tk != full_k
                est = 2 * (tm * tk * 2 + tk * tn * 2 + tm * tn * out_bytes) + tm * tn * 4 * (2 if multi else 1)
                if has_res:
                    est += 2 * tm * tn * 4
                if est > MM_VMEM_BUDGET:
                    continue
                key = (tm * tn * tk, tk)
                if best is None or key > best_key:
                    best, best_key = (tm, tn, tk), key
    return best


def _mm_call(name, dims, grid, in_specs, out_spec, out_shape, operands, nk, acc_shape, scale, has_res):
    def body(*refs):
        a_ref, b_ref = refs[0], refs[1]
        res_ref = refs[2] if has_res else None
        o_ref = refs[3] if has_res else refs[2]

        def finalize(acc):
            acc = acc * scale if scale != 1.0 else acc
            if has_res:
                acc = acc + res_ref[...]
            o_ref[...] = acc.astype(o_ref.dtype)

        part = lax.dot_general(a_ref[...], b_ref[...], dims, preferred_element_type=F32)
        if nk == 1:
            finalize(part)
        else:
            acc_ref = refs[-1]
            k = pl.program_id(2)

            @pl.when(k == 0)
            def _():
                acc_ref[...] = part

            @pl.when(k > 0)
            def _():
                acc_ref[...] += part

            @pl.when(k == nk - 1)
            def _():
                finalize(acc_ref[...])

    return pl.pallas_call(
        body, name=name, grid=grid, in_specs=in_specs, out_specs=out_spec, out_shape=out_shape,
        scratch_shapes=[pltpu.VMEM(acc_shape, F32)] if nk > 1 else [],
        compiler_params=pltpu.CompilerParams(
            dimension_semantics=("parallel", "parallel", "arbitrary"), vmem_limit_bytes=VMEM_LIMIT),
    )(*operands)


def _mm_nn(a, w, name, out_dtype=F32, res=None, scale=1.0):
    M, K = a.shape
    N = w.N
    opt_n = _tile_options(w.C)
    opt_k = _tile_options(w.R)
    tm, tn, tk = _pick_tiles(_tile_options(M), opt_n, opt_k, jnp.dtype(out_dtype).itemsize, res is not None, K)
    nk = K // tk
    in_specs = [pl.BlockSpec((tm, tk), lambda n, m, k: (m, k)),
                w.spec(tk, tn, lambda n, m, k: k, lambda n, m, k: n)]
    operands = [a, w.arr]
    if res is not None:
        in_specs.append(pl.BlockSpec((tm, tn), lambda n, m, k: (m, n)))
        operands.append(res)
    return _mm_call(name, (((1,), (0,)), ((), ())), (N // tn, M // tm, nk), in_specs,
                    pl.BlockSpec((tm, tn), lambda n, m, k: (m, n)), jax.ShapeDtypeStruct((M, N), out_dtype),
                    operands, nk, (tm, tn), scale, res is not None)


def _mm_nt(a, w, name, out_dtype=F32, res=None, scale=1.0):
    M, C = a.shape
    Ko = w.K
    opt_n = _tile_options(w.R)
    opt_k = _tile_options(w.C)
    tm, tn, tk = _pick_tiles(_tile_options(M), opt_n, opt_k, jnp.dtype(out_dtype).itemsize, res is not None, C)
    nk = C // tk
    in_specs = [pl.BlockSpec((tm, tk), lambda n, m, k: (m, k)),
                w.spec(tn, tk, lambda n, m, k: n, lambda n, m, k: k)]
    operands = [a, w.arr]
    if res is not None:
        in_specs.append(pl.BlockSpec((tm, tn), lambda n, m, k: (m, n)))
        operands.append(res)
    return _mm_call(name, (((1,), (1,)), ((), ())), (Ko // tn, M // tm, nk), in_specs,
                    pl.BlockSpec((tm, tn), lambda n, m, k: (m, n)), jax.ShapeDtypeStruct((M, Ko), out_dtype),
                    operands, nk, (tm, tn), scale, res is not None)


def _mm_tn(a, dy, like, name, out_dtype=BF16, scale=1.0):
    M, K = a.shape
    N = dy.shape[1]
    out = WV(jax.ShapeDtypeStruct((like.ns, like.R, like.C), out_dtype), (), like.kind)
    opt_m = _tile_options(like.R)
    opt_n = _tile_options(like.C)
    tko, tno, tc = _pick_tiles(opt_m, opt_n, _tile_options(M), jnp.dtype(out_dtype).itemsize, False, M)
    nk = M // tc
    in_specs = [pl.BlockSpec((tc, tko), lambda i, j, c: (c, i)),
                pl.BlockSpec((tc, tno), lambda i, j, c: (c, j))]
    return _mm_call(name, (((0,), (0,)), ((), ())), (K // tko, N // tno, nk), in_specs,
                    out.spec(tko, tno, lambda i, j, c: i, lambda i, j, c: j), out.arr,
                    [a, dy], nk, (tko, tno), scale, False)


SCAN_TB = 16


def _subsum(x):
    r = x.shape[0]
    while r > 8:
        r //= 2
        x = x[:r] + x[r:]
    while r > 1:
        r //= 2
        x = x + pltpu.roll(x, r, 0)
    return x


def _head_ones(n_heads):
    i = lax.broadcasted_iota(jnp.int32, (LANES, LANES), 0) % n_heads
    j = lax.broadcasted_iota(jnp.int32, (LANES, LANES), 1) % n_heads
    return jnp.where(i == j, 1.0, 0.0).astype(BF16)


def _lanesum_mxu(x, ones):
    hi = x.astype(BF16)
    r1 = x - hi.astype(F32)
    mid = r1.astype(BF16)
    lo = (r1 - mid.astype(F32)).astype(BF16)
    acc = jnp.dot(hi, ones, preferred_element_type=F32)
    acc = acc + jnp.dot(mid, ones, preferred_element_type=F32)
    return acc + jnp.dot(lo, ones, preferred_element_type=F32)


def _head_sel(n_heads):
    h = lax.broadcasted_iota(jnp.int32, (n_heads, LANES), 0)
    lane = lax.broadcasted_iota(jnp.int32, (n_heads, LANES), 1) % n_heads
    return jnp.where(h == lane, 1.0, 0.0).astype(BF16)


def _split3(x):
    hi = x.astype(BF16)
    r1 = x - hi.astype(F32)
    mid = r1.astype(BF16)
    return hi, mid, (r1 - mid.astype(F32)).astype(BF16)


def _nat_from_rows(rows, sel):
    dn = (((1,), (1,)), ((), ()))
    hi, mid, lo = _split3(rows)
    acc = lax.dot_general(sel, hi, dn, preferred_element_type=F32)
    acc = acc + lax.dot_general(sel, mid, dn, preferred_element_type=F32)
    return acc + lax.dot_general(sel, lo, dn, preferred_element_type=F32)


def _rows_of(vals):
    n = len(vals)
    if vals[0].shape[0] >= n:
        idx = lax.broadcasted_iota(jnp.int32, (n, LANES), 0)
        out = vals[0][:n]
        for i in range(1, n):
            out = jnp.where(idx == i, vals[i][:n], out)
        return out
    return jnp.concatenate([v[0:1] for v in vals], axis=0)


def _scan_fwd(name, n_heads, wk, kk, rk, vrow, ak=None, bk=None):
    S, R, _ = wk.shape
    dv = vrow.shape[1]
    ab = ak is not None
    tb = min(SCAN_TB, S)
    grp = min(8, dv)

    def body(*refs):
        if ab:
            w_ref, k_ref, r_ref, v_ref, a_ref, b_ref, y_ref, hist_ref, sa_ref, s_ref = refs
        else:
            w_ref, k_ref, r_ref, v_ref, y_ref, hist_ref, s_ref = refs

        @pl.when(pl.program_id(0) == 0)
        def _():
            s_ref[...] = jnp.zeros(s_ref.shape, F32)

        sel = _head_sel(n_heads)
        ones = _head_ones(n_heads) if ab else None

        def joined(tiles):
            return tiles[0] if len(tiles) == 1 else jnp.concatenate(tiles, axis=0)

        def step(t, carry):
            w, k, r = w_ref[t], k_ref[t], r_ref[t]
            if ab:
                a, b = a_ref[t], b_ref[t]
                tiles = []
                for g0 in range(0, dv, grp):
                    tiles.append(_rows_of([_subsum(s_ref[g0 + i] * a) for i in range(grp)]))
                sa_ref[t] = _lanesum_mxu(joined(tiles), ones)
            tiles = []
            for g0 in range(0, dv, grp):
                ys = []
                for i in range(grp):
                    v = g0 + i
                    st = s_ref[v]
                    hist_ref[t, v] = st
                    st = st * w + v_ref[t, pl.ds(v, 1), :] * k
                    if ab:
                        st = st + sa_ref[t, pl.ds(v, 1), :] * b
                    s_ref[v] = st
                    ys.append(_subsum(st * r))
                tiles.append(_rows_of(ys))
            y_ref[t] = _nat_from_rows(joined(tiles), sel)
            return carry

        lax.fori_loop(0, tb, step, 0, unroll=8)

    kspec = pl.BlockSpec((tb, R, LANES), lambda i: (i, 0, 0))
    nspec = pl.BlockSpec((tb, n_heads, dv), lambda i: (i, 0, 0))
    rspec = pl.BlockSpec((tb, dv, LANES), lambda i: (i, 0, 0))
    operands = [wk, kk, rk, vrow] + ([ak, bk] if ab else [])
    in_specs = [kspec, kspec, kspec, rspec] + ([kspec, kspec] if ab else [])
    out_shape = [jax.ShapeDtypeStruct((S, n_heads, dv), F32), jax.ShapeDtypeStruct((S, dv, R, LANES), F32)]
    out_specs = [nspec, pl.BlockSpec((tb, dv, R, LANES), lambda i: (i, 0, 0, 0))]
    if ab:
        out_shape.append(jax.ShapeDtypeStruct((S, dv, LANES), F32))
        out_specs.append(rspec)
    res = pl.pallas_call(
        body, name=name, grid=(S // tb,), in_specs=in_specs, out_specs=out_specs, out_shape=out_shape,
        scratch_shapes=[pltpu.VMEM((dv, R, LANES), F32)],
        compiler_params=pltpu.CompilerParams(dimension_semantics=("arbitrary",), vmem_limit_bytes=VMEM_LIMIT),
    )(*operands)
    return (res[0], res[1], res[2]) if ab else (res[0], res[1], None)


def _scan_bwd(name, n_heads, wk, kk, rk, vrow, hist, dyrow, ak=None, bk=None, sarow=None):
    S, R, _ = wk.shape
    dv = vrow.shape[1]
    ab = ak is not None
    tb = min(SCAN_TB, S)
    nb = S // tb
    grp = min(8, dv)

    def body(*refs):
        if ab:
            (w_ref, k_ref, r_ref, v_ref, hist_ref, dy_ref, a_ref, b_ref, sa_ref,
             dw_ref, dk_ref, dr_ref, dv_ref, da_ref, db_ref, ds_ref, dsa_ref) = refs
        else:
            (w_ref, k_ref, r_ref, v_ref, hist_ref, dy_ref,
             dw_ref, dk_ref, dr_ref, dv_ref, ds_ref) = refs

        @pl.when(pl.program_id(0) == 0)
        def _():
            ds_ref[...] = jnp.zeros(ds_ref.shape, F32)

        sel = _head_sel(n_heads)
        ones = _head_ones(n_heads) if ab else None

        def joined(tiles):
            return tiles[0] if len(tiles) == 1 else jnp.concatenate(tiles, axis=0)

        def step(j, carry):
            t = tb - 1 - j
            w, k, r = w_ref[t], k_ref[t], r_ref[t]
            zero = jnp.zeros((R, LANES), F32)
            u, dw, dk, da, db = zero, zero, zero, zero, zero
            vd = jnp.sum(v_ref[t] * dy_ref[t], axis=0, keepdims=True)
            sd = jnp.sum(sa_ref[t] * dy_ref[t], axis=0, keepdims=True) if ab else None
            if ab:
                a, b = a_ref[t], b_ref[t]
                tiles = []
                for g0 in range(0, dv, grp):
                    ps = []
                    for i in range(grp):
                        v = g0 + i
                        dst = ds_ref[v] + dy_ref[t, pl.ds(v, 1), :] * r
                        ds_ref[v] = dst
                        ps.append(_subsum(dst * b))
                    tiles.append(_rows_of(ps))
                dsa_ref[...] = _lanesum_mxu(joined(tiles), ones)
            tiles = []
            for g0 in range(0, dv, grp):
                dvs = []
                for i in range(grp):
                    v = g0 + i
                    sp = hist_ref[t, v]
                    dyr = dy_ref[t, pl.ds(v, 1), :]
                    vr = v_ref[t, pl.ds(v, 1), :]
                    dst = ds_ref[v] if ab else ds_ref[v] + dyr * r
                    u = u + sp * dyr
                    dw = dw + dst * sp
                    dk = dk + dst * vr
                    dvs.append(_subsum(dst * k))
                    if ab:
                        db = db + dst * sa_ref[t, pl.ds(v, 1), :]
                        dsa = dsa_ref[pl.ds(v, 1), :]
                        da = da + sp * dsa
                        dst = dst * w + dsa * a
                    else:
                        dst = dst * w
                    ds_ref[v] = dst
                tiles.append(_rows_of(dvs))
            dv_ref[t] = _nat_from_rows(joined(tiles), sel)
            dr = w * u + k * vd
            if ab:
                dr = dr + b * sd
                da_ref[t] = da
                db_ref[t] = db
            dw_ref[t] = dw
            dk_ref[t] = dk
            dr_ref[t] = dr
            return carry

        lax.fori_loop(0, tb, step, 0, unroll=8)

    kspec = pl.BlockSpec((tb, R, LANES), lambda i: (nb - 1 - i, 0, 0))
    rspec = pl.BlockSpec((tb, dv, LANES), lambda i: (nb - 1 - i, 0, 0))
    hspec = pl.BlockSpec((tb, dv, R, LANES), lambda i: (nb - 1 - i, 0, 0, 0))
    nspec = pl.BlockSpec((tb, n_heads, dv), lambda i: (nb - 1 - i, 0, 0))
    operands = [wk, kk, rk, vrow, hist, dyrow] + ([ak, bk, sarow] if ab else [])
    in_specs = [kspec, kspec, kspec, rspec, hspec, rspec] + ([kspec, kspec, rspec] if ab else [])
    kshape = jax.ShapeDtypeStruct((S, R, LANES), F32)
    out_shape = [kshape, kshape, kshape, jax.ShapeDtypeStruct((S, n_heads, dv), F32)] + ([kshape, kshape] if ab else [])
    out_specs = [kspec, kspec, kspec, nspec] + ([kspec, kspec] if ab else [])
    return pl.pallas_call(
        body, name=name, grid=(nb,), in_specs=in_specs, out_specs=out_specs, out_shape=out_shape,
        scratch_shapes=[pltpu.VMEM((dv, R, LANES), F32)] + ([pltpu.VMEM((dv, LANES), F32)] if ab else []),
        compiler_params=pltpu.CompilerParams(dimension_semantics=("arbitrary",), vmem_limit_bytes=VMEM_LIMIT),
    )(*operands)


def _to_k(x, n_heads, dk):
    S = x.shape[0]
    kl = LANES // n_heads
    return x.reshape(S, n_heads, dk // kl, kl).transpose(0, 2, 3, 1).reshape(S, dk // kl, LANES)


def _from_k(x, n_heads, dk):
    S = x.shape[0]
    kl = LANES // n_heads
    return x.reshape(S, dk // kl, kl, n_heads).transpose(0, 3, 1, 2).reshape(S, n_heads * dk)


def _to_row(x, n_heads, dv):
    S = x.shape[0]
    return jnp.tile(x.reshape(S, n_heads, dv).transpose(0, 2, 1), (1, 1, LANES // n_heads))


def _tril_mask():
    t = lax.broadcasted_iota(jnp.int32, (A_CHUNK, A_CHUNK), 0)
    s = lax.broadcasted_iota(jnp.int32, (A_CHUNK, A_CHUNK), 1)
    return s <= t


def _gmlp_fwd(name, proj, aw, vgain, ws, bs_t):
    S = proj.shape[0]
    G = aw // A_GROUP

    def body(u_ref, v_ref, gain_ref, ws_ref, bs_ref, o_ref):
        mask = _tril_mask()
        lane = lax.broadcasted_iota(jnp.int32, (A_CHUNK, G), 1)
        bs = bs_ref[...]
        for g in range(G):
            seg = slice(g * A_GROUP, (g + 1) * A_GROUP)
            ua = _gelu(u_ref[:, seg])
            va = _gelu(v_ref[:, seg])
            rs = lax.rsqrt(jnp.mean(va * va, axis=-1, keepdims=True) + RMS_EPS)
            vg = (va * rs) * gain_ref[:, seg]
            wm = jnp.where(mask, ws_ref[g], 0.0).astype(BF16)
            bcol = jnp.sum(jnp.where(lane == g, bs, 0.0), axis=1, keepdims=True)
            s = jnp.dot(wm, vg.astype(BF16), preferred_element_type=F32) + bcol
            o_ref[:, seg] = (ua * s).astype(o_ref.dtype)

    return pl.pallas_call(
        body, name=name, grid=(S // A_CHUNK,),
        in_specs=[pl.BlockSpec((A_CHUNK, aw), lambda i: (i, 0)), pl.BlockSpec((A_CHUNK, aw), lambda i: (i, 1)),
                  pl.BlockSpec((1, aw), lambda i: (0, 0)), pl.BlockSpec((G, A_CHUNK, A_CHUNK), lambda i: (0, 0, 0)),
                  pl.BlockSpec((A_CHUNK, G), lambda i: (0, 0))],
        out_specs=pl.BlockSpec((A_CHUNK, aw), lambda i: (i, 0)),
        out_shape=jax.ShapeDtypeStruct((S, aw), BF16),
        compiler_params=pltpu.CompilerParams(dimension_semantics=("parallel",), vmem_limit_bytes=VMEM_LIMIT),
    )(proj, proj, vgain, ws, bs_t)


def _gmlp_bwd(name, proj, dout, dout_cb, aw, vgain, ws, bs_t):
    S = proj.shape[0]
    G = aw // A_GROUP

    def body(u_ref, v_ref, do_ref, gain_ref, ws_ref, bs_ref, du_ref, dv_ref, dws_ref, dbs_ref, dgain_ref):
        @pl.when(pl.program_id(0) == 0)
        def _():
            dws_ref[...] = jnp.zeros(dws_ref.shape, F32)
            dbs_ref[...] = jnp.zeros(dbs_ref.shape, F32)
            dgain_ref[...] = jnp.zeros(dgain_ref.shape, F32)

        mask = _tril_mask()
        lane = lax.broadcasted_iota(jnp.int32, (A_CHUNK, G), 1)
        bs = bs_ref[...]
        dbs = jnp.zeros((A_CHUNK, G), F32)
        for g in range(G):
            seg = slice(g * A_GROUP, (g + 1) * A_GROUP)
            u, v = u_ref[:, seg], v_ref[:, seg]
            do = do_ref[:, seg].astype(F32)
            ua, va = _gelu(u), _gelu(v)
            rs = lax.rsqrt(jnp.mean(va * va, axis=-1, keepdims=True) + RMS_EPS)
            xh = va * rs
            gain = gain_ref[:, seg]
            vg = (xh * gain).astype(BF16)
            wm = jnp.where(mask, ws_ref[g], 0.0).astype(BF16)
            bcol = jnp.sum(jnp.where(lane == g, bs, 0.0), axis=1, keepdims=True)
            s = jnp.dot(wm, vg, preferred_element_type=F32) + bcol
            du_ref[:, seg] = (do * s * _dgelu(u)).astype(du_ref.dtype)
            ds = do * ua
            dsb = ds.astype(BF16)
            dw = lax.dot_general(dsb, vg, (((1,), (1,)), ((), ())), preferred_element_type=F32)
            dws_ref[g] += jnp.where(mask, dw, 0.0)
            dbs = dbs + jnp.where(lane == g, jnp.sum(ds, axis=1, keepdims=True), 0.0)
            dvg = lax.dot_general(wm, dsb, (((0,), (0,)), ((), ())), preferred_element_type=F32)
            dgain_ref[:, seg] += jnp.sum(dvg * xh, axis=0, keepdims=True)
            dxh = dvg * gain
            dva = rs * (dxh - xh * jnp.mean(dxh * xh, axis=-1, keepdims=True))
            dv_ref[:, seg] = (dva * _dgelu(v)).astype(dv_ref.dtype)
        dbs_ref[...] += dbs

    return pl.pallas_call(
        body, name=name, grid=(S // A_CHUNK,),
        in_specs=[pl.BlockSpec((A_CHUNK, aw), lambda i: (i, 0)), pl.BlockSpec((A_CHUNK, aw), lambda i: (i, 1)),
                  pl.BlockSpec((A_CHUNK, aw), functools.partial(lambda i, cb: (i, cb), cb=dout_cb)),
                  pl.BlockSpec((1, aw), lambda i: (0, 0)), pl.BlockSpec((G, A_CHUNK, A_CHUNK), lambda i: (0, 0, 0)),
                  pl.BlockSpec((A_CHUNK, G), lambda i: (0, 0))],
        out_specs=[pl.BlockSpec((A_CHUNK, aw), lambda i: (i, 0)), pl.BlockSpec((A_CHUNK, aw), lambda i: (i, 0)),
                   pl.BlockSpec((G, A_CHUNK, A_CHUNK), lambda i: (0, 0, 0)), pl.BlockSpec((A_CHUNK, G), lambda i: (0, 0)),
                   pl.BlockSpec((1, aw), lambda i: (0, 0))],
        out_shape=[jax.ShapeDtypeStruct((S, aw), BF16), jax.ShapeDtypeStruct((S, aw), BF16),
                   jax.ShapeDtypeStruct((G, A_CHUNK, A_CHUNK), F32), jax.ShapeDtypeStruct((A_CHUNK, G), F32),
                   jax.ShapeDtypeStruct((1, aw), F32)],
        compiler_params=pltpu.CompilerParams(dimension_semantics=("arbitrary",), vmem_limit_bytes=VMEM_LIMIT),
    )(proj, proj, dout, vgain, ws, bs_t)


ANY = pl.BlockSpec(memory_space=pl.ANY)


def _place():
    return lax.axis_index("x"), lax.axis_index("y"), lax.axis_index("c")


def _other_chips(x, y):
    return [(1 - x, y), (x, 1 - y), (1 - x, 1 - y)]


def _remote(src, dst, send_sem, recv_sem, device):
    return pltpu.make_async_remote_copy(src_ref=src, dst_ref=dst, send_sem=send_sem, recv_sem=recv_sem,
                                        device_id=device, device_id_type=MESH)


def _comm_call(body, name, operands, out_shape, n_dma, n_local):
    return pl.pallas_call(
        body, name=name, in_specs=[ANY] * len(operands), out_specs=[ANY] * len(out_shape), out_shape=out_shape,
        scratch_shapes=[pltpu.SemaphoreType.DMA((n_dma,)), pltpu.SemaphoreType.DMA((n_dma,)),
                        pltpu.SemaphoreType.DMA((max(n_local, 1),))],
        compiler_params=pltpu.CompilerParams(has_side_effects=True),
    )(*operands)


def _all_gather(name, bigs, smalls):
    nb, n = len(bigs), len(bigs) + len(smalls)
    arrays = list(bigs) + list(smalls)

    def body(*refs):
        ins, outs = refs[:n], refs[n:2 * n]
        send_sems, recv_sems, local_sems = refs[2 * n:]
        x, y, c = _place()
        me = 2 * x + y
        chip_x, chip_y = (1 - x, y), (x, 1 - y)
        k_x, k_y, k_d = 2 * (1 - x) + y, 2 * x + (1 - y), 2 * (1 - x) + (1 - y)
        sibling = (x, y, 1 - c)
        started = []

        def go(src, dst, s, device):
            cp = _remote(src, dst, send_sems.at[s], recv_sems.at[s], device)
            cp.start()
            started.append(cp)

        def landed(ref, s):
            _remote(ref, ref, send_sems.at[s], recv_sems.at[s], sibling).wait_recv()

        for e in range(nb):
            half = ins[e].shape[0] // 2
            src = ins[e].at[pl.ds(c * half, half)]
            dst = outs[e].at[me, pl.ds(c * half, half)]
            go(src, dst, 8 * e, (*chip_x, c))
            go(src, dst, 8 * e + 1, (*chip_y, c))
        small = 8 * nb
        for e in range(nb, n):
            for j, chip in enumerate(_other_chips(x, y)):
                go(ins[e], outs[e].at[me], small + 3 * (e - nb) + j, (*chip, c))
        for e in range(nb):
            half = ins[e].shape[0] // 2
            q = half // 2
            from_x = outs[e].at[k_x, pl.ds(c * half, half)]
            landed(from_x, 8 * e)
            first = outs[e].at[k_x, pl.ds(c * half, q)]
            go(first, first, 8 * e + 2, (*chip_y, c))
            go(from_x, from_x, 8 * e + 4, sibling)
            from_y = outs[e].at[k_y, pl.ds(c * half, half)]
            landed(from_y, 8 * e + 1)
            second = outs[e].at[k_y, pl.ds(c * half + q, q)]
            go(second, second, 8 * e + 3, (*chip_x, c))
            go(from_y, from_y, 8 * e + 5, sibling)
        for e in range(nb):
            half = ins[e].shape[0] // 2
            q = half // 2
            first = outs[e].at[k_d, pl.ds(c * half, q)]
            landed(first, 8 * e + 2)
            go(first, first, 8 * e + 6, sibling)
            second = outs[e].at[k_d, pl.ds(c * half + q, q)]
            landed(second, 8 * e + 3)
            go(second, second, 8 * e + 7, sibling)
        for e in range(nb, n):
            for j, chip in enumerate(_other_chips(x, y)):
                landed(outs[e].at[2 * chip[0] + chip[1]], small + 3 * (e - nb) + j)
        for e in range(nb):
            half = ins[e].shape[0] // 2
            q = half // 2
            o = (1 - c) * half
            landed(outs[e].at[k_x, pl.ds(o, half)], 8 * e + 4)
            landed(outs[e].at[k_y, pl.ds(o, half)], 8 * e + 5)
            landed(outs[e].at[k_d, pl.ds(o, q)], 8 * e + 6)
            landed(outs[e].at[k_d, pl.ds(o + q, q)], 8 * e + 7)
        for cp in started:
            cp.wait_send()

    out_shape = [jax.ShapeDtypeStruct((N_CHIPS,) + a.shape, a.dtype) for a in arrays]
    gathered = _comm_call(body, name, arrays, out_shape, 8 * nb + 3 * len(smalls), 0)
    me = 2 * lax.axis_index("x") + lax.axis_index("y")
    return [lax.dynamic_update_slice(g, a[None], (me,) + (0,) * a.ndim) for g, a in zip(gathered, arrays)]


def _swap_halves(name, grads):
    n = len(grads)

    def body(*refs):
        ins, outs = refs[:n], refs[n:2 * n]
        send_sems, recv_sems, _ = refs[2 * n:]
        x, y, c = _place()
        cps = []
        for e in range(n):
            half = ins[e].shape[1] // 2
            src = ins[e].at[pl.ds(0, N_CHIPS), pl.ds((1 - c) * half, half)]
            cp = _remote(src, outs[e], send_sems.at[e], recv_sems.at[e], (x, y, 1 - c))
            cp.start()
            cps.append(cp)
        for cp in cps:
            cp.wait()

    out_shape = [jax.ShapeDtypeStruct((N_CHIPS, g.shape[1] // 2, g.shape[2]), g.dtype) for g in grads]
    return _comm_call(body, name, list(grads), out_shape, n, 0)


def _scatter_chips(name, parts):
    n = len(parts)

    def body(*refs):
        ins, outs = refs[:n], refs[n:2 * n]
        send_sems, recv_sems, local_sems = refs[2 * n:]
        x, y, c = _place()
        me = 2 * x + y
        cps = []
        for e in range(n):
            cp = pltpu.make_async_copy(ins[e].at[me], outs[e].at[3], local_sems.at[e])
            cp.start()
            cps.append(cp)
            for j, chip in enumerate(_other_chips(x, y)):
                cp = _remote(ins[e].at[2 * chip[0] + chip[1]], outs[e].at[j], send_sems.at[3 * e + j],
                             recv_sems.at[3 * e + j], (*chip, c))
                cp.start()
                cps.append(cp)
        for cp in cps:
            cp.wait()

    out_shape = [jax.ShapeDtypeStruct(p.shape, p.dtype) for p in parts]
    return _comm_call(body, name, list(parts), out_shape, 3 * n, n)


def _join_halves(name, halves, places, out_shapes):
    n, n_out = len(halves), len(out_shapes)

    def body(*refs):
        ins, outs = refs[:n], refs[n:n + n_out]
        send_sems, recv_sems, local_sems = refs[n + n_out:]
        x, y, c = _place()
        cps = []
        for e in range(n):
            o, idx = places[e]
            half = ins[e].shape[0]
            dst = outs[o].at[(*idx, pl.ds(c * half, half))]
            cp = _remote(ins[e], dst, send_sems.at[e], recv_sems.at[e], (x, y, 1 - c))
            cp.start()
            cps.append(cp)
        for e, cp in enumerate(cps):
            o, idx = places[e]
            half = ins[e].shape[0]
            landed = outs[o].at[(*idx, pl.ds((1 - c) * half, half))]
            cp.wait_send()
            _remote(ins[e], landed, send_sems.at[e], recv_sems.at[e], (x, y, 1 - c)).wait_recv()

    joined = list(_comm_call(body, name, list(halves), list(out_shapes), n, 0))
    c = lax.axis_index("c")
    for e, hv in enumerate(halves):
        o, idx = places[e]
        start = tuple(idx) + (c * hv.shape[0], 0)
        joined[o] = lax.dynamic_update_slice(joined[o], hv.reshape((1,) * len(idx) + hv.shape), start)
    return joined


def _spread_all(name, per_chip, everywhere):
    def body(pc_ref, ev_ref, pc_out, ev_out, send_sems, recv_sems, local_sems):
        x, y, c = _place()
        me = 4 * x + 2 * y + c
        cps = [pltpu.make_async_copy(pc_ref.at[2 * x + y], pc_out.at[me], local_sems.at[0]),
               pltpu.make_async_copy(ev_ref, ev_out.at[me], local_sems.at[1])]
        for f in range(1, 8):
            fx, fy, fc = f // 4, (f // 2) % 2, f % 2
            tx = 1 - x if fx else x
            ty = 1 - y if fy else y
            tc = 1 - c if fc else c
            cps.append(_remote(pc_ref.at[2 * tx + ty], pc_out.at[me], send_sems.at[2 * f], recv_sems.at[2 * f],
                               (tx, ty, tc)))
            cps.append(_remote(ev_ref, ev_out.at[me], send_sems.at[2 * f + 1], recv_sems.at[2 * f + 1],
                               (tx, ty, tc)))
        for cp in cps:
            cp.start()
        for cp in cps:
            cp.wait()

    out_shape = [jax.ShapeDtypeStruct((8,) + per_chip.shape[1:], F32), jax.ShapeDtypeStruct((8,) + everywhere.shape, F32)]
    return _comm_call(body, name, [per_chip, everywhere], out_shape, 16, 2)


def _row_tile(rows, width, itemsize, n_arrays):
    tm = 1
    while rows % (tm * 2) == 0 and (tm * 2) * width * itemsize * n_arrays * 2 <= 24 * 1024 * 1024 and tm * 2 <= 1024:
        tm *= 2
    return tm


def _add_own_half(name, grad, swapped, c_arr):
    ns, R, C = grad.shape
    half = R // 2
    th = _row_tile(half, C, 2, 3)
    g4 = grad.reshape(ns, 2, half, C)

    def body(c_ref, g_ref, s_ref, o_ref):
        o_ref[...] = (g_ref[...].astype(F32) + s_ref[...].astype(F32)).astype(o_ref.dtype)

    return pl.pallas_call(
        body, name=name,
        grid_spec=pltpu.PrefetchScalarGridSpec(
            num_scalar_prefetch=1, grid=(ns, half // th),
            in_specs=[pl.BlockSpec((None, None, th, C), lambda k, i, c_ref: (k, c_ref[0], i, 0)),
                      pl.BlockSpec((None, th, C), lambda k, i, c_ref: (k, i, 0))],
            out_specs=pl.BlockSpec((None, th, C), lambda k, i, c_ref: (k, i, 0))),
        out_shape=jax.ShapeDtypeStruct((ns, half, C), grad.dtype),
        compiler_params=pltpu.CompilerParams(dimension_semantics=("parallel", "parallel"), vmem_limit_bytes=VMEM_LIMIT),
    )(c_arr, g4, swapped)


def _sum_slots(name, slots, order, out_dtype=F32):
    n, rows, C = slots.shape
    th = _row_tile(rows, C, 4, n + 1)

    def body(s_ref, o_ref):
        acc = s_ref[order[0]].astype(F32)
        for k in order[1:]:
            acc = acc + s_ref[k].astype(F32)
        o_ref[...] = acc.astype(o_ref.dtype)

    return pl.pallas_call(
        body, name=name, grid=(rows // th,),
        in_specs=[pl.BlockSpec((n, th, C), lambda i: (0, i, 0))],
        out_specs=pl.BlockSpec((th, C), lambda i: (i, 0)),
        out_shape=jax.ShapeDtypeStruct((rows, C), out_dtype),
        compiler_params=pltpu.CompilerParams(dimension_semantics=("parallel",), vmem_limit_bytes=VMEM_LIMIT),
    )(slots)


def _adamw(name, g, w, m, v):
    rows, C = g.shape
    tm = _row_tile(rows, C, 4, 7)
    c1 = 1.0 - ADAM_B1 ** ADAM_STEP
    c2 = 1.0 - ADAM_B2 ** ADAM_STEP

    def fn(g, w, m, v):
        m = ADAM_B1 * m + (1.0 - ADAM_B1) * g
        v = ADAM_B2 * v + (1.0 - ADAM_B2) * (g * g)
        delta = -ADAM_LR * ((m / c1) / (jnp.sqrt(v / c2) + ADAM_EPS) + ADAM_WD * w)
        return delta, m, v

    return _rowwise(fn, name, rows, tm, [('row', g), ('row', w), ('row', m), ('row', v)],
                    [(C, F32), (C, F32), (C, F32)])


def _tm(S, width, n_arrays):
    return _row_tile(S, width, 4, n_arrays)


def _rms_fwd(name, h, gain, out_dtype):
    S, D = h.shape

    def fn(hb, g):
        r = lax.rsqrt(jnp.mean(hb * hb, axis=-1, keepdims=True) + RMS_EPS)
        return ((hb * r) * g,)

    return _rowwise(fn, name, S, _tm(S, D, 4), [('row', h), ('full', gain)], [(D, out_dtype)])[0]


def _rms_bwd(name, h_in, gain, dn, dh):
    S, D = h_in.shape

    def fn(hb, g, dnb, dhb):
        r = lax.rsqrt(jnp.mean(hb * hb, axis=-1, keepdims=True) + RMS_EPS)
        xh = hb * r
        dnb = dnb.astype(F32)
        dxh = dnb * g
        dx = r * (dxh - xh * jnp.mean(dxh * xh, axis=-1, keepdims=True))
        return dhb + dx, jnp.sum(dnb * xh, axis=0, keepdims=True)

    return _rowwise(fn, name, S, _tm(S, D, 8), [('row', h_in), ('full', gain), ('row', dn), ('row', dh)],
                    [(D, F32)], [(1, D)])


def _ffn_fwd(h, gain, wg, wu, wd):
    S, D = h.shape
    n = _rms_fwd("rms_fwd_bf16", h, gain, BF16)
    g = _mm_nn(n, wg, "ffn_up", BF16)
    u = _mm_nn(n, wu, "ffn_up", BF16)
    FF = g.shape[1]

    def fn(gb, ub):
        return (_silu(gb.astype(F32)) * ub.astype(F32),)

    act = _rowwise(fn, "ffn_act", S, _tm(S, FF, 6), [('row', g), ('row', u)], [(FF, BF16)])[0]
    return _mm_nn(act, wd, "ffn_down", F32, res=h, scale=0.5), (h, n, g, u)


def _ffn_bwd(dh, cache, gain, wg, wu, wd):
    h, n, g, u = cache
    S, FF = g.shape
    dhb = dh.astype(BF16)
    dact = _mm_nt(dhb, wd, "ffn_dact", BF16, scale=0.5)

    def fn(gb, ub, db):
        gb, ub, db = gb.astype(F32), ub.astype(F32), db.astype(F32)
        sg = _sigmoid(gb)
        sl = gb * sg
        return db * ub * (sg * (1.0 + gb * (1.0 - sg))), db * sl, sl * ub

    dg, du, act = _rowwise(fn, "ffn_act_bwd", S, _tm(S, FF, 10), [('row', g), ('row', u), ('row', dact)],
                           [(FF, BF16), (FF, BF16), (FF, BF16)])
    dwd = _mm_tn(act, dhb, wd, "ffn_dwd", BF16, scale=0.5)
    dwg = _mm_tn(n, dg, wg, "ffn_dwup", BF16)
    dwu = _mm_tn(n, du, wu, "ffn_dwup", BF16)
    dn = _mm_nt(dg, wg, "ffn_dn", F32)
    dn = _mm_nt(du, wu, "ffn_dn_acc", F32, res=dn)
    dh2, dgain = _rms_bwd("rms_bwd", h, gain, dn, dh)
    return dh2, dgain, dwg, dwu, dwd


def _ple_fwd(h, gain, pb, wgate, wproj):
    S, D = h.shape
    n = _rms_fwd("rms_fwd_bf16", h, gain, BF16)
    pre = _mm_nn(n, wgate, "ple_gate", F32)
    e = _mm_nn(pb, wproj, "ple_proj", F32)

    def fn(hb, pr, eb):
        return (hb + _sigmoid(pr) * eb,)

    h2 = _rowwise(fn, "ple_add", S, _tm(S, D, 6), [('row', h), ('row', pre), ('row', e)], [(D, F32)])[0]
    return h2, (h, n, pre, e)


def _ple_bwd(dh, cache, gain, pb, wgate, wproj):
    h, n, pre, e = cache
    S, D = h.shape

    def fn(db, pr, eb):
        gt = _sigmoid(pr)
        return db * eb * gt * (1.0 - gt), db * gt

    dpre, de = _rowwise(fn, "ple_bwd", S, _tm(S, D, 6), [('row', dh), ('row', pre), ('row', e)],
                        [(D, BF16), (D, BF16)])
    dwproj = _mm_tn(pb, de, wproj, "ple_dwproj", BF16)
    dwgate = _mm_tn(n, dpre, wgate, "ple_dwgate", BF16)
    dn = _mm_nt(dpre, wgate, "ple_dn", F32)
    dh2, dgain = _rms_bwd("rms_bwd", h, gain, dn, dh)
    return dh2, dgain, dwgate, dwproj


def _even_fwd(h, gain, w_in, w_out, vgain, ws, bs_t, onorm, lb):
    S, D = h.shape
    aw = D // 2
    nh = aw // B_HEAD
    hn = _rms_fwd("rms_fwd_bf16", h, gain, BF16)
    proj = _mm_nn(hn, w_in, "even_in", F32)
    a_out = _gmlp_fwd("gmlp_fwd", proj, aw, vgain, ws, bs_t)

    def pre(bq, bf, lbv):
        f = lbv + (1.0 - lbv) * _sigmoid(bf)
        return _silu(bq), jnp.maximum(f, B_MIN_F), 1.0 - f

    q, w, k = _rowwise(pre, "hgrn_pre", S, _tm(S, aw, 8), [('col', proj, aw, 2), ('col', proj, aw, 3), ('full', lb)],
                       [(aw, F32), (aw, F32), (aw, F32)])
    wk, kk, qk = _to_k(w, nh, B_HEAD), _to_k(k, nh, B_HEAD), _to_k(q, nh, B_HEAD)
    vrow = _to_row(proj[:, 4 * aw:5 * aw], nh, B_HEAD)
    ynat, hist, _ = _scan_fwd("hgrn_scan_fwd", nh, wk, kk, qk, vrow)
    o = ynat.reshape(S, aw)

    def post(ob, bg, on):
        rs = lax.rsqrt(_segsum(ob * ob, B_HEAD) * (1.0 / B_HEAD) + RMS_EPS)
        return ((ob * rs * on) * _silu(bg),)

    b_out = _rowwise(post, "hgrn_post", S, _tm(S, aw, 8), [('row', o), ('col', proj, aw, 5), ('full', onorm)],
                     [(aw, BF16)])[0]
    cat = jnp.concatenate([a_out, b_out], axis=1)
    h2 = _mm_nn(cat, w_out, "even_out", F32, res=h)
    return h2, (h, hn, proj, wk, kk, qk, vrow, hist, o, cat)


def _even_bwd(dh, cache, gain, w_in, w_out, vgain, ws, bs_t, onorm, lb):
    h, hn, proj, wk, kk, qk, vrow, hist, o, cat = cache
    S, D = h.shape
    aw = D // 2
    nh = aw // B_HEAD
    dhb = dh.astype(BF16)
    dw_out = _mm_tn(cat, dhb, w_out, "even_dwout", BF16)
    dcat = _mm_nt(dhb, w_out, "even_dcat", F32)

    def post_bwd(ob, bg, on, db):
        rs = lax.rsqrt(_segsum(ob * ob, B_HEAD) * (1.0 / B_HEAD) + RMS_EPS)
        xh = ob * rs
        dy = db * _silu(bg)
        dbg = db * (xh * on) * _dsilu(bg)
        dxh = dy * on
        do = rs * (dxh - xh * (_segsum(dxh * xh, B_HEAD) * (1.0 / B_HEAD)))
        return do, dbg, jnp.sum(dy * xh, axis=0, keepdims=True)

    do, dbg, donorm = _rowwise(post_bwd, "hgrn_post_bwd", S, _tm(S, aw, 10),
                               [('row', o), ('col', proj, aw, 5), ('full', onorm), ('col', dcat, aw, 1)],
                               [(aw, F32), (aw, BF16)], [(1, aw)])
    dwk, dkk, dqk, dvnat = _scan_bwd("hgrn_scan_bwd", nh, wk, kk, qk, vrow, hist, _to_row(do, nh, B_HEAD))
    dq, dw, dk = _from_k(dqk, nh, B_HEAD), _from_k(dwk, nh, B_HEAD), _from_k(dkk, nh, B_HEAD)
    dbi = dvnat.reshape(S, aw).astype(BF16)

    def pre_bwd(bq, bf, lbv, dqb, dwb, dkb):
        sig = _sigmoid(bf)
        f = lbv + (1.0 - lbv) * sig
        df = jnp.where(f > B_MIN_F, dwb, 0.0) - dkb
        return dqb * _dsilu(bq), df * (1.0 - lbv) * sig * (1.0 - sig), jnp.sum(df * (1.0 - sig), axis=0, keepdims=True)

    dbq, dbf, dlb = _rowwise(pre_bwd, "hgrn_pre_bwd", S, _tm(S, aw, 12),
                             [('col', proj, aw, 2), ('col', proj, aw, 3), ('full', lb), ('row', dq), ('row', dw),
                              ('row', dk)], [(aw, BF16), (aw, BF16)], [(1, aw)])
    dau, dav, dws, dbs_t, dvgain = _gmlp_bwd("gmlp_bwd", proj, dcat, 0, aw, vgain, ws, bs_t)
    dproj = jnp.concatenate([dau, dav, dbq, dbf, dbi, dbg], axis=1)
    dw_in = _mm_tn(hn, dproj, w_in, "even_dwin", BF16)
    dn = _mm_nt(dproj, w_in, "even_dn", F32)
    dh2, dgain = _rms_bwd("rms_bwd", h, gain, dn, dh)
    return dh2, dict(gain=dgain, w_in=dw_in, w_out=dw_out, vgain=dvgain, ws=dws, bs_t=dbs_t, onorm=donorm, lb=dlb)


def _rwkv_prep(r, k, v0, wpl, apl, w0, a0, kkg, kag, svl=None, vf=None, v0p=None):
    wp = w0 + wpl
    w = -_softplus(-wp) - 0.5
    ew = jnp.exp(w)
    decay = jnp.exp(-ew)
    a = _sigmoid(a0 + apl)
    if svl is not None:
        sv = _sigmoid(v0p + svl)
        v = v0 + (vf - v0) * sv
    else:
        sv, v = None, v0
    kkp = k * kkg
    nrm = jnp.sqrt(_segsum(kkp * kkp, C_HEAD))
    inv = 1.0 / jnp.maximum(nrm, 1e-12)
    kk = kkp * inv
    k2 = k * (1.0 + (a - 1.0) * kag)
    return dict(wp=wp, ew=ew, decay=decay, a=a, sv=sv, v=v, kkp=kkp, nrm=nrm, inv=inv, kk=kk, k2=k2)


def _rwkv_post(y, r, k2, v, gn_g, gn_b, rk):
    mu = _segsum(y, C_HEAD) * (1.0 / C_HEAD)
    yc = y - mu
    rstd = lax.rsqrt(_segsum(yc * yc, C_HEAD) * (1.0 / C_HEAD) + C_GN_EPS)
    yh = yc * rstd
    s = _segsum(r * k2 * rk, C_HEAD)
    return yh, rstd, s, yh * gn_g + gn_b + s * v


def _rwkv_fwd(h, gain, P, vfirst):
    S, D = h.shape
    nh = D // C_HEAD
    vres = vfirst is not None
    hn = _rms_fwd("rms_fwd_f32", h, gain, F32)
    hs = jnp.concatenate([jnp.zeros((1, D), F32), hn[:-1]], axis=0)

    def mixf(x, xs, m0, m1, m2, m3, m4, m5):
        xx = xs - x
        return tuple(x + xx * m for m in (m0, m1, m2, m3, m4, m5))

    xr, xw, xk, xv, xa, xg = _rowwise(mixf, "rwkv_mix", S, _tm(S, D, 8),
                                      [('row', hn), ('row', hs)] + [('full', m) for m in P['mix']], [(D, BF16)] * 6)
    r = _mm_nn(xr, P['wr'], "rwkv_proj", F32)
    k = _mm_nn(xk, P['wk'], "rwkv_proj", F32)
    v0 = _mm_nn(xv, P['wv'], "rwkv_proj", F32)
    lw1 = _mm_nn(xw, P['w1'], "rwkv_lora_in", F32)
    la1 = _mm_nn(xa, P['a1'], "rwkv_lora_in", F32)
    lg1 = _mm_nn(xg, P['g1'], "rwkv_lora_in_g", F32)
    ins = [('row', lw1), ('row', la1), ('row', lg1)]
    outs = [(lw1.shape[1], BF16), (la1.shape[1], BF16), (lg1.shape[1], BF16)]
    if vres:
        lv1 = _mm_nn(xv, P['v1'], "rwkv_lora_in_v", F32)
        ins.append(('row', lv1))
        outs.append((lv1.shape[1], BF16))

    def lora_act(*xs):
        res = [jnp.tanh(xs[0]), xs[1], _sigmoid(xs[2])]
        return tuple(res + list(xs[3:]))

    acts = _rowwise(lora_act, "rwkv_lora_act", S, _tm(S, 1024, 4), ins, outs)
    tw, la1b, sg = acts[0], acts[1], acts[2]
    wpl = _mm_nn(tw, P['w2'], "rwkv_lora_out", F32)
    apl = _mm_nn(la1b, P['a2'], "rwkv_lora_out", F32)
    g = _mm_nn(sg, P['g2'], "rwkv_lora_out_g", F32)
    prep_ins = [('row', r), ('row', k), ('row', v0), ('row', wpl), ('row', apl),
                ('full', P['w0']), ('full', P['a0']), ('full', P['kk']), ('full', P['ka'])]
    svl = lv1b = None
    if vres:
        lv1b = acts[3]
        svl = _mm_nn(lv1b, P['v2'], "rwkv_lora_out_v", F32)
        prep_ins += [('row', svl), ('row', vfirst), ('full', P['v0'])]

    def prep(*xs):
        q = _rwkv_prep(*xs)
        return q['decay'], q['k2'], q['v'], -q['kk'], q['kk'] * q['a']

    decay, k2, v, av, bv = _rowwise(prep, "rwkv_prep", S, _tm(S, D, 24), prep_ins, [(D, F32)] * 5)
    tk = functools.partial(_to_k, n_heads=nh, dk=C_HEAD)
    wk_, kk_, rk_, ak_, bk_ = tk(decay), tk(k2), tk(r), tk(av), tk(bv)
    vrow = _to_row(v, nh, C_HEAD)
    ynat, hist, sarow = _scan_fwd("rwkv_scan_fwd", nh, wk_, kk_, rk_, vrow, ak_, bk_)
    y = ynat.reshape(S, D)

    def post(yb, rb, k2b, vb, gb, gn_g, gn_b, rkf):
        return (_rwkv_post(yb, rb, k2b, vb, gn_g, gn_b, rkf)[3] * gb,)

    zg = _rowwise(post, "rwkv_post", S, _tm(S, D, 16),
                  [('row', y), ('row', r), ('row', k2), ('row', v), ('row', g),
                   ('full', P['gn_g']), ('full', P['gn_b']), ('full', P['rk'])], [(D, BF16)])[0]
    h2 = _mm_nn(zg, P['wo'], "rwkv_out", F32, res=h)
    cache = dict(h=h, hn=hn, hs=hs, x=(xr, xw, xk, xv, xa, xg), r=r, k=k, v0=v0, tw=tw, la1b=la1b, sg=sg, lv1b=lv1b,
                 wpl=wpl, apl=apl, svl=svl, g=g, k2=k2, v=v, scan=(wk_, kk_, rk_, vrow, ak_, bk_, hist, sarow), y=y,
                 zg=zg, vfirst=vfirst)
    return h2, cache, (v if not vres else vfirst)


def _rwkv_bwd(dh, cache, gain, P, dvfirst_in):
    c = cache
    h = c['h']
    S, D = h.shape
    nh = D // C_HEAD
    vres = c['vfirst'] is not None
    xr, xw, xk, xv, xa, xg = c['x']
    dhb = dh.astype(BF16)
    dwo = _mm_tn(c['zg'], dhb, P['wo'], "rwkv_dwo", BF16)
    dzg = _mm_nt(dhb, P['wo'], "rwkv_dzg", F32)

    def post_bwd(dzgb, yb, rb, k2b, vb, gb, gn_g, gn_b, rkf):
        yh, rstd, s, z = _rwkv_post(yb, rb, k2b, vb, gn_g, gn_b, rkf)
        dz = dzgb * gb
        dyh = dz * gn_g
        m1 = _segsum(dyh, C_HEAD) * (1.0 / C_HEAD)
        m2 = _segsum(dyh * yh, C_HEAD) * (1.0 / C_HEAD)
        dy = rstd * (dyh - m1 - yh * m2)
        ds = _segsum(dz * vb, C_HEAD)
        return (dy, dzgb * z, ds * k2b * rkf, ds * rb * rkf, dz * s,
                jnp.sum(dz * yh, axis=0, keepdims=True), jnp.sum(dz, axis=0, keepdims=True),
                jnp.sum(ds * rb * k2b, axis=0, keepdims=True))

    dy, dgb, dr_b, dk2_b, dv_b, dgn_g, dgn_b, drk = _rowwise(
        post_bwd, "rwkv_post_bwd", S, _tm(S, D, 28),
        [('row', dzg), ('row', c['y']), ('row', c['r']), ('row', c['k2']), ('row', c['v']), ('row', c['g']),
         ('full', P['gn_g']), ('full', P['gn_b']), ('full', P['rk'])],
        [(D, F32), (D, BF16), (D, F32), (D, F32), (D, F32)], [(1, D)] * 3)
    wk_, kk_, rk_, vrow, ak_, bk_, hist, sarow = c['scan']
    dwk, dkk, drk_s, dvnat, dak, dbk = _scan_bwd("rwkv_scan_bwd", nh, wk_, kk_, rk_, vrow, hist,
                                                 _to_row(dy, nh, C_HEAD), ak_, bk_, sarow)
    fk = functools.partial(_from_k, n_heads=nh, dk=C_HEAD)
    ddecay, dk2_s, dr_s, dA, dB = fk(dwk), fk(dkk), fk(drk_s), fk(dak), fk(dbk)
    dv_s = dvnat.reshape(S, D)
    dr_t, dk2_t, dv_t = dr_s + dr_b, dk2_s + dk2_b, dv_s + dv_b
    if dvfirst_in is not None:
        dv_t = dv_t + dvfirst_in
    ins = [('row', c['r']), ('row', c['k']), ('row', c['v0']), ('row', c['wpl']), ('row', c['apl']),
           ('full', P['w0']), ('full', P['a0']), ('full', P['kk']), ('full', P['ka'])]
    if vres:
        ins += [('row', c['svl']), ('row', c['vfirst']), ('full', P['v0'])]
    n_fwd = len(ins)
    ins += [('row', t) for t in (dr_t, ddecay, dk2_t, dv_t, dA, dB)]

    def prep_bwd(*xs):
        q = _rwkv_prep(*xs[:n_fwd])
        kb, kkg, kag = xs[1], xs[7], xs[8]
        dr, ddec, dk2, dv, dav, dbv = xs[n_fwd:]
        a, kk, kkp, inv = q['a'], q['kk'], q['kkp'], q['inv']
        dkk = dbv * a - dav
        da = dbv * kk + dk2 * kb * kag
        dk = dk2 * (1.0 + (a - 1.0) * kag)
        pr = _segsum(dkk * kkp, C_HEAD)
        dkkp = dkk * inv - jnp.where(q['nrm'] > 1e-12, kkp * pr * inv * inv * inv, 0.0)
        dk = dk + dkkp * kkg
        dap = da * a * (1.0 - a)
        dwp = ddec * q['decay'] * (-q['ew']) * _sigmoid(-q['wp'])
        outs = [dr, dk]
        accs = [jnp.sum(dwp, axis=0, keepdims=True), jnp.sum(dap, axis=0, keepdims=True),
                jnp.sum(dkkp * kb, axis=0, keepdims=True), jnp.sum(dk2 * kb * (a - 1.0), axis=0, keepdims=True)]
        if vres:
            v0b, vfb, sv = xs[2], xs[10], q['sv']
            dsvp = dv * (vfb - v0b) * sv * (1.0 - sv)
            outs += [dv * (1.0 - sv), dwp, dap, dsvp, dv * sv]
            accs.append(jnp.sum(dsvp, axis=0, keepdims=True))
        else:
            outs += [dv, dwp, dap]
        return tuple(outs + accs)

    outs = [(D, BF16)] * 5 + ([(D, BF16), (D, F32)] if vres else [])
    res = _rowwise(prep_bwd, "rwkv_prep_bwd", S, _tm(S, D, 40), ins, outs, [(1, D)] * (5 if vres else 4))
    drb, dkb, dv0b, dwpb, dapb = res[:5]
    dvfirst_out = res[6] if vres else None
    accs = res[len(outs):]
    G = dict(wo=dwo, w0=accs[0], a0=accs[1], kk=accs[2], ka=accs[3], gn_g=dgn_g, gn_b=dgn_b, rk=drk)
    G['w2'] = _mm_tn(c['tw'], dwpb, P['w2'], "rwkv_dlora_out", BF16)
    G['a2'] = _mm_tn(c['la1b'], dapb, P['a2'], "rwkv_dlora_out", BF16)
    G['g2'] = _mm_tn(c['sg'], dgb, P['g2'], "rwkv_dlora_out_g", BF16)
    dtw = _mm_nt(dwpb, P['w2'], "rwkv_dlora_mid", F32)
    dla1 = _mm_nt(dapb, P['a2'], "rwkv_dlora_mid", F32)
    dsg = _mm_nt(dgb, P['g2'], "rwkv_dlora_mid_g", F32)
    ins = [('row', dtw), ('row', c['tw']), ('row', dla1), ('row', dsg), ('row', c['sg'])]
    outs = [(dtw.shape[1], BF16), (dla1.shape[1], BF16), (dsg.shape[1], BF16)]
    if vres:
        dsvpb = res[5]
        G['v0'] = accs[4]
        G['v2'] = _mm_tn(c['lv1b'], dsvpb, P['v2'], "rwkv_dlora_out_v", BF16)
        dlv1 = _mm_nt(dsvpb, P['v2'], "rwkv_dlora_mid_v", F32)
        ins.append(('row', dlv1))
        outs.append((dlv1.shape[1], BF16))

    def lora_act_bwd(dtwb, twb, dla1b_, dsgb, sgb, *rest):
        twb, sgb = twb.astype(F32), sgb.astype(F32)
        return tuple([dtwb * (1.0 - twb * twb), dla1b_, dsgb * sgb * (1.0 - sgb)] + list(rest))

    acts = _rowwise(lora_act_bwd, "rwkv_lora_act_bwd", S, _tm(S, 1024, 6), ins, outs)
    dlw1b, dla1b, dlg1b = acts[0], acts[1], acts[2]
    G['w1'] = _mm_tn(xw, dlw1b, P['w1'], "rwkv_dlora_in", BF16)
    G['a1'] = _mm_tn(xa, dla1b, P['a1'], "rwkv_dlora_in", BF16)
    G['g1'] = _mm_tn(xg, dlg1b, P['g1'], "rwkv_dlora_in_g", BF16)
    G['wr'] = _mm_tn(xr, drb, P['wr'], "rwkv_dwproj", BF16)
    G['wk'] = _mm_tn(xk, dkb, P['wk'], "rwkv_dwproj", BF16)
    G['wv'] = _mm_tn(xv, dv0b, P['wv'], "rwkv_dwproj", BF16)
    dxw = _mm_nt(dlw1b, P['w1'], "rwkv_dx_lora", F32)
    dxa = _mm_nt(dla1b, P['a1'], "rwkv_dx_lora", F32)
    dxg = _mm_nt(dlg1b, P['g1'], "rwkv_dx_lora_g", F32)
    dxr = _mm_nt(drb, P['wr'], "rwkv_dx", F32)
    dxk = _mm_nt(dkb, P['wk'], "rwkv_dx", F32)
    dxv = _mm_nt(dv0b, P['wv'], "rwkv_dx", F32)
    if vres:
        G['v1'] = _mm_tn(xv, acts[3], P['v1'], "rwkv_dlora_in_v", BF16)
        dxv = _mm_nt(acts[3], P['v1'], "rwkv_dx_lora_v", F32, res=dxv)

    def mix_bwd(x, xs, m0, m1, m2, m3, m4, m5, d0, d1, d2, d3, d4, d5):
        xx = xs - x
        ds_ = (d0, d1, d2, d3, d4, d5)
        dxx = d0 * m0 + d1 * m1 + d2 * m2 + d3 * m3 + d4 * m4 + d5 * m5
        dsum = d0 + d1 + d2 + d3 + d4 + d5
        return tuple([dsum - dxx, dxx] + [jnp.sum(d * xx, axis=0, keepdims=True) for d in ds_])

    res = _rowwise(mix_bwd, "rwkv_mix_bwd", S, _tm(S, D, 24),
                   [('row', c['hn']), ('row', c['hs'])] + [('full', m) for m in P['mix']]
                   + [('row', d) for d in (dxr, dxw, dxk, dxv, dxa, dxg)], [(D, F32), (D, F32)], [(1, D)] * 6)
    dx_here, dxs = res[0], res[1]
    G['mix'] = res[2:]
    dhn = dx_here + jnp.concatenate([dxs[1:], jnp.zeros((1, D), F32)], axis=0)
    dh2, G['gain'] = _rms_bwd("rms_bwd", h, gain, dhn, dh)
    return dh2, G, dvfirst_out


def _loss_bwd(h, target, gain):
    S, D = h.shape

    def fn(hb, tb, g):
        r = lax.rsqrt(jnp.mean(hb * hb, axis=-1, keepdims=True) + RMS_EPS)
        xh = hb * r
        e = xh * g - tb
        dy = e * (1.0 / D)
        dxh = dy * g
        dx = r * (dxh - xh * jnp.mean(dxh * xh, axis=-1, keepdims=True))
        part = jnp.sum(jnp.sum(e * e, axis=-1, keepdims=True), axis=0, keepdims=True) * (0.5 / D)
        return dx, jnp.sum(dy * xh, axis=0, keepdims=True), jnp.broadcast_to(part, (1, LANES))

    dh, dgain, part = _rowwise(fn, "loss", S, _tm(S, D, 8), [('row', h), ('row', target), ('full', gain)],
                               [(D, F32)], [(1, D), (1, LANES)])
    return part[0, 0], dh, dgain


WEIGHTS = ['norms', 'final_norm', 'ffn_wg', 'ffn_wu', 'ffn_wd', 'ple_wp', 'ple_wg', 'e_w_in', 'e_w_out', 'a_vnorm',
           'a_ws', 'a_bs', 'b_onorm', 'b_lb_logits', 'c_mix', 'c_wr', 'c_wk', 'c_wv', 'c_wo', 'c_w0', 'c_w1', 'c_w2',
           'c_a0', 'c_a1', 'c_a2', 'c_g1', 'c_g2', 'c_kk', 'c_ka', 'c_rk', 'c_gn_g', 'c_gn_b', 'c_v0', 'c_v1', 'c_v2']
BIG = {'ffn_wg': 'col', 'ffn_wu': 'col', 'ffn_wd': 'row', 'ple_wg': 'row', 'e_w_in': 'col', 'e_w_out': 'row',
       'c_wr': 'row', 'c_wk': 'row', 'c_wv': 'row', 'c_wo': 'row', 'ple_wp': 'col', 'c_w1': 'row', 'c_w2': 'col',
       'c_a1': 'row', 'c_a2': 'col', 'c_g1': 'row', 'c_g2': 'col', 'c_v1': 'row', 'c_v2': 'col'}
SMALL = ['norms', 'c_mix', 'c_w0', 'c_a0', 'c_kk', 'c_ka', 'c_gn_g', 'c_gn_b', 'c_v0']
REP = ['final_norm', 'a_vnorm', 'a_ws', 'a_bs', 'b_onorm', 'b_lb_logits', 'c_rk']
PACK_QUANTUM = 1024 * LANES


def _pack(arrays, lead=0):
    lead_shape = arrays[0].shape[:lead]
    flat = jnp.concatenate([a.reshape(lead_shape + (-1,)).astype(F32) for a in arrays], axis=-1)
    pad = (-flat.shape[-1]) % PACK_QUANTUM
    if pad:
        flat = jnp.concatenate([flat, jnp.zeros(lead_shape + (pad,), F32)], axis=-1)
    return flat.reshape(lead_shape + (-1, LANES))


def _unpack(packed, names, shapes, lead=0):
    lead_shape = packed.shape[:lead]
    flat = packed.reshape(lead_shape + (-1,))
    out, off = {}, 0
    for n, s in zip(names, shapes):
        size = 1
        for d in s:
            size *= d
        out[n] = flat[..., off:off + size].reshape(lead_shape + tuple(s))
        off += size
    return out


def _full_vec(g):
    return jnp.moveaxis(g, 0, -2).reshape(g.shape[1:-1] + (N_CHIPS * g.shape[-1],))


def _vec_shards(g):
    return jnp.moveaxis(g.reshape(g.shape[:-1] + (N_CHIPS, g.shape[-1] // N_CHIPS)), -2, 0)


def _lower_bounds(logits):
    probs = jax.nn.softmax(logits.astype(F32), axis=0)
    return jnp.cumsum(probs, axis=0) - probs[0]


def kernel(x, p, norms, final_norm, ffn_wg, ffn_wu, ffn_wd, ple_wp, ple_wg, e_w_in, e_w_out, a_vnorm, a_ws, a_bs, b_onorm, b_lb_logits, c_mix, c_wr, c_wk, c_wv, c_wo, c_w0, c_w1, c_w2, c_a0, c_a1, c_a2, c_g1, c_g2, c_kk, c_ka, c_rk, c_gn_g, c_gn_b, c_v0, c_v1, c_v2, loss_target, m_norms, m_final_norm, m_ffn_wg, m_ffn_wu, m_ffn_wd, m_ple_wp, m_ple_wg, m_e_w_in, m_e_w_out, m_a_vnorm, m_a_ws, m_a_bs, m_b_onorm, m_b_lb_logits, m_c_mix, m_c_wr, m_c_wk, m_c_wv, m_c_wo, m_c_w0, m_c_w1, m_c_w2, m_c_a0, m_c_a1, m_c_a2, m_c_g1, m_c_g2, m_c_kk, m_c_ka, m_c_rk, m_c_gn_g, m_c_gn_b, m_c_v0, m_c_v1, m_c_v2, v_norms, v_final_norm, v_ffn_wg, v_ffn_wu, v_ffn_wd, v_ple_wp, v_ple_wg, v_e_w_in, v_e_w_out, v_a_vnorm, v_a_ws, v_a_bs, v_b_onorm, v_b_lb_logits, v_c_mix, v_c_wr, v_c_wk, v_c_wv, v_c_wo, v_c_w0, v_c_w1, v_c_w2, v_c_a0, v_c_a1, v_c_a2, v_c_g1, v_c_g2, v_c_kk, v_c_ka, v_c_rk, v_c_gn_g, v_c_gn_b, v_c_v0, v_c_v1, v_c_v2):
    A = dict(locals())
    assert x.shape[0] == 1, "one example per device"

    big_names = list(BIG)
    gathered = _all_gather("gather_weights", [A[n].astype(BF16).reshape(-1, A[n].shape[-1]) for n in big_names],
                           [_pack([A[n] for n in SMALL])])
    GB = {n: gathered[i].reshape((N_CHIPS,) + A[n].shape) for i, n in enumerate(big_names)}
    GS = _unpack(gathered[-1], SMALL, [A[n].shape for n in SMALL], lead=1)

    part, grad_x, big_entries, sm_pack, rep_pack = _local_step(A, GB, GS)
    loss = lax.psum(part, ("x", "y", "c"))

    c_arr = lax.axis_index("c").astype(jnp.int32).reshape(1)
    grads = [e[2] for e in big_entries]
    swapped = _swap_halves("reduce_swap_cores", grads)
    parts = [_add_own_half("reduce_add_cores", g, s, c_arr) for g, s in zip(grads, swapped)]
    slots = _scatter_chips("reduce_scatter_chips", parts)
    halves = [_sum_slots("reduce_sum_chips", s, (3, 0, 1, 2)) for s in slots]
    places = [(big_names.index(n), idx) for n, idx, _ in big_entries]
    big_grads = _join_halves("reduce_join_cores", halves, places,
                             [jax.ShapeDtypeStruct(A[n].shape, F32) for n in big_names])
    sm_slots, rep_slots = _spread_all("reduce_small", sm_pack, rep_pack)
    sm_grad = _sum_slots("reduce_sum_small", sm_slots, tuple(range(8)))
    rep_grad = _sum_slots("reduce_sum_small", rep_slots, tuple(range(8)))

    outs = {}
    for o, n in enumerate(big_names):
        shp, C = A[n].shape, A[n].shape[-1]
        g = big_grads[o]
        d, nm, nv = _adamw("adamw", g.reshape(-1, C), A[n].reshape(-1, C), A['m_' + n].reshape(-1, C),
                           A['v_' + n].reshape(-1, C))
        outs[n] = (g, d.reshape(shp), nm.reshape(shp), nv.reshape(shp))
    for names, g in ((SMALL, sm_grad), (REP, rep_grad)):
        shapes = [A[n].shape for n in names]
        res = _adamw("adamw_packed", g, _pack([A[n] for n in names]), _pack([A['m_' + n] for n in names]),
                     _pack([A['v_' + n] for n in names]))
        un = [_unpack(t, names, shapes) for t in (g,) + tuple(res)]
        for n in names:
            outs[n] = tuple(u[n] for u in un)
    return (loss, grad_x, *[outs[n][0] for n in WEIGHTS], *[outs[n][1] for n in WEIGHTS],
            *[outs[n][2] for n in WEIGHTS], *[outs[n][3] for n in WEIGHTS])


def _local_step(A, GB, GS):
    x, p, a_vnorm, a_ws, a_bs, b_onorm, b_lb_logits, c_rk = (A[n] for n in (
        'x', 'p', 'a_vnorm', 'a_ws', 'a_bs', 'b_onorm', 'b_lb_logits', 'c_rk'))
    S, D = x.shape[1], x.shape[2]
    depth = A['ffn_wg'].shape[0]
    aw = D // 2
    h = x[0]
    target = A['loss_target'][0]
    final_norm = A['final_norm']
    pb = p[:, 0].astype(BF16)

    def bigv(n, *idx):
        return WV(GB[n], idx, BIG[n])

    smv = bigv

    def row(v):
        return v.reshape(1, -1)

    vecs = {n: _full_vec(GS[n]) for n in SMALL}
    lb_all, lb_vjp = jax.vjp(_lower_bounds, b_lb_logits)

    def rwkv_params(j):
        P = dict(mix=[vecs['c_mix'][j, q:q + 1] for q in range(6)],
                 wr=bigv('c_wr', j), wk=bigv('c_wk', j), wv=bigv('c_wv', j), wo=bigv('c_wo', j),
                 w1=smv('c_w1', j), w2=smv('c_w2', j), a1=smv('c_a1', j), a2=smv('c_a2', j),
                 g1=smv('c_g1', j), g2=smv('c_g2', j),
                 w0=row(vecs['c_w0'][j]), a0=row(vecs['c_a0'][j]), kk=row(vecs['c_kk'][j]), ka=row(vecs['c_ka'][j]),
                 gn_g=row(vecs['c_gn_g'][j]), gn_b=row(vecs['c_gn_b'][j]), rk=c_rk[j].reshape(1, D))
        if j > 0:
            P.update(v0=row(vecs['c_v0'][j - 1]), v1=smv('c_v1', j - 1), v2=smv('c_v2', j - 1))
        return P

    def even_params(i):
        j = i // 2
        return (bigv('e_w_in', j), bigv('e_w_out', j), a_vnorm[j:j + 1], a_ws[j], a_bs[j].T, b_onorm[j:j + 1],
                lb_all[i:i + 1])

    def gain(i, q):
        return row(vecs['norms'][i, q])

    def ffn_views(i, q):
        return bigv('ffn_wg', i, q), bigv('ffn_wu', i, q), bigv('ffn_wd', i, q)

    caches, vfirst = [], None
    for i in range(depth):
        c = {}
        h, c['f1'] = _ffn_fwd(h, gain(i, 0), *ffn_views(i, 0))
        if i % 2 == 0:
            h, c['mix'] = _even_fwd(h, gain(i, 1), *even_params(i))
        else:
            h, c['mix'], vfirst = _rwkv_fwd(h, gain(i, 1), rwkv_params(i // 2), vfirst if i // 2 > 0 else None)
        h, c['f2'] = _ffn_fwd(h, gain(i, 2), *ffn_views(i, 1))
        h, c['ple'] = _ple_fwd(h, gain(i, 3), pb[i], bigv('ple_wg', i), smv('ple_wp', i))
        caches.append(c)

    part, dh, dfinal = _loss_bwd(h, target, final_norm.reshape(1, D))
    big_entries = []
    sm = {n: {} for n in SMALL}
    rep = {n: {} for n in REP}
    dvfirst = None
    for i in reversed(range(depth)):
        j, c = i // 2, caches[i]
        dh, dg, dwgate, dwproj = _ple_bwd(dh, c['ple'], gain(i, 3), pb[i], bigv('ple_wg', i), smv('ple_wp', i))
        sm['norms'][(i, 3)] = dg
        big_entries += [('ple_wg', (i,), dwgate), ('ple_wp', (i,), dwproj)]
        for q, key in ((1, 'f2'), (0, 'f1')):
            if key == 'f1':
                if i % 2 == 0:
                    dh, G = _even_bwd(dh, c['mix'], gain(i, 1), *even_params(i))
                    big_entries += [('e_w_in', (j,), G['w_in']), ('e_w_out', (j,), G['w_out'])]
                    rep['a_vnorm'][j], rep['a_ws'][j], rep['a_bs'][j] = G['vgain'][0], G['ws'], G['bs_t'].T
                    rep['b_onorm'][j], rep['b_lb_logits'][i] = G['onorm'][0], G['lb'][0]
                else:
                    dh, G, dvf = _rwkv_bwd(dh, c['mix'], gain(i, 1), rwkv_params(j), dvfirst if j == 0 else None)
                    if dvf is not None:
                        dvfirst = dvf if dvfirst is None else dvfirst + dvf
                    big_entries += [('c_wr', (j,), G['wr']), ('c_wk', (j,), G['wk']), ('c_wv', (j,), G['wv']),
                                    ('c_wo', (j,), G['wo'])]
                    big_entries += [('c_' + n, (j,), G[n]) for n in ('w1', 'w2', 'a1', 'a2', 'g1', 'g2')]
                    for n in ('w0', 'a0', 'kk', 'ka', 'gn_g', 'gn_b'):
                        sm['c_' + n][(j,)] = G[n]
                    sm['c_mix'][(j,)] = jnp.concatenate(G['mix'], axis=0)
                    rep['c_rk'][j] = G['rk'].reshape(c_rk.shape[1:])
                    if j > 0:
                        sm['c_v0'][(j - 1,)] = G['v0']
                        big_entries += [('c_v1', (j - 1,), G['v1']), ('c_v2', (j - 1,), G['v2'])]
                sm['norms'][(i, 1)] = G['gain']
            dh, dg, dwg, dwu, dwd = _ffn_bwd(dh, c[key], gain(i, 2 * q), *ffn_views(i, q))
            sm['norms'][(i, 2 * q)] = dg
            big_entries += [('ffn_wg', (i, q), dwg), ('ffn_wu', (i, q), dwu), ('ffn_wd', (i, q), dwd)]
    grad_x = dh[None]

    def stacked(blocks, lead_shape):
        def rec(prefix, dims):
            if not dims:
                return blocks[prefix]
            return jnp.stack([rec(prefix + (q,), dims[1:]) for q in range(dims[0])], axis=0)
        return rec((), tuple(lead_shape))

    sm_shards = []
    for n in SMALL:
        blk = A[n].shape
        full = stacked(sm[n], blk[:-1] if n != 'c_mix' else blk[:-2])
        sm_shards.append(_vec_shards(full.reshape(blk[:-1] + (D,))))
    sm_pack = _pack(sm_shards, lead=1)
    dlb = jnp.stack([rep['b_lb_logits'].get(i, jnp.zeros((aw,), F32)) for i in range(depth)], axis=0)
    rep_grads = dict(final_norm=dfinal[0], a_vnorm=stacked({(k,): v for k, v in rep['a_vnorm'].items()}, a_vnorm.shape[:1]),
                     a_ws=stacked({(k,): v for k, v in rep['a_ws'].items()}, a_ws.shape[:1]),
                     a_bs=stacked({(k,): v for k, v in rep['a_bs'].items()}, a_bs.shape[:1]),
                     b_onorm=stacked({(k,): v for k, v in rep['b_onorm'].items()}, b_onorm.shape[:1]),
                     b_lb_logits=lb_vjp(dlb)[0],
                     c_rk=stacked({(k,): v for k, v in rep['c_rk'].items()}, c_rk.shape[:1]))
    rep_pack = _pack([rep_grads[n] for n in REP])
    return part, grad_x, big_entries, sm_pack, rep_pack
```

```python
import functools

import jax
import jax.numpy as jnp
from jax import lax
from jax.experimental import pallas as pl
from jax.experimental.pallas import tpu as pltpu

F32 = jnp.float32
BF16 = jnp.bfloat16
MESH = pl.DeviceIdType.MESH

LANES = 128
VMEM_LIMIT = 56 * 1024 * 1024
MM_VMEM_BUDGET = 36 * 1024 * 1024
N_CHIPS = 4

RMS_EPS = 1e-6
A_GROUP = 128
A_CHUNK = 128
B_HEAD = 128
B_MIN_F = 1e-30
C_HEAD = 64
C_GN_EPS = 64e-5
ADAM_LR = 0.001
ADAM_B1 = 0.9
ADAM_B2 = 0.999
ADAM_EPS = 1e-08
ADAM_WD = 0.01
ADAM_STEP = 10


def _sigmoid(x):
    return 1.0 / (1.0 + jnp.exp(-x))


def _silu(x):
    return x * _sigmoid(x)


def _dsilu(x):
    s = _sigmoid(x)
    return s * (1.0 + x * (1.0 - s))


_GELU_C = 0.7978845608028654


def _gelu(x):
    return 0.5 * x * (1.0 + jnp.tanh(_GELU_C * (x + 0.044715 * x * x * x)))


def _dgelu(x):
    th = jnp.tanh(_GELU_C * (x + 0.044715 * x * x * x))
    return 0.5 * (1.0 + th) + 0.5 * x * (1.0 - th * th) * _GELU_C * (1.0 + 3.0 * 0.044715 * x * x)


def _softplus(x):
    return jnp.maximum(x, 0.0) + jnp.log(1.0 + jnp.exp(-jnp.abs(x)))


def _seg_ones(seg):
    i = lax.broadcasted_iota(jnp.int32, (LANES, LANES), 0) // seg
    j = lax.broadcasted_iota(jnp.int32, (LANES, LANES), 1) // seg
    return jnp.where(i == j, 1.0, 0.0).astype(BF16)


def _segsum(x, seg):
    ones = _seg_ones(seg)
    outs = []
    for j in range(x.shape[1] // LANES):
        xb = x[:, j * LANES:(j + 1) * LANES]
        hi = xb.astype(BF16)
        r1 = xb - hi.astype(F32)
        mid = r1.astype(BF16)
        lo = (r1 - mid.astype(F32)).astype(BF16)
        acc = jnp.dot(hi, ones, preferred_element_type=F32)
        acc = acc + jnp.dot(mid, ones, preferred_element_type=F32)
        acc = acc + jnp.dot(lo, ones, preferred_element_type=F32)
        outs.append(acc)
    return outs[0] if len(outs) == 1 else jnp.concatenate(outs, axis=1)


def _rowwise(fn, name, rows, tm, ins, outs, accs=()):
    n_in, n_out, n_acc = len(ins), len(outs), len(accs)
    arrays, in_specs = [], []
    for spec in ins:
        kind, arr = spec[0], spec[1]
        arrays.append(arr)
        if kind == 'row':
            in_specs.append(pl.BlockSpec((tm, arr.shape[1]), lambda i: (i, 0)))
        elif kind == 'col':
            in_specs.append(pl.BlockSpec((tm, spec[2]), functools.partial(lambda i, cb: (i, cb), cb=spec[3])))
        else:
            in_specs.append(pl.BlockSpec(arr.shape, functools.partial(lambda i, nd: (0,) * nd, nd=arr.ndim)))
    out_shape = [jax.ShapeDtypeStruct((rows, w), dt) for (w, dt) in outs]
    out_specs = [pl.BlockSpec((tm, w), lambda i: (i, 0)) for (w, _) in outs]
    out_shape += [jax.ShapeDtypeStruct(s, F32) for s in accs]
    out_specs += [pl.BlockSpec(s, functools.partial(lambda i, nd: (0,) * nd, nd=len(s))) for s in accs]

    def body(*refs):
        vals = fn(*[r[...] for r in refs[:n_in]])
        if not isinstance(vals, (tuple, list)):
            vals = (vals,)
        for r, v in zip(refs[n_in:n_in + n_out], vals[:n_out]):
            r[...] = v.astype(r.dtype)
        if n_acc:
            acc_refs = refs[n_in + n_out:]

            @pl.when(pl.program_id(0) == 0)
            def _():
                for r in acc_refs:
                    r[...] = jnp.zeros(r.shape, F32)

            for r, v in zip(acc_refs, vals[n_out:]):
                r[...] += v

    res = pl.pallas_call(
        body, name=name, grid=(rows // tm,), in_specs=in_specs, out_specs=out_specs, out_shape=out_shape,
        compiler_params=pltpu.CompilerParams(
            dimension_semantics=("arbitrary",) if n_acc else ("parallel",), vmem_limit_bytes=VMEM_LIMIT),
    )(*arrays)
    return res


class WV:
    def __init__(self, arr, idx, kind):
        self.arr, self.idx, self.kind = arr, tuple(idx), kind
        self.ns = arr.shape[0]
        self.R, self.C = arr.shape[-2:]
        self.K = self.R * (self.ns if kind == 'row' else 1)
        self.N = self.C * (self.ns if kind == 'col' else 1)

    def spec(self, br, bc, rmap, cmap):
        lead = (None,) * (1 + len(self.idx))
        nrb, ncb = self.R // br, self.C // bc
        idx, kind = self.idx, self.kind

        def index_map(*g):
            ri, ci = rmap(*g), cmap(*g)
            if kind == 'row':
                return (ri // nrb, *idx, ri % nrb, ci)
            return (ci // ncb, *idx, ri, ci % ncb)

        return pl.BlockSpec(lead + (br, bc), index_map)


def _tile_options(n):
    return [n] + [d for d in range(n - LANES, LANES - 1, -LANES) if n % d == 0]


def _pick_tiles(opt_m, opt_n, opt_k, out_bytes, has_res, full_k):
    best, best_key = None, None
    for tm in opt_m:
        for tn in opt_n:
            for tk in opt_k:
                multi = tk != full_k
                est = 2 * (tm * tk * 2 + tk * tn * 2 + tm * tn * out_bytes) + tm * tn * 4 * (2 if multi else 1)
                if has_res:
                    est += 2 * tm * tn * 4
                if est > MM_VMEM_BUDGET:
                    continue
                key = (tm * tn * tk, tk)
                if best is None or key > best_key:
                    best, best_key = (tm, tn, tk), key
    return best


def _mm_call(name, dims, grid, in_specs, out_spec, out_shape, operands, nk, acc_shape, scale, has_res):
    def body(*refs):
        a_ref, b_ref = refs[0], refs[1]
        res_ref = refs[2] if has_res else None
        o_ref = refs[3] if has_res else refs[2]

        def finalize(acc):
            acc = acc * scale if scale != 1.0 else acc
            if has_res:
                acc = acc + res_ref[...]
            o_ref[...] = acc.astype(o_ref.dtype)

        part = lax.dot_general(a_ref[...], b_ref[...], dims, preferred_element_type=F32)
        if nk == 1:
            finalize(part)
        else:
            acc_ref = refs[-1]
            k = pl.program_id(2)

            @pl.when(k == 0)
            def _():
                acc_ref[...] = part

            @pl.when(k > 0)
            def _():
                acc_ref[...] += part

            @pl.when(k == nk - 1)
            def _():
                finalize(acc_ref[...])

    return pl.pallas_call(
        body, name=name, grid=grid, in_specs=in_specs, out_specs=out_spec, out_shape=out_shape,
        scratch_shapes=[pltpu.VMEM(acc_shape, F32)] if nk > 1 else [],
        compiler_params=pltpu.CompilerParams(
            dimension_semantics=("parallel", "parallel", "arbitrary"), vmem_limit_bytes=VMEM_LIMIT),
    )(*operands)


def _mm_nn(a, w, name, out_dtype=F32, res=None, scale=1.0):
    M, K = a.shape
    N = w.N
    opt_n = _tile_options(w.C)
    opt_k = _tile_options(w.R)
    tm, tn, tk = _pick_tiles(_tile_options(M), opt_n, opt_k, jnp.dtype(out_dtype).itemsize, res is not None, K)
    nk = K // tk
    in_specs = [pl.BlockSpec((tm, tk), lambda n, m, k: (m, k)),
                w.spec(tk, tn, lambda n, m, k: k, lambda n, m, k: n)]
    operands = [a, w.arr]
    if res is not None:
        in_specs.append(pl.BlockSpec((tm, tn), lambda n, m, k: (m, n)))
        operands.append(res)
    return _mm_call(name, (((1,), (0,)), ((), ())), (N // tn, M // tm, nk), in_specs,
                    pl.BlockSpec((tm, tn), lambda n, m, k: (m, n)), jax.ShapeDtypeStruct((M, N), out_dtype),
                    operands, nk, (tm, tn), scale, res is not None)


def _mm_nt(a, w, name, out_dtype=F32, res=None, scale=1.0):
    M, C = a.shape
    Ko = w.K
    opt_n = _tile_options(w.R)
    opt_k = _tile_options(w.C)
    tm, tn, tk = _pick_tiles(_tile_options(M), opt_n, opt_k, jnp.dtype(out_dtype).itemsize, res is not None, C)
    nk = C // tk
    in_specs = [pl.BlockSpec((tm, tk), lambda n, m, k: (m, k)),
                w.spec(tn, tk, lambda n, m, k: n, lambda n, m, k: k)]
    operands = [a, w.arr]
    if res is not None:
        in_specs.append(pl.BlockSpec((tm, tn), lambda n, m, k: (m, n)))
        operands.append(res)
    return _mm_call(name, (((1,), (1,)), ((), ())), (Ko // tn, M // tm, nk), in_specs,
                    pl.BlockSpec((tm, tn), lambda n, m, k: (m, n)), jax.ShapeDtypeStruct((M, Ko), out_dtype),
                    operands, nk, (tm, tn), scale, res is not None)


def _mm_tn(a, dy, like, name, out_dtype=BF16, scale=1.0):
    M, K = a.shape
    N = dy.shape[1]
    out = WV(jax.ShapeDtypeStruct((like.ns, like.R, like.C), out_dtype), (), like.kind)
    opt_m = _tile_options(like.R)
    opt_n = _tile_options(like.C)
    tko, tno, tc = _pick_tiles(opt_m, opt_n, _tile_options(M), jnp.dtype(out_dtype).itemsize, False, M)
    nk = M // tc
    in_specs = [pl.BlockSpec((tc, tko), lambda i, j, c: (c, i)),
                pl.BlockSpec((tc, tno), lambda i, j, c: (c, j))]
    return _mm_call(name, (((0,), (0,)), ((), ())), (K // tko, N // tno, nk), in_specs,
                    out.spec(tko, tno, lambda i, j, c: i, lambda i, j, c: j), out.arr,
                    [a, dy], nk, (tko, tno), scale, False)


SCAN_TB = 16


def _subsum(x):
    r = x.shape[0]
    while r > 8:
        r //= 2
        x = x[:r] + x[r:]
    while r > 1:
        r //= 2
        x = x + pltpu.roll(x, r, 0)
    return x


def _head_ones(n_heads):
    i = lax.broadcasted_iota(jnp.int32, (LANES, LANES), 0) % n_heads
    j = lax.broadcasted_iota(jnp.int32, (LANES, LANES), 1) % n_heads
    return jnp.where(i == j, 1.0, 0.0).astype(BF16)


def _lanesum_mxu(x, ones):
    hi = x.astype(BF16)
    r1 = x - hi.astype(F32)
    mid = r1.astype(BF16)
    lo = (r1 - mid.astype(F32)).astype(BF16)
    acc = jnp.dot(hi, ones, preferred_element_type=F32)
    acc = acc + jnp.dot(mid, ones, preferred_element_type=F32)
    return acc + jnp.dot(lo, ones, preferred_element_type=F32)


def _head_sel(n_heads):
    h = lax.broadcasted_iota(jnp.int32, (n_heads, LANES), 0)
    lane = lax.broadcasted_iota(jnp.int32, (n_heads, LANES), 1) % n_heads
    return jnp.where(h == lane, 1.0, 0.0).astype(BF16)


def _split3(x):
    hi = x.astype(BF16)
    r1 = x - hi.astype(F32)
    mid = r1.astype(BF16)
    return hi, mid, (r1 - mid.astype(F32)).astype(BF16)


def _nat_from_rows(rows, sel):
    dn = (((1,), (1,)), ((), ()))
    hi, mid, lo = _split3(rows)
    acc = lax.dot_general(sel, hi, dn, preferred_element_type=F32)
    acc = acc + lax.dot_general(sel, mid, dn, preferred_element_type=F32)
    return acc + lax.dot_general(sel, lo, dn, preferred_element_type=F32)


def _rows_of(vals):
    n = len(vals)
    if vals[0].shape[0] >= n:
        idx = lax.broadcasted_iota(jnp.int32, (n, LANES), 0)
        out = vals[0][:n]
        for i in range(1, n):
            out = jnp.where(idx == i, vals[i][:n], out)
        return out
    return jnp.concatenate([v[0:1] for v in vals], axis=0)


def _scan_fwd(name, n_heads, wk, kk, rk, vrow, ak=None, bk=None):
    S, R, _ = wk.shape
    dv = vrow.shape[1]
    ab = ak is not None
    tb = min(SCAN_TB, S)
    grp = min(8, dv)

    def body(*refs):
        if ab:
            w_ref, k_ref, r_ref, v_ref, a_ref, b_ref, y_ref, hist_ref, sa_ref, s_ref = refs
        else:
            w_ref, k_ref, r_ref, v_ref, y_ref, hist_ref, s_ref = refs

        @pl.when(pl.program_id(0) == 0)
        def _():
            s_ref[...] = jnp.zeros(s_ref.shape, F32)

        sel = _head_sel(n_heads)
        ones = _head_ones(n_heads) if ab else None

        def joined(tiles):
            return tiles[0] if len(tiles) == 1 else jnp.concatenate(tiles, axis=0)

        def step(t, carry):
            w, k, r = w_ref[t], k_ref[t], r_ref[t]
            if ab:
                a, b = a_ref[t], b_ref[t]
                tiles = []
                for g0 in range(0, dv, grp):
                    tiles.append(_rows_of([_subsum(s_ref[g0 + i] * a) for i in range(grp)]))
                sa_ref[t] = _lanesum_mxu(joined(tiles), ones)
            tiles = []
            for g0 in range(0, dv, grp):
                ys = []
                for i in range(grp):
                    v = g0 + i
                    st = s_ref[v]
                    hist_ref[t, v] = st
                    st = st * w + v_ref[t, pl.ds(v, 1), :] * k
                    if ab:
                        st = st + sa_ref[t, pl.ds(v, 1), :] * b
                    s_ref[v] = st
                    ys.append(_subsum(st * r))
                tiles.append(_rows_of(ys))
            y_ref[t] = _nat_from_rows(joined(tiles), sel)
            return carry

        lax.fori_loop(0, tb, step, 0, unroll=8)

    kspec = pl.BlockSpec((tb, R, LANES), lambda i: (i, 0, 0))
    nspec = pl.BlockSpec((tb, n_heads, dv), lambda i: (i, 0, 0))
    rspec = pl.BlockSpec((tb, dv, LANES), lambda i: (i, 0, 0))
    operands = [wk, kk, rk, vrow] + ([ak, bk] if ab else [])
    in_specs = [kspec, kspec, kspec, rspec] + ([kspec, kspec] if ab else [])
    out_shape = [jax.ShapeDtypeStruct((S, n_heads, dv), F32), jax.ShapeDtypeStruct((S, dv, R, LANES), F32)]
    out_specs = [nspec, pl.BlockSpec((tb, dv, R, LANES), lambda i: (i, 0, 0, 0))]
    if ab:
        out_shape.append(jax.ShapeDtypeStruct((S, dv, LANES), F32))
        out_specs.append(rspec)
    res = pl.pallas_call(
        body, name=name, grid=(S // tb,), in_specs=in_specs, out_specs=out_specs, out_shape=out_shape,
        scratch_shapes=[pltpu.VMEM((dv, R, LANES), F32)],
        compiler_params=pltpu.CompilerParams(dimension_semantics=("arbitrary",), vmem_limit_bytes=VMEM_LIMIT),
    )(*operands)
    return (res[0], res[1], res[2]) if ab else (res[0], res[1], None)


def _scan_bwd(name, n_heads, wk, kk, rk, vrow, hist, dyrow, ak=None, bk=None, sarow=None):
    S, R, _ = wk.shape
    dv = vrow.shape[1]
    ab = ak is not None
    tb = min(SCAN_TB, S)
    nb = S // tb
    grp = min(8, dv)

    def body(*refs):
        if ab:
            (w_ref, k_ref, r_ref, v_ref, hist_ref, dy_ref, a_ref, b_ref, sa_ref,
             dw_ref, dk_ref, dr_ref, dv_ref, da_ref, db_ref, ds_ref, dsa_ref) = refs
        else:
            (w_ref, k_ref, r_ref, v_ref, hist_ref, dy_ref,
             dw_ref, dk_ref, dr_ref, dv_ref, ds_ref) = refs

        @pl.when(pl.program_id(0) == 0)
        def _():
            ds_ref[...] = jnp.zeros(ds_ref.shape, F32)

        sel = _head_sel(n_heads)
        ones = _head_ones(n_heads) if ab else None

        def joined(tiles):
            return tiles[0] if len(tiles) == 1 else jnp.concatenate(tiles, axis=0)

        def step(j, carry):
            t = tb - 1 - j
            w, k, r = w_ref[t], k_ref[t], r_ref[t]
            zero = jnp.zeros((R, LANES), F32)
            u, dw, dk, da, db = zero, zero, zero, zero, zero
            vd = jnp.sum(v_ref[t] * dy_ref[t], axis=0, keepdims=True)
            sd = jnp.sum(sa_ref[t] * dy_ref[t], axis=0, keepdims=True) if ab else None
            if ab:
                a, b = a_ref[t], b_ref[t]
                tiles = []
                for g0 in range(0, dv, grp):
                    ps = []
                    for i in range(grp):
                        v = g0 + i
                        dst = ds_ref[v] + dy_ref[t, pl.ds(v, 1), :] * r
                        ds_ref[v] = dst
                        ps.append(_subsum(dst * b))
                    tiles.append(_rows_of(ps))
                dsa_ref[...] = _lanesum_mxu(joined(tiles), ones)
            tiles = []
            for g0 in range(0, dv, grp):
                dvs = []
                for i in range(grp):
                    v = g0 + i
                    sp = hist_ref[t, v]
                    dyr = dy_ref[t, pl.ds(v, 1), :]
                    vr = v_ref[t, pl.ds(v, 1), :]
                    dst = ds_ref[v] if ab else ds_ref[v] + dyr * r
                    u = u + sp * dyr
                    dw = dw + dst * sp
                    dk = dk + dst * vr
                    dvs.append(_subsum(dst * k))
                    if ab:
                        db = db + dst * sa_ref[t, pl.ds(v, 1), :]
                        dsa = dsa_ref[pl.ds(v, 1), :]
                        da = da + sp * dsa
                        dst = dst * w + dsa * a
                    else:
                        dst = dst * w
                    ds_ref[v] = dst
                tiles.append(_rows_of(dvs))
            dv_ref[t] = _nat_from_rows(joined(tiles), sel)
            dr = w * u + k * vd
            if ab:
                dr = dr + b * sd
                da_ref[t] = da
                db_ref[t] = db
            dw_ref[t] = dw
            dk_ref[t] = dk
            dr_ref[t] = dr
            return carry

        lax.fori_loop(0, tb, step, 0, unroll=2 if ab else 8)

    kspec = pl.BlockSpec((tb, R, LANES), lambda i: (nb - 1 - i, 0, 0))
    rspec = pl.BlockSpec((tb, dv, LANES), lambda i: (nb - 1 - i, 0, 0))
    hspec = pl.BlockSpec((tb, dv, R, LANES), lambda i: (nb - 1 - i, 0, 0, 0))
    nspec = pl.BlockSpec((tb, n_heads, dv), lambda i: (nb - 1 - i, 0, 0))
    operands = [wk, kk, rk, vrow, hist, dyrow] + ([ak, bk, sarow] if ab else [])
    in_specs = [kspec, kspec, kspec, rspec, hspec, rspec] + ([kspec, kspec, rspec] if ab else [])
    kshape = jax.ShapeDtypeStruct((S, R, LANES), F32)
    out_shape = [kshape, kshape, kshape, jax.ShapeDtypeStruct((S, n_heads, dv), F32)] + ([kshape, kshape] if ab else [])
    out_specs = [kspec, kspec, kspec, nspec] + ([kspec, kspec] if ab else [])
    return pl.pallas_call(
        body, name=name, grid=(nb,), in_specs=in_specs, out_specs=out_specs, out_shape=out_shape,
        scratch_shapes=[pltpu.VMEM((dv, R, LANES), F32)] + ([pltpu.VMEM((dv, LANES), F32)] if ab else []),
        compiler_params=pltpu.CompilerParams(dimension_semantics=("arbitrary",), vmem_limit_bytes=VMEM_LIMIT),
    )(*operands)


def _to_k(x, n_heads, dk):
    S = x.shape[0]
    kl = LANES // n_heads
    return x.reshape(S, n_heads, dk // kl, kl).transpose(0, 2, 3, 1).reshape(S, dk // kl, LANES)


def _from_k(x, n_heads, dk):
    S = x.shape[0]
    kl = LANES // n_heads
    return x.reshape(S, dk // kl, kl, n_heads).transpose(0, 3, 1, 2).reshape(S, n_heads * dk)


def _to_row(x, n_heads, dv):
    S = x.shape[0]
    return jnp.tile(x.reshape(S, n_heads, dv).transpose(0, 2, 1), (1, 1, LANES // n_heads))


def _tril_mask():
    t = lax.broadcasted_iota(jnp.int32, (A_CHUNK, A_CHUNK), 0)
    s = lax.broadcasted_iota(jnp.int32, (A_CHUNK, A_CHUNK), 1)
    return s <= t


def _gmlp_fwd(name, proj, aw, vgain, ws, bs_t):
    S = proj.shape[0]
    G = aw // A_GROUP

    def body(u_ref, v_ref, gain_ref, ws_ref, bs_ref, o_ref):
        mask = _tril_mask()
        lane = lax.broadcasted_iota(jnp.int32, (A_CHUNK, G), 1)
        bs = bs_ref[...]
        for g in range(G):
            seg = slice(g * A_GROUP, (g + 1) * A_GROUP)
            ua = _gelu(u_ref[:, seg])
            va = _gelu(v_ref[:, seg])
            rs = lax.rsqrt(jnp.mean(va * va, axis=-1, keepdims=True) + RMS_EPS)
            vg = (va * rs) * gain_ref[:, seg]
            wm = jnp.where(mask, ws_ref[g], 0.0).astype(BF16)
            bcol = jnp.sum(jnp.where(lane == g, bs, 0.0), axis=1, keepdims=True)
            s = jnp.dot(wm, vg.astype(BF16), preferred_element_type=F32) + bcol
            o_ref[:, seg] = (ua * s).astype(o_ref.dtype)

    return pl.pallas_call(
        body, name=name, grid=(S // A_CHUNK,),
        in_specs=[pl.BlockSpec((A_CHUNK, aw), lambda i: (i, 0)), pl.BlockSpec((A_CHUNK, aw), lambda i: (i, 1)),
                  pl.BlockSpec((1, aw), lambda i: (0, 0)), pl.BlockSpec((G, A_CHUNK, A_CHUNK), lambda i: (0, 0, 0)),
                  pl.BlockSpec((A_CHUNK, G), lambda i: (0, 0))],
        out_specs=pl.BlockSpec((A_CHUNK, aw), lambda i: (i, 0)),
        out_shape=jax.ShapeDtypeStruct((S, aw), BF16),
        compiler_params=pltpu.CompilerParams(dimension_semantics=("parallel",), vmem_limit_bytes=VMEM_LIMIT),
    )(proj, proj, vgain, ws, bs_t)


def _gmlp_bwd(name, proj, dout, dout_cb, aw, vgain, ws, bs_t):
    S = proj.shape[0]
    G = aw // A_GROUP

    def body(u_ref, v_ref, do_ref, gain_ref, ws_ref, bs_ref, du_ref, dv_ref, dws_ref, dbs_ref, dgain_ref):
        @pl.when(pl.program_id(0) == 0)
        def _():
            dws_ref[...] = jnp.zeros(dws_ref.shape, F32)
            dbs_ref[...] = jnp.zeros(dbs_ref.shape, F32)
            dgain_ref[...] = jnp.zeros(dgain_ref.shape, F32)

        mask = _tril_mask()
        lane = lax.broadcasted_iota(jnp.int32, (A_CHUNK, G), 1)
        bs = bs_ref[...]
        dbs = jnp.zeros((A_CHUNK, G), F32)
        for g in range(G):
            seg = slice(g * A_GROUP, (g + 1) * A_GROUP)
            u, v = u_ref[:, seg], v_ref[:, seg]
            do = do_ref[:, seg].astype(F32)
            ua, va = _gelu(u), _gelu(v)
            rs = lax.rsqrt(jnp.mean(va * va, axis=-1, keepdims=True) + RMS_EPS)
            xh = va * rs
            gain = gain_ref[:, seg]
            vg = (xh * gain).astype(BF16)
            wm = jnp.where(mask, ws_ref[g], 0.0).astype(BF16)
            bcol = jnp.sum(jnp.where(lane == g, bs, 0.0), axis=1, keepdims=True)
            s = jnp.dot(wm, vg, preferred_element_type=F32) + bcol
            du_ref[:, seg] = (do * s * _dgelu(u)).astype(du_ref.dtype)
            ds = do * ua
            dsb = ds.astype(BF16)
            dw = lax.dot_general(dsb, vg, (((1,), (1,)), ((), ())), preferred_element_type=F32)
            dws_ref[g] += jnp.where(mask, dw, 0.0)
            dbs = dbs + jnp.where(lane == g, jnp.sum(ds, axis=1, keepdims=True), 0.0)
            dvg = lax.dot_general(wm, dsb, (((0,), (0,)), ((), ())), preferred_element_type=F32)
            dgain_ref[:, seg] += jnp.sum(dvg * xh, axis=0, keepdims=True)
            dxh = dvg * gain
            dva = rs * (dxh - xh * jnp.mean(dxh * xh, axis=-1, keepdims=True))
            dv_ref[:, seg] = (dva * _dgelu(v)).astype(dv_ref.dtype)
        dbs_ref[...] += dbs

    return pl.pallas_call(
        body, name=name, grid=(S // A_CHUNK,),
        in_specs=[pl.BlockSpec((A_CHUNK, aw), lambda i: (i, 0)), pl.BlockSpec((A_CHUNK, aw), lambda i: (i, 1)),
                  pl.BlockSpec((A_CHUNK, aw), functools.partial(lambda i, cb: (i, cb), cb=dout_cb)),
                  pl.BlockSpec((1, aw), lambda i: (0, 0)), pl.BlockSpec((G, A_CHUNK, A_CHUNK), lambda i: (0, 0, 0)),
                  pl.BlockSpec((A_CHUNK, G), lambda i: (0, 0))],
        out_specs=[pl.BlockSpec((A_CHUNK, aw), lambda i: (i, 0)), pl.BlockSpec((A_CHUNK, aw), lambda i: (i, 0)),
                   pl.BlockSpec((G, A_CHUNK, A_CHUNK), lambda i: (0, 0, 0)), pl.BlockSpec((A_CHUNK, G), lambda i: (0, 0)),
                   pl.BlockSpec((1, aw), lambda i: (0, 0))],
        out_shape=[jax.ShapeDtypeStruct((S, aw), BF16), jax.ShapeDtypeStruct((S, aw), BF16),
                   jax.ShapeDtypeStruct((G, A_CHUNK, A_CHUNK), F32), jax.ShapeDtypeStruct((A_CHUNK, G), F32),
                   jax.ShapeDtypeStruct((1, aw), F32)],
        compiler_params=pltpu.CompilerParams(dimension_semantics=("arbitrary",), vmem_limit_bytes=VMEM_LIMIT),
    )(proj, proj, dout, vgain, ws, bs_t)


ANY = pl.BlockSpec(memory_space=pl.ANY)


def _place():
    return lax.axis_index("x"), lax.axis_index("y"), lax.axis_index("c")


def _other_chips(x, y):
    return [(1 - x, y), (x, 1 - y), (1 - x, 1 - y)]


def _remote(src, dst, send_sem, recv_sem, device):
    return pltpu.make_async_remote_copy(src_ref=src, dst_ref=dst, send_sem=send_sem, recv_sem=recv_sem,
                                        device_id=device, device_id_type=MESH)


def _comm_call(body, name, operands, out_shape, n_dma, n_local):
    return pl.pallas_call(
        body, name=name, in_specs=[ANY] * len(operands), out_specs=[ANY] * len(out_shape), out_shape=out_shape,
        scratch_shapes=[pltpu.SemaphoreType.DMA((n_dma,)), pltpu.SemaphoreType.DMA((n_dma,)),
                        pltpu.SemaphoreType.DMA((max(n_local, 1),))],
        compiler_params=pltpu.CompilerParams(has_side_effects=True),
    )(*operands)


def _all_gather(name, bigs, smalls):
    nb, n = len(bigs), len(bigs) + len(smalls)
    arrays = list(bigs) + list(smalls)

    def body(*refs):
        ins, outs = refs[:n], refs[n:2 * n]
        send_sems, recv_sems, local_sems = refs[2 * n:]
        x, y, c = _place()
        me = 2 * x + y
        chip_x, chip_y = (1 - x, y), (x, 1 - y)
        k_x, k_y, k_d = 2 * (1 - x) + y, 2 * x + (1 - y), 2 * (1 - x) + (1 - y)
        sibling = (x, y, 1 - c)
        started = []

        def go(src, dst, s, device):
            cp = _remote(src, dst, send_sems.at[s], recv_sems.at[s], device)
            cp.start()
            started.append(cp)

        def landed(ref, s):
            _remote(ref, ref, send_sems.at[s], recv_sems.at[s], sibling).wait_recv()

        for e in range(nb):
            half = ins[e].shape[0] // 2
            src = ins[e].at[pl.ds(c * half, half)]
            dst = outs[e].at[me, pl.ds(c * half, half)]
            go(src, dst, 8 * e, (*chip_x, c))
            go(src, dst, 8 * e + 1, (*chip_y, c))
        small = 8 * nb
        for e in range(nb, n):
            for j, chip in enumerate(_other_chips(x, y)):
                go(ins[e], outs[e].at[me], small + 3 * (e - nb) + j, (*chip, c))
        for e in range(nb):
            half = ins[e].shape[0] // 2
            q = half // 2
            from_x = outs[e].at[k_x, pl.ds(c * half, half)]
            landed(from_x, 8 * e)
            first = outs[e].at[k_x, pl.ds(c * half, q)]
            go(first, first, 8 * e + 2, (*chip_y, c))
            go(from_x, from_x, 8 * e + 4, sibling)
            from_y = outs[e].at[k_y, pl.ds(c * half, half)]
            landed(from_y, 8 * e + 1)
            second = outs[e].at[k_y, pl.ds(c * half + q, q)]
            go(second, second, 8 * e + 3, (*chip_x, c))
            go(from_y, from_y, 8 * e + 5, sibling)
        for e in range(nb):
            half = ins[e].shape[0] // 2
            q = half // 2
            first = outs[e].at[k_d, pl.ds(c * half, q)]
            landed(first, 8 * e + 2)
            go(first, first, 8 * e + 6, sibling)
            second = outs[e].at[k_d, pl.ds(c * half + q, q)]
            landed(second, 8 * e + 3)
            go(second, second, 8 * e + 7, sibling)
        for e in range(nb, n):
            for j, chip in enumerate(_other_chips(x, y)):
                landed(outs[e].at[2 * chip[0] + chip[1]], small + 3 * (e - nb) + j)
        for e in range(nb):
            half = ins[e].shape[0] // 2
            q = half // 2
            o = (1 - c) * half
            landed(outs[e].at[k_x, pl.ds(o, half)], 8 * e + 4)
            landed(outs[e].at[k_y, pl.ds(o, half)], 8 * e + 5)
            landed(outs[e].at[k_d, pl.ds(o, q)], 8 * e + 6)
            landed(outs[e].at[k_d, pl.ds(o + q, q)], 8 * e + 7)
        for cp in started:
            cp.wait_send()

    out_shape = [jax.ShapeDtypeStruct((N_CHIPS,) + a.shape, a.dtype) for a in arrays]
    gathered = _comm_call(body, name, arrays, out_shape, 8 * nb + 3 * len(smalls), 0)
    me = 2 * lax.axis_index("x") + lax.axis_index("y")
    return [lax.dynamic_update_slice(g, a[None], (me,) + (0,) * a.ndim) for g, a in zip(gathered, arrays)]


def _swap_halves(name, grads):
    n = len(grads)

    def body(*refs):
        ins, outs = refs[:n], refs[n:2 * n]
        send_sems, recv_sems, _ = refs[2 * n:]
        x, y, c = _place()
        cps = []
        for e in range(n):
            half = ins[e].shape[1] // 2
            src = ins[e].at[pl.ds(0, N_CHIPS), pl.ds((1 - c) * half, half)]
            cp = _remote(src, outs[e], send_sems.at[e], recv_sems.at[e], (x, y, 1 - c))
            cp.start()
            cps.append(cp)
        for cp in cps:
            cp.wait()

    out_shape = [jax.ShapeDtypeStruct((N_CHIPS, g.shape[1] // 2, g.shape[2]), g.dtype) for g in grads]
    return _comm_call(body, name, list(grads), out_shape, n, 0)


def _scatter_chips(name, parts):
    n = len(parts)

    def body(*refs):
        ins, outs = refs[:n], refs[n:2 * n]
        send_sems, recv_sems, local_sems = refs[2 * n:]
        x, y, c = _place()
        me = 2 * x + y
        cps = []
        for e in range(n):
            cp = pltpu.make_async_copy(ins[e].at[me], outs[e].at[3], local_sems.at[e])
            cp.start()
            cps.append(cp)
            for j, chip in enumerate(_other_chips(x, y)):
                cp = _remote(ins[e].at[2 * chip[0] + chip[1]], outs[e].at[j], send_sems.at[3 * e + j],
                             recv_sems.at[3 * e + j], (*chip, c))
                cp.start()
                cps.append(cp)
        for cp in cps:
            cp.wait()

    out_shape = [jax.ShapeDtypeStruct(p.shape, p.dtype) for p in parts]
    return _comm_call(body, name, list(parts), out_shape, 3 * n, n)


def _join_halves(name, halves, places, out_shapes):
    n, n_out = len(halves), len(out_shapes)

    def body(*refs):
        ins, outs = refs[:n], refs[n:n + n_out]
        send_sems, recv_sems, local_sems = refs[n + n_out:]
        x, y, c = _place()
        cps = []
        for e in range(n):
            o, idx = places[e]
            half = ins[e].shape[0]
            dst = outs[o].at[(*idx, pl.ds(c * half, half))]
            cp = _remote(ins[e], dst, send_sems.at[e], recv_sems.at[e], (x, y, 1 - c))
            cp.start()
            cps.append(cp)
        for e, cp in enumerate(cps):
            o, idx = places[e]
            half = ins[e].shape[0]
            landed = outs[o].at[(*idx, pl.ds((1 - c) * half, half))]
            cp.wait_send()
            _remote(ins[e], landed, send_sems.at[e], recv_sems.at[e], (x, y, 1 - c)).wait_recv()

    joined = list(_comm_call(body, name, list(halves), list(out_shapes), n, 0))
    c = lax.axis_index("c")
    for e, hv in enumerate(halves):
        o, idx = places[e]
        start = tuple(idx) + (c * hv.shape[0], 0)
        joined[o] = lax.dynamic_update_slice(joined[o], hv.reshape((1,) * len(idx) + hv.shape), start)
    return joined


def _spread_all(name, per_chip, everywhere):
    def body(pc_ref, ev_ref, pc_out, ev_out, send_sems, recv_sems, local_sems):
        x, y, c = _place()
        me = 4 * x + 2 * y + c
        cps = [pltpu.make_async_copy(pc_ref.at[2 * x + y], pc_out.at[me], local_sems.at[0]),
               pltpu.make_async_copy(ev_ref, ev_out.at[me], local_sems.at[1])]
        for f in range(1, 8):
            fx, fy, fc = f // 4, (f // 2) % 2, f % 2
            tx = 1 - x if fx else x
            ty = 1 - y if fy else y
            tc = 1 - c if fc else c
            cps.append(_remote(pc_ref.at[2 * tx + ty], pc_out.at[me], send_sems.at[2 * f], recv_sems.at[2 * f],
                               (tx, ty, tc)))
            cps.append(_remote(ev_ref, ev_out.at[me], send_sems.at[2 * f + 1], recv_sems.at[2 * f + 1],
                               (tx, ty, tc)))
        for cp in cps:
            cp.start()
        for cp in cps:
            cp.wait()

    out_shape = [jax.ShapeDtypeStruct((8,) + per_chip.shape[1:], F32), jax.ShapeDtypeStruct((8,) + everywhere.shape, F32)]
    return _comm_call(body, name, [per_chip, everywhere], out_shape, 16, 2)


def _row_tile(rows, width, itemsize, n_arrays):
    tm = 1
    while rows % (tm * 2) == 0 and (tm * 2) * width * itemsize * n_arrays * 2 <= 24 * 1024 * 1024 and tm * 2 <= 1024:
        tm *= 2
    return tm


def _add_own_half(name, grad, swapped, c_arr):
    ns, R, C = grad.shape
    half = R // 2
    th = _row_tile(half, C, 2, 3)
    g4 = grad.reshape(ns, 2, half, C)

    def body(c_ref, g_ref, s_ref, o_ref):
        o_ref[...] = (g_ref[...].astype(F32) + s_ref[...].astype(F32)).astype(o_ref.dtype)

    return pl.pallas_call(
        body, name=name,
        grid_spec=pltpu.PrefetchScalarGridSpec(
            num_scalar_prefetch=1, grid=(ns, half // th),
            in_specs=[pl.BlockSpec((None, None, th, C), lambda k, i, c_ref: (k, c_ref[0], i, 0)),
                      pl.BlockSpec((None, th, C), lambda k, i, c_ref: (k, i, 0))],
            out_specs=pl.BlockSpec((None, th, C), lambda k, i, c_ref: (k, i, 0))),
        out_shape=jax.ShapeDtypeStruct((ns, half, C), grad.dtype),
        compiler_params=pltpu.CompilerParams(dimension_semantics=("parallel", "parallel"), vmem_limit_bytes=VMEM_LIMIT),
    )(c_arr, g4, swapped)


def _sum_slots(name, slots, order, out_dtype=F32):
    n, rows, C = slots.shape
    th = _row_tile(rows, C, 4, n + 1)

    def body(s_ref, o_ref):
        acc = s_ref[order[0]].astype(F32)
        for k in order[1:]:
            acc = acc + s_ref[k].astype(F32)
        o_ref[...] = acc.astype(o_ref.dtype)

    return pl.pallas_call(
        body, name=name, grid=(rows // th,),
        in_specs=[pl.BlockSpec((n, th, C), lambda i: (0, i, 0))],
        out_specs=pl.BlockSpec((th, C), lambda i: (i, 0)),
        out_shape=jax.ShapeDtypeStruct((rows, C), out_dtype),
        compiler_params=pltpu.CompilerParams(dimension_semantics=("parallel",), vmem_limit_bytes=VMEM_LIMIT),
    )(slots)


def _adamw(name, g, w, m, v):
    rows, C = g.shape
    tm = _row_tile(rows, C, 4, 7)
    c1 = 1.0 - ADAM_B1 ** ADAM_STEP
    c2 = 1.0 - ADAM_B2 ** ADAM_STEP

    def fn(g, w, m, v):
        m = ADAM_B1 * m + (1.0 - ADAM_B1) * g
        v = ADAM_B2 * v + (1.0 - ADAM_B2) * (g * g)
        delta = -ADAM_LR * ((m / c1) / (jnp.sqrt(v / c2) + ADAM_EPS) + ADAM_WD * w)
        return delta, m, v

    return _rowwise(fn, name, rows, tm, [('row', g), ('row', w), ('row', m), ('row', v)],
                    [(C, F32), (C, F32), (C, F32)])


def _tm(S, width, n_arrays):
    return _row_tile(S, width, 4, n_arrays)


def _rms_fwd(name, h, gain, out_dtype):
    S, D = h.shape

    def fn(hb, g):
        r = lax.rsqrt(jnp.mean(hb * hb, axis=-1, keepdims=True) + RMS_EPS)
        return ((hb * r) * g,)

    return _rowwise(fn, name, S, _tm(S, D, 4), [('row', h), ('full', gain)], [(D, out_dtype)])[0]


def _rms_bwd(name, h_in, gain, dn, dh):
    S, D = h_in.shape

    def fn(hb, g, dnb, dhb):
        r = lax.rsqrt(jnp.mean(hb * hb, axis=-1, keepdims=True) + RMS_EPS)
        xh = hb * r
        dnb = dnb.astype(F32)
        dxh = dnb * g
        dx = r * (dxh - xh * jnp.mean(dxh * xh, axis=-1, keepdims=True))
        return dhb + dx, jnp.sum(dnb * xh, axis=0, keepdims=True)

    return _rowwise(fn, name, S, _tm(S, D, 8), [('row', h_in), ('full', gain), ('row', dn), ('row', dh)],
                    [(D, F32)], [(1, D)])


def _ffn_up(n, wg, wu):
    M, K = n.shape
    tn = wg.C
    tm = min(256, M)

    def body(a_ref, g_w, u_w, g_ref, u_ref, act_ref):
        a = a_ref[...]
        g = jnp.dot(a, g_w[...], preferred_element_type=F32)
        u = jnp.dot(a, u_w[...], preferred_element_type=F32)
        g_ref[...] = g.astype(BF16)
        u_ref[...] = u.astype(BF16)
        act_ref[...] = (_silu(g) * u).astype(BF16)

    out = pl.BlockSpec((tm, tn), lambda j, i: (i, j))
    shape = jax.ShapeDtypeStruct((M, wg.N), BF16)
    return pl.pallas_call(
        body, name="ffn_up_act", grid=(wg.N // tn, M // tm),
        in_specs=[pl.BlockSpec((tm, K), lambda j, i: (i, 0)),
                  wg.spec(K, tn, lambda j, i: 0, lambda j, i: j), wu.spec(K, tn, lambda j, i: 0, lambda j, i: j)],
        out_specs=[out, out, out], out_shape=[shape, shape, shape],
        compiler_params=pltpu.CompilerParams(dimension_semantics=("parallel", "parallel"), vmem_limit_bytes=VMEM_LIMIT),
    )(n, wg.arr, wu.arr)


def _ffn_fwd(h, gain, wg, wu, wd):
    n = _rms_fwd("rms_fwd_bf16", h, gain, BF16)
    g, u, act = _ffn_up(n, wg, wu)
    return _mm_nn(act, wd, "ffn_down", F32, res=h, scale=0.5), (h, n, g, u)


def _ffn_bwd(dh, cache, gain, wg, wu, wd):
    h, n, g, u = cache
    S, FF = g.shape
    dhb = dh.astype(BF16)
    dact = _mm_nt(dhb, wd, "ffn_dact", BF16, scale=0.5)

    def fn(gb, ub, db):
        gb, ub, db = gb.astype(F32), ub.astype(F32), db.astype(F32)
        sg = _sigmoid(gb)
        sl = gb * sg
        return db * ub * (sg * (1.0 + gb * (1.0 - sg))), db * sl, sl * ub

    dg, du, act = _rowwise(fn, "ffn_act_bwd", S, _tm(S, FF, 10), [('row', g), ('row', u), ('row', dact)],
                           [(FF, BF16), (FF, BF16), (FF, BF16)])
    dwd = _mm_tn(act, dhb, wd, "ffn_dwd", BF16, scale=0.5)
    dwg = _mm_tn(n, dg, wg, "ffn_dwup", BF16)
    dwu = _mm_tn(n, du, wu, "ffn_dwup", BF16)
    dn = _mm_nt(dg, wg, "ffn_dn", F32)
    dn = _mm_nt(du, wu, "ffn_dn_acc", F32, res=dn)
    dh2, dgain = _rms_bwd("rms_bwd", h, gain, dn, dh)
    return dh2, dgain, dwg, dwu, dwd


def _ple_fwd(h, gain, pb, wgate, wproj):
    S, D = h.shape
    n = _rms_fwd("rms_fwd_bf16", h, gain, BF16)
    pre = _mm_nn(n, wgate, "ple_gate", F32)
    e = _mm_nn(pb, wproj, "ple_proj", F32)

    def fn(hb, pr, eb):
        return (hb + _sigmoid(pr) * eb,)

    h2 = _rowwise(fn, "ple_add", S, _tm(S, D, 6), [('row', h), ('row', pre), ('row', e)], [(D, F32)])[0]
    return h2, (h, n, pre, e)


def _ple_bwd(dh, cache, gain, pb, wgate, wproj):
    h, n, pre, e = cache
    S, D = h.shape

    def fn(db, pr, eb):
        gt = _sigmoid(pr)
        return db * eb * gt * (1.0 - gt), db * gt

    dpre, de = _rowwise(fn, "ple_bwd", S, _tm(S, D, 6), [('row', dh), ('row', pre), ('row', e)],
                        [(D, BF16), (D, BF16)])
    dwproj = _mm_tn(pb, de, wproj, "ple_dwproj", BF16)
    dwgate = _mm_tn(n, dpre, wgate, "ple_dwgate", BF16)
    dn = _mm_nt(dpre, wgate, "ple_dn", F32)
    dh2, dgain = _rms_bwd("rms_bwd", h, gain, dn, dh)
    return dh2, dgain, dwgate, dwproj


def _even_fwd(h, gain, w_in, w_out, vgain, ws, bs_t, onorm, lb):
    S, D = h.shape
    aw = D // 2
    nh = aw // B_HEAD
    hn = _rms_fwd("rms_fwd_bf16", h, gain, BF16)
    proj = _mm_nn(hn, w_in, "even_in", F32)
    a_out = _gmlp_fwd("gmlp_fwd", proj, aw, vgain, ws, bs_t)

    def pre(bq, bf, lbv):
        f = lbv + (1.0 - lbv) * _sigmoid(bf)
        return _silu(bq), jnp.maximum(f, B_MIN_F), 1.0 - f

    q, w, k = _rowwise(pre, "hgrn_pre", S, _tm(S, aw, 8), [('col', proj, aw, 2), ('col', proj, aw, 3), ('full', lb)],
                       [(aw, F32), (aw, F32), (aw, F32)])
    wk, kk, qk = _to_k(w, nh, B_HEAD), _to_k(k, nh, B_HEAD), _to_k(q, nh, B_HEAD)
    vrow = _to_row(proj[:, 4 * aw:5 * aw], nh, B_HEAD)
    ynat, hist, _ = _scan_fwd("hgrn_scan_fwd", nh, wk, kk, qk, vrow)
    o = ynat.reshape(S, aw)

    def post(ob, bg, on):
        rs = lax.rsqrt(_segsum(ob * ob, B_HEAD) * (1.0 / B_HEAD) + RMS_EPS)
        return ((ob * rs * on) * _silu(bg),)

    b_out = _rowwise(post, "hgrn_post", S, _tm(S, aw, 8), [('row', o), ('col', proj, aw, 5), ('full', onorm)],
                     [(aw, BF16)])[0]
    cat = jnp.concatenate([a_out, b_out], axis=1)
    h2 = _mm_nn(cat, w_out, "even_out", F32, res=h)
    return h2, (h, hn, proj, wk, kk, qk, vrow, hist, o, cat)


def _even_bwd(dh, cache, gain, w_in, w_out, vgain, ws, bs_t, onorm, lb):
    h, hn, proj, wk, kk, qk, vrow, hist, o, cat = cache
    S, D = h.shape
    aw = D // 2
    nh = aw // B_HEAD
    dhb = dh.astype(BF16)
    dw_out = _mm_tn(cat, dhb, w_out, "even_dwout", BF16)
    dcat = _mm_nt(dhb, w_out, "even_dcat", F32)

    def post_bwd(ob, bg, on, db):
        rs = lax.rsqrt(_segsum(ob * ob, B_HEAD) * (1.0 / B_HEAD) + RMS_EPS)
        xh = ob * rs
        dy = db * _silu(bg)
        dbg = db * (xh * on) * _dsilu(bg)
        dxh = dy * on
        do = rs * (dxh - xh * (_segsum(dxh * xh, B_HEAD) * (1.0 / B_HEAD)))
        return do, dbg, jnp.sum(dy * xh, axis=0, keepdims=True)

    do, dbg, donorm = _rowwise(post_bwd, "hgrn_post_bwd", S, _tm(S, aw, 10),
                               [('row', o), ('col', proj, aw, 5), ('full', onorm), ('col', dcat, aw, 1)],
                               [(aw, F32), (aw, BF16)], [(1, aw)])
    dwk, dkk, dqk, dvnat = _scan_bwd("hgrn_scan_bwd", nh, wk, kk, qk, vrow, hist, _to_row(do, nh, B_HEAD))
    dq, dw, dk = _from_k(dqk, nh, B_HEAD), _from_k(dwk, nh, B_HEAD), _from_k(dkk, nh, B_HEAD)
    dbi = dvnat.reshape(S, aw).astype(BF16)

    def pre_bwd(bq, bf, lbv, dqb, dwb, dkb):
        sig = _sigmoid(bf)
        f = lbv + (1.0 - lbv) * sig
        df = jnp.where(f > B_MIN_F, dwb, 0.0) - dkb
        return dqb * _dsilu(bq), df * (1.0 - lbv) * sig * (1.0 - sig), jnp.sum(df * (1.0 - sig), axis=0, keepdims=True)

    dbq, dbf, dlb = _rowwise(pre_bwd, "hgrn_pre_bwd", S, _tm(S, aw, 12),
                             [('col', proj, aw, 2), ('col', proj, aw, 3), ('full', lb), ('row', dq), ('row', dw),
                              ('row', dk)], [(aw, BF16), (aw, BF16)], [(1, aw)])
    dau, dav, dws, dbs_t, dvgain = _gmlp_bwd("gmlp_bwd", proj, dcat, 0, aw, vgain, ws, bs_t)
    dproj = jnp.concatenate([dau, dav, dbq, dbf, dbi, dbg], axis=1)
    dw_in = _mm_tn(hn, dproj, w_in, "even_dwin", BF16)
    dn = _mm_nt(dproj, w_in, "even_dn", F32)
    dh2, dgain = _rms_bwd("rms_bwd", h, gain, dn, dh)
    return dh2, dict(gain=dgain, w_in=dw_in, w_out=dw_out, vgain=dvgain, ws=dws, bs_t=dbs_t, onorm=donorm, lb=dlb)


def _rwkv_prep(r, k, v0, wpl, apl, w0, a0, kkg, kag, svl=None, vf=None, v0p=None):
    wp = w0 + wpl
    w = -_softplus(-wp) - 0.5
    ew = jnp.exp(w)
    decay = jnp.exp(-ew)
    a = _sigmoid(a0 + apl)
    if svl is not None:
        sv = _sigmoid(v0p + svl)
        v = v0 + (vf - v0) * sv
    else:
        sv, v = None, v0
    kkp = k * kkg
    nrm = jnp.sqrt(_segsum(kkp * kkp, C_HEAD))
    inv = 1.0 / jnp.maximum(nrm, 1e-12)
    kk = kkp * inv
    k2 = k * (1.0 + (a - 1.0) * kag)
    return dict(wp=wp, ew=ew, decay=decay, a=a, sv=sv, v=v, kkp=kkp, nrm=nrm, inv=inv, kk=kk, k2=k2)


def _rwkv_post(y, r, k2, v, gn_g, gn_b, rk):
    mu = _segsum(y, C_HEAD) * (1.0 / C_HEAD)
    yc = y - mu
    rstd = lax.rsqrt(_segsum(yc * yc, C_HEAD) * (1.0 / C_HEAD) + C_GN_EPS)
    yh = yc * rstd
    s = _segsum(r * k2 * rk, C_HEAD)
    return yh, rstd, s, yh * gn_g + gn_b + s * v


def _rwkv_fwd(h, gain, P, vfirst):
    S, D = h.shape
    nh = D // C_HEAD
    vres = vfirst is not None
    hn = _rms_fwd("rms_fwd_f32", h, gain, F32)
    hs = jnp.concatenate([jnp.zeros((1, D), F32), hn[:-1]], axis=0)

    def mixf(x, xs, m0, m1, m2, m3, m4, m5):
        xx = xs - x
        return tuple(x + xx * m for m in (m0, m1, m2, m3, m4, m5))

    xr, xw, xk, xv, xa, xg = _rowwise(mixf, "rwkv_mix", S, _tm(S, D, 8),
                                      [('row', hn), ('row', hs)] + [('full', m) for m in P['mix']], [(D, BF16)] * 6)
    r = _mm_nn(xr, P['wr'], "rwkv_proj", F32)
    k = _mm_nn(xk, P['wk'], "rwkv_proj", F32)
    v0 = _mm_nn(xv, P['wv'], "rwkv_proj", F32)
    lw1 = _mm_nn(xw, P['w1'], "rwkv_lora_in", F32)
    la1 = _mm_nn(xa, P['a1'], "rwkv_lora_in", F32)
    lg1 = _mm_nn(xg, P['g1'], "rwkv_lora_in_g", F32)
    ins = [('row', lw1), ('row', la1), ('row', lg1)]
    outs = [(lw1.shape[1], BF16), (la1.shape[1], BF16), (lg1.shape[1], BF16)]
    if vres:
        lv1 = _mm_nn(xv, P['v1'], "rwkv_lora_in_v", F32)
        ins.append(('row', lv1))
        outs.append((lv1.shape[1], BF16))

    def lora_act(*xs):
        res = [jnp.tanh(xs[0]), xs[1], _sigmoid(xs[2])]
        return tuple(res + list(xs[3:]))

    acts = _rowwise(lora_act, "rwkv_lora_act", S, _tm(S, 1024, 4), ins, outs)
    tw, la1b, sg = acts[0], acts[1], acts[2]
    wpl = _mm_nn(tw, P['w2'], "rwkv_lora_out", F32)
    apl = _mm_nn(la1b, P['a2'], "rwkv_lora_out", F32)
    g = _mm_nn(sg, P['g2'], "rwkv_lora_out_g", F32)
    prep_ins = [('row', r), ('row', k), ('row', v0), ('row', wpl), ('row', apl),
                ('full', P['w0']), ('full', P['a0']), ('full', P['kk']), ('full', P['ka'])]
    svl = lv1b = None
    if vres:
        lv1b = acts[3]
        svl = _mm_nn(lv1b, P['v2'], "rwkv_lora_out_v", F32)
        prep_ins += [('row', svl), ('row', vfirst), ('full', P['v0'])]

    def prep(*xs):
        q = _rwkv_prep(*xs)
        return q['decay'], q['k2'], q['v'], -q['kk'], q['kk'] * q['a']

    decay, k2, v, av, bv = _rowwise(prep, "rwkv_prep", S, _tm(S, D, 24), prep_ins, [(D, F32)] * 5)
    tk = functools.partial(_to_k, n_heads=nh, dk=C_HEAD)
    wk_, kk_, rk_, ak_, bk_ = tk(decay), tk(k2), tk(r), tk(av), tk(bv)
    vrow = _to_row(v, nh, C_HEAD)
    ynat, hist, sarow = _scan_fwd("rwkv_scan_fwd", nh, wk_, kk_, rk_, vrow, ak_, bk_)
    y = ynat.reshape(S, D)

    def post(yb, rb, k2b, vb, gb, gn_g, gn_b, rkf):
        return (_rwkv_post(yb, rb, k2b, vb, gn_g, gn_b, rkf)[3] * gb,)

    zg = _rowwise(post, "rwkv_post", S, _tm(S, D, 16),
                  [('row', y), ('row', r), ('row', k2), ('row', v), ('row', g),
                   ('full', P['gn_g']), ('full', P['gn_b']), ('full', P['rk'])], [(D, BF16)])[0]
    h2 = _mm_nn(zg, P['wo'], "rwkv_out", F32, res=h)
    cache = dict(h=h, hn=hn, hs=hs, x=(xr, xw, xk, xv, xa, xg), r=r, k=k, v0=v0, tw=tw, la1b=la1b, sg=sg, lv1b=lv1b,
                 wpl=wpl, apl=apl, svl=svl, g=g, k2=k2, v=v, scan=(wk_, kk_, rk_, vrow, ak_, bk_, hist, sarow), y=y,
                 zg=zg, vfirst=vfirst)
    return h2, cache, (v if not vres else vfirst)


def _rwkv_bwd(dh, cache, gain, P, dvfirst_in):
    c = cache
    h = c['h']
    S, D = h.shape
    nh = D // C_HEAD
    vres = c['vfirst'] is not None
    xr, xw, xk, xv, xa, xg = c['x']
    dhb = dh.astype(BF16)
    dwo = _mm_tn(c['zg'], dhb, P['wo'], "rwkv_dwo", BF16)
    dzg = _mm_nt(dhb, P['wo'], "rwkv_dzg", F32)

    def post_bwd(dzgb, yb, rb, k2b, vb, gb, gn_g, gn_b, rkf):
        yh, rstd, s, z = _rwkv_post(yb, rb, k2b, vb, gn_g, gn_b, rkf)
        dz = dzgb * gb
        dyh = dz * gn_g
        m1 = _segsum(dyh, C_HEAD) * (1.0 / C_HEAD)
        m2 = _segsum(dyh * yh, C_HEAD) * (1.0 / C_HEAD)
        dy = rstd * (dyh - m1 - yh * m2)
        ds = _segsum(dz * vb, C_HEAD)
        return (dy, dzgb * z, ds * k2b * rkf, ds * rb * rkf, dz * s,
                jnp.sum(dz * yh, axis=0, keepdims=True), jnp.sum(dz, axis=0, keepdims=True),
                jnp.sum(ds * rb * k2b, axis=0, keepdims=True))

    dy, dgb, dr_b, dk2_b, dv_b, dgn_g, dgn_b, drk = _rowwise(
        post_bwd, "rwkv_post_bwd", S, _tm(S, D, 28),
        [('row', dzg), ('row', c['y']), ('row', c['r']), ('row', c['k2']), ('row', c['v']), ('row', c['g']),
         ('full', P['gn_g']), ('full', P['gn_b']), ('full', P['rk'])],
        [(D, F32), (D, BF16), (D, F32), (D, F32), (D, F32)], [(1, D)] * 3)
    wk_, kk_, rk_, vrow, ak_, bk_, hist, sarow = c['scan']
    dwk, dkk, drk_s, dvnat, dak, dbk = _scan_bwd("rwkv_scan_bwd", nh, wk_, kk_, rk_, vrow, hist,
                                                 _to_row(dy, nh, C_HEAD), ak_, bk_, sarow)
    fk = functools.partial(_from_k, n_heads=nh, dk=C_HEAD)
    ddecay, dk2_s, dr_s, dA, dB = fk(dwk), fk(dkk), fk(drk_s), fk(dak), fk(dbk)
    dv_s = dvnat.reshape(S, D)
    dr_t, dk2_t, dv_t = dr_s + dr_b, dk2_s + dk2_b, dv_s + dv_b
    if dvfirst_in is not None:
        dv_t = dv_t + dvfirst_in
    ins = [('row', c['r']), ('row', c['k']), ('row', c['v0']), ('row', c['wpl']), ('row', c['apl']),
           ('full', P['w0']), ('full', P['a0']), ('full', P['kk']), ('full', P['ka'])]
    if vres:
        ins += [('row', c['svl']), ('row', c['vfirst']), ('full', P['v0'])]
    n_fwd = len(ins)
    ins += [('row', t) for t in (dr_t, ddecay, dk2_t, dv_t, dA, dB)]

    def prep_bwd(*xs):
        q = _rwkv_prep(*xs[:n_fwd])
        kb, kkg, kag = xs[1], xs[7], xs[8]
        dr, ddec, dk2, dv, dav, dbv = xs[n_fwd:]
        a, kk, kkp, inv = q['a'], q['kk'], q['kkp'], q['inv']
        dkk = dbv * a - dav
        da = dbv * kk + dk2 * kb * kag
        dk = dk2 * (1.0 + (a - 1.0) * kag)
        pr = _segsum(dkk * kkp, C_HEAD)
        dkkp = dkk * inv - jnp.where(q['nrm'] > 1e-12, kkp * pr * inv * inv * inv, 0.0)
        dk = dk + dkkp * kkg
        dap = da * a * (1.0 - a)
        dwp = ddec * q['decay'] * (-q['ew']) * _sigmoid(-q['wp'])
        outs = [dr, dk]
        accs = [jnp.sum(dwp, axis=0, keepdims=True), jnp.sum(dap, axis=0, keepdims=True),
                jnp.sum(dkkp * kb, axis=0, keepdims=True), jnp.sum(dk2 * kb * (a - 1.0), axis=0, keepdims=True)]
        if vres:
            v0b, vfb, sv = xs[2], xs[10], q['sv']
            dsvp = dv * (vfb - v0b) * sv * (1.0 - sv)
            outs += [dv * (1.0 - sv), dwp, dap, dsvp, dv * sv]
            accs.append(jnp.sum(dsvp, axis=0, keepdims=True))
        else:
            outs += [dv, dwp, dap]
        return tuple(outs + accs)

    outs = [(D, BF16)] * 5 + ([(D, BF16), (D, F32)] if vres else [])
    res = _rowwise(prep_bwd, "rwkv_prep_bwd", S, _tm(S, D, 40), ins, outs, [(1, D)] * (5 if vres else 4))
    drb, dkb, dv0b, dwpb, dapb = res[:5]
    dvfirst_out = res[6] if vres else None
    accs = res[len(outs):]
    G = dict(wo=dwo, w0=accs[0], a0=accs[1], kk=accs[2], ka=accs[3], gn_g=dgn_g, gn_b=dgn_b, rk=drk)
    G['w2'] = _mm_tn(c['tw'], dwpb, P['w2'], "rwkv_dlora_out", BF16)
    G['a2'] = _mm_tn(c['la1b'], dapb, P['a2'], "rwkv_dlora_out", BF16)
    G['g2'] = _mm_tn(c['sg'], dgb, P['g2'], "rwkv_dlora_out_g", BF16)
    dtw = _mm_nt(dwpb, P['w2'], "rwkv_dlora_mid", F32)
    dla1 = _mm_nt(dapb, P['a2'], "rwkv_dlora_mid", F32)
    dsg = _mm_nt(dgb, P['g2'], "rwkv_dlora_mid_g", F32)
    ins = [('row', dtw), ('row', c['tw']), ('row', dla1), ('row', dsg), ('row', c['sg'])]
    outs = [(dtw.shape[1], BF16), (dla1.shape[1], BF16), (dsg.shape[1], BF16)]
    if vres:
        dsvpb = res[5]
        G['v0'] = accs[4]
        G['v2'] = _mm_tn(c['lv1b'], dsvpb, P['v2'], "rwkv_dlora_out_v", BF16)
        dlv1 = _mm_nt(dsvpb, P['v2'], "rwkv_dlora_mid_v", F32)
        ins.append(('row', dlv1))
        outs.append((dlv1.shape[1], BF16))

    def lora_act_bwd(dtwb, twb, dla1b_, dsgb, sgb, *rest):
        twb, sgb = twb.astype(F32), sgb.astype(F32)
        return tuple([dtwb * (1.0 - twb * twb), dla1b_, dsgb * sgb * (1.0 - sgb)] + list(rest))

    acts = _rowwise(lora_act_bwd, "rwkv_lora_act_bwd", S, _tm(S, 1024, 6), ins, outs)
    dlw1b, dla1b, dlg1b = acts[0], acts[1], acts[2]
    G['w1'] = _mm_tn(xw, dlw1b, P['w1'], "rwkv_dlora_in", BF16)
    G['a1'] = _mm_tn(xa, dla1b, P['a1'], "rwkv_dlora_in", BF16)
    G['g1'] = _mm_tn(xg, dlg1b, P['g1'], "rwkv_dlora_in_g", BF16)
    G['wr'] = _mm_tn(xr, drb, P['wr'], "rwkv_dwproj", BF16)
    G['wk'] = _mm_tn(xk, dkb, P['wk'], "rwkv_dwproj", BF16)
    G['wv'] = _mm_tn(xv, dv0b, P['wv'], "rwkv_dwproj", BF16)
    dxw = _mm_nt(dlw1b, P['w1'], "rwkv_dx_lora", F32)
    dxa = _mm_nt(dla1b, P['a1'], "rwkv_dx_lora", F32)
    dxg = _mm_nt(dlg1b, P['g1'], "rwkv_dx_lora_g", F32)
    dxr = _mm_nt(drb, P['wr'], "rwkv_dx", F32)
    dxk = _mm_nt(dkb, P['wk'], "rwkv_dx", F32)
    dxv = _mm_nt(dv0b, P['wv'], "rwkv_dx", F32)
    if vres:
        G['v1'] = _mm_tn(xv, acts[3], P['v1'], "rwkv_dlora_in_v", BF16)
        dxv = _mm_nt(acts[3], P['v1'], "rwkv_dx_lora_v", F32, res=dxv)

    def mix_bwd(x, xs, m0, m1, m2, m3, m4, m5, d0, d1, d2, d3, d4, d5):
        xx = xs - x
        ds_ = (d0, d1, d2, d3, d4, d5)
        dxx = d0 * m0 + d1 * m1 + d2 * m2 + d3 * m3 + d4 * m4 + d5 * m5
        dsum = d0 + d1 + d2 + d3 + d4 + d5
        return tuple([dsum - dxx, dxx] + [jnp.sum(d * xx, axis=0, keepdims=True) for d in ds_])

    res = _rowwise(mix_bwd, "rwkv_mix_bwd", S, _tm(S, D, 24),
                   [('row', c['hn']), ('row', c['hs'])] + [('full', m) for m in P['mix']]
                   + [('row', d) for d in (dxr, dxw, dxk, dxv, dxa, dxg)], [(D, F32), (D, F32)], [(1, D)] * 6)
    dx_here, dxs = res[0], res[1]
    G['mix'] = res[2:]
    dhn = dx_here + jnp.concatenate([dxs[1:], jnp.zeros((1, D), F32)], axis=0)
    dh2, G['gain'] = _rms_bwd("rms_bwd", h, gain, dhn, dh)
    return dh2, G, dvfirst_out


def _loss_bwd(h, target, gain):
    S, D = h.shape

    def fn(hb, tb, g):
        r = lax.rsqrt(jnp.mean(hb * hb, axis=-1, keepdims=True) + RMS_EPS)
        xh = hb * r
        e = xh * g - tb
        dy = e * (1.0 / D)
        dxh = dy * g
        dx = r * (dxh - xh * jnp.mean(dxh * xh, axis=-1, keepdims=True))
        part = jnp.sum(jnp.sum(e * e, axis=-1, keepdims=True), axis=0, keepdims=True) * (0.5 / D)
        return dx, jnp.sum(dy * xh, axis=0, keepdims=True), jnp.broadcast_to(part, (1, LANES))

    dh, dgain, part = _rowwise(fn, "loss", S, _tm(S, D, 8), [('row', h), ('row', target), ('full', gain)],
                               [(D, F32)], [(1, D), (1, LANES)])
    return part[0, 0], dh, dgain


WEIGHTS = ['norms', 'final_norm', 'ffn_wg', 'ffn_wu', 'ffn_wd', 'ple_wp', 'ple_wg', 'e_w_in', 'e_w_out', 'a_vnorm',
           'a_ws', 'a_bs', 'b_onorm', 'b_lb_logits', 'c_mix', 'c_wr', 'c_wk', 'c_wv', 'c_wo', 'c_w0', 'c_w1', 'c_w2',
           'c_a0', 'c_a1', 'c_a2', 'c_g1', 'c_g2', 'c_kk', 'c_ka', 'c_rk', 'c_gn_g', 'c_gn_b', 'c_v0', 'c_v1', 'c_v2']
BIG = {'ffn_wg': 'col', 'ffn_wu': 'col', 'ffn_wd': 'row', 'ple_wg': 'row', 'e_w_in': 'col', 'e_w_out': 'row',
       'c_wr': 'row', 'c_wk': 'row', 'c_wv': 'row', 'c_wo': 'row', 'ple_wp': 'col', 'c_w1': 'row', 'c_w2': 'col',
       'c_a1': 'row', 'c_a2': 'col', 'c_g1': 'row', 'c_g2': 'col', 'c_v1': 'row', 'c_v2': 'col'}
SMALL = ['norms', 'c_mix', 'c_w0', 'c_a0', 'c_kk', 'c_ka', 'c_gn_g', 'c_gn_b', 'c_v0']
REP = ['final_norm', 'a_vnorm', 'a_ws', 'a_bs', 'b_onorm', 'b_lb_logits', 'c_rk']
PACK_QUANTUM = 1024 * LANES


def _pack(arrays, lead=0):
    lead_shape = arrays[0].shape[:lead]
    flat = jnp.concatenate([a.reshape(lead_shape + (-1,)).astype(F32) for a in arrays], axis=-1)
    pad = (-flat.shape[-1]) % PACK_QUANTUM
    if pad:
        flat = jnp.concatenate([flat, jnp.zeros(lead_shape + (pad,), F32)], axis=-1)
    return flat.reshape(lead_shape + (-1, LANES))


def _unpack(packed, names, shapes, lead=0):
    lead_shape = packed.shape[:lead]
    flat = packed.reshape(lead_shape + (-1,))
    out, off = {}, 0
    for n, s in zip(names, shapes):
        size = 1
        for d in s:
            size *= d
        out[n] = flat[..., off:off + size].reshape(lead_shape + tuple(s))
        off += size
    return out


def _full_vec(g):
    return jnp.moveaxis(g, 0, -2).reshape(g.shape[1:-1] + (N_CHIPS * g.shape[-1],))


def _vec_shards(g):
    return jnp.moveaxis(g.reshape(g.shape[:-1] + (N_CHIPS, g.shape[-1] // N_CHIPS)), -2, 0)


def _lower_bounds(logits):
    probs = jax.nn.softmax(logits.astype(F32), axis=0)
    return jnp.cumsum(probs, axis=0) - probs[0]


def kernel(x, p, norms, final_norm, ffn_wg, ffn_wu, ffn_wd, ple_wp, ple_wg, e_w_in, e_w_out, a_vnorm, a_ws, a_bs, b_onorm, b_lb_logits, c_mix, c_wr, c_wk, c_wv, c_wo, c_w0, c_w1, c_w2, c_a0, c_a1, c_a2, c_g1, c_g2, c_kk, c_ka, c_rk, c_gn_g, c_gn_b, c_v0, c_v1, c_v2, loss_target, m_norms, m_final_norm, m_ffn_wg, m_ffn_wu, m_ffn_wd, m_ple_wp, m_ple_wg, m_e_w_in, m_e_w_out, m_a_vnorm, m_a_ws, m_a_bs, m_b_onorm, m_b_lb_logits, m_c_mix, m_c_wr, m_c_wk, m_c_wv, m_c_wo, m_c_w0, m_c_w1, m_c_w2, m_c_a0, m_c_a1, m_c_a2, m_c_g1, m_c_g2, m_c_kk, m_c_ka, m_c_rk, m_c_gn_g, m_c_gn_b, m_c_v0, m_c_v1, m_c_v2, v_norms, v_final_norm, v_ffn_wg, v_ffn_wu, v_ffn_wd, v_ple_wp, v_ple_wg, v_e_w_in, v_e_w_out, v_a_vnorm, v_a_ws, v_a_bs, v_b_onorm, v_b_lb_logits, v_c_mix, v_c_wr, v_c_wk, v_c_wv, v_c_wo, v_c_w0, v_c_w1, v_c_w2, v_c_a0, v_c_a1, v_c_a2, v_c_g1, v_c_g2, v_c_kk, v_c_ka, v_c_rk, v_c_gn_g, v_c_gn_b, v_c_v0, v_c_v1, v_c_v2):
    A = dict(locals())
    assert x.shape[0] == 1, "one example per device"

    big_names = list(BIG)
    gathered = _all_gather("gather_weights", [A[n].astype(BF16).reshape(-1, A[n].shape[-1]) for n in big_names],
                           [_pack([A[n] for n in SMALL])])
    GB = {n: gathered[i].reshape((N_CHIPS,) + A[n].shape) for i, n in enumerate(big_names)}
    GS = _unpack(gathered[-1], SMALL, [A[n].shape for n in SMALL], lead=1)

    part, grad_x, big_entries, sm_pack, rep_pack = _local_step(A, GB, GS)
    loss = lax.psum(part, ("x", "y", "c"))

    c_arr = lax.axis_index("c").astype(jnp.int32).reshape(1)
    grads = [e[2] for e in big_entries]
    swapped = _swap_halves("reduce_swap_cores", grads)
    parts = [_add_own_half("reduce_add_cores", g, s, c_arr) for g, s in zip(grads, swapped)]
    slots = _scatter_chips("reduce_scatter_chips", parts)
    halves = [_sum_slots("reduce_sum_chips", s, (3, 0, 1, 2)) for s in slots]
    places = [(big_names.index(n), idx) for n, idx, _ in big_entries]
    big_grads = _join_halves("reduce_join_cores", halves, places,
                             [jax.ShapeDtypeStruct(A[n].shape, F32) for n in big_names])
    sm_slots, rep_slots = _spread_all("reduce_small", sm_pack, rep_pack)
    sm_grad = _sum_slots("reduce_sum_small", sm_slots, tuple(range(8)))
    rep_grad = _sum_slots("reduce_sum_small", rep_slots, tuple(range(8)))

    outs = {}
    for o, n in enumerate(big_names):
        shp, C = A[n].shape, A[n].shape[-1]
        g = big_grads[o]
        d, nm, nv = _adamw("adamw", g.reshape(-1, C), A[n].reshape(-1, C), A['m_' + n].reshape(-1, C),
                           A['v_' + n].reshape(-1, C))
        outs[n] = (g, d.reshape(shp), nm.reshape(shp), nv.reshape(shp))
    for names, g in ((SMALL, sm_grad), (REP, rep_grad)):
        shapes = [A[n].shape for n in names]
        res = _adamw("adamw_packed", g, _pack([A[n] for n in names]), _pack([A['m_' + n] for n in names]),
                     _pack([A['v_' + n] for n in names]))
        un = [_unpack(t, names, shapes) for t in (g,) + tuple(res)]
        for n in names:
            outs[n] = tuple(u[n] for u in un)
    return (loss, grad_x, *[outs[n][0] for n in WEIGHTS], *[outs[n][1] for n in WEIGHTS],
            *[outs[n][2] for n in WEIGHTS], *[outs[n][3] for n in WEIGHTS])


def _local_step(A, GB, GS):
    x, p, a_vnorm, a_ws, a_bs, b_onorm, b_lb_logits, c_rk = (A[n] for n in (
        'x', 'p', 'a_vnorm', 'a_ws', 'a_bs', 'b_onorm', 'b_lb_logits', 'c_rk'))
    S, D = x.shape[1], x.shape[2]
    depth = A['ffn_wg'].shape[0]
    aw = D // 2
    h = x[0]
    target = A['loss_target'][0]
    final_norm = A['final_norm']
    pb = p[:, 0].astype(BF16)

    def bigv(n, *idx):
        return WV(GB[n], idx, BIG[n])

    smv = bigv

    def row(v):
        return v.reshape(1, -1)

    vecs = {n: _full_vec(GS[n]) for n in SMALL}
    lb_all, lb_vjp = jax.vjp(_lower_bounds, b_lb_logits)

    def rwkv_params(j):
        P = dict(mix=[vecs['c_mix'][j, q:q + 1] for q in range(6)],
                 wr=bigv('c_wr', j), wk=bigv('c_wk', j), wv=bigv('c_wv', j), wo=bigv('c_wo', j),
                 w1=smv('c_w1', j), w2=smv('c_w2', j), a1=smv('c_a1', j), a2=smv('c_a2', j),
                 g1=smv('c_g1', j), g2=smv('c_g2', j),
                 w0=row(vecs['c_w0'][j]), a0=row(vecs['c_a0'][j]), kk=row(vecs['c_kk'][j]), ka=row(vecs['c_ka'][j]),
                 gn_g=row(vecs['c_gn_g'][j]), gn_b=row(vecs['c_gn_b'][j]), rk=c_rk[j].reshape(1, D))
        if j > 0:
            P.update(v0=row(vecs['c_v0'][j - 1]), v1=smv('c_v1', j - 1), v2=smv('c_v2', j - 1))
        return P

    def even_params(i):
        j = i // 2
        return (bigv('e_w_in', j), bigv('e_w_out', j), a_vnorm[j:j + 1], a_ws[j], a_bs[j].T, b_onorm[j:j + 1],
                lb_all[i:i + 1])

    def gain(i, q):
        return row(vecs['norms'][i, q])

    def ffn_views(i, q):
        return bigv('ffn_wg', i, q), bigv('ffn_wu', i, q), bigv('ffn_wd', i, q)

    caches, vfirst = [], None
    for i in range(depth):
        c = {}
        h, c['f1'] = _ffn_fwd(h, gain(i, 0), *ffn_views(i, 0))
        if i % 2 == 0:
            h, c['mix'] = _even_fwd(h, gain(i, 1), *even_params(i))
        else:
            h, c['mix'], vfirst = _rwkv_fwd(h, gain(i, 1), rwkv_params(i // 2), vfirst if i // 2 > 0 else None)
        h, c['f2'] = _ffn_fwd(h, gain(i, 2), *ffn_views(i, 1))
        h, c['ple'] = _ple_fwd(h, gain(i, 3), pb[i], bigv('ple_wg', i), smv('ple_wp', i))
        caches.append(c)

    part, dh, dfinal = _loss_bwd(h, target, final_norm.reshape(1, D))
    big_entries = []
    sm = {n: {} for n in SMALL}
    rep = {n: {} for n in REP}
    dvfirst = None
    for i in reversed(range(depth)):
        j, c = i // 2, caches[i]
        dh, dg, dwgate, dwproj = _ple_bwd(dh, c['ple'], gain(i, 3), pb[i], bigv('ple_wg', i), smv('ple_wp', i))
        sm['norms'][(i, 3)] = dg
        big_entries += [('ple_wg', (i,), dwgate), ('ple_wp', (i,), dwproj)]
        for q, key in ((1, 'f2'), (0, 'f1')):
            if key == 'f1':
                if i % 2 == 0:
                    dh, G = _even_bwd(dh, c['mix'], gain(i, 1), *even_params(i))
                    big_entries += [('e_w_in', (j,), G['w_in']), ('e_w_out', (j,), G['w_out'])]
                    rep['a_vnorm'][j], rep['a_ws'][j], rep['a_bs'][j] = G['vgain'][0], G['ws'], G['bs_t'].T
                    rep['b_onorm'][j], rep['b_lb_logits'][i] = G['onorm'][0], G['lb'][0]
                else:
                    dh, G, dvf = _rwkv_bwd(dh, c['mix'], gain(i, 1), rwkv_params(j), dvfirst if j == 0 else None)
                    if dvf is not None:
                        dvfirst = dvf if dvfirst is None else dvfirst + dvf
                    big_entries += [('c_wr', (j,), G['wr']), ('c_wk', (j,), G['wk']), ('c_wv', (j,), G['wv']),
                                    ('c_wo', (j,), G['wo'])]
                    big_entries += [('c_' + n, (j,), G[n]) for n in ('w1', 'w2', 'a1', 'a2', 'g1', 'g2')]
                    for n in ('w0', 'a0', 'kk', 'ka', 'gn_g', 'gn_b'):
                        sm['c_' + n][(j,)] = G[n]
                    sm['c_mix'][(j,)] = jnp.concatenate(G['mix'], axis=0)
                    rep['c_rk'][j] = G['rk'].reshape(c_rk.shape[1:])
                    if j > 0:
                        sm['c_v0'][(j - 1,)] = G['v0']
                        big_entries += [('c_v1', (j - 1,), G['v1']), ('c_v2', (j - 1,), G['v2'])]
                sm['norms'][(i, 1)] = G['gain']
            dh, dg, dwg, dwu, dwd = _ffn_bwd(dh, c[key], gain(i, 2 * q), *ffn_views(i, q))
            sm['norms'][(i, 2 * q)] = dg
            big_entries += [('ffn_wg', (i, q), dwg), ('ffn_wu', (i, q), dwu), ('ffn_wd', (i, q), dwd)]
    grad_x = dh[None]

    def stacked(blocks, lead_shape):
        def rec(prefix, dims):
            if not dims:
                return blocks[prefix]
            return jnp.stack([rec(prefix + (q,), dims[1:]) for q in range(dims[0])], axis=0)
        return rec((), tuple(lead_shape))

    sm_shards = []
    for n in SMALL:
        blk = A[n].shape
        full = stacked(sm[n], blk[:-1] if n != 'c_mix' else blk[:-2])
        sm_shards.append(_vec_shards(full.reshape(blk[:-1] + (D,))))
    sm_pack = _pack(sm_shards, lead=1)
    dlb = jnp.stack([rep['b_lb_logits'].get(i, jnp.zeros((aw,), F32)) for i in range(depth)], axis=0)
    rep_grads = dict(final_norm=dfinal[0], a_vnorm=stacked({(k,): v for k, v in rep['a_vnorm'].items()}, a_vnorm.shape[:1]),
                     a_ws=stacked({(k,): v for k, v in rep['a_ws'].items()}, a_ws.shape[:1]),
                     a_bs=stacked({(k,): v for k, v in rep['a_bs'].items()}, a_bs.shape[:1]),
                     b_onorm=stacked({(k,): v for k, v in rep['b_onorm'].items()}, b_onorm.shape[:1]),
                     b_lb_logits=lb_vjp(dlb)[0],
                     c_rk=stacked({(k,): v for k, v in rep['c_rk'].items()}, c_rk.shape[:1]))
    rep_pack = _pack([rep_grads[n] for n in REP])
    return part, grad_x, big_entries, sm_pack, rep_pack
```
